```python
import jax
import jax.numpy as jnp
from jax import lax
import numpy as np

D_MODEL = 1024
BATCH = 16
SEQ = 2048
DEPTH = 2

HEAD_DIM = 64
N_HEADS = D_MODEL // HEAD_DIM
H_MOBA = N_HEADS // 4
H_NSA = (N_HEADS - H_MOBA) // 2
H_DIL = N_HEADS - H_MOBA - H_NSA
H_NSA_KV = 2
NSA_GROUP = H_NSA // H_NSA_KV
ROPE_DIM = HEAD_DIM // 4
ROPE_THETA = 500000.0
MOBA_BLOCK = 256
MOBA_TOPK = 3
MOBA_QCHUNK = 16
NSA_CMP_LEN = 32
NSA_CMP_STRIDE = 16
NSA_CMP_HIDDEN = 128
NSA_SEL_BLOCK = 64
NSA_N_SEL = 6
NSA_WINDOW = 512
NSA_QCHUNK = 64
DIL_CFG = ((128, 1), (512, 4), (2048, 16))
DIL_HEADS_PER_GROUP = H_DIL // len(DIL_CFG)
BAND_BLOCK = 128
D_FF = 2816
CONV_WIDTH = 3
EPS = 1e-6
NEG = -1e30
TINY = 1e-30
FORCE = 1e9
SCALE = HEAD_DIM ** -0.5
QKV_A = H_MOBA * HEAD_DIM
Q_B = H_NSA * HEAD_DIM
KV_B = H_NSA_KV * HEAD_DIM
G_B = H_NSA * 3
QKV_C = H_DIL * HEAD_DIM
IN_SPLITS = (QKV_A, QKV_A, QKV_A, Q_B, KV_B, KV_B, KV_B, KV_B, KV_B, KV_B, G_B, QKV_C, QKV_C, QKV_C)
D_IN = sum(IN_SPLITS)

kernel_name = 'hymba_style_moba_nsa_dilated_convffn'


def rms_norm(x, g):
    xf = x.astype(jnp.float32)
    y = xf * lax.rsqrt(jnp.mean(xf * xf, axis=-1, keepdims=True) + EPS)
    return (y * g.astype(jnp.float32)).astype(x.dtype)


def apply_rope(x, pos):
    half = ROPE_DIM // 2
    inv_freq = ROPE_THETA ** (-jnp.arange(half, dtype=jnp.float32) * 2.0 / ROPE_DIM)
    ang = pos.astype(jnp.float32)[..., None] * inv_freq
    c, s = jnp.cos(ang), jnp.sin(ang)
    xf = x.astype(jnp.float32)
    x1, x2, rest = xf[..., :half], xf[..., half:ROPE_DIM], xf[..., ROPE_DIM:]
    out = jnp.concatenate([x1 * c - x2 * s, x2 * c + x1 * s, rest], axis=-1)
    return out.astype(x.dtype)


def masked_softmax(s, mask):
    s = jnp.where(mask, s, NEG)
    m = jnp.max(s, axis=-1, keepdims=True)
    p = jnp.where(mask, jnp.exp(s - m), 0.0)
    den = jnp.sum(p, axis=-1, keepdims=True)
    lse = (m + jnp.log(jnp.maximum(den, TINY)))[..., 0]
    return p / jnp.maximum(den, TINY), lse


def banded_attention(q, k, v, max_dist, block):
    B, Hk, G, L, hd = q.shape
    nblk = L // block
    nprev = -(-max_dist // block)
    pad = nprev * block
    nk = (nprev + 1) * block
    def windows(t):
        tp = jnp.pad(t, ((0, 0), (0, 0), (pad, 0), (0, 0)))
        parts = [tp[:, :, p * block:p * block + L].reshape(B, Hk, nblk, block, hd) for p in range(nprev + 1)]
        return jnp.concatenate(parts, axis=3)
    kw, vw = windows(k), windows(v)
    qb = q.reshape(B, Hk, G, nblk, block, hd)
    s = jnp.einsum('bkgnqd,bkncd->bkgnqc', qb, kw).astype(jnp.float32) * SCALE
    qpos = jnp.arange(L).reshape(nblk, block)
    kpos = jnp.arange(nblk)[:, None] * block - pad + jnp.arange(nk)[None, :]
    dist = qpos[:, :, None] - kpos[:, None, :]
    mask = (dist >= 0) & (dist <= max_dist) & (kpos[:, None, :] >= 0)
    p, lse = masked_softmax(s, mask)
    o = jnp.einsum('bkgnqc,bkncd->bkgnqd', p.astype(v.dtype), vw)
    return o.reshape(B, Hk, G, L, hd), lse.reshape(B, Hk, G, L)


def moba_mixer(q, k, v):
    B, S, H, hd = q.shape
    nb = -(-S // MOBA_BLOCK)
    Sp = nb * MOBA_BLOCK
    qh = q.transpose(0, 2, 1, 3)
    def blocks(t):
        t = jnp.pad(t.transpose(0, 2, 1, 3), ((0, 0), (0, 0), (0, Sp - S), (0, 0)))
        return t.reshape(B, H, nb, MOBA_BLOCK, hd)
    kh, vh = blocks(k), blocks(v)
    tpos = jnp.arange(S)
    cur = tpos // MOBA_BLOCK
    idx_own = jnp.broadcast_to(cur, (B, H, S))[..., None]
    n_top = min(MOBA_TOPK, nb - 1)
    if n_top > 0:
        kmean = jnp.mean(kh.astype(jnp.float32), axis=3)
        gate = jnp.einsum('bhtd,bhnd->bhtn', qh.astype(jnp.float32), kmean)
        gate = jnp.where(jnp.arange(nb)[None, :] < cur[:, None], gate, NEG)
        _, idx_top = lax.top_k(gate, n_top)
        idx = jnp.concatenate([idx_top, idx_own], axis=-1)
        blk_ok = jnp.concatenate([jnp.arange(n_top)[None, :] < cur[:, None], jnp.ones((S, 1), bool)], axis=-1)
    else:
        idx = idx_own
        blk_ok = jnp.ones((S, 1), bool)
    nsel = idx.shape[-1]
    C = MOBA_QCHUNK
    nc = S // C
    qc = qh.reshape(B, H, nc, C, hd).transpose(2, 0, 1, 3, 4)
    ic = idx.reshape(B, H, nc, C, nsel).transpose(2, 0, 1, 3, 4)
    okc = blk_ok.reshape(nc, C, nsel)
    tc = tpos.reshape(nc, C)
    bi = jnp.arange(B)[:, None, None, None]
    hi = jnp.arange(H)[None, :, None, None]
    offs = jnp.arange(MOBA_BLOCK)
    def chunk(args):
        qi, ii, oki, ti = args
        kg = kh[bi, hi, ii]
        vg = vh[bi, hi, ii]
        s = jnp.einsum('bhcd,bhcnld->bhcnl', qi, kg).astype(jnp.float32) * SCALE
        kpos = ii[..., None] * MOBA_BLOCK + offs
        mask = oki[None, None, :, :, None] & (kpos <= ti[None, None, :, None, None])
        p, _ = masked_softmax(s.reshape(B, H, C, nsel * MOBA_BLOCK), mask.reshape(B, H, C, nsel * MOBA_BLOCK))
        return jnp.einsum('bhcm,bhcmd->bhcd', p.astype(vg.dtype), vg.reshape(B, H, C, nsel * MOBA_BLOCK, hd))
    o = lax.map(chunk, (qc, ic, okc, tc))
    return o.transpose(1, 0, 3, 2, 4).reshape(B, S, H * hd)


def nsa_mixer(q, k_cmp, v_cmp, k_sel, v_sel, k_win, v_win, gate_logits, kn_cmp, pe_k, pe_v, wk1, wk2, wv1, wv2):
    B, S, _, hd = q.shape
    qg = q.reshape(B, S, H_NSA_KV, NSA_GROUP, hd)
    tpos = jnp.arange(S)
    n_cmp = (S - NSA_CMP_LEN) // NSA_CMP_STRIDE + 1
    starts = np.arange(n_cmp) * NSA_CMP_STRIDE
    gidx = starts[:, None] + np.arange(NSA_CMP_LEN)[None, :]
    ends = jnp.asarray(starts + NSA_CMP_LEN - 1)
    def compress(t, pe, w1, w2):
        blk = t[:, gidx] + pe[:, None, :]
        blk = blk.transpose(0, 1, 3, 2, 4).reshape(B, n_cmp, H_NSA_KV, NSA_CMP_LEN * hd)
        return jax.nn.gelu(blk @ w1) @ w2
    kc = apply_rope(rms_norm(compress(k_cmp, pe_k, wk1, wk2), kn_cmp), ends[:, None])
    vc = compress(v_cmp, pe_v, wv1, wv2)
    s = jnp.einsum('btkgd,bnkd->bkgtn', qg, kc).astype(jnp.float32) * SCALE
    p_cmp, _ = masked_softmax(s, ends[None, :] <= tpos[:, None])
    o_cmp = jnp.einsum('bkgtn,bnkd->btkgd', p_cmp.astype(vc.dtype), vc)
    n_slc = S // NSA_SEL_BLOCK
    j = np.arange(n_slc)
    overlap = (starts[:, None] < (j[None, :] + 1) * NSA_SEL_BLOCK) & (starts[:, None] + NSA_CMP_LEN > j[None, :] * NSA_SEL_BLOCK)
    imp = jnp.einsum('bkgtn,nj->bktj', p_cmp, jnp.asarray(overlap, jnp.float32))
    cur = tpos // NSA_SEL_BLOCK
    jj = jnp.arange(n_slc)[None, :]
    forced = (jj == 0) | (jj == cur[:, None]) | (jj == cur[:, None] - 1)
    imp = jnp.where(jj > cur[:, None], NEG, jnp.where(forced, FORCE, imp))
    n_sel = min(NSA_N_SEL, n_slc)
    _, sidx = lax.top_k(imp, n_sel)
    ksb = k_sel.transpose(0, 2, 1, 3).reshape(B, H_NSA_KV, n_slc, NSA_SEL_BLOCK, hd)
    vsb = v_sel.transpose(0, 2, 1, 3).reshape(B, H_NSA_KV, n_slc, NSA_SEL_BLOCK, hd)
    C = NSA_QCHUNK
    nc = S // C
    qt = qg.transpose(0, 2, 3, 1, 4)
    qc = qt.reshape(B, H_NSA_KV, NSA_GROUP, nc, C, hd).transpose(3, 0, 1, 2, 4, 5)
    ic = sidx.reshape(B, H_NSA_KV, nc, C, n_sel).transpose(2, 0, 1, 3, 4)
    tc = tpos.reshape(nc, C)
    bi = jnp.arange(B)[:, None, None, None]
    ki = jnp.arange(H_NSA_KV)[None, :, None, None]
    offs = jnp.arange(NSA_SEL_BLOCK)
    m_sel = n_sel * NSA_SEL_BLOCK
    def chunk(args):
        qi, ii, ti = args
        kg = ksb[bi, ki, ii]
        vg = vsb[bi, ki, ii]
        s = jnp.einsum('bkgcd,bkcnld->bkgcnl', qi, kg).astype(jnp.float32) * SCALE
        kpos = ii[..., None] * NSA_SEL_BLOCK + offs
        mask = (kpos <= ti[None, None, :, None, None]).reshape(B, H_NSA_KV, 1, C, m_sel)
        p, _ = masked_softmax(s.reshape(B, H_NSA_KV, NSA_GROUP, C, m_sel), mask)
        return jnp.einsum('bkgcm,bkcmd->bkgcd', p.astype(vg.dtype), vg.reshape(B, H_NSA_KV, C, m_sel, hd))
    o_sel = lax.map(chunk, (qc, ic, tc))
    o_sel = o_sel.transpose(1, 0, 4, 2, 3, 5).reshape(B, S, H_NSA_KV, NSA_GROUP, hd)
    o_win, _ = banded_attention(qt, k_win.transpose(0, 2, 1, 3), v_win.transpose(0, 2, 1, 3), NSA_WINDOW - 1, BAND_BLOCK)
    o_win = o_win.transpose(0, 3, 1, 2, 4)
    g = jax.nn.sigmoid(gate_logits.astype(jnp.float32)).reshape(B, S, H_NSA_KV, NSA_GROUP, 3).astype(q.dtype)
    out = g[..., 0:1] * o_cmp + g[..., 1:2] * o_sel + g[..., 2:3] * o_win
    return out.reshape(B, S, H_NSA * hd)


def dilated_group(q, k, v, dil, max_dist):
    B, S, H, hd = q.shape
    L = S // dil
    Lp = -(-L // BAND_BLOCK) * BAND_BLOCK
    def to_classes(t):
        t = t.reshape(B, L, dil, H, hd).transpose(0, 2, 3, 1, 4).reshape(B * dil, H, L, hd)
        return jnp.pad(t, ((0, 0), (0, 0), (0, Lp - L), (0, 0)))
    o, lse = banded_attention(to_classes(q)[:, :, None], to_classes(k), to_classes(v), max_dist, BAND_BLOCK)
    o = o[:, :, 0, :L].reshape(B, dil, H, L, hd).transpose(0, 3, 1, 2, 4).reshape(B, S, H, hd)
    lse = lse[:, :, 0, :L].reshape(B, dil, H, L).transpose(0, 3, 1, 2).reshape(B, S, H)
    return o, lse


def dilated_mixer(q, k, v):
    B, S, _, hd = q.shape
    outs, lses = [], []
    for g, (window, dil) in enumerate(DIL_CFG):
        sl = slice(g * DIL_HEADS_PER_GROUP, (g + 1) * DIL_HEADS_PER_GROUP)
        o, lse = dilated_group(q[:, :, sl], k[:, :, sl], v[:, :, sl], dil, window // dil)
        outs.append(o)
        lses.append(lse)
    alpha = jax.nn.softmax(jnp.stack(lses, axis=2), axis=2)
    o = jnp.stack(outs, axis=2) * alpha[..., None].astype(q.dtype)
    return o.reshape(B, S, H_DIL * hd)


def conv_ffn(h, w_gate, w_up, conv_w, conv_b, w_down):
    g = h @ w_gate
    u = h @ w_up
    g = lax.conv_general_dilated(g, conv_w[:, None, :], window_strides=(1,), padding=[(CONV_WIDTH - 1, 0)],
                                 dimension_numbers=('NWC', 'WIO', 'NWC'), feature_group_count=D_FF) + conv_b
    return (jax.nn.silu(g) * u) @ w_down


def hybrid_layer(x, ln1, w_in, qn_a, kn_a, qn_b, kn_b, cmp_pe_k, cmp_pe_v, cmp_k_w1, cmp_k_w2, cmp_v_w1, cmp_v_w2,
                 qn_c, kn_c, w_out, ln2, w_gate, w_up, conv_w, conv_b, w_down):
    B, S, _ = x.shape
    h = rms_norm(x, ln1)
    proj = h @ w_in
    (qa, ka, va, qb, kcb, vcb, ksb, vsb, kwb, vwb, gb, qc, kc, vc) = jnp.split(
        proj, np.cumsum(IN_SPLITS)[:-1].tolist(), axis=-1)
    heads = lambda t, n: t.reshape(B, S, n, HEAD_DIM)
    pos = jnp.arange(S)[:, None]
    qk = lambda t, n, g: apply_rope(rms_norm(heads(t, n), g), pos)
    o_a = moba_mixer(qk(qa, H_MOBA, qn_a), qk(ka, H_MOBA, kn_a), heads(va, H_MOBA))
    o_b = nsa_mixer(qk(qb, H_NSA, qn_b), heads(kcb, H_NSA_KV), heads(vcb, H_NSA_KV),
                    qk(ksb, H_NSA_KV, kn_b[1]), heads(vsb, H_NSA_KV), qk(kwb, H_NSA_KV, kn_b[2]), heads(vwb, H_NSA_KV),
                    gb, kn_b[0], cmp_pe_k, cmp_pe_v, cmp_k_w1, cmp_k_w2, cmp_v_w1, cmp_v_w2)
    o_c = dilated_mixer(qk(qc, H_DIL, qn_c), qk(kc, H_DIL, kn_c), heads(vc, H_DIL))
    x = x + jnp.concatenate([o_a, o_b, o_c], axis=-1) @ w_out
    return x + conv_ffn(rms_norm(x, ln2), w_gate, w_up, conv_w, conv_b, w_down)


def setup_inputs(seed: int = 0) -> dict:
    key = jax.random.key(seed)
    ks = jax.random.split(key, 24)
    f32 = jnp.float32
    nrm = lambda k, shape, scale: jax.random.normal(k, shape, f32) * scale
    gain = lambda k, shape: 1.0 + 0.05 * jax.random.normal(k, shape, f32)
    flat = NSA_CMP_LEN * HEAD_DIM
    return {
        'x': nrm(ks[0], (BATCH, SEQ, D_MODEL), 1.0),
        'ln1': gain(ks[1], (DEPTH, D_MODEL)),
        'w_in': nrm(ks[2], (DEPTH, D_MODEL, D_IN), D_MODEL ** -0.5),
        'qn_a': gain(ks[3], (DEPTH, HEAD_DIM)),
        'kn_a': gain(ks[4], (DEPTH, HEAD_DIM)),
        'qn_b': gain(ks[5], (DEPTH, HEAD_DIM)),
        'kn_b': gain(ks[6], (DEPTH, 3, HEAD_DIM)),
        'cmp_pe_k': nrm(ks[7], (DEPTH, NSA_CMP_LEN, HEAD_DIM), 0.1),
        'cmp_pe_v': nrm(ks[8], (DEPTH, NSA_CMP_LEN, HEAD_DIM), 0.1),
        'cmp_k_w1': nrm(ks[9], (DEPTH, flat, NSA_CMP_HIDDEN), flat ** -0.5),
        'cmp_k_w2': nrm(ks[10], (DEPTH, NSA_CMP_HIDDEN, HEAD_DIM), NSA_CMP_HIDDEN ** -0.5),
        'cmp_v_w1': nrm(ks[11], (DEPTH, flat, NSA_CMP_HIDDEN), flat ** -0.5),
        'cmp_v_w2': nrm(ks[12], (DEPTH, NSA_CMP_HIDDEN, HEAD_DIM), NSA_CMP_HIDDEN ** -0.5),
        'qn_c': gain(ks[13], (DEPTH, HEAD_DIM)),
        'kn_c': gain(ks[14], (DEPTH, HEAD_DIM)),
        'w_out': nrm(ks[15], (DEPTH, D_MODEL, D_MODEL), D_MODEL ** -0.5),
        'ln2': gain(ks[16], (DEPTH, D_MODEL)),
        'w_gate': nrm(ks[17], (DEPTH, D_MODEL, D_FF), D_MODEL ** -0.5),
        'w_up': nrm(ks[18], (DEPTH, D_MODEL, D_FF), D_MODEL ** -0.5),
        'conv_w': nrm(ks[19], (DEPTH, CONV_WIDTH, D_FF), CONV_WIDTH ** -0.5),
        'conv_b': nrm(ks[20], (DEPTH, D_FF), 0.01),
        'w_down': nrm(ks[21], (DEPTH, D_FF, D_MODEL), D_FF ** -0.5),
    }


def reference(x, ln1, w_in, qn_a, kn_a, qn_b, kn_b, cmp_pe_k, cmp_pe_v, cmp_k_w1, cmp_k_w2, cmp_v_w1, cmp_v_w2,
              qn_c, kn_c, w_out, ln2, w_gate, w_up, conv_w, conv_b, w_down):
    for l in range(DEPTH):
        x = hybrid_layer(x, ln1[l], w_in[l], qn_a[l], kn_a[l], qn_b[l], kn_b[l], cmp_pe_k[l], cmp_pe_v[l],
                         cmp_k_w1[l], cmp_k_w2[l], cmp_v_w1[l], cmp_v_w2[l], qn_c[l], kn_c[l], w_out[l],
                         ln2[l], w_gate[l], w_up[l], conv_w[l], conv_b[l], w_down[l])
    return x
```

```python
import functools

import numpy as np
import jax
import jax.numpy as jnp
from jax import lax
from jax.experimental import pallas as pl
from jax.experimental.pallas import tpu as pltpu

F32 = jnp.float32
BF16 = jnp.bfloat16

D_MODEL = 1024
HEAD_DIM = 64
N_HEADS = D_MODEL // HEAD_DIM
H_MOBA = N_HEADS // 4
H_NSA = (N_HEADS - H_MOBA) // 2
H_DIL = N_HEADS - H_MOBA - H_NSA
H_NSA_KV = 2
NSA_GROUP = H_NSA // H_NSA_KV
ROPE_DIM = HEAD_DIM // 4
ROPE_THETA = 500000.0
MOBA_BLOCK = 256
MOBA_TOPK = 3
NSA_CMP_LEN = 32
NSA_CMP_STRIDE = 16
NSA_CMP_HIDDEN = 128
NSA_SEL_BLOCK = 64
NSA_N_SEL = 6
NSA_WINDOW = 512
DIL_CFG = ((128, 1), (512, 4), (2048, 16))
BAND_BLOCK = 128
D_FF = 2816
EPS = 1e-6
NEG = -1e30
TINY = 1e-30
FORCE = 1e9
SCALE = HEAD_DIM ** -0.5
QKV_A = H_MOBA * HEAD_DIM
Q_B = H_NSA * HEAD_DIM
KV_B = H_NSA_KV * HEAD_DIM
G_B = H_NSA * 3
QKV_C = H_DIL * HEAD_DIM
IN_SPLITS = (QKV_A, QKV_A, QKV_A, Q_B, KV_B, KV_B, KV_B, KV_B, KV_B, KV_B, G_B, QKV_C, QKV_C, QKV_C)

LANES = 128
TQ = 256
NSA_ORDER = (0, 3, 1, 4, 2, 5)
VMEM_LIMIT = 56 * 1024 * 1024

N_T_NORM = 5
N_R_NORM = 10
N_T_RAW = 4
N_R_RAW = 3
N_F32 = 3
N_NORM = N_T_NORM + N_R_NORM
N_BLOCKS = N_NORM + N_T_RAW + N_R_RAW + N_F32
N_PBT = N_T_NORM + N_T_RAW
N_PB = N_R_NORM + N_R_RAW
_PB_DST = {9: 0, 10: 1, 11: 2, 12: 3, 13: 4, 14: 5, 19: 6, 20: 7, 21: 8, 5: 9, 6: 10, 7: 11, 8: 12}
_PBT_DST = {0: 0, 1: 1, 2: 2, 3: 3, 4: 4, 15: 5, 16: 6, 17: 7, 18: 8}
PBT_QB, PBT_QA, PBT_VA, PBT_VS, PBT_VW = 0, 3, 5, 7, 8
PB_QC, PB_KC, PB_VC, PB_KA, PB_KS, PB_KW = 0, 3, 6, 9, 11, 12
CHUNK = 5


def _cparams(sem):
    return pltpu.CompilerParams(dimension_semantics=sem, vmem_limit_bytes=VMEM_LIMIT)


def _const_spec(shape):
    return pl.BlockSpec(shape, lambda *_: (0,) * len(shape), pipeline_mode=pl.Buffered(1))


def _norm_rope(y, gain, cos_t, sin_p, sin_m, lo):
    ss = y * y
    s_lo = jnp.sum(jnp.where(lo, ss, 0.0), axis=-1, keepdims=True)
    s_hi = jnp.sum(jnp.where(lo, 0.0, ss), axis=-1, keepdims=True)
    ms = jnp.where(lo, s_lo, s_hi) * (1.0 / HEAD_DIM)
    z = (y * lax.rsqrt(ms + EPS)) * gain
    half = ROPE_DIM // 2
    return z * cos_t + pltpu.roll(z, half, 1) * sin_p + pltpu.roll(z, LANES - half, 1) * sin_m


def _softmax_update(carry, s, mask, v_t):
    m, l, acc = carry
    s = jnp.where(mask, s, NEG)
    m_new = jnp.maximum(m, jnp.max(s, axis=0, keepdims=True))
    p = jnp.where(mask, jnp.exp(s - m_new), 0.0)
    alpha = jnp.exp(m - m_new)
    l = alpha * l + jnp.sum(p, axis=0, keepdims=True)
    acc = alpha * acc + jnp.dot(v_t, p.astype(BF16), preferred_element_type=F32)
    return m_new, l, acc


def _softmax_init(tq):
    return (jnp.full((1, tq), NEG, F32), jnp.zeros((1, tq), F32), jnp.zeros((HEAD_DIM, tq), F32))


def _half_rows(q_t, hh):
    z = jnp.zeros((HEAD_DIM, q_t.shape[1]), q_t.dtype)
    if hh == 0:
        return jnp.concatenate([q_t[:HEAD_DIM], z], axis=0)
    return jnp.concatenate([z, q_t[HEAD_DIM:]], axis=0)


def _split_bf16(a):
    hi = a.astype(BF16)
    return hi, (a - hi.astype(F32)).astype(BF16)


def _inproj_kernel(x_ref, ln_ref, w_ref, gain_ref, cos_ref, sp_ref, sm_ref, pbt_ref, pb_ref, pf_ref, *, tm):
    x = x_ref[0]
    h = x * lax.rsqrt(jnp.mean(x * x, axis=-1, keepdims=True) + EPS)
    h = (h * ln_ref[...]).astype(BF16)
    lo = lax.broadcasted_iota(jnp.int32, (tm, LANES), 1) < HEAD_DIM
    cos_t, sin_p, sin_m = cos_ref[...], sp_ref[...], sm_ref[...]
    for c0 in range(0, N_BLOCKS, CHUNK):
        y5 = jnp.dot(h, w_ref[:, c0 * LANES:(c0 + CHUNK) * LANES], preferred_element_type=F32)
        for c in range(c0, c0 + CHUNK):
            y = y5[:, (c - c0) * LANES:(c - c0 + 1) * LANES]
            if c < N_NORM:
                y = _norm_rope(y, gain_ref[:, c * LANES:(c + 1) * LANES], cos_t, sin_p, sin_m, lo)
            if c in _PBT_DST:
                y_t = y.T.astype(BF16)
                for u in range(tm // TQ):
                    pbt_ref[0, _PBT_DST[c], u] = y_t[:, u * TQ:(u + 1) * TQ]
            elif c in _PB_DST:
                d = _PB_DST[c]
                pb_ref[0, :, d * LANES:(d + 1) * LANES] = y.astype(BF16)
            else:
                d = c - (N_BLOCKS - N_F32)
                pf_ref[0, :, d * LANES:(d + 1) * LANES] = y


def _in_proj(x, ln, w, gain, cos_t, sin_p, sin_m, tm=512):
    B, S, D = x.shape
    nt = S // tm
    n_cols = N_BLOCKS * LANES
    return pl.pallas_call(
        functools.partial(_inproj_kernel, tm=tm),
        grid=(B * nt,),
        in_specs=[
            pl.BlockSpec((1, tm, D), lambda i: (i // nt, i % nt, 0)),
            _const_spec((1, D)),
            _const_spec((D, n_cols)),
            _const_spec((1, N_NORM * LANES)),
            pl.BlockSpec((tm, LANES), lambda i: (i % nt, 0)),
            pl.BlockSpec((tm, LANES), lambda i: (i % nt, 0)),
            pl.BlockSpec((tm, LANES), lambda i: (i % nt, 0)),
        ],
        out_specs=[
            pl.BlockSpec((1, N_PBT, tm // TQ, LANES, TQ), lambda i: (i // nt, 0, i % nt, 0, 0)),
            pl.BlockSpec((1, tm, N_PB * LANES), lambda i: (i // nt, i % nt, 0)),
            pl.BlockSpec((1, tm, N_F32 * LANES), lambda i: (i // nt, i % nt, 0)),
        ],
        out_shape=[
            jax.ShapeDtypeStruct((B, N_PBT, S // TQ, LANES, TQ), BF16),
            jax.ShapeDtypeStruct((B, S, N_PB * LANES), BF16),
            jax.ShapeDtypeStruct((B, S, N_F32 * LANES), F32),
        ],
        compiler_params=_cparams(("parallel",)),
        name="in_proj",
    )(x, ln, w, gain, cos_t, sin_p, sin_m)


def _moba_kernel(qt_ref, k_ref, vt_ref, o_ref, sel_ref, *, n_blk):
    qi = pl.program_id(2)
    q_t = qt_ref[0, 0, 0]
    rows = [jnp.sum(k_ref[0, n * TQ:(n + 1) * TQ, :].astype(F32), axis=0, keepdims=True) for n in range(n_blk)]
    kmean = jnp.concatenate(rows + [jnp.zeros((16 - n_blk, LANES), F32)], axis=0) * (1.0 / MOBA_BLOCK)
    km_hi, km_lo = _split_bf16(kmean)
    nidx = lax.broadcasted_iota(jnp.int32, (16, TQ), 0)
    past = nidx < qi
    kr = lax.broadcasted_iota(jnp.int32, (TQ, TQ), 0)
    ql = lax.broadcasted_iota(jnp.int32, (TQ, TQ), 1)
    causal = kr <= ql
    outs = []
    for hh in range(2):
        q_h = _half_rows(q_t, hh)
        gate = (jnp.dot(km_hi, q_h, preferred_element_type=F32)
                + jnp.dot(km_lo, q_h, preferred_element_type=F32))
        for n in range(n_blk):
            row = gate[n:n + 1, :]
            beats = past & ((gate > row) | ((gate == row) & (nidx < n)))
            cnt = jnp.sum(jnp.where(beats, 1.0, 0.0), axis=0, keepdims=True)
            sel_ref[hh, n] = jnp.where(cnt < MOBA_TOPK, 1.0, 0.0)

        def past_tile(kt, carry, q_h=q_h, hh=hh):
            off = pl.multiple_of(kt * TQ, TQ)
            s = jnp.dot(k_ref[0, pl.ds(off, TQ), :], q_h, preferred_element_type=F32) * SCALE
            mask = jnp.broadcast_to(sel_ref[hh, kt], (TQ, TQ)) > 0.5
            return _softmax_update(carry, s, mask, vt_ref[0, 0, kt, hh * HEAD_DIM:(hh + 1) * HEAD_DIM, :])

        carry = lax.fori_loop(0, qi, past_tile, _softmax_init(TQ))
        off = pl.multiple_of(qi * TQ, TQ)
        s = jnp.dot(k_ref[0, pl.ds(off, TQ), :], q_h, preferred_element_type=F32) * SCALE
        _, l, acc = _softmax_update(carry, s, causal, vt_ref[0, 0, qi, hh * HEAD_DIM:(hh + 1) * HEAD_DIM, :])
        outs.append(acc / jnp.maximum(l, TINY))
    o_ref[0] = jnp.concatenate(outs, axis=0).T.astype(BF16)


def _moba(pbt, pb):
    B, _, n_t, _, _ = pbt.shape
    S = pb.shape[1]
    n_pairs = H_MOBA // 2
    return pl.pallas_call(
        functools.partial(_moba_kernel, n_blk=S // MOBA_BLOCK),
        grid=(B, n_pairs, n_t),
        in_specs=[
            pl.BlockSpec((1, 1, 1, LANES, TQ), lambda b, p, i: (b, PBT_QA + p, i, 0, 0)),
            pl.BlockSpec((1, S, LANES), lambda b, p, i: (b, 0, PB_KA + p)),
            pl.BlockSpec((1, 1, n_t, LANES, TQ), lambda b, p, i: (b, PBT_VA + p, 0, 0, 0)),
        ],
        out_specs=pl.BlockSpec((1, TQ, LANES), lambda b, p, i: (b, i, p)),
        out_shape=jax.ShapeDtypeStruct((B, S, QKV_A), BF16),
        scratch_shapes=[pltpu.VMEM((2, S // MOBA_BLOCK, 1, TQ), F32)],
        compiler_params=_cparams(("parallel", "parallel", "arbitrary")),
        name="moba",
    )(pbt, pb, pbt)


def _compress_kernel(kc_ref, vc_ref, pek_ref, pev_ref, w1k_ref, w2k_ref, w1v_ref, w2v_ref, gain_ref,
                     cos_ref, sp_ref, sm_ref, ko_ref, vo_ref, *, n_cmp):
    half_len = NSA_CMP_LEN // 2

    def mlp(t_ref, pe_ref, w1_ref, w2_ref):
        first = jnp.zeros((n_cmp, 2 * NSA_CMP_HIDDEN), F32)
        second = jnp.zeros((n_cmp, 2 * NSA_CMP_HIDDEN), F32)
        for l in range(half_len):
            a = t_ref[0, pl.ds(l, n_cmp, stride=NSA_CMP_STRIDE), :]
            first = first + jnp.dot((a + pe_ref[l:l + 1, :]).astype(BF16), w1_ref[l],
                                    preferred_element_type=F32)
            second = second + jnp.dot((a + pe_ref[half_len + l:half_len + l + 1, :]).astype(BF16),
                                      w1_ref[half_len + l], preferred_element_type=F32)
        hid = jax.nn.gelu(first + pltpu.roll(second, n_cmp - 1, 0))
        return jnp.dot(hid.astype(BF16), w2_ref[...], preferred_element_type=F32)

    valid = lax.broadcasted_iota(jnp.int32, (n_cmp, LANES), 0) < n_cmp - 1
    lo = lax.broadcasted_iota(jnp.int32, (n_cmp, LANES), 1) < HEAD_DIM
    kc = mlp(kc_ref, pek_ref, w1k_ref, w2k_ref)
    kc = _norm_rope(kc, gain_ref[...], cos_ref[...], sp_ref[...], sm_ref[...], lo)
    ko_ref[0] = jnp.where(valid, kc, 0.0).astype(BF16)
    vc = mlp(vc_ref, pev_ref, w1v_ref, w2v_ref)
    vo_ref[0] = jnp.where(valid, vc, 0.0).T.astype(BF16)


def _compress(pf, pe_k, pe_v, w1k, w2k, w1v, w2v, gain, cos_c, sp_c, sm_c):
    B, S, _ = pf.shape
    n_cmp = S // NSA_CMP_STRIDE
    cs = _const_spec
    return pl.pallas_call(
        functools.partial(_compress_kernel, n_cmp=n_cmp),
        grid=(B,),
        in_specs=[
            pl.BlockSpec((1, S, LANES), lambda b: (b, 0, 0)),
            pl.BlockSpec((1, S, LANES), lambda b: (b, 0, 1)),
            cs(pe_k.shape), cs(pe_v.shape), cs(w1k.shape), cs(w2k.shape), cs(w1v.shape), cs(w2v.shape),
            cs(gain.shape), cs(cos_c.shape), cs(sp_c.shape), cs(sm_c.shape),
        ],
        out_specs=[pl.BlockSpec((1, n_cmp, LANES), lambda b: (b, 0, 0)),
                   pl.BlockSpec((1, LANES, n_cmp), lambda b: (b, 0, 0))],
        out_shape=[jax.ShapeDtypeStruct((B, n_cmp, LANES), BF16),
                   jax.ShapeDtypeStruct((B, LANES, n_cmp), BF16)],
        compiler_params=_cparams(("parallel",)),
        name="nsa_compress",
    )(pf, pf, pe_k, pe_v, w1k, w2k, w1v, w2v, gain, cos_c, sp_c, sm_c)


def _nsa_kernel(qt_ref, kc_ref, vct_ref, ks_ref, vst_ref, kw_ref, vwt_ref, g_ref, ovl_ref, o_ref,
                imp_ref, sel_ref, ot_ref, *, n_cmp, n_slc):
    qi = pl.program_id(1)
    blk_per_tile = TQ // NSA_SEL_BLOCK
    tpos = qi * TQ + lax.broadcasted_iota(jnp.int32, (1, TQ), 1)
    sig = jax.nn.sigmoid(g_ref[0].T)
    kr = lax.broadcasted_iota(jnp.int32, (TQ, TQ), 0)
    ql = lax.broadcasted_iota(jnp.int32, (TQ, TQ), 1)
    causal = kr <= ql
    nidx = lax.broadcasted_iota(jnp.int32, (n_cmp, TQ), 0)
    cmask = (nidx * NSA_CMP_STRIDE + (NSA_CMP_LEN - 1) <= tpos) & (nidx < n_cmp - 1)
    jj = lax.broadcasted_iota(jnp.int32, (n_slc, TQ), 0)
    cur = lax.shift_right_arithmetic(tpos, NSA_SEL_BLOCK.bit_length() - 1)
    forced = (jj == 0) | (jj == cur) | (jj == cur - 1)
    kc = kc_ref[0]

    for kk in range(H_NSA_KV):
        hs = slice(kk * HEAD_DIM, (kk + 1) * HEAD_DIM)
        q_hs = [_half_rows(qt_ref[0, j, 0], kk) for j in range(NSA_GROUP)]
        o_cmp = []
        p_sum = jnp.zeros((n_cmp, TQ), F32)
        for j in range(NSA_GROUP):
            s = jnp.where(cmask, jnp.dot(kc, q_hs[j], preferred_element_type=F32) * SCALE, NEG)
            m = jnp.max(s, axis=0, keepdims=True)
            p = jnp.where(cmask, jnp.exp(s - m), 0.0)
            p = p / jnp.maximum(jnp.sum(p, axis=0, keepdims=True), TINY)
            o_cmp.append(jnp.dot(vct_ref[0, hs, :], p.astype(BF16), preferred_element_type=F32))
            p_sum = p_sum + p
        p_hi, p_lo = _split_bf16(p_sum)
        imp = (jnp.dot(ovl_ref[...], p_hi, preferred_element_type=F32)
               + jnp.dot(ovl_ref[...], p_lo, preferred_element_type=F32))
        imp = jnp.where(jj > cur, NEG, jnp.where(forced, FORCE, imp))
        for j in range(n_slc):
            imp_ref[j] = imp[j:j + 1, :]

        def rank(j, _, imp=imp):
            row = imp_ref[j]
            beats = (imp > row) | ((imp == row) & (jj < j))
            cnt = jnp.sum(jnp.where(beats, 1.0, 0.0), axis=0, keepdims=True)
            sel_ref[j] = jnp.where(cnt < NSA_N_SEL, 1.0, 0.0)
            return 0

        lax.fori_loop(0, (qi + 1) * blk_per_tile, rank, 0)

        def sel_mask(kt):
            parts = [jnp.broadcast_to(sel_ref[kt * blk_per_tile + c], (NSA_SEL_BLOCK, TQ))
                     for c in range(blk_per_tile)]
            return jnp.concatenate(parts, axis=0) > 0.5

        for j in range(NSA_GROUP):
            q_h = q_hs[j]
            def sel_tile(kt, carry, q_h=q_h):
                off = pl.multiple_of(kt * TQ, TQ)
                s = jnp.dot(ks_ref[0, pl.ds(off, TQ), :], q_h, preferred_element_type=F32) * SCALE
                return _softmax_update(carry, s, sel_mask(kt), vst_ref[0, 0, kt, hs, :])

            carry = lax.fori_loop(0, qi, sel_tile, _softmax_init(TQ))
            off = pl.multiple_of(qi * TQ, TQ)
            s = jnp.dot(ks_ref[0, pl.ds(off, TQ), :], q_h, preferred_element_type=F32) * SCALE
            _, l, acc = _softmax_update(carry, s, sel_mask(qi) & causal, vst_ref[0, 0, qi, hs, :])
            o_sel = acc / jnp.maximum(l, TINY)
            carry = _softmax_init(TQ)
            off2 = jnp.where(qi >= 2, 0, TQ)
            off1 = jnp.where(qi >= 1, 0, TQ)
            for back, mask in ((2, kr > ql + off2), (1, kr >= off1), (0, causal)):
                kt = jnp.maximum(qi - back, 0)
                off = pl.multiple_of(kt * TQ, TQ)
                s = jnp.dot(kw_ref[0, pl.ds(off, TQ), :], q_h, preferred_element_type=F32) * SCALE
                carry = _softmax_update(carry, s, mask, vwt_ref[0, 0, kt, hs, :])
            _, l, acc = carry
            o_win = acc / jnp.maximum(l, TINY)
            g0 = (j + NSA_GROUP * kk) * 3
            ot_ref[j, hs, :] = (sig[g0:g0 + 1, :] * o_cmp[j] + sig[g0 + 1:g0 + 2, :] * o_sel
                                + sig[g0 + 2:g0 + 3, :] * o_win)
    for j in range(NSA_GROUP):
        o_ref[0, :, j * LANES:(j + 1) * LANES] = ot_ref[j].T.astype(BF16)


def _nsa(pbt, pb, pf, kcmp, vcmp_t, ovl_t):
    B, _, n_t, _, _ = pbt.shape
    S = pb.shape[1]
    n_cmp = kcmp.shape[1]
    n_slc = S // NSA_SEL_BLOCK
    return pl.pallas_call(
        functools.partial(_nsa_kernel, n_cmp=n_cmp, n_slc=n_slc),
        grid=(B, n_t),
        in_specs=[
            pl.BlockSpec((1, NSA_GROUP, 1, LANES, TQ), lambda b, i: (b, PBT_QB // NSA_GROUP, i, 0, 0)),
            pl.BlockSpec((1, n_cmp, LANES), lambda b, i: (b, 0, 0)),
            pl.BlockSpec((1, LANES, n_cmp), lambda b, i: (b, 0, 0)),
            pl.BlockSpec((1, S, LANES), lambda b, i: (b, 0, PB_KS)),
            pl.BlockSpec((1, 1, n_t, LANES, TQ), lambda b, i: (b, PBT_VS, 0, 0, 0)),
            pl.BlockSpec((1, S, LANES), lambda b, i: (b, 0, PB_KW)),
            pl.BlockSpec((1, 1, n_t, LANES, TQ), lambda b, i: (b, PBT_VW, 0, 0, 0)),
            pl.BlockSpec((1, TQ, LANES), lambda b, i: (b, i, 2)),
            _const_spec(ovl_t.shape),
        ],
        out_specs=pl.BlockSpec((1, TQ, Q_B), lambda b, i: (b, i, 0)),
        out_shape=jax.ShapeDtypeStruct((B, S, Q_B), BF16),
        scratch_shapes=[pltpu.VMEM((n_slc, 1, TQ), F32), pltpu.VMEM((n_slc, 1, TQ), F32),
                        pltpu.VMEM((NSA_GROUP, LANES, TQ), F32)],
        compiler_params=_cparams(("parallel", "arbitrary")),
        name="nsa_attn",
    )(pbt, kcmp, vcmp_t, pb, pbt, pb, pbt, pf, ovl_t)


def _dilated_kernel(q_ref, k_ref, v_ref, o_ref, qf_ref, kf_ref, vf_ref, os_ref, ls_ref, *, seq):
    blk = BAND_BLOCK
    n_tiles = seq // blk
    ri = lax.broadcasted_iota(jnp.int32, (blk, 2 * blk), 0)
    ci = lax.broadcasted_iota(jnp.int32, (blk, 2 * blk), 1)
    dist = ri + blk - ci
    lo = lax.broadcasted_iota(jnp.int32, (blk, LANES), 1) < HEAD_DIM
    for g, (window, dil) in enumerate(DIL_CFG):
        max_dist = window // dil
        tiles_per_class = n_tiles // dil
        band = (dist >= 0) & (dist <= max_dist)
        gs = slice(g * LANES, (g + 1) * LANES)
        qf_ref[...] = q_ref[0, :, gs].astype(F32)
        kf_ref[...] = k_ref[0, :, gs].astype(F32)
        vf_ref[...] = v_ref[0, :, gs].astype(F32)

        def rows(ref, start, dil=dil):
            if dil == 1:
                return ref[pl.ds(pl.multiple_of(start, blk), blk), :]
            return ref[pl.ds(start, blk, stride=dil), :]

        def tile(t, _, g=g, dil=dil, tiles_per_class=tiles_per_class, band=band, rows=rows):
            cls = t // tiles_per_class
            mt = t % tiles_per_class
            start = cls + dil * blk * mt
            prev = jnp.maximum(start - dil * blk, cls)
            q = rows(qf_ref, start)
            k2 = jnp.concatenate([rows(kf_ref, prev), rows(kf_ref, start)], axis=0).astype(BF16)
            v2 = jnp.concatenate([rows(vf_ref, prev), rows(vf_ref, start)], axis=0).astype(BF16)
            mask = band & (ci >= jnp.where(mt > 0, 0, blk))
            o_h, lse_h = [], []
            for hh in range(2):
                q_h = jnp.where(lo if hh == 0 else ~lo, q, 0.0).astype(BF16)
                s = lax.dot_general(q_h, k2, (((1,), (1,)), ((), ())), preferred_element_type=F32) * SCALE
                s = jnp.where(mask, s, NEG)
                m = jnp.max(s, axis=-1, keepdims=True)
                p = jnp.where(mask, jnp.exp(s - m), 0.0)
                den = jnp.sum(p, axis=-1, keepdims=True)
                lse_h.append(m + jnp.log(jnp.maximum(den, TINY)))
                p = p / jnp.maximum(den, TINY)
                o_h.append(jnp.dot(p.astype(BF16), v2, preferred_element_type=F32))
            o_t = jnp.where(lo, o_h[0], o_h[1])
            l_t = jnp.where(lo, lse_h[0], lse_h[1])
            if dil == 1:
                idx = pl.ds(pl.multiple_of(start, blk), blk)
            else:
                idx = pl.ds(start, blk, stride=dil)
            os_ref[g, idx, :] = o_t
            ls_ref[g, idx, :] = l_t
            return 0

        lax.fori_loop(0, n_tiles, tile, 0)

    def combine(t, _):
        idx = pl.ds(pl.multiple_of(t * blk, blk), blk)
        lses = [ls_ref[g, idx, :] for g in range(len(DIL_CFG))]
        mx = jnp.maximum(jnp.maximum(lses[0], lses[1]), lses[2])
        es = [jnp.exp(l - mx) for l in lses]
        tot = es[0] + es[1] + es[2]
        for g in range(len(DIL_CFG)):
            o_ref[0, idx, g * LANES:(g + 1) * LANES] = (os_ref[g, idx, :] * (es[g] / tot)).astype(BF16)
        return 0

    lax.fori_loop(0, n_tiles, combine, 0)


def _dilated(pb):
    B, S, _ = pb.shape
    n_g = len(DIL_CFG)
    return pl.pallas_call(
        functools.partial(_dilated_kernel, seq=S),
        grid=(B,),
        in_specs=[pl.BlockSpec((1, S, QKV_C), lambda b: (b, 0, PB_QC // n_g)),
                  pl.BlockSpec((1, S, QKV_C), lambda b: (b, 0, PB_KC // n_g)),
                  pl.BlockSpec((1, S, QKV_C), lambda b: (b, 0, PB_VC // n_g))],
        out_specs=pl.BlockSpec((1, S, QKV_C), lambda b: (b, 0, 0)),
        out_shape=jax.ShapeDtypeStruct((B, S, QKV_C), BF16),
        scratch_shapes=[pltpu.VMEM((S, LANES), F32), pltpu.VMEM((S, LANES), F32), pltpu.VMEM((S, LANES), F32),
                        pltpu.VMEM((n_g, S, LANES), F32), pltpu.VMEM((n_g, S, LANES), F32)],
        compiler_params=_cparams(("parallel",)),
        name="dilated_attn",
    )(pb, pb, pb)


def _outproj_kernel(x_ref, oa_ref, ob_ref, oc_ref, wa_ref, wb_ref, wc_ref, y_ref):
    y = x_ref[...]
    y = y + jnp.dot(oa_ref[...], wa_ref[...], preferred_element_type=F32)
    y = y + jnp.dot(ob_ref[...], wb_ref[...], preferred_element_type=F32)
    y_ref[...] = y + jnp.dot(oc_ref[...], wc_ref[...], preferred_element_type=F32)


def _out_proj(x2, o_a, o_b, o_c, wa, wb, wc, tm=512):
    M, D = x2.shape
    row = lambda w: pl.BlockSpec((tm, w), lambda i: (i, 0))
    return pl.pallas_call(
        _outproj_kernel,
        grid=(M // tm,),
        in_specs=[row(D), row(QKV_A), row(Q_B), row(QKV_C),
                  _const_spec(wa.shape), _const_spec(wb.shape), _const_spec(wc.shape)],
        out_specs=row(D),
        out_shape=jax.ShapeDtypeStruct((M, D), F32),
        compiler_params=_cparams(("parallel",)),
        name="out_proj",
    )(x2, o_a, o_b, o_c, wa, wb, wc)


HALO = 16
FF_CHUNK = 256


def _ffn_kernel(x_ref, halo_ref, ln_ref, wg_ref, wu_ref, cw_ref, cb_ref, wd_ref, y_ref, h_ref, acc_ref,
                *, tm, tiles_per_seq):
    i = pl.program_id(0)

    def norm(v):
        return (v * lax.rsqrt(jnp.mean(v * v, axis=-1, keepdims=True) + EPS)) * ln_ref[...]

    x = x_ref[...]
    first = (i % tiles_per_seq) == 0
    h_ref[:HALO, :] = jnp.where(first, 0.0, norm(halo_ref[...])).astype(BF16)
    h_ref[HALO:, :] = norm(x).astype(BF16)
    for f0 in range(0, D_FF, FF_CHUNK):
        fs = slice(f0, f0 + FF_CHUNK)
        g = jnp.dot(h_ref[...], wg_ref[:, fs], preferred_element_type=F32)
        u = jnp.dot(h_ref[HALO:, :], wu_ref[:, fs], preferred_element_type=F32)
        gc = (cw_ref[0:1, fs] * pltpu.roll(g, 2, 0)[HALO:] + cw_ref[1:2, fs] * pltpu.roll(g, 1, 0)[HALO:]
              + cw_ref[2:3, fs] * g[HALO:] + cb_ref[:, fs])
        a = (jax.nn.silu(gc) * u).astype(BF16)
        d = jnp.dot(a, wd_ref[fs, :], preferred_element_type=F32)
        if f0 == 0:
            acc_ref[...] = d
        else:
            acc_ref[...] += d
    y_ref[...] = x + acc_ref[...]


def _conv_ffn(x2, ln, wg, wu, cw, cb, wd, seq, tm=512):
    M, D = x2.shape
    halo_blocks = tm // HALO
    return pl.pallas_call(
        functools.partial(_ffn_kernel, tm=tm, tiles_per_seq=seq // tm),
        grid=(M // tm,),
        in_specs=[
            pl.BlockSpec((tm, D), lambda i: (i, 0)),
            pl.BlockSpec((HALO, D), lambda i: (jnp.maximum(i * halo_blocks - 1, 0), 0)),
            _const_spec((1, D)),
            _const_spec(wg.shape), _const_spec(wu.shape), _const_spec(cw.shape), _const_spec(cb.shape),
            _const_spec(wd.shape),
        ],
        out_specs=pl.BlockSpec((tm, D), lambda i: (i, 0)),
        out_shape=jax.ShapeDtypeStruct((M, D), F32),
        scratch_shapes=[pltpu.VMEM((HALO + tm, D), BF16), pltpu.VMEM((tm, D), F32)],
        compiler_params=_cparams(("parallel",)),
        name="conv_ffn",
    )(x2, x2, ln, wg, wu, cw, cb, wd)


def _rope_tables(pos):
    half = ROPE_DIM // 2
    inv_freq = ROPE_THETA ** (-jnp.arange(half, dtype=F32) * 2.0 / ROPE_DIM)
    ang = pos.astype(F32)[:, None] * inv_freq
    c, s = jnp.cos(ang), jnp.sin(ang)
    n = pos.shape[0]
    rest = HEAD_DIM - ROPE_DIM
    cos_h = jnp.concatenate([c, c, jnp.ones((n, rest), F32)], axis=-1)
    sp_h = jnp.concatenate([jnp.zeros((n, half), F32), s, jnp.zeros((n, rest), F32)], axis=-1)
    sm_h = jnp.concatenate([-s, jnp.zeros((n, half + rest), F32)], axis=-1)
    two = lambda t: jnp.concatenate([t, t], axis=-1)
    return two(cos_h), two(sp_h), two(sm_h)


def _pair_diag(w):
    z = jnp.zeros_like(w)
    return jnp.concatenate([jnp.concatenate([w, z], axis=-1), jnp.concatenate([z, w], axis=-1)], axis=-2)


def _overlap_t(seq):
    n_cmp = seq // NSA_CMP_STRIDE
    n_slc = seq // NSA_SEL_BLOCK
    starts = np.arange(n_cmp) * NSA_CMP_STRIDE
    j = np.arange(n_slc)
    ovl = (starts[None, :] < (j[:, None] + 1) * NSA_SEL_BLOCK) & (starts[None, :] + NSA_CMP_LEN > j[:, None] * NSA_SEL_BLOCK)
    ovl[:, n_cmp - 1] = False
    return jnp.asarray(ovl, BF16)


def _layer(x, seq_tabs, cmp_tabs, ovl_t, ln1, w_in, qn_a, kn_a, qn_b, kn_b, pe_k, pe_v, wk1, wk2, wv1, wv2,
           qn_c, kn_c, w_out, ln2, w_gate, w_up, conv_w, conv_b, w_down):
    B, S, D = x.shape
    offs = np.cumsum((0,) + IN_SPLITS)
    qa, ka, va, qb, kcb, vcb, ksb, vsb, kwb, vwb, gb, qc, kc, vc = [
        w_in[:, offs[n]:offs[n + 1]] for n in range(len(IN_SPLITS))]
    order = jnp.asarray(NSA_ORDER)
    qb = qb.reshape(D, H_NSA, HEAD_DIM)[:, order].reshape(D, Q_B)
    gb = jnp.pad(gb, ((0, 0), (0, LANES - G_B)))
    w = jnp.concatenate([qb, qa, ka, ksb, kwb, qc, kc, va, vsb, vwb, vc, kcb, vcb, gb], axis=1).astype(BF16)
    rep = lambda g, n: jnp.tile(g, n)
    gain = jnp.concatenate([rep(qn_b, H_NSA), rep(qn_a, H_MOBA), rep(kn_a, H_MOBA), rep(kn_b[1], H_NSA_KV),
                            rep(kn_b[2], H_NSA_KV), rep(qn_c, H_DIL), rep(kn_c, H_DIL)])[None, :]
    pbt, pb, pf = _in_proj(x, ln1[None, :], w, gain, *seq_tabs)

    o_a = _moba(pbt, pb)
    two = lambda t: jnp.concatenate([t, t], axis=-1)
    w1 = lambda t: _pair_diag(t.reshape(NSA_CMP_LEN, HEAD_DIM, NSA_CMP_HIDDEN)).astype(BF16)
    kcmp, vcmp_t = _compress(pf, two(pe_k), two(pe_v), w1(wk1), _pair_diag(wk2).astype(BF16),
                             w1(wv1), _pair_diag(wv2).astype(BF16), two(kn_b[0])[None, :], *cmp_tabs)
    o_b = _nsa(pbt, pb, pf, kcmp, vcmp_t, ovl_t)
    o_c = _dilated(pb)

    w_o = w_out.astype(BF16)
    wb = w_o[QKV_A:QKV_A + Q_B].reshape(H_NSA, HEAD_DIM, D)[order].reshape(Q_B, D)
    M = B * S
    x2 = _out_proj(x.reshape(M, D), o_a.reshape(M, QKV_A), o_b.reshape(M, Q_B), o_c.reshape(M, QKV_C),
                   w_o[:QKV_A], wb, w_o[QKV_A + Q_B:])
    x2 = _conv_ffn(x2, ln2[None, :], w_gate.astype(BF16), w_up.astype(BF16), conv_w, conv_b[None, :],
                   w_down.astype(BF16), S)
    return x2.reshape(B, S, D)


def kernel(x, ln1, w_in, qn_a, kn_a, qn_b, kn_b, cmp_pe_k, cmp_pe_v, cmp_k_w1, cmp_k_w2, cmp_v_w1, cmp_v_w2,
           qn_c, kn_c, w_out, ln2, w_gate, w_up, conv_w, conv_b, w_down):
    S = x.shape[1]
    seq_tabs = _rope_tables(jnp.arange(S))
    n_cmp = S // NSA_CMP_STRIDE
    cmp_tabs = _rope_tables(jnp.arange(n_cmp) * NSA_CMP_STRIDE + (NSA_CMP_LEN - 1))
    ovl_t = _overlap_t(S)
    for l in range(ln1.shape[0]):
        x = _layer(x, seq_tabs, cmp_tabs, ovl_t, ln1[l], w_in[l], qn_a[l], kn_a[l], qn_b[l], kn_b[l],
                   cmp_pe_k[l], cmp_pe_v[l], cmp_k_w1[l], cmp_k_w2[l], cmp_v_w1[l], cmp_v_w2[l], qn_c[l], kn_c[l],
                   w_out[l], ln2[l], w_gate[l], w_up[l], conv_w[l], conv_b[l], w_down[l])
    return x
```

```python
import functools

import numpy as np
import jax
import jax.numpy as jnp
from jax import lax
from jax.experimental import pallas as pl
from jax.experimental.pallas import tpu as pltpu

F32 = jnp.float32
BF16 = jnp.bfloat16

D_MODEL = 1024
HEAD_DIM = 64
N_HEADS = D_MODEL // HEAD_DIM
H_MOBA = N_HEADS // 4
H_NSA = (N_HEADS - H_MOBA) // 2
H_DIL = N_HEADS - H_MOBA - H_NSA
H_NSA_KV = 2
NSA_GROUP = H_NSA // H_NSA_KV
ROPE_DIM = HEAD_DIM // 4
ROPE_THETA = 500000.0
MOBA_BLOCK = 256
MOBA_TOPK = 3
NSA_CMP_LEN = 32
NSA_CMP_STRIDE = 16
NSA_CMP_HIDDEN = 128
NSA_SEL_BLOCK = 64
NSA_N_SEL = 6
NSA_WINDOW = 512
DIL_CFG = ((128, 1), (512, 4), (2048, 16))
BAND_BLOCK = 128
D_FF = 2816
EPS = 1e-6
NEG = -1e30
TINY = 1e-30
FORCE = 1e9
SCALE = HEAD_DIM ** -0.5
QKV_A = H_MOBA * HEAD_DIM
Q_B = H_NSA * HEAD_DIM
KV_B = H_NSA_KV * HEAD_DIM
G_B = H_NSA * 3
QKV_C = H_DIL * HEAD_DIM
IN_SPLITS = (QKV_A, QKV_A, QKV_A, Q_B, KV_B, KV_B, KV_B, KV_B, KV_B, KV_B, G_B, QKV_C, QKV_C, QKV_C)

LANES = 128
TQ = 256
NSA_ORDER = (0, 3, 1, 4, 2, 5)
VMEM_LIMIT = 56 * 1024 * 1024

M_FLOOR = -1e29

N_T_NORM = 5
N_R_NORM = 10
N_T_RAW = 4
N_R_RAW = 3
N_F32 = 3
N_NORM = N_T_NORM + N_R_NORM
N_BLOCKS = N_NORM + N_T_RAW + N_R_RAW + N_F32
N_PBT = N_T_NORM + N_T_RAW
N_PB = N_R_NORM + N_R_RAW
PBT_QA, PBT_VA, PBT_VS, PBT_VW, PBT_QB = 0, 2, 4, 5, 6
PB_QC, PB_KC, PB_KA, PB_KS, PB_VC, PB_KW = 0, 3, 6, 8, 9, 12
_PBT_DST = {0: 0, 1: 1, 2: 6, 3: 7, 4: 8, 15: 2, 16: 3, 17: 4, 18: 5}
_PB_DST = {5: 6, 6: 7, 7: 8, 8: 12, 9: 0, 10: 1, 11: 2, 12: 3, 13: 4, 14: 5, 19: 9, 20: 10, 21: 11}
CHUNK = 5


def _cparams(sem):
    return pltpu.CompilerParams(dimension_semantics=sem, vmem_limit_bytes=VMEM_LIMIT)


def _const_spec(shape):
    return pl.BlockSpec(shape, lambda *_: (0,) * len(shape), pipeline_mode=pl.Buffered(1))


def _norm_rope(y, gain, cos_t, sin_p, sin_m, lo):
    ss = y * y
    s_lo = jnp.sum(jnp.where(lo, ss, 0.0), axis=-1, keepdims=True)
    s_hi = jnp.sum(jnp.where(lo, 0.0, ss), axis=-1, keepdims=True)
    ms = jnp.where(lo, s_lo, s_hi) * (1.0 / HEAD_DIM)
    z = (y * lax.rsqrt(ms + EPS)) * gain
    half = ROPE_DIM // 2
    return z * cos_t + pltpu.roll(z, half, 1) * sin_p + pltpu.roll(z, LANES - half, 1) * sin_m


def _online_update(m_ref, l_ref, acc_ref, g, s, pv):
    m_old = m_ref[g]
    m_new = jnp.maximum(m_old, jnp.max(s, axis=0, keepdims=True))
    p = jnp.exp(s - m_new)
    alpha = jnp.exp(m_old - m_new)
    l_ref[g] = alpha * l_ref[g] + jnp.sum(p, axis=0, keepdims=True)
    acc_ref[g] = alpha * acc_ref[g] + pv(p.astype(BF16))
    m_ref[g] = m_new


def _online_init(m_ref, l_ref, acc_ref, g):
    m_ref[g] = jnp.full(m_ref.shape[1:], M_FLOOR, F32)
    l_ref[g] = jnp.zeros(l_ref.shape[1:], F32)
    acc_ref[g] = jnp.zeros(acc_ref.shape[1:], F32)


def _dot(a, b):
    return jnp.dot(a, b, preferred_element_type=F32)


def _half_rows(q_t, hh):
    z = jnp.zeros((HEAD_DIM, q_t.shape[1]), q_t.dtype)
    if hh == 0:
        return jnp.concatenate([q_t[:HEAD_DIM], z], axis=0)
    return jnp.concatenate([z, q_t[HEAD_DIM:]], axis=0)


def _split_bf16(a):
    hi = a.astype(BF16)
    return hi, (a - hi.astype(F32)).astype(BF16)


def _inproj_kernel(x_ref, ln_ref, w_ref, gain_ref, cos_ref, sp_ref, sm_ref, pbt_ref, pb_ref, pf_ref, *, tm):
    x = x_ref[0]
    h = x * lax.rsqrt(jnp.mean(x * x, axis=-1, keepdims=True) + EPS)
    h = (h * ln_ref[...]).astype(BF16)
    lo = lax.broadcasted_iota(jnp.int32, (tm, LANES), 1) < HEAD_DIM
    cos_t, sin_p, sin_m = cos_ref[...], sp_ref[...], sm_ref[...]
    for c0 in range(0, N_BLOCKS, CHUNK):
        y5 = jnp.dot(h, w_ref[:, c0 * LANES:(c0 + CHUNK) * LANES], preferred_element_type=F32)
        for c in range(c0, c0 + CHUNK):
            y = y5[:, (c - c0) * LANES:(c - c0 + 1) * LANES]
            if c < N_NORM:
                y = _norm_rope(y, gain_ref[:, c * LANES:(c + 1) * LANES], cos_t, sin_p, sin_m, lo)
            if c in _PBT_DST:
                y_t = y.T.astype(BF16)
                for u in range(tm // TQ):
                    pbt_ref[0, _PBT_DST[c], u] = y_t[:, u * TQ:(u + 1) * TQ]
            elif c in _PB_DST:
                d = _PB_DST[c]
                pb_ref[0, :, d * LANES:(d + 1) * LANES] = y.astype(BF16)
            else:
                d = c - (N_BLOCKS - N_F32)
                pf_ref[0, :, d * LANES:(d + 1) * LANES] = y


def _in_proj(x, ln, w, gain, cos_t, sin_p, sin_m, tm=512):
    B, S, D = x.shape
    nt = S // tm
    n_cols = N_BLOCKS * LANES
    return pl.pallas_call(
        functools.partial(_inproj_kernel, tm=tm),
        grid=(B * nt,),
        in_specs=[
            pl.BlockSpec((1, tm, D), lambda i: (i // nt, i % nt, 0)),
            _const_spec((1, D)),
            _const_spec((D, n_cols)),
            _const_spec((1, N_NORM * LANES)),
            pl.BlockSpec((tm, LANES), lambda i: (i % nt, 0)),
            pl.BlockSpec((tm, LANES), lambda i: (i % nt, 0)),
            pl.BlockSpec((tm, LANES), lambda i: (i % nt, 0)),
        ],
        out_specs=[
            pl.BlockSpec((1, N_PBT, tm // TQ, LANES, TQ), lambda i: (i // nt, 0, i % nt, 0, 0)),
            pl.BlockSpec((1, tm, N_PB * LANES), lambda i: (i // nt, i % nt, 0)),
            pl.BlockSpec((1, tm, N_F32 * LANES), lambda i: (i // nt, i % nt, 0)),
        ],
        out_shape=[
            jax.ShapeDtypeStruct((B, N_PBT, S // TQ, LANES, TQ), BF16),
            jax.ShapeDtypeStruct((B, S, N_PB * LANES), BF16),
            jax.ShapeDtypeStruct((B, S, N_F32 * LANES), F32),
        ],
        compiler_params=_cparams(("parallel",)),
        name="in_proj",
    )(x, ln, w, gain, cos_t, sin_p, sin_m)


def _moba_kernel(qt_ref, k_ref, vt_ref, o_ref, q2_ref, bias_ref, m_ref, l_ref, acc_ref, *, n_blk):
    qi = pl.program_id(1)
    n_pairs = H_MOBA // 2
    nidx = lax.broadcasted_iota(jnp.int32, (16, 2 * TQ), 0)
    past = nidx < qi
    kr = lax.broadcasted_iota(jnp.int32, (TQ, TQ), 0)
    ql = lax.broadcasted_iota(jnp.int32, (TQ, TQ), 1)
    causal_bias = jnp.where(kr <= ql, 0.0, NEG)
    for p in range(n_pairs):
        ps = slice(p * LANES, (p + 1) * LANES)
        rows = [jnp.sum(k_ref[0, n * TQ:(n + 1) * TQ, ps].astype(F32), axis=0, keepdims=True)
                for n in range(n_blk)]
        kmean = jnp.concatenate(rows + [jnp.zeros((16 - n_blk, LANES), F32)], axis=0) * (1.0 / MOBA_BLOCK)
        km_hi, km_lo = _split_bf16(kmean)
        q2 = jnp.concatenate([_half_rows(qt_ref[0, p, 0], hh) for hh in range(2)], axis=1)
        q2_ref[p] = q2
        gate = _dot(km_hi, q2) + _dot(km_lo, q2)
        for n in range(n_blk):
            row = gate[n:n + 1, :]
            ge = jnp.where(gate >= row, 1.0, 0.0)
            gt = jnp.where(gate > row, 1.0, 0.0)
            beats = jnp.where(past, jnp.where(nidx < n, ge, gt), 0.0)
            cnt = jnp.sum(beats, axis=0, keepdims=True)
            bias_ref[p, n] = jnp.where(cnt < MOBA_TOPK, 0.0, NEG)
        _online_init(m_ref, l_ref, acc_ref, p)

    def scores(off, p):
        return _dot(k_ref[0, pl.ds(off, TQ), p * LANES:(p + 1) * LANES], q2_ref[p])

    def pv(p, kt):
        def fn(pr):
            return jnp.concatenate([_dot(vt_ref[0, p, kt, :HEAD_DIM, :], pr[:, :TQ]),
                                    _dot(vt_ref[0, p, kt, HEAD_DIM:, :], pr[:, TQ:])], axis=1)
        return fn

    def past_tile(kt, _):
        off = pl.multiple_of(kt * TQ, TQ)
        s = [scores(off, p) + bias_ref[p, kt] for p in range(n_pairs)]
        for p in range(n_pairs):
            _online_update(m_ref, l_ref, acc_ref, p, s[p], pv(p, kt))
        return 0

    lax.fori_loop(0, qi, past_tile, 0)
    off = pl.multiple_of(qi * TQ, TQ)
    causal2 = jnp.concatenate([causal_bias, causal_bias], axis=1)
    s = [scores(off, p) + causal2 for p in range(n_pairs)]
    for p in range(n_pairs):
        _online_update(m_ref, l_ref, acc_ref, p, s[p], pv(p, qi))
        o2 = acc_ref[p] / jnp.maximum(l_ref[p], TINY)
        o_ref[0, :, p * LANES:(p + 1) * LANES] = jnp.concatenate([o2[:, :TQ], o2[:, TQ:]], axis=0).T.astype(BF16)


def _moba(pbt, pb):
    B, _, n_t, _, _ = pbt.shape
    S = pb.shape[1]
    n_pairs = H_MOBA // 2
    return pl.pallas_call(
        functools.partial(_moba_kernel, n_blk=S // MOBA_BLOCK),
        grid=(B, n_t),
        in_specs=[
            pl.BlockSpec((1, n_pairs, 1, LANES, TQ), lambda b, i: (b, PBT_QA // n_pairs, i, 0, 0)),
            pl.BlockSpec((1, S, QKV_A), lambda b, i: (b, 0, PB_KA // n_pairs)),
            pl.BlockSpec((1, n_pairs, n_t, LANES, TQ), lambda b, i: (b, PBT_VA // n_pairs, 0, 0, 0)),
        ],
        out_specs=pl.BlockSpec((1, TQ, QKV_A), lambda b, i: (b, i, 0)),
        out_shape=jax.ShapeDtypeStruct((B, S, QKV_A), BF16),
        scratch_shapes=[pltpu.VMEM((n_pairs, LANES, 2 * TQ), BF16),
                        pltpu.VMEM((n_pairs, S // MOBA_BLOCK, 1, 2 * TQ), F32),
                        pltpu.VMEM((n_pairs, 1, 2 * TQ), F32), pltpu.VMEM((n_pairs, 1, 2 * TQ), F32),
                        pltpu.VMEM((n_pairs, HEAD_DIM, 2 * TQ), F32)],
        compiler_params=_cparams(("parallel", "arbitrary")),
        name="moba",
    )(pbt, pb, pbt)


def _compress_kernel(kc_ref, vc_ref, pek_ref, pev_ref, w1k_ref, w2k_ref, w1v_ref, w2v_ref, gain_ref,
                     cos_ref, sp_ref, sm_ref, ko_ref, vo_ref, *, n_cmp):
    half_len = NSA_CMP_LEN // 2

    def mlp(t_ref, pe_ref, w1_ref, w2_ref):
        first = jnp.zeros((n_cmp, 2 * NSA_CMP_HIDDEN), F32)
        second = jnp.zeros((n_cmp, 2 * NSA_CMP_HIDDEN), F32)
        for l in range(half_len):
            a = t_ref[0, pl.ds(l, n_cmp, stride=NSA_CMP_STRIDE), :]
            first = first + jnp.dot((a + pe_ref[l:l + 1, :]).astype(BF16), w1_ref[l],
                                    preferred_element_type=F32)
            second = second + jnp.dot((a + pe_ref[half_len + l:half_len + l + 1, :]).astype(BF16),
                                      w1_ref[half_len + l], preferred_element_type=F32)
        hid = jax.nn.gelu(first + pltpu.roll(second, n_cmp - 1, 0))
        return jnp.dot(hid.astype(BF16), w2_ref[...], preferred_element_type=F32)

    valid = lax.broadcasted_iota(jnp.int32, (n_cmp, LANES), 0) < n_cmp - 1
    lo = lax.broadcasted_iota(jnp.int32, (n_cmp, LANES), 1) < HEAD_DIM
    kc = mlp(kc_ref, pek_ref, w1k_ref, w2k_ref)
    kc = _norm_rope(kc, gain_ref[...], cos_ref[...], sp_ref[...], sm_ref[...], lo)
    ko_ref[0] = jnp.where(valid, kc, 0.0).astype(BF16)
    vc = mlp(vc_ref, pev_ref, w1v_ref, w2v_ref)
    vo_ref[0] = jnp.where(valid, vc, 0.0).T.astype(BF16)


def _compress(pf, pe_k, pe_v, w1k, w2k, w1v, w2v, gain, cos_c, sp_c, sm_c):
    B, S, _ = pf.shape
    n_cmp = S // NSA_CMP_STRIDE
    cs = _const_spec
    return pl.pallas_call(
        functools.partial(_compress_kernel, n_cmp=n_cmp),
        grid=(B,),
        in_specs=[
            pl.BlockSpec((1, S, LANES), lambda b: (b, 0, 0)),
            pl.BlockSpec((1, S, LANES), lambda b: (b, 0, 1)),
            cs(pe_k.shape), cs(pe_v.shape), cs(w1k.shape), cs(w2k.shape), cs(w1v.shape), cs(w2v.shape),
            cs(gain.shape), cs(cos_c.shape), cs(sp_c.shape), cs(sm_c.shape),
        ],
        out_specs=[pl.BlockSpec((1, n_cmp, LANES), lambda b: (b, 0, 0)),
                   pl.BlockSpec((1, LANES, n_cmp), lambda b: (b, 0, 0))],
        out_shape=[jax.ShapeDtypeStruct((B, n_cmp, LANES), BF16),
                   jax.ShapeDtypeStruct((B, LANES, n_cmp), BF16)],
        compiler_params=_cparams(("parallel",)),
        name="nsa_compress",
    )(pf, pf, pe_k, pe_v, w1k, w2k, w1v, w2v, gain, cos_c, sp_c, sm_c)


def _nsa_kernel(qt_ref, kc_ref, vct_ref, ks_ref, vst_ref, kw_ref, vwt_ref, g_ref, ovl_ref, o_ref,
                q6_ref, impf_ref, improw_ref, m_ref, l_ref, acc_ref, ot_ref, *, n_cmp, n_slc):
    qi = pl.program_id(1)
    blk_per_tile = TQ // NSA_SEL_BLOCK
    tpos = qi * TQ + lax.broadcasted_iota(jnp.int32, (1, TQ), 1)
    sig = jax.nn.sigmoid(g_ref[0].T)
    kr = lax.broadcasted_iota(jnp.int32, (TQ, TQ), 0)
    ql = lax.broadcasted_iota(jnp.int32, (TQ, TQ), 1)
    causal_bias = jnp.where(kr <= ql, 0.0, NEG)
    nidx = lax.broadcasted_iota(jnp.int32, (n_cmp, TQ), 0)
    cmask = (nidx * NSA_CMP_STRIDE + (NSA_CMP_LEN - 1) <= tpos) & (nidx < n_cmp - 1)
    cmp_bias = jnp.where(cmask, 0.0, NEG)
    jj = lax.broadcasted_iota(jnp.int32, (n_slc, TQ), 0)
    cur = lax.shift_right_arithmetic(tpos, NSA_SEL_BLOCK.bit_length() - 1)
    forced = (jj == 0) | (jj == cur) | (jj == cur - 1)
    gw = NSA_GROUP * TQ

    def half(kk):
        return slice(kk * HEAD_DIM, (kk + 1) * HEAD_DIM)

    def cols(kk):
        return slice(kk * gw, (kk + 1) * gw)

    def tile3(a):
        return jnp.concatenate([a] * NSA_GROUP, axis=1)

    def gate3(kk, branch):
        return jnp.concatenate([sig[3 * (NSA_GROUP * kk + j) + branch:3 * (NSA_GROUP * kk + j) + branch + 1, :]
                                for j in range(NSA_GROUP)], axis=1)

    q6 = jnp.concatenate([_half_rows(qt_ref[0, j, 0], kk) for kk in range(H_NSA_KV) for j in range(NSA_GROUP)],
                         axis=1)
    q6_ref[...] = q6

    s_cmp = _dot(kc_ref[0], q6)
    cmp_bias3 = tile3(cmp_bias)
    for kk in range(H_NSA_KV):
        s = s_cmp[:, cols(kk)] + cmp_bias3
        m = jnp.maximum(jnp.max(s, axis=0, keepdims=True), M_FLOOR)
        p = jnp.exp(s - m)
        p = p / jnp.maximum(jnp.sum(p, axis=0, keepdims=True), TINY)
        ot_ref[kk] = gate3(kk, 0) * _dot(vct_ref[0, half(kk), :], p.astype(BF16))
        p_sum = (p[:, :TQ] + p[:, TQ:2 * TQ]) + p[:, 2 * TQ:]
        _online_init(m_ref, l_ref, acc_ref, kk)
        p_hi, p_lo = _split_bf16(p_sum)
        imp = _dot(ovl_ref[...], p_hi) + _dot(ovl_ref[...], p_lo)
        imp = jnp.where(jj > cur, NEG, jnp.where(forced, FORCE, imp))
        impf_ref[kk] = imp
        for j in range(n_slc):
            improw_ref[kk, j] = imp[j:j + 1, :]

    def sel_bias(kk, kt):
        imp = impf_ref[kk]
        parts = []
        for c in range(blk_per_tile):
            j = kt * blk_per_tile + c
            row = improw_ref[kk, j]
            ge = jnp.where(imp >= row, 1.0, 0.0)
            gt = jnp.where(imp > row, 1.0, 0.0)
            cnt = jnp.sum(jnp.where(jj < j, ge, gt), axis=0, keepdims=True)
            parts.append(jnp.broadcast_to(jnp.where(cnt < NSA_N_SEL, 0.0, NEG), (NSA_SEL_BLOCK, TQ)))
        return jnp.concatenate(parts, axis=0)

    def sel_pv(kk, kt):
        return lambda pr: _dot(vst_ref[0, 0, kt, half(kk), :], pr)

    def sel_tile(kt, _):
        off = pl.multiple_of(kt * TQ, TQ)
        s6 = _dot(ks_ref[0, pl.ds(off, TQ), :], q6_ref[...])
        for kk in range(H_NSA_KV):
            s = s6[:, cols(kk)] + tile3(sel_bias(kk, kt))
            _online_update(m_ref, l_ref, acc_ref, kk, s, sel_pv(kk, kt))
        return 0

    lax.fori_loop(0, qi, sel_tile, 0)

    off = pl.multiple_of(qi * TQ, TQ)
    kt1 = jnp.maximum(qi - 1, 0)
    kt2 = jnp.maximum(qi - 2, 0)
    ks_d = ks_ref[0, pl.ds(off, TQ), :]
    kw_0 = kw_ref[0, pl.ds(off, TQ), :]
    kw_1 = kw_ref[0, pl.ds(pl.multiple_of(kt1 * TQ, TQ), TQ), :]
    kw_2 = kw_ref[0, pl.ds(pl.multiple_of(kt2 * TQ, TQ), TQ), :]
    win_bias2 = jnp.where(kr > ql + jnp.where(qi >= 2, 0, TQ), 0.0, NEG)
    win_bias1 = jnp.where(qi >= 1, 0.0, NEG)
    q6 = q6_ref[...]
    sd6, w2, w1, w0 = _dot(ks_d, q6), _dot(kw_2, q6), _dot(kw_1, q6), _dot(kw_0, q6)
    causal3 = tile3(causal_bias)
    win_bias2 = tile3(win_bias2)
    for kk in range(H_NSA_KV):
        s = sd6[:, cols(kk)] + tile3(sel_bias(kk, qi) + causal_bias)
        _online_update(m_ref, l_ref, acc_ref, kk, s, sel_pv(kk, qi))
        o_sel = acc_ref[kk] / jnp.maximum(l_ref[kk], TINY)
        s2 = w2[:, cols(kk)] + win_bias2
        s1 = w1[:, cols(kk)] + win_bias1
        s0 = w0[:, cols(kk)] + causal3
        m = jnp.maximum(jnp.maximum(jnp.max(s2, axis=0, keepdims=True), jnp.max(s1, axis=0, keepdims=True)),
                        jnp.max(s0, axis=0, keepdims=True))
        p2, p1, p0 = jnp.exp(s2 - m), jnp.exp(s1 - m), jnp.exp(s0 - m)
        den = (jnp.sum(p2, axis=0, keepdims=True) + jnp.sum(p1, axis=0, keepdims=True)
               + jnp.sum(p0, axis=0, keepdims=True))
        acc = (_dot(vwt_ref[0, 0, kt2, half(kk), :], p2.astype(BF16))
               + _dot(vwt_ref[0, 0, kt1, half(kk), :], p1.astype(BF16))
               + _dot(vwt_ref[0, 0, qi, half(kk), :], p0.astype(BF16)))
        o_win = acc / jnp.maximum(den, TINY)
        ot_ref[kk] = (ot_ref[kk] + gate3(kk, 1) * o_sel) + gate3(kk, 2) * o_win
    for j in range(NSA_GROUP):
        js = slice(j * TQ, (j + 1) * TQ)
        o_ref[0, :, j * LANES:(j + 1) * LANES] = jnp.concatenate(
            [ot_ref[0, :, js], ot_ref[1, :, js]], axis=0).T.astype(BF16)


def _nsa(pbt, pb, pf, kcmp, vcmp_t, ovl_t):
    B, _, n_t, _, _ = pbt.shape
    S = pb.shape[1]
    n_cmp = kcmp.shape[1]
    n_slc = S // NSA_SEL_BLOCK
    gw = NSA_GROUP * TQ
    return pl.pallas_call(
        functools.partial(_nsa_kernel, n_cmp=n_cmp, n_slc=n_slc),
        grid=(B, n_t),
        in_specs=[
            pl.BlockSpec((1, NSA_GROUP, 1, LANES, TQ), lambda b, i: (b, PBT_QB // NSA_GROUP, i, 0, 0)),
            pl.BlockSpec((1, n_cmp, LANES), lambda b, i: (b, 0, 0)),
            pl.BlockSpec((1, LANES, n_cmp), lambda b, i: (b, 0, 0)),
            pl.BlockSpec((1, S, LANES), lambda b, i: (b, 0, PB_KS)),
            pl.BlockSpec((1, 1, n_t, LANES, TQ), lambda b, i: (b, PBT_VS, 0, 0, 0)),
            pl.BlockSpec((1, S, LANES), lambda b, i: (b, 0, PB_KW)),
            pl.BlockSpec((1, 1, n_t, LANES, TQ), lambda b, i: (b, PBT_VW, 0, 0, 0)),
            pl.BlockSpec((1, TQ, LANES), lambda b, i: (b, i, 2)),
            _const_spec(ovl_t.shape),
        ],
        out_specs=pl.BlockSpec((1, TQ, Q_B), lambda b, i: (b, i, 0)),
        out_shape=jax.ShapeDtypeStruct((B, S, Q_B), BF16),
        scratch_shapes=[pltpu.VMEM((LANES, H_NSA * TQ), BF16),
                        pltpu.VMEM((H_NSA_KV, n_slc, TQ), F32), pltpu.VMEM((H_NSA_KV, n_slc, 1, TQ), F32),
                        pltpu.VMEM((H_NSA_KV, 1, gw), F32), pltpu.VMEM((H_NSA_KV, 1, gw), F32),
                        pltpu.VMEM((H_NSA_KV, HEAD_DIM, gw), F32),
                        pltpu.VMEM((H_NSA_KV, HEAD_DIM, gw), F32)],
        compiler_params=_cparams(("parallel", "arbitrary")),
        name="nsa_attn",
    )(pbt, kcmp, vcmp_t, pb, pbt, pb, pbt, pf, ovl_t)


def _dilated_kernel(q_ref, k_ref, v_ref, o_ref, qf_ref, kf_ref, vf_ref, os_ref, ls_ref, *, seq):
    blk = BAND_BLOCK
    n_tiles = seq // blk
    ri = lax.broadcasted_iota(jnp.int32, (blk, 2 * blk), 0)
    ci = lax.broadcasted_iota(jnp.int32, (blk, 2 * blk), 1)
    dist = ri + blk - ci
    lo = lax.broadcasted_iota(jnp.int32, (blk, LANES), 1) < HEAD_DIM
    for g, (window, dil) in enumerate(DIL_CFG):
        max_dist = window // dil
        tiles_per_class = n_tiles // dil
        band = (dist >= 0) & (dist <= max_dist)
        gs = slice(g * LANES, (g + 1) * LANES)
        qf_ref[...] = q_ref[0, :, gs].astype(F32)
        kf_ref[...] = k_ref[0, :, gs].astype(F32)
        vf_ref[...] = v_ref[0, :, gs].astype(F32)

        def rows(ref, start, dil=dil):
            if dil == 1:
                return ref[pl.ds(pl.multiple_of(start, blk), blk), :]
            return ref[pl.ds(start, blk, stride=dil), :]

        def tile(t, _, g=g, dil=dil, tiles_per_class=tiles_per_class, band=band, rows=rows):
            cls = t // tiles_per_class
            mt = t % tiles_per_class
            start = cls + dil * blk * mt
            prev = jnp.maximum(start - dil * blk, cls)
            q = rows(qf_ref, start)
            k2 = jnp.concatenate([rows(kf_ref, prev), rows(kf_ref, start)], axis=0).astype(BF16)
            v2 = jnp.concatenate([rows(vf_ref, prev), rows(vf_ref, start)], axis=0).astype(BF16)
            mask = band & (ci >= jnp.where(mt > 0, 0, blk))
            o_h, lse_h = [], []
            for hh in range(2):
                q_h = jnp.where(lo if hh == 0 else ~lo, q, 0.0).astype(BF16)
                s = lax.dot_general(q_h, k2, (((1,), (1,)), ((), ())), preferred_element_type=F32)
                s = jnp.where(mask, s, NEG)
                m = jnp.max(s, axis=-1, keepdims=True)
                p = jnp.where(mask, jnp.exp(s - m), 0.0)
                den = jnp.sum(p, axis=-1, keepdims=True)
                lse_h.append(m + jnp.log(jnp.maximum(den, TINY)))
                p = p / jnp.maximum(den, TINY)
                o_h.append(jnp.dot(p.astype(BF16), v2, preferred_element_type=F32))
            o_t = jnp.where(lo, o_h[0], o_h[1])
            l_t = jnp.where(lo, lse_h[0], lse_h[1])
            if dil == 1:
                idx = pl.ds(pl.multiple_of(start, blk), blk)
            else:
                idx = pl.ds(start, blk, stride=dil)
            os_ref[g, idx, :] = o_t
            ls_ref[g, idx, :] = l_t
            return 0

        lax.fori_loop(0, n_tiles, tile, 0)

    def combine(t, _):
        idx = pl.ds(pl.multiple_of(t * blk, blk), blk)
        lses = [ls_ref[g, idx, :] for g in range(len(DIL_CFG))]
        mx = jnp.maximum(jnp.maximum(lses[0], lses[1]), lses[2])
        es = [jnp.exp(l - mx) for l in lses]
        tot = es[0] + es[1] + es[2]
        for g in range(len(DIL_CFG)):
            o_ref[0, idx, g * LANES:(g + 1) * LANES] = (os_ref[g, idx, :] * (es[g] / tot)).astype(BF16)
        return 0

    lax.fori_loop(0, n_tiles, combine, 0)


def _dilated(pb):
    B, S, _ = pb.shape
    n_g = len(DIL_CFG)
    return pl.pallas_call(
        functools.partial(_dilated_kernel, seq=S),
        grid=(B,),
        in_specs=[pl.BlockSpec((1, S, QKV_C), lambda b: (b, 0, PB_QC // n_g)),
                  pl.BlockSpec((1, S, QKV_C), lambda b: (b, 0, PB_KC // n_g)),
                  pl.BlockSpec((1, S, QKV_C), lambda b: (b, 0, PB_VC // n_g))],
        out_specs=pl.BlockSpec((1, S, QKV_C), lambda b: (b, 0, 0)),
        out_shape=jax.ShapeDtypeStruct((B, S, QKV_C), BF16),
        scratch_shapes=[pltpu.VMEM((S, LANES), F32), pltpu.VMEM((S, LANES), F32), pltpu.VMEM((S, LANES), F32),
                        pltpu.VMEM((n_g, S, LANES), F32), pltpu.VMEM((n_g, S, LANES), F32)],
        compiler_params=_cparams(("parallel",)),
        name="dilated_attn",
    )(pb, pb, pb)


def _outproj_kernel(x_ref, oa_ref, ob_ref, oc_ref, wa_ref, wb_ref, wc_ref, y_ref):
    y = x_ref[...]
    y = y + jnp.dot(oa_ref[...], wa_ref[...], preferred_element_type=F32)
    y = y + jnp.dot(ob_ref[...], wb_ref[...], preferred_element_type=F32)
    y_ref[...] = y + jnp.dot(oc_ref[...], wc_ref[...], preferred_element_type=F32)


def _out_proj(x2, o_a, o_b, o_c, wa, wb, wc, tm=512):
    M, D = x2.shape
    row = lambda w: pl.BlockSpec((tm, w), lambda i: (i, 0))
    return pl.pallas_call(
        _outproj_kernel,
        grid=(M // tm,),
        in_specs=[row(D), row(QKV_A), row(Q_B), row(QKV_C),
                  _const_spec(wa.shape), _const_spec(wb.shape), _const_spec(wc.shape)],
        out_specs=row(D),
        out_shape=jax.ShapeDtypeStruct((M, D), F32),
        compiler_params=_cparams(("parallel",)),
        name="out_proj",
    )(x2, o_a, o_b, o_c, wa, wb, wc)


HALO = 16
FF_CHUNK = 256


def _ffn_kernel(x_ref, halo_ref, ln_ref, wg_ref, wu_ref, cw_ref, cb_ref, wd_ref, y_ref, h_ref, acc_ref,
                *, tm, tiles_per_seq):
    i = pl.program_id(0)

    def norm(v):
        return (v * lax.rsqrt(jnp.mean(v * v, axis=-1, keepdims=True) + EPS)) * ln_ref[...]

    x = x_ref[...]
    first = (i % tiles_per_seq) == 0
    h_ref[:HALO, :] = jnp.where(first, 0.0, norm(halo_ref[...])).astype(BF16)
    h_ref[HALO:, :] = norm(x).astype(BF16)
    for f0 in range(0, D_FF, FF_CHUNK):
        fs = slice(f0, f0 + FF_CHUNK)
        g = jnp.dot(h_ref[...], wg_ref[:, fs], preferred_element_type=F32)
        u = jnp.dot(h_ref[HALO:, :], wu_ref[:, fs], preferred_element_type=F32)
        gc = (cw_ref[0:1, fs] * pltpu.roll(g, 2, 0)[HALO:] + cw_ref[1:2, fs] * pltpu.roll(g, 1, 0)[HALO:]
              + cw_ref[2:3, fs] * g[HALO:] + cb_ref[:, fs])
        a = (jax.nn.silu(gc) * u).astype(BF16)
        d = jnp.dot(a, wd_ref[fs, :], preferred_element_type=F32)
        if f0 == 0:
            acc_ref[...] = d
        else:
            acc_ref[...] += d
    y_ref[...] = x + acc_ref[...]


def _conv_ffn(x2, ln, wg, wu, cw, cb, wd, seq, tm=512):
    M, D = x2.shape
    halo_blocks = tm // HALO
    return pl.pallas_call(
        functools.partial(_ffn_kernel, tm=tm, tiles_per_seq=seq // tm),
        grid=(M // tm,),
        in_specs=[
            pl.BlockSpec((tm, D), lambda i: (i, 0)),
            pl.BlockSpec((HALO, D), lambda i: (jnp.maximum(i * halo_blocks - 1, 0), 0)),
            _const_spec((1, D)),
            _const_spec(wg.shape), _const_spec(wu.shape), _const_spec(cw.shape), _const_spec(cb.shape),
            _const_spec(wd.shape),
        ],
        out_specs=pl.BlockSpec((tm, D), lambda i: (i, 0)),
        out_shape=jax.ShapeDtypeStruct((M, D), F32),
        scratch_shapes=[pltpu.VMEM((HALO + tm, D), BF16), pltpu.VMEM((tm, D), F32)],
        compiler_params=_cparams(("parallel",)),
        name="conv_ffn",
    )(x2, x2, ln, wg, wu, cw, cb, wd)


def _rope_tables(pos):
    half = ROPE_DIM // 2
    inv_freq = ROPE_THETA ** (-jnp.arange(half, dtype=F32) * 2.0 / ROPE_DIM)
    ang = pos.astype(F32)[:, None] * inv_freq
    c, s = jnp.cos(ang), jnp.sin(ang)
    n = pos.shape[0]
    rest = HEAD_DIM - ROPE_DIM
    cos_h = jnp.concatenate([c, c, jnp.ones((n, rest), F32)], axis=-1)
    sp_h = jnp.concatenate([jnp.zeros((n, half), F32), s, jnp.zeros((n, rest), F32)], axis=-1)
    sm_h = jnp.concatenate([-s, jnp.zeros((n, half + rest), F32)], axis=-1)
    two = lambda t: jnp.concatenate([t, t], axis=-1)
    return two(cos_h), two(sp_h), two(sm_h)


def _pair_diag(w):
    z = jnp.zeros_like(w)
    return jnp.concatenate([jnp.concatenate([w, z], axis=-1), jnp.concatenate([z, w], axis=-1)], axis=-2)


def _overlap_t(seq):
    n_cmp = seq // NSA_CMP_STRIDE
    n_slc = seq // NSA_SEL_BLOCK
    starts = np.arange(n_cmp) * NSA_CMP_STRIDE
    j = np.arange(n_slc)
    ovl = (starts[None, :] < (j[:, None] + 1) * NSA_SEL_BLOCK) & (starts[None, :] + NSA_CMP_LEN > j[:, None] * NSA_SEL_BLOCK)
    ovl[:, n_cmp - 1] = False
    return jnp.asarray(ovl, BF16)


def _mixers(x, seq_tabs, cmp_tabs, ovl_t, ln1, w_in, qn_a, kn_a, qn_b, kn_b, pe_k, pe_v, wk1, wk2, wv1, wv2,
            qn_c, kn_c):
    B, S, D = x.shape
    offs = np.cumsum((0,) + IN_SPLITS)
    qa, ka, va, qb, kcb, vcb, ksb, vsb, kwb, vwb, gb, qc, kc, vc = [
        w_in[:, offs[n]:offs[n + 1]] for n in range(len(IN_SPLITS))]
    order = jnp.asarray(NSA_ORDER)
    qb = qb.reshape(D, H_NSA, HEAD_DIM)[:, order].reshape(D, Q_B)
    gb = jnp.pad(gb, ((0, 0), (0, LANES - G_B)))
    w = jnp.concatenate([qa, qb, ka, ksb, kwb, qc, kc, va, vsb, vwb, vc, kcb, vcb, gb], axis=1).astype(BF16)
    rep = lambda g, n: jnp.tile(g, n)
    gain = jnp.concatenate([rep(qn_a * SCALE, H_MOBA), rep(qn_b * SCALE, H_NSA), rep(kn_a, H_MOBA),
                            rep(kn_b[1], H_NSA_KV), rep(kn_b[2], H_NSA_KV), rep(qn_c * SCALE, H_DIL),
                            rep(kn_c, H_DIL)])[None, :]
    pbt, pb, pf = _in_proj(x, ln1[None, :], w, gain, *seq_tabs)

    o_a = _moba(pbt, pb)
    two = lambda t: jnp.concatenate([t, t], axis=-1)
    w1 = lambda t: _pair_diag(t.reshape(NSA_CMP_LEN, HEAD_DIM, NSA_CMP_HIDDEN)).astype(BF16)
    kcmp, vcmp_t = _compress(pf, two(pe_k), two(pe_v), w1(wk1), _pair_diag(wk2).astype(BF16),
                             w1(wv1), _pair_diag(wv2).astype(BF16), two(kn_b[0])[None, :], *cmp_tabs)
    o_b = _nsa(pbt, pb, pf, kcmp, vcmp_t, ovl_t)
    o_c = _dilated(pb)
    return o_a, o_b, o_c


def _layer(x, seq_tabs, cmp_tabs, ovl_t, ln1, w_in, qn_a, kn_a, qn_b, kn_b, pe_k, pe_v, wk1, wk2, wv1, wv2,
           qn_c, kn_c, w_out, ln2, w_gate, w_up, conv_w, conv_b, w_down):
    B, S, D = x.shape
    o_a, o_b, o_c = _mixers(x, seq_tabs, cmp_tabs, ovl_t, ln1, w_in, qn_a, kn_a, qn_b, kn_b, pe_k, pe_v,
                            wk1, wk2, wv1, wv2, qn_c, kn_c)
    w_o = w_out.astype(BF16)
    wb = w_o[QKV_A:QKV_A + Q_B].reshape(H_NSA, HEAD_DIM, D)[jnp.asarray(NSA_ORDER)].reshape(Q_B, D)
    M = B * S
    x2 = _out_proj(x.reshape(M, D), o_a.reshape(M, QKV_A), o_b.reshape(M, Q_B), o_c.reshape(M, QKV_C),
                   w_o[:QKV_A], wb, w_o[QKV_A + Q_B:])
    x2 = _conv_ffn(x2, ln2[None, :], w_gate.astype(BF16), w_up.astype(BF16), conv_w, conv_b[None, :],
                   w_down.astype(BF16), S)
    return x2.reshape(B, S, D)


def kernel(x, ln1, w_in, qn_a, kn_a, qn_b, kn_b, cmp_pe_k, cmp_pe_v, cmp_k_w1, cmp_k_w2, cmp_v_w1, cmp_v_w2,
           qn_c, kn_c, w_out, ln2, w_gate, w_up, conv_w, conv_b, w_down):
    S = x.shape[1]
    seq_tabs = _rope_tables(jnp.arange(S))
    n_cmp = S // NSA_CMP_STRIDE
    cmp_tabs = _rope_tables(jnp.arange(n_cmp) * NSA_CMP_STRIDE + (NSA_CMP_LEN - 1))
    ovl_t = _overlap_t(S)
    for l in range(ln1.shape[0]):
        x = _layer(x, seq_tabs, cmp_tabs, ovl_t, ln1[l], w_in[l], qn_a[l], kn_a[l], qn_b[l], kn_b[l],
                   cmp_pe_k[l], cmp_pe_v[l], cmp_k_w1[l], cmp_k_w2[l], cmp_v_w1[l], cmp_v_w2[l], qn_c[l], kn_c[l],
                   w_out[l], ln2[l], w_gate[l], w_up[l], conv_w[l], conv_b[l], w_down[l])
    return x
```

```python
import functools

import numpy as np
import jax
import jax.numpy as jnp
from jax import lax
from jax.experimental import pallas as pl
from jax.experimental.pallas import tpu as pltpu

F32 = jnp.float32
BF16 = jnp.bfloat16

D_MODEL = 1024
HEAD_DIM = 64
N_HEADS = D_MODEL // HEAD_DIM
H_MOBA = N_HEADS // 4
H_NSA = (N_HEADS - H_MOBA) // 2
H_DIL = N_HEADS - H_MOBA - H_NSA
H_NSA_KV = 2
NSA_GROUP = H_NSA // H_NSA_KV
ROPE_DIM = HEAD_DIM // 4
ROPE_THETA = 500000.0
MOBA_BLOCK = 256
MOBA_TOPK = 3
NSA_CMP_LEN = 32
NSA_CMP_STRIDE = 16
NSA_CMP_HIDDEN = 128
NSA_SEL_BLOCK = 64
NSA_N_SEL = 6
NSA_WINDOW = 512
DIL_CFG = ((128, 1), (512, 4), (2048, 16))
BAND_BLOCK = 128
D_FF = 2816
EPS = 1e-6
NEG = -1e30
TINY = 1e-30
FORCE = 1e9
SCALE = HEAD_DIM ** -0.5
Q_SCALE = SCALE * float(np.log2(np.e))
QKV_A = H_MOBA * HEAD_DIM
Q_B = H_NSA * HEAD_DIM
KV_B = H_NSA_KV * HEAD_DIM
G_B = H_NSA * 3
QKV_C = H_DIL * HEAD_DIM
IN_SPLITS = (QKV_A, QKV_A, QKV_A, Q_B, KV_B, KV_B, KV_B, KV_B, KV_B, KV_B, G_B, QKV_C, QKV_C, QKV_C)

LANES = 128
TQ = 256
NSA_ORDER = (0, 3, 1, 4, 2, 5)
VMEM_LIMIT = 56 * 1024 * 1024

M_FLOOR = -1e29

T_GROUP = ("qa", "qb", "va", "vsb", "vwb")
N_T_NORM, N_T = 5, 9
R_GROUP = ("ka", "ksb", "kwb", "qc", "kc", "vc", "kcb", "vcb", "gb")
N_R_NORM, N_R_BF16, N_R = 10, 13, 16
N_PBT, N_PB, N_F32 = N_T, N_R_BF16, N_R - N_R_BF16
PBT_QA, PBT_VA, PBT_VS, PBT_VW, PBT_QB = 0, 2, 4, 5, 6
PB_QC, PB_KC, PB_KA, PB_KS, PB_VC, PB_KW = 0, 3, 6, 8, 9, 12
_PBT_DST = (0, 1, 6, 7, 8, 2, 3, 4, 5)
_PB_DST = (6, 7, 8, 12, 0, 1, 2, 3, 4, 5, 9, 10, 11)
CHUNK = 4


def _cparams(sem):
    return pltpu.CompilerParams(dimension_semantics=sem, vmem_limit_bytes=VMEM_LIMIT)


def _const_spec(shape):
    return pl.BlockSpec(shape, lambda *_: (0,) * len(shape), pipeline_mode=pl.Buffered(1))


def _norm_rope(y, gain, cos_t, sin_p, sin_m, lo):
    ss = y * y
    s_lo = jnp.sum(jnp.where(lo, ss, 0.0), axis=-1, keepdims=True)
    s_hi = jnp.sum(jnp.where(lo, 0.0, ss), axis=-1, keepdims=True)
    ms = jnp.where(lo, s_lo, s_hi) * (1.0 / HEAD_DIM)
    z = (y * lax.rsqrt(ms + EPS)) * gain
    half = ROPE_DIM // 2
    return z * cos_t + pltpu.roll(z, half, 1) * sin_p + pltpu.roll(z, LANES - half, 1) * sin_m


def _online_update(m_ref, l_ref, acc_ref, g, s, pv):
    m_old = m_ref[g]
    m_new = jnp.maximum(m_old, jnp.max(s, axis=0, keepdims=True))
    p = jnp.exp2(s - m_new)
    alpha = jnp.exp2(m_old - m_new)
    l_ref[g] = alpha * l_ref[g] + jnp.sum(p, axis=0, keepdims=True)
    acc_ref[g] = alpha * acc_ref[g] + pv(p.astype(BF16))
    m_ref[g] = m_new


def _online_init(m_ref, l_ref, acc_ref, g):
    m_ref[g] = jnp.full(m_ref.shape[1:], M_FLOOR, F32)
    l_ref[g] = jnp.zeros(l_ref.shape[1:], F32)
    acc_ref[g] = jnp.zeros(acc_ref.shape[1:], F32)


def _dot(a, b):
    return jnp.dot(a, b, preferred_element_type=F32)


def _half_rows(q_t, hh):
    z = jnp.zeros((HEAD_DIM, q_t.shape[1]), q_t.dtype)
    if hh == 0:
        return jnp.concatenate([q_t[:HEAD_DIM], z], axis=0)
    return jnp.concatenate([z, q_t[HEAD_DIM:]], axis=0)


def _split_bf16(a):
    hi = a.astype(BF16)
    return hi, (a - hi.astype(F32)).astype(BF16)


def _in_proj_plan(group):
    names = ("qa", "ka", "va", "qb", "kcb", "vcb", "ksb", "vsb", "kwb", "vwb", "gb", "qc", "kc", "vc")
    start = dict(zip(names, np.cumsum((0,) + IN_SPLITS[:-1]).tolist()))
    width = dict(zip(names, IN_SPLITS))
    plan = []
    for name in group:
        if name == "qb":
            heads = [start[name] + h * HEAD_DIM for h in NSA_ORDER]
            plan += [[(heads[2 * j], HEAD_DIM), (heads[2 * j + 1], HEAD_DIM)] for j in range(H_NSA // 2)]
        else:
            plan += [[(start[name] + c, min(LANES, width[name] - c))] for c in range(0, width[name], LANES)]
    return plan


def _wprep_kernel(w_ref, wt_ref, wr_ref, *, plan_t, plan_r, n_in):
    def take(a, width):
        b0 = a // LANES * LANES
        v0 = w_ref[0, :, b0:min(b0 + LANES, n_in)]
        if a + width <= b0 + LANES:
            return v0[:, a - b0:a - b0 + width]
        v1 = w_ref[0, :, b0 + LANES:min(b0 + 2 * LANES, n_in)]
        return jnp.concatenate([v0[:, a - b0:], v1[:, :a + width - b0 - LANES]], axis=1)

    def block(pieces):
        cols = [take(a, width) for a, width in pieces]
        filled = sum(width for _, width in pieces)
        if filled < LANES:
            cols.append(jnp.zeros((w_ref.shape[1], LANES - filled), F32))
        return cols[0] if len(cols) == 1 else jnp.concatenate(cols, axis=1)

    for d, pieces in enumerate(plan_t):
        wt_ref[d * LANES:(d + 1) * LANES, :] = block(pieces).T.astype(BF16)
    for d, pieces in enumerate(plan_r):
        wr_ref[:, d * LANES:(d + 1) * LANES] = block(pieces).astype(BF16)


def _prep_w_in(w_in, layer, tr=256):
    _, D, n_in = w_in.shape
    plan_t, plan_r = _in_proj_plan(T_GROUP), _in_proj_plan(R_GROUP)
    return pl.pallas_call(
        functools.partial(_wprep_kernel, plan_t=plan_t, plan_r=plan_r, n_in=n_in),
        grid=(D // tr,),
        in_specs=[pl.BlockSpec((1, tr, n_in), lambda i: (layer, i, 0))],
        out_specs=[pl.BlockSpec((N_T * LANES, tr), lambda i: (0, i)),
                   pl.BlockSpec((tr, N_R * LANES), lambda i: (i, 0))],
        out_shape=[jax.ShapeDtypeStruct((N_T * LANES, D), BF16), jax.ShapeDtypeStruct((D, N_R * LANES), BF16)],
        compiler_params=_cparams(("parallel",)),
        name="w_in_layout",
    )(w_in)


def _norm_rope_t(y, gain, cos8, sin8):
    half = ROPE_DIM // 2
    parts = []
    for hh in range(2):
        yh = y[hh * HEAD_DIM:(hh + 1) * HEAD_DIM]
        ms = jnp.sum(yh * yh, axis=0, keepdims=True) * (1.0 / HEAD_DIM)
        z = (yh * lax.rsqrt(ms + EPS)) * gain[hh * HEAD_DIM:(hh + 1) * HEAD_DIM]
        x1, x2 = z[:half], z[half:ROPE_DIM]
        parts += [x1 * cos8 - x2 * sin8, x2 * cos8 + x1 * sin8, z[ROPE_DIM:]]
    return jnp.concatenate(parts, axis=0)


def _inproj_kernel(x_ref, ln_ref, wt_ref, wr_ref, gt_ref, gr_ref, cos_ref, sp_ref, sm_ref, ct_ref, st_ref,
                   pbt_ref, pb_ref, pf_ref, *, tm):
    x = x_ref[0]
    h = x * lax.rsqrt(jnp.mean(x * x, axis=-1, keepdims=True) + EPS)
    h = (h * ln_ref[...]).astype(BF16)
    lo = lax.broadcasted_iota(jnp.int32, (tm, LANES), 1) < HEAD_DIM
    cos_t, sin_p, sin_m = cos_ref[...], sp_ref[...], sm_ref[...]
    for c0 in range(0, N_R, CHUNK):
        y4 = _dot(h, wr_ref[:, c0 * LANES:(c0 + CHUNK) * LANES])
        for c in range(c0, c0 + CHUNK):
            y = y4[:, (c - c0) * LANES:(c - c0 + 1) * LANES]
            if c < N_R_NORM:
                y = _norm_rope(y, gr_ref[:, c * LANES:(c + 1) * LANES], cos_t, sin_p, sin_m, lo)
            if c < N_R_BF16:
                d = _PB_DST[c]
                pb_ref[0, :, d * LANES:(d + 1) * LANES] = y.astype(BF16)
            else:
                d = c - N_R_BF16
                pf_ref[0, :, d * LANES:(d + 1) * LANES] = y
    cos8, sin8 = ct_ref[...], st_ref[...]
    for t0, t1 in ((0, N_T_NORM), (N_T_NORM, N_T)):
        y_t = lax.dot_general(wt_ref[t0 * LANES:t1 * LANES, :], h, (((1,), (1,)), ((), ())),
                              preferred_element_type=F32)
        for t in range(t0, t1):
            y = y_t[(t - t0) * LANES:(t - t0 + 1) * LANES]
            if t < N_T_NORM:
                gain = jnp.concatenate([gt_ref[t * LANES:(t + 1) * LANES, :]] * (tm // LANES), axis=1)
                y = _norm_rope_t(y, gain, cos8, sin8)
            y = y.astype(BF16)
            for u in range(tm // TQ):
                pbt_ref[0, _PBT_DST[t], u] = y[:, u * TQ:(u + 1) * TQ]


def _in_proj(x, ln, w_t, w_r, gain_t, gain_r, cos_t, sin_p, sin_m, cos8, sin8, tm=512):
    B, S, D = x.shape
    nt = S // tm
    half = ROPE_DIM // 2
    return pl.pallas_call(
        functools.partial(_inproj_kernel, tm=tm),
        grid=(B * nt,),
        in_specs=[
            pl.BlockSpec((1, tm, D), lambda i: (i // nt, i % nt, 0)),
            _const_spec((1, D)),
            _const_spec(w_t.shape), _const_spec(w_r.shape), _const_spec(gain_t.shape), _const_spec(gain_r.shape),
            pl.BlockSpec((tm, LANES), lambda i: (i % nt, 0)),
            pl.BlockSpec((tm, LANES), lambda i: (i % nt, 0)),
            pl.BlockSpec((tm, LANES), lambda i: (i % nt, 0)),
            pl.BlockSpec((half, tm), lambda i: (0, i % nt)),
            pl.BlockSpec((half, tm), lambda i: (0, i % nt)),
        ],
        out_specs=[
            pl.BlockSpec((1, N_PBT, tm // TQ, LANES, TQ), lambda i: (i // nt, 0, i % nt, 0, 0)),
            pl.BlockSpec((1, tm, N_PB * LANES), lambda i: (i // nt, i % nt, 0)),
            pl.BlockSpec((1, tm, N_F32 * LANES), lambda i: (i // nt, i % nt, 0)),
        ],
        out_shape=[
            jax.ShapeDtypeStruct((B, N_PBT, S // TQ, LANES, TQ), BF16),
            jax.ShapeDtypeStruct((B, S, N_PB * LANES), BF16),
            jax.ShapeDtypeStruct((B, S, N_F32 * LANES), F32),
        ],
        compiler_params=_cparams(("parallel",)),
        name="in_proj",
    )(x, ln, w_t, w_r, gain_t, gain_r, cos_t, sin_p, sin_m, cos8, sin8)


def _moba_kernel(qt_ref, k_ref, vt_ref, o_ref, q2_ref, bias_ref, m_ref, l_ref, acc_ref, *, n_blk):
    qi = pl.program_id(1)
    n_pairs = H_MOBA // 2
    nidx = lax.broadcasted_iota(jnp.int32, (16, 2 * TQ), 0)
    past = nidx < qi
    kr = lax.broadcasted_iota(jnp.int32, (TQ, TQ), 0)
    ql = lax.broadcasted_iota(jnp.int32, (TQ, TQ), 1)
    causal_bias = jnp.where(kr <= ql, 0.0, NEG)
    for p in range(n_pairs):
        ps = slice(p * LANES, (p + 1) * LANES)
        rows = [jnp.sum(k_ref[0, n * TQ:(n + 1) * TQ, ps].astype(F32), axis=0, keepdims=True)
                for n in range(n_blk)]
        kmean = jnp.concatenate(rows + [jnp.zeros((16 - n_blk, LANES), F32)], axis=0) * (1.0 / MOBA_BLOCK)
        km_hi, km_lo = _split_bf16(kmean)
        q2 = jnp.concatenate([_half_rows(qt_ref[0, p, 0], hh) for hh in range(2)], axis=1)
        q2_ref[p] = q2
        gate = _dot(km_hi, q2) + _dot(km_lo, q2)
        for n in range(n_blk):
            row = gate[n:n + 1, :]
            ge = jnp.where(gate >= row, 1.0, 0.0)
            gt = jnp.where(gate > row, 1.0, 0.0)
            beats = jnp.where(past, jnp.where(nidx < n, ge, gt), 0.0)
            cnt = jnp.sum(beats, axis=0, keepdims=True)
            bias_ref[p, n] = jnp.where(cnt < MOBA_TOPK, 0.0, NEG)
        _online_init(m_ref, l_ref, acc_ref, p)

    def scores(off, p):
        return _dot(k_ref[0, pl.ds(off, TQ), p * LANES:(p + 1) * LANES], q2_ref[p])

    def pv(p, kt):
        def fn(pr):
            return jnp.concatenate([_dot(vt_ref[0, p, kt, :HEAD_DIM, :], pr[:, :TQ]),
                                    _dot(vt_ref[0, p, kt, HEAD_DIM:, :], pr[:, TQ:])], axis=1)
        return fn

    def past_tile(kt, _):
        off = pl.multiple_of(kt * TQ, TQ)
        s = [scores(off, p) + bias_ref[p, kt] for p in range(n_pairs)]
        for p in range(n_pairs):
            _online_update(m_ref, l_ref, acc_ref, p, s[p], pv(p, kt))
        return 0

    lax.fori_loop(0, qi, past_tile, 0)
    off = pl.multiple_of(qi * TQ, TQ)
    causal2 = jnp.concatenate([causal_bias, causal_bias], axis=1)
    s = [scores(off, p) + causal2 for p in range(n_pairs)]
    for p in range(n_pairs):
        _online_update(m_ref, l_ref, acc_ref, p, s[p], pv(p, qi))
        o2 = acc_ref[p] / jnp.maximum(l_ref[p], TINY)
        o_ref[0, :, p * LANES:(p + 1) * LANES] = jnp.concatenate([o2[:, :TQ], o2[:, TQ:]], axis=0).T.astype(BF16)


def _moba(pbt, pb):
    B, _, n_t, _, _ = pbt.shape
    S = pb.shape[1]
    n_pairs = H_MOBA // 2
    return pl.pallas_call(
        functools.partial(_moba_kernel, n_blk=S // MOBA_BLOCK),
        grid=(B, n_t),
        in_specs=[
            pl.BlockSpec((1, n_pairs, 1, LANES, TQ), lambda b, i: (b, PBT_QA // n_pairs, i, 0, 0)),
            pl.BlockSpec((1, S, QKV_A), lambda b, i: (b, 0, PB_KA // n_pairs)),
            pl.BlockSpec((1, n_pairs, n_t, LANES, TQ), lambda b, i: (b, PBT_VA // n_pairs, 0, 0, 0)),
        ],
        out_specs=pl.BlockSpec((1, TQ, QKV_A), lambda b, i: (b, i, 0)),
        out_shape=jax.ShapeDtypeStruct((B, S, QKV_A), BF16),
        scratch_shapes=[pltpu.VMEM((n_pairs, LANES, 2 * TQ), BF16),
                        pltpu.VMEM((n_pairs, S // MOBA_BLOCK, 1, 2 * TQ), F32),
                        pltpu.VMEM((n_pairs, 1, 2 * TQ), F32), pltpu.VMEM((n_pairs, 1, 2 * TQ), F32),
                        pltpu.VMEM((n_pairs, HEAD_DIM, 2 * TQ), F32)],
        compiler_params=_cparams(("parallel", "arbitrary")),
        name="moba",
    )(pbt, pb, pbt)


def _compress_kernel(kc_ref, vc_ref, pek_ref, pev_ref, w1k_ref, w2k_ref, w1v_ref, w2v_ref, gain_ref,
                     cos_ref, sp_ref, sm_ref, ko_ref, vo_ref, *, n_cmp):
    half_len = NSA_CMP_LEN // 2

    def mlp(t_ref, pe_ref, w1_ref, w2_ref):
        first = jnp.zeros((n_cmp, 2 * NSA_CMP_HIDDEN), F32)
        second = jnp.zeros((n_cmp, 2 * NSA_CMP_HIDDEN), F32)
        for l in range(half_len):
            a = t_ref[0, pl.ds(l, n_cmp, stride=NSA_CMP_STRIDE), :]
            first = first + jnp.dot((a + pe_ref[l:l + 1, :]).astype(BF16), w1_ref[l],
                                    preferred_element_type=F32)
            second = second + jnp.dot((a + pe_ref[half_len + l:half_len + l + 1, :]).astype(BF16),
                                      w1_ref[half_len + l], preferred_element_type=F32)
        hid = jax.nn.gelu(first + pltpu.roll(second, n_cmp - 1, 0))
        return jnp.dot(hid.astype(BF16), w2_ref[...], preferred_element_type=F32)

    valid = lax.broadcasted_iota(jnp.int32, (n_cmp, LANES), 0) < n_cmp - 1
    lo = lax.broadcasted_iota(jnp.int32, (n_cmp, LANES), 1) < HEAD_DIM
    kc = mlp(kc_ref, pek_ref, w1k_ref, w2k_ref)
    kc = _norm_rope(kc, gain_ref[...], cos_ref[...], sp_ref[...], sm_ref[...], lo)
    ko_ref[0] = jnp.where(valid, kc, 0.0).astype(BF16)
    vc = mlp(vc_ref, pev_ref, w1v_ref, w2v_ref)
    vo_ref[0] = jnp.where(valid, vc, 0.0).T.astype(BF16)


def _compress(pf, pe_k, pe_v, w1k, w2k, w1v, w2v, gain, cos_c, sp_c, sm_c):
    B, S, _ = pf.shape
    n_cmp = S // NSA_CMP_STRIDE
    cs = _const_spec
    return pl.pallas_call(
        functools.partial(_compress_kernel, n_cmp=n_cmp),
        grid=(B,),
        in_specs=[
            pl.BlockSpec((1, S, LANES), lambda b: (b, 0, 0)),
            pl.BlockSpec((1, S, LANES), lambda b: (b, 0, 1)),
            cs(pe_k.shape), cs(pe_v.shape), cs(w1k.shape), cs(w2k.shape), cs(w1v.shape), cs(w2v.shape),
            cs(gain.shape), cs(cos_c.shape), cs(sp_c.shape), cs(sm_c.shape),
        ],
        out_specs=[pl.BlockSpec((1, n_cmp, LANES), lambda b: (b, 0, 0)),
                   pl.BlockSpec((1, LANES, n_cmp), lambda b: (b, 0, 0))],
        out_shape=[jax.ShapeDtypeStruct((B, n_cmp, LANES), BF16),
                   jax.ShapeDtypeStruct((B, LANES, n_cmp), BF16)],
        compiler_params=_cparams(("parallel",)),
        name="nsa_compress",
    )(pf, pf, pe_k, pe_v, w1k, w2k, w1v, w2v, gain, cos_c, sp_c, sm_c)


def _nsa_kernel(qt_ref, kc_ref, vct_ref, ks_ref, vst_ref, kw_ref, vwt_ref, g_ref, ovl_ref, o_ref,
                q6_ref, impf_ref, improw_ref, m_ref, l_ref, acc_ref, ot_ref, *, n_cmp, n_slc):
    qi = pl.program_id(1)
    blk_per_tile = TQ // NSA_SEL_BLOCK
    tpos = qi * TQ + lax.broadcasted_iota(jnp.int32, (1, TQ), 1)
    sig = jax.nn.sigmoid(g_ref[0].T)
    kr = lax.broadcasted_iota(jnp.int32, (TQ, TQ), 0)
    ql = lax.broadcasted_iota(jnp.int32, (TQ, TQ), 1)
    causal_bias = jnp.where(kr <= ql, 0.0, NEG)
    nidx = lax.broadcasted_iota(jnp.int32, (n_cmp, TQ), 0)
    cmask = (nidx * NSA_CMP_STRIDE + (NSA_CMP_LEN - 1) <= tpos) & (nidx < n_cmp - 1)
    cmp_bias = jnp.where(cmask, 0.0, NEG)
    jj = lax.broadcasted_iota(jnp.int32, (n_slc, TQ), 0)
    cur = lax.shift_right_arithmetic(tpos, NSA_SEL_BLOCK.bit_length() - 1)
    forced = (jj == 0) | (jj == cur) | (jj == cur - 1)
    gw = NSA_GROUP * TQ

    def half(kk):
        return slice(kk * HEAD_DIM, (kk + 1) * HEAD_DIM)

    def cols(kk):
        return slice(kk * gw, (kk + 1) * gw)

    def tile3(a):
        return jnp.concatenate([a] * NSA_GROUP, axis=1)

    def gate3(kk, branch):
        return jnp.concatenate([sig[3 * (NSA_GROUP * kk + j) + branch:3 * (NSA_GROUP * kk + j) + branch + 1, :]
                                for j in range(NSA_GROUP)], axis=1)

    q6 = jnp.concatenate([_half_rows(qt_ref[0, j, 0], kk) for kk in range(H_NSA_KV) for j in range(NSA_GROUP)],
                         axis=1)
    q6_ref[...] = q6

    s_cmp = _dot(kc_ref[0], q6)
    cmp_bias3 = tile3(cmp_bias)
    for kk in range(H_NSA_KV):
        s = s_cmp[:, cols(kk)] + cmp_bias3
        m = jnp.maximum(jnp.max(s, axis=0, keepdims=True), M_FLOOR)
        p = jnp.exp2(s - m)
        p = p / jnp.maximum(jnp.sum(p, axis=0, keepdims=True), TINY)
        ot_ref[kk] = gate3(kk, 0) * _dot(vct_ref[0, half(kk), :], p.astype(BF16))
        p_sum = (p[:, :TQ] + p[:, TQ:2 * TQ]) + p[:, 2 * TQ:]
        _online_init(m_ref, l_ref, acc_ref, kk)
        p_hi, p_lo = _split_bf16(p_sum)
        imp = _dot(ovl_ref[...], p_hi) + _dot(ovl_ref[...], p_lo)
        imp = jnp.where(jj > cur, NEG, jnp.where(forced, FORCE, imp))
        impf_ref[kk] = imp
        for j in range(n_slc):
            improw_ref[kk, j] = imp[j:j + 1, :]

    def sel_bias(kk, kt):
        imp = impf_ref[kk]
        parts = []
        for c in range(blk_per_tile):
            j = kt * blk_per_tile + c
            row = improw_ref[kk, j]
            ge = jnp.where(imp >= row, 1.0, 0.0)
            gt = jnp.where(imp > row, 1.0, 0.0)
            cnt = jnp.sum(jnp.where(jj < j, ge, gt), axis=0, keepdims=True)
            parts.append(jnp.broadcast_to(jnp.where(cnt < NSA_N_SEL, 0.0, NEG), (NSA_SEL_BLOCK, TQ)))
        return jnp.concatenate(parts, axis=0)

    def sel_pv(kk, kt):
        return lambda pr: _dot(vst_ref[0, 0, kt, half(kk), :], pr)

    def sel_tile(kt, _):
        off = pl.multiple_of(kt * TQ, TQ)
        s6 = _dot(ks_ref[0, pl.ds(off, TQ), :], q6_ref[...])
        for kk in range(H_NSA_KV):
            s = s6[:, cols(kk)] + tile3(sel_bias(kk, kt))
            _online_update(m_ref, l_ref, acc_ref, kk, s, sel_pv(kk, kt))
        return 0

    lax.fori_loop(0, qi, sel_tile, 0)

    off = pl.multiple_of(qi * TQ, TQ)
    kt1 = jnp.maximum(qi - 1, 0)
    kt2 = jnp.maximum(qi - 2, 0)
    ks_d = ks_ref[0, pl.ds(off, TQ), :]
    kw_0 = kw_ref[0, pl.ds(off, TQ), :]
    kw_1 = kw_ref[0, pl.ds(pl.multiple_of(kt1 * TQ, TQ), TQ), :]
    kw_2 = kw_ref[0, pl.ds(pl.multiple_of(kt2 * TQ, TQ), TQ), :]
    win_bias2 = jnp.where(kr > ql + jnp.where(qi >= 2, 0, TQ), 0.0, NEG)
    win_bias1 = jnp.where(qi >= 1, 0.0, NEG)
    q6 = q6_ref[...]
    sd6, w2, w1, w0 = _dot(ks_d, q6), _dot(kw_2, q6), _dot(kw_1, q6), _dot(kw_0, q6)
    causal3 = tile3(causal_bias)
    win_bias2 = tile3(win_bias2)
    for kk in range(H_NSA_KV):
        s = sd6[:, cols(kk)] + tile3(sel_bias(kk, qi) + causal_bias)
        _online_update(m_ref, l_ref, acc_ref, kk, s, sel_pv(kk, qi))
        o_sel = acc_ref[kk] / jnp.maximum(l_ref[kk], TINY)
        s2 = w2[:, cols(kk)] + win_bias2
        s1 = w1[:, cols(kk)] + win_bias1
        s0 = w0[:, cols(kk)] + causal3
        m = jnp.maximum(jnp.maximum(jnp.max(s2, axis=0, keepdims=True), jnp.max(s1, axis=0, keepdims=True)),
                        jnp.max(s0, axis=0, keepdims=True))
        p2, p1, p0 = jnp.exp2(s2 - m), jnp.exp2(s1 - m), jnp.exp2(s0 - m)
        den = (jnp.sum(p2, axis=0, keepdims=True) + jnp.sum(p1, axis=0, keepdims=True)
               + jnp.sum(p0, axis=0, keepdims=True))
        acc = (_dot(vwt_ref[0, 0, kt2, half(kk), :], p2.astype(BF16))
               + _dot(vwt_ref[0, 0, kt1, half(kk), :], p1.astype(BF16))
               + _dot(vwt_ref[0, 0, qi, half(kk), :], p0.astype(BF16)))
        o_win = acc / jnp.maximum(den, TINY)
        ot_ref[kk] = (ot_ref[kk] + gate3(kk, 1) * o_sel) + gate3(kk, 2) * o_win
    for j in range(NSA_GROUP):
        js = slice(j * TQ, (j + 1) * TQ)
        o_ref[0, :, j * LANES:(j + 1) * LANES] = jnp.concatenate(
            [ot_ref[0, :, js], ot_ref[1, :, js]], axis=0).T.astype(BF16)


def _nsa(pbt, pb, pf, kcmp, vcmp_t, ovl_t):
    B, _, n_t, _, _ = pbt.shape
    S = pb.shape[1]
    n_cmp = kcmp.shape[1]
    n_slc = S // NSA_SEL_BLOCK
    gw = NSA_GROUP * TQ
    return pl.pallas_call(
        functools.partial(_nsa_kernel, n_cmp=n_cmp, n_slc=n_slc),
        grid=(B, n_t),
        in_specs=[
            pl.BlockSpec((1, NSA_GROUP, 1, LANES, TQ), lambda b, i: (b, PBT_QB // NSA_GROUP, i, 0, 0)),
            pl.BlockSpec((1, n_cmp, LANES), lambda b, i: (b, 0, 0)),
            pl.BlockSpec((1, LANES, n_cmp), lambda b, i: (b, 0, 0)),
            pl.BlockSpec((1, S, LANES), lambda b, i: (b, 0, PB_KS)),
            pl.BlockSpec((1, 1, n_t, LANES, TQ), lambda b, i: (b, PBT_VS, 0, 0, 0)),
            pl.BlockSpec((1, S, LANES), lambda b, i: (b, 0, PB_KW)),
            pl.BlockSpec((1, 1, n_t, LANES, TQ), lambda b, i: (b, PBT_VW, 0, 0, 0)),
            pl.BlockSpec((1, TQ, LANES), lambda b, i: (b, i, 2)),
            _const_spec(ovl_t.shape),
        ],
        out_specs=pl.BlockSpec((1, TQ, Q_B), lambda b, i: (b, i, 0)),
        out_shape=jax.ShapeDtypeStruct((B, S, Q_B), BF16),
        scratch_shapes=[pltpu.VMEM((LANES, H_NSA * TQ), BF16),
                        pltpu.VMEM((H_NSA_KV, n_slc, TQ), F32), pltpu.VMEM((H_NSA_KV, n_slc, 1, TQ), F32),
                        pltpu.VMEM((H_NSA_KV, 1, gw), F32), pltpu.VMEM((H_NSA_KV, 1, gw), F32),
                        pltpu.VMEM((H_NSA_KV, HEAD_DIM, gw), F32),
                        pltpu.VMEM((H_NSA_KV, HEAD_DIM, gw), F32)],
        compiler_params=_cparams(("parallel", "arbitrary")),
        name="nsa_attn",
    )(pbt, kcmp, vcmp_t, pb, pbt, pb, pbt, pf, ovl_t)


DIL_TILES = 4

def _dilated_kernel(q_ref, k_ref, v_ref, o_ref, qf_ref, kf_ref, vf_ref, os_ref, ls_ref, *, seq):
    blk = BAND_BLOCK
    n_tiles = seq // blk
    lo = lax.broadcasted_iota(jnp.int32, (blk, LANES), 1) < HEAD_DIM
    for g, (window, dil) in enumerate(DIL_CFG):
        max_dist = window // dil
        tiles_per_class = n_tiles // dil
        use_prev = tiles_per_class > 1
        n_keys = 2 * blk if use_prev else blk
        ri = lax.broadcasted_iota(jnp.int32, (2 * blk, n_keys), 0) & (blk - 1)
        ci = lax.broadcasted_iota(jnp.int32, (2 * blk, n_keys), 1)
        dist = ri + (n_keys - blk) - ci
        band = (dist >= 0) & (dist <= max_dist)
        band_bias = jnp.where(band, 0.0, NEG)
        qf_ref[g] = q_ref[0, :, g * LANES:(g + 1) * LANES].astype(F32)
        kf_ref[g] = k_ref[0, :, g * LANES:(g + 1) * LANES].astype(F32)
        vf_ref[g] = v_ref[0, :, g * LANES:(g + 1) * LANES].astype(F32)

        def index(start, dil=dil):
            if dil == 1:
                return pl.ds(pl.multiple_of(start, blk), blk)
            return pl.ds(start, blk, stride=dil)

        def tiles(i, _, g=g, dil=dil, tiles_per_class=tiles_per_class, use_prev=use_prev, band=band,
                  band_bias=band_bias, ci=ci, index=index):
            scored = []
            for u in range(DIL_TILES):
                t = i * DIL_TILES + u
                cls = t // tiles_per_class
                mt = t % tiles_per_class
                start = cls + dil * blk * mt
                q = qf_ref[g, index(start), :]
                q2 = jnp.concatenate([jnp.where(lo, q, 0.0), jnp.where(lo, 0.0, q)], axis=0).astype(BF16)
                if use_prev:
                    prev = jnp.maximum(start - dil * blk, cls)
                    k2 = jnp.concatenate([kf_ref[g, index(prev), :], kf_ref[g, index(start), :]], axis=0)
                    v2 = jnp.concatenate([vf_ref[g, index(prev), :], vf_ref[g, index(start), :]], axis=0)
                    bias = jnp.where(band & (ci >= jnp.where(mt > 0, 0, blk)), 0.0, NEG)
                else:
                    k2, v2, bias = kf_ref[g, index(start), :], vf_ref[g, index(start), :], band_bias
                s = lax.dot_general(q2, k2.astype(BF16), (((1,), (1,)), ((), ())), preferred_element_type=F32)
                scored.append((start, s + bias, v2.astype(BF16)))
            probs = []
            for start, s, v2 in scored:
                m = jnp.max(s, axis=-1, keepdims=True)
                p = jnp.exp2(s - m)
                den = jnp.sum(p, axis=-1, keepdims=True)
                lse = m + jnp.log2(jnp.maximum(den, TINY))
                probs.append((start, lse, (p / jnp.maximum(den, TINY)).astype(BF16), v2))
            for start, lse, p, v2 in probs:
                o = _dot(p, v2)
                os_ref[g, index(start), :] = jnp.where(lo, o[:blk], o[blk:])
                ls_ref[g, index(start), :] = jnp.where(lo, lse[:blk], lse[blk:])
            return 0

        lax.fori_loop(0, n_tiles // DIL_TILES, tiles, 0)

    rows_per_trip = DIL_TILES * blk

    def combine(t, _):
        idx = pl.ds(pl.multiple_of(t * rows_per_trip, rows_per_trip), rows_per_trip)
        lses = [ls_ref[g, idx, :] for g in range(len(DIL_CFG))]
        mx = jnp.maximum(jnp.maximum(lses[0], lses[1]), lses[2])
        es = [jnp.exp2(l - mx) for l in lses]
        tot = es[0] + es[1] + es[2]
        for g in range(len(DIL_CFG)):
            o_ref[0, idx, g * LANES:(g + 1) * LANES] = (os_ref[g, idx, :] * (es[g] / tot)).astype(BF16)
        return 0

    lax.fori_loop(0, seq // rows_per_trip, combine, 0)


def _dilated(pb):
    B, S, _ = pb.shape
    n_g = len(DIL_CFG)
    return pl.pallas_call(
        functools.partial(_dilated_kernel, seq=S),
        grid=(B,),
        in_specs=[pl.BlockSpec((1, S, QKV_C), lambda b: (b, 0, PB_QC // n_g)),
                  pl.BlockSpec((1, S, QKV_C), lambda b: (b, 0, PB_KC // n_g)),
                  pl.BlockSpec((1, S, QKV_C), lambda b: (b, 0, PB_VC // n_g))],
        out_specs=pl.BlockSpec((1, S, QKV_C), lambda b: (b, 0, 0)),
        out_shape=jax.ShapeDtypeStruct((B, S, QKV_C), BF16),
        scratch_shapes=[pltpu.VMEM((n_g, S, LANES), F32)] * 5,
        compiler_params=_cparams(("parallel",)),
        name="dilated_attn",
    )(pb, pb, pb)


HALO = 16
FF_CHUNK = 256


def _proj_ffn_kernel(x_ref, xh_ref, oa_ref, oah_ref, ob_ref, obh_ref, oc_ref, och_ref, wo_ref, ln_ref,
                     wg_ref, wu_ref, cw_ref, cb_ref, wd_ref, y_ref, o_ref, h_ref, x1_ref, acc_ref,
                     *, tiles_per_seq):
    i = pl.program_id(0)

    def norm(v):
        return (v * lax.rsqrt(jnp.mean(v * v, axis=-1, keepdims=True) + EPS)) * ln_ref[...]

    col = 0
    for main, halo in ((oa_ref, oah_ref), (ob_ref, obh_ref), (oc_ref, och_ref)):
        width = main.shape[1]
        o_ref[:HALO, col:col + width] = halo[...]
        o_ref[HALO:, col:col + width] = main[...]
        col += width
    attn = _dot(o_ref[...], wo_ref[...])
    x1 = x_ref[...] + attn[HALO:]
    x1_ref[...] = x1
    first = (i % tiles_per_seq) == 0
    h_ref[:HALO, :] = jnp.where(first, 0.0, norm(xh_ref[...] + attn[:HALO])).astype(BF16)
    h_ref[HALO:, :] = norm(x1).astype(BF16)

    def gate_up(f0):
        fs = slice(f0, f0 + FF_CHUNK)
        return _dot(h_ref[...], wg_ref[:, fs]), _dot(h_ref[HALO:, :], wu_ref[:, fs])

    nxt = gate_up(0)
    for f0 in range(0, D_FF, FF_CHUNK):
        fs = slice(f0, f0 + FF_CHUNK)
        g, u = nxt
        if f0 + FF_CHUNK < D_FF:
            nxt = gate_up(f0 + FF_CHUNK)
        gc = (cw_ref[0:1, fs] * pltpu.roll(g, 2, 0)[HALO:] + cw_ref[1:2, fs] * pltpu.roll(g, 1, 0)[HALO:]
              + cw_ref[2:3, fs] * g[HALO:] + cb_ref[:, fs])
        a = (jax.nn.silu(gc) * u).astype(BF16)
        d = _dot(a, wd_ref[fs, :])
        if f0 == 0:
            acc_ref[...] = d
        else:
            acc_ref[...] += d
    y_ref[...] = x1_ref[...] + acc_ref[...]


def _proj_ffn(x2, o_a, o_b, o_c, wo, ln, wg, wu, cw, cb, wd, seq, tm=512):
    M, D = x2.shape
    halo_blocks = tm // HALO
    main = lambda w: pl.BlockSpec((tm, w), lambda i: (i, 0))
    halo = lambda w: pl.BlockSpec((HALO, w), lambda i: (jnp.maximum(i * halo_blocks - 1, 0), 0))
    return pl.pallas_call(
        functools.partial(_proj_ffn_kernel, tiles_per_seq=seq // tm),
        grid=(M // tm,),
        in_specs=[
            main(D), halo(D), main(QKV_A), halo(QKV_A), main(Q_B), halo(Q_B), main(QKV_C), halo(QKV_C),
            _const_spec(wo.shape), _const_spec((1, D)),
            _const_spec(wg.shape), _const_spec(wu.shape), _const_spec(cw.shape), _const_spec(cb.shape),
            _const_spec(wd.shape),
        ],
        out_specs=main(D),
        out_shape=jax.ShapeDtypeStruct((M, D), F32),
        scratch_shapes=[pltpu.VMEM((HALO + tm, D), BF16), pltpu.VMEM((HALO + tm, D), BF16),
                        pltpu.VMEM((tm, D), F32), pltpu.VMEM((tm, D), F32)],
        compiler_params=_cparams(("parallel",)),
        name="proj_ffn",
    )(x2, x2, o_a, o_a, o_b, o_b, o_c, o_c, wo, ln, wg, wu, cw, cb, wd)


def _rope_tables(pos):
    half = ROPE_DIM // 2
    inv_freq = ROPE_THETA ** (-jnp.arange(half, dtype=F32) * 2.0 / ROPE_DIM)
    ang = pos.astype(F32)[:, None] * inv_freq
    c, s = jnp.cos(ang), jnp.sin(ang)
    n = pos.shape[0]
    rest = HEAD_DIM - ROPE_DIM
    cos_h = jnp.concatenate([c, c, jnp.ones((n, rest), F32)], axis=-1)
    sp_h = jnp.concatenate([jnp.zeros((n, half), F32), s, jnp.zeros((n, rest), F32)], axis=-1)
    sm_h = jnp.concatenate([-s, jnp.zeros((n, half + rest), F32)], axis=-1)
    two = lambda t: jnp.concatenate([t, t], axis=-1)
    return two(cos_h), two(sp_h), two(sm_h), c.T, s.T


def _pair_diag(w):
    z = jnp.zeros_like(w)
    return jnp.concatenate([jnp.concatenate([w, z], axis=-1), jnp.concatenate([z, w], axis=-1)], axis=-2)


def _overlap_t(seq):
    n_cmp = seq // NSA_CMP_STRIDE
    n_slc = seq // NSA_SEL_BLOCK
    starts = np.arange(n_cmp) * NSA_CMP_STRIDE
    j = np.arange(n_slc)
    ovl = (starts[None, :] < (j[:, None] + 1) * NSA_SEL_BLOCK) & (starts[None, :] + NSA_CMP_LEN > j[:, None] * NSA_SEL_BLOCK)
    ovl[:, n_cmp - 1] = False
    return jnp.asarray(ovl, BF16)


def _mixers(x, seq_tabs, cmp_tabs, ovl_t, ln1, w, qn_a, kn_a, qn_b, kn_b, pe_k, pe_v, wk1, wk2, wv1, wv2,
            qn_c, kn_c):
    rep = lambda g, n: jnp.tile(g, n)
    w_t, w_r = w
    gain_t = jnp.concatenate([rep(qn_a * Q_SCALE, H_MOBA), rep(qn_b * Q_SCALE, H_NSA)])
    gain_t = jnp.broadcast_to(gain_t[:, None], (gain_t.shape[0], LANES))
    gain_r = jnp.concatenate([rep(kn_a, H_MOBA), rep(kn_b[1], H_NSA_KV), rep(kn_b[2], H_NSA_KV),
                              rep(qn_c * Q_SCALE, H_DIL), rep(kn_c, H_DIL)])[None, :]
    pbt, pb, pf = _in_proj(x, ln1[None, :], w_t, w_r, gain_t, gain_r, *seq_tabs)

    o_a = _moba(pbt, pb)
    two = lambda t: jnp.concatenate([t, t], axis=-1)
    w1 = lambda t: _pair_diag(t.reshape(NSA_CMP_LEN, HEAD_DIM, NSA_CMP_HIDDEN)).astype(BF16)
    kcmp, vcmp_t = _compress(pf, two(pe_k), two(pe_v), w1(wk1), _pair_diag(wk2).astype(BF16),
                             w1(wv1), _pair_diag(wv2).astype(BF16), two(kn_b[0])[None, :], *cmp_tabs[:3])
    o_b = _nsa(pbt, pb, pf, kcmp, vcmp_t, ovl_t)
    o_c = _dilated(pb)
    return o_a, o_b, o_c


def _layer(x, seq_tabs, cmp_tabs, ovl_t, ln1, w_in, qn_a, kn_a, qn_b, kn_b, pe_k, pe_v, wk1, wk2, wv1, wv2,
           qn_c, kn_c, w_out, ln2, w_gate, w_up, conv_w, conv_b, w_down):
    B, S, D = x.shape
    o_a, o_b, o_c = _mixers(x, seq_tabs, cmp_tabs, ovl_t, ln1, w_in, qn_a, kn_a, qn_b, kn_b, pe_k, pe_v,
                            wk1, wk2, wv1, wv2, qn_c, kn_c)
    nsa_rows = [w_out[QKV_A + h * HEAD_DIM:QKV_A + (h + 1) * HEAD_DIM] for h in NSA_ORDER]
    w_o = jnp.concatenate([w_out[:QKV_A]] + nsa_rows + [w_out[QKV_A + Q_B:]], axis=0).astype(BF16)
    M = B * S
    x2 = _proj_ffn(x.reshape(M, D), o_a.reshape(M, QKV_A), o_b.reshape(M, Q_B), o_c.reshape(M, QKV_C), w_o,
                   ln2[None, :], w_gate.astype(BF16), w_up.astype(BF16), conv_w, conv_b[None, :],
                   w_down.astype(BF16), S)
    return x2.reshape(B, S, D)


def kernel(x, ln1, w_in, qn_a, kn_a, qn_b, kn_b, cmp_pe_k, cmp_pe_v, cmp_k_w1, cmp_k_w2, cmp_v_w1, cmp_v_w2,
           qn_c, kn_c, w_out, ln2, w_gate, w_up, conv_w, conv_b, w_down):
    S = x.shape[1]
    seq_tabs = _rope_tables(jnp.arange(S))
    n_cmp = S // NSA_CMP_STRIDE
    cmp_tabs = _rope_tables(jnp.arange(n_cmp) * NSA_CMP_STRIDE + (NSA_CMP_LEN - 1))
    ovl_t = _overlap_t(S)
    for l in range(ln1.shape[0]):
        x = _layer(x, seq_tabs, cmp_tabs, ovl_t, ln1[l], _prep_w_in(w_in, l), qn_a[l], kn_a[l], qn_b[l], kn_b[l],
                   cmp_pe_k[l], cmp_pe_v[l], cmp_k_w1[l], cmp_k_w2[l], cmp_v_w1[l], cmp_v_w2[l], qn_c[l], kn_c[l],
                   w_out[l], ln2[l], w_gate[l], w_up[l], conv_w[l], conv_b[l], w_down[l])
    return x
```

```python
import functools

import numpy as np
import jax
import jax.numpy as jnp
from jax import lax
from jax.experimental import pallas as pl
from jax.experimental.pallas import tpu as pltpu

F32 = jnp.float32
BF16 = jnp.bfloat16

D_MODEL = 1024
HEAD_DIM = 64
N_HEADS = D_MODEL // HEAD_DIM
H_MOBA = N_HEADS // 4
H_NSA = (N_HEADS - H_MOBA) // 2
H_DIL = N_HEADS - H_MOBA - H_NSA
H_NSA_KV = 2
NSA_GROUP = H_NSA // H_NSA_KV
ROPE_DIM = HEAD_DIM // 4
ROPE_THETA = 500000.0
MOBA_BLOCK = 256
MOBA_TOPK = 3
NSA_CMP_LEN = 32
NSA_CMP_STRIDE = 16
NSA_CMP_HIDDEN = 128
NSA_SEL_BLOCK = 64
NSA_N_SEL = 6
NSA_WINDOW = 512
DIL_CFG = ((128, 1), (512, 4), (2048, 16))
BAND_BLOCK = 128
D_FF = 2816
EPS = 1e-6
NEG = -1e30
TINY = 1e-30
FORCE = 1e9
SCALE = HEAD_DIM ** -0.5
Q_SCALE = SCALE * float(np.log2(np.e))
QKV_A = H_MOBA * HEAD_DIM
Q_B = H_NSA * HEAD_DIM
KV_B = H_NSA_KV * HEAD_DIM
G_B = H_NSA * 3
QKV_C = H_DIL * HEAD_DIM
IN_SPLITS = (QKV_A, QKV_A, QKV_A, Q_B, KV_B, KV_B, KV_B, KV_B, KV_B, KV_B, G_B, QKV_C, QKV_C, QKV_C)

LANES = 128
TQ = 256
NSA_ORDER = (0, 3, 1, 4, 2, 5)
VMEM_LIMIT = 56 * 1024 * 1024

M_FLOOR = -1e29

T_GROUP = ("qa", "qb", "va", "vsb", "vwb")
N_T_NORM, N_T = 5, 9
R_GROUP = ("ka", "ksb", "kwb", "qc", "kc", "vc", "kcb", "vcb", "gb")
N_R_NORM, N_R_BF16, N_R = 10, 13, 16
N_PBT, N_PB, N_F32 = N_T, N_R_BF16, N_R - N_R_BF16
PBT_QA, PBT_VA, PBT_VS, PBT_VW, PBT_QB = 0, 2, 4, 5, 6
PB_QC, PB_KC, PB_KA, PB_KS, PB_VC, PB_KW = 0, 3, 6, 8, 9, 12
_PBT_DST = (0, 1, 6, 7, 8, 2, 3, 4, 5)
_PB_DST = (6, 7, 8, 12, 0, 1, 2, 3, 4, 5, 9, 10, 11)
CHUNK = 4


def _cparams(sem):
    return pltpu.CompilerParams(dimension_semantics=sem, vmem_limit_bytes=VMEM_LIMIT)


def _const_spec(shape):
    return pl.BlockSpec(shape, lambda *_: (0,) * len(shape), pipeline_mode=pl.Buffered(1))


def _norm_rope(y, gain, cos_t, sin_p, sin_m, lo):
    ss = y * y
    s_lo = jnp.sum(jnp.where(lo, ss, 0.0), axis=-1, keepdims=True)
    s_hi = jnp.sum(jnp.where(lo, 0.0, ss), axis=-1, keepdims=True)
    ms = jnp.where(lo, s_lo, s_hi) * (1.0 / HEAD_DIM)
    z = (y * lax.rsqrt(ms + EPS)) * gain
    half = ROPE_DIM // 2
    return z * cos_t + pltpu.roll(z, half, 1) * sin_p + pltpu.roll(z, LANES - half, 1) * sin_m


def _online_update(m_ref, l_ref, acc_ref, g, s, pv):
    m_old = m_ref[g]
    m_new = jnp.maximum(m_old, jnp.max(s, axis=0, keepdims=True))
    p = jnp.exp2(s - m_new)
    alpha = jnp.exp2(m_old - m_new)
    l_ref[g] = alpha * l_ref[g] + jnp.sum(p, axis=0, keepdims=True)
    acc_ref[g] = alpha * acc_ref[g] + pv(p.astype(BF16))
    m_ref[g] = m_new


def _online_init(m_ref, l_ref, acc_ref, g):
    m_ref[g] = jnp.full(m_ref.shape[1:], M_FLOOR, F32)
    l_ref[g] = jnp.zeros(l_ref.shape[1:], F32)
    acc_ref[g] = jnp.zeros(acc_ref.shape[1:], F32)


def _dot(a, b):
    return jnp.dot(a, b, preferred_element_type=F32)


def _half_rows(q_t, hh):
    z = jnp.zeros((HEAD_DIM, q_t.shape[1]), q_t.dtype)
    if hh == 0:
        return jnp.concatenate([q_t[:HEAD_DIM], z], axis=0)
    return jnp.concatenate([z, q_t[HEAD_DIM:]], axis=0)


def _split_bf16(a):
    hi = a.astype(BF16)
    return hi, (a - hi.astype(F32)).astype(BF16)


def _in_proj_plan(group):
    names = ("qa", "ka", "va", "qb", "kcb", "vcb", "ksb", "vsb", "kwb", "vwb", "gb", "qc", "kc", "vc")
    start = dict(zip(names, np.cumsum((0,) + IN_SPLITS[:-1]).tolist()))
    width = dict(zip(names, IN_SPLITS))
    plan = []
    for name in group:
        if name == "qb":
            heads = [start[name] + h * HEAD_DIM for h in NSA_ORDER]
            plan += [[(heads[2 * j], HEAD_DIM), (heads[2 * j + 1], HEAD_DIM)] for j in range(H_NSA // 2)]
        else:
            plan += [[(start[name] + c, min(LANES, width[name] - c))] for c in range(0, width[name], LANES)]
    return plan


def _wprep_kernel(w_ref, wt_ref, wr_ref, *, plan_t, plan_r, n_in):
    def take(a, width):
        b0 = a // LANES * LANES
        v0 = w_ref[0, :, b0:min(b0 + LANES, n_in)]
        if a + width <= b0 + LANES:
            return v0[:, a - b0:a - b0 + width]
        v1 = w_ref[0, :, b0 + LANES:min(b0 + 2 * LANES, n_in)]
        return jnp.concatenate([v0[:, a - b0:], v1[:, :a + width - b0 - LANES]], axis=1)

    def block(pieces):
        cols = [take(a, width) for a, width in pieces]
        filled = sum(width for _, width in pieces)
        if filled < LANES:
            cols.append(jnp.zeros((w_ref.shape[1], LANES - filled), F32))
        return cols[0] if len(cols) == 1 else jnp.concatenate(cols, axis=1)

    for d, pieces in enumerate(plan_t):
        wt_ref[d * LANES:(d + 1) * LANES, :] = block(pieces).T.astype(BF16)
    for d, pieces in enumerate(plan_r):
        wr_ref[:, d * LANES:(d + 1) * LANES] = block(pieces).astype(BF16)


def _prep_w_in(w_in, layer, tr=256):
    _, D, n_in = w_in.shape
    plan_t, plan_r = _in_proj_plan(T_GROUP), _in_proj_plan(R_GROUP)
    return pl.pallas_call(
        functools.partial(_wprep_kernel, plan_t=plan_t, plan_r=plan_r, n_in=n_in),
        grid=(D // tr,),
        in_specs=[pl.BlockSpec((1, tr, n_in), lambda i: (layer, i, 0))],
        out_specs=[pl.BlockSpec((N_T * LANES, tr), lambda i: (0, i)),
                   pl.BlockSpec((tr, N_R * LANES), lambda i: (i, 0))],
        out_shape=[jax.ShapeDtypeStruct((N_T * LANES, D), BF16), jax.ShapeDtypeStruct((D, N_R * LANES), BF16)],
        compiler_params=_cparams(("parallel",)),
        name="w_in_layout",
    )(w_in)


def _norm_rope_t(y, gain, cos8, sin8):
    half = ROPE_DIM // 2
    parts = []
    for hh in range(2):
        yh = y[hh * HEAD_DIM:(hh + 1) * HEAD_DIM]
        ms = jnp.sum(yh * yh, axis=0, keepdims=True) * (1.0 / HEAD_DIM)
        z = (yh * lax.rsqrt(ms + EPS)) * gain[hh * HEAD_DIM:(hh + 1) * HEAD_DIM]
        x1, x2 = z[:half], z[half:ROPE_DIM]
        parts += [x1 * cos8 - x2 * sin8, x2 * cos8 + x1 * sin8, z[ROPE_DIM:]]
    return jnp.concatenate(parts, axis=0)


def _inproj_kernel(x_ref, ln_ref, wt_ref, wr_ref, gt_ref, gr_ref, cos_ref, sp_ref, sm_ref, ct_ref, st_ref,
                   pbt_ref, pb_ref, pf_ref, *, tm):
    x = x_ref[0]
    h = x * lax.rsqrt(jnp.mean(x * x, axis=-1, keepdims=True) + EPS)
    h = (h * ln_ref[...]).astype(BF16)
    lo = lax.broadcasted_iota(jnp.int32, (tm, LANES), 1) < HEAD_DIM
    cos_t, sin_p, sin_m = cos_ref[...], sp_ref[...], sm_ref[...]
    for c0 in range(0, N_R, CHUNK):
        y4 = _dot(h, wr_ref[:, c0 * LANES:(c0 + CHUNK) * LANES])
        for c in range(c0, c0 + CHUNK):
            y = y4[:, (c - c0) * LANES:(c - c0 + 1) * LANES]
            if c < N_R_NORM:
                y = _norm_rope(y, gr_ref[:, c * LANES:(c + 1) * LANES], cos_t, sin_p, sin_m, lo)
            if c < N_R_BF16:
                d = _PB_DST[c]
                pb_ref[0, :, d * LANES:(d + 1) * LANES] = y.astype(BF16)
            else:
                d = c - N_R_BF16
                pf_ref[0, :, d * LANES:(d + 1) * LANES] = y
    cos8, sin8 = ct_ref[...], st_ref[...]
    for t0, t1 in ((0, N_T_NORM), (N_T_NORM, N_T)):
        y_t = lax.dot_general(wt_ref[t0 * LANES:t1 * LANES, :], h, (((1,), (1,)), ((), ())),
                              preferred_element_type=F32)
        for t in range(t0, t1):
            y = y_t[(t - t0) * LANES:(t - t0 + 1) * LANES]
            if t < N_T_NORM:
                gain = jnp.concatenate([gt_ref[t * LANES:(t + 1) * LANES, :]] * (tm // LANES), axis=1)
                y = _norm_rope_t(y, gain, cos8, sin8)
            y = y.astype(BF16)
            for u in range(tm // TQ):
                pbt_ref[0, _PBT_DST[t], u] = y[:, u * TQ:(u + 1) * TQ]


def _in_proj(x, ln, w_t, w_r, gain_t, gain_r, cos_t, sin_p, sin_m, cos8, sin8, tm=512):
    B, S, D = x.shape
    nt = S // tm
    half = ROPE_DIM // 2
    return pl.pallas_call(
        functools.partial(_inproj_kernel, tm=tm),
        grid=(B * nt,),
        in_specs=[
            pl.BlockSpec((1, tm, D), lambda i: (i // nt, i % nt, 0)),
            _const_spec((1, D)),
            _const_spec(w_t.shape), _const_spec(w_r.shape), _const_spec(gain_t.shape), _const_spec(gain_r.shape),
            pl.BlockSpec((tm, LANES), lambda i: (i % nt, 0)),
            pl.BlockSpec((tm, LANES), lambda i: (i % nt, 0)),
            pl.BlockSpec((tm, LANES), lambda i: (i % nt, 0)),
            pl.BlockSpec((half, tm), lambda i: (0, i % nt)),
            pl.BlockSpec((half, tm), lambda i: (0, i % nt)),
        ],
        out_specs=[
            pl.BlockSpec((1, N_PBT, tm // TQ, LANES, TQ), lambda i: (i // nt, 0, i % nt, 0, 0)),
            pl.BlockSpec((1, tm, N_PB * LANES), lambda i: (i // nt, i % nt, 0)),
            pl.BlockSpec((1, tm, N_F32 * LANES), lambda i: (i // nt, i % nt, 0)),
        ],
        out_shape=[
            jax.ShapeDtypeStruct((B, N_PBT, S // TQ, LANES, TQ), BF16),
            jax.ShapeDtypeStruct((B, S, N_PB * LANES), BF16),
            jax.ShapeDtypeStruct((B, S, N_F32 * LANES), F32),
        ],
        compiler_params=_cparams(("parallel",)),
        name="in_proj",
    )(x, ln, w_t, w_r, gain_t, gain_r, cos_t, sin_p, sin_m, cos8, sin8)


def _moba_kernel(qt_ref, k_ref, vt_ref, o_ref, q2_ref, bias_ref, m_ref, l_ref, acc_ref, kmean_ref, *, n_blk):
    qi = pl.program_id(1)
    n_pairs = H_MOBA // 2
    nidx = lax.broadcasted_iota(jnp.int32, (16, 2 * TQ), 0)
    past = nidx < qi
    kr = lax.broadcasted_iota(jnp.int32, (TQ, TQ), 0)
    ql = lax.broadcasted_iota(jnp.int32, (TQ, TQ), 1)
    causal_bias = jnp.where(kr <= ql, 0.0, NEG)
    @pl.when(qi == 0)
    def _():
        for p in range(n_pairs):
            ps = slice(p * LANES, (p + 1) * LANES)
            rows = [jnp.sum(k_ref[0, n * TQ:(n + 1) * TQ, ps].astype(F32), axis=0, keepdims=True)
                    for n in range(n_blk)]
            kmean_ref[p] = (jnp.concatenate(rows + [jnp.zeros((16 - n_blk, LANES), F32)], axis=0)
                            * (1.0 / MOBA_BLOCK))

    for p in range(n_pairs):
        km_hi, km_lo = _split_bf16(kmean_ref[p])
        q2 = jnp.concatenate([_half_rows(qt_ref[0, p, 0], hh) for hh in range(2)], axis=1)
        q2_ref[p] = q2
        gate = _dot(km_hi, q2) + _dot(km_lo, q2)
        for n in range(n_blk):
            row = gate[n:n + 1, :]
            ge = jnp.where(gate >= row, 1.0, 0.0)
            gt = jnp.where(gate > row, 1.0, 0.0)
            beats = jnp.where(past, jnp.where(nidx < n, ge, gt), 0.0)
            cnt = jnp.sum(beats, axis=0, keepdims=True)
            bias_ref[p, n] = jnp.where(cnt < MOBA_TOPK, 0.0, NEG)
        _online_init(m_ref, l_ref, acc_ref, p)

    def scores(off, p):
        return _dot(k_ref[0, pl.ds(off, TQ), p * LANES:(p + 1) * LANES], q2_ref[p])

    def pv(p, kt):
        def fn(pr):
            return jnp.concatenate([_dot(vt_ref[0, p, kt, :HEAD_DIM, :], pr[:, :TQ]),
                                    _dot(vt_ref[0, p, kt, HEAD_DIM:, :], pr[:, TQ:])], axis=1)
        return fn

    def past_scores(kt):
        off = pl.multiple_of(kt * TQ, TQ)
        return [scores(off, p) + bias_ref[p, kt] for p in range(n_pairs)]

    def past_tile(kt, s):
        for p in range(n_pairs):
            _online_update(m_ref, l_ref, acc_ref, p, s[p], pv(p, kt))

    def past_pair(i, _):
        s_a, s_b = past_scores(2 * i), past_scores(2 * i + 1)
        past_tile(2 * i, s_a)
        past_tile(2 * i + 1, s_b)
        return 0

    lax.fori_loop(0, qi // 2, past_pair, 0)

    @pl.when(qi % 2 == 1)
    def _():
        past_tile(qi - 1, past_scores(qi - 1))
    off = pl.multiple_of(qi * TQ, TQ)
    causal2 = jnp.concatenate([causal_bias, causal_bias], axis=1)
    s = [scores(off, p) + causal2 for p in range(n_pairs)]
    for p in range(n_pairs):
        _online_update(m_ref, l_ref, acc_ref, p, s[p], pv(p, qi))
        o2 = acc_ref[p] / jnp.maximum(l_ref[p], TINY)
        o_ref[0, :, p * LANES:(p + 1) * LANES] = jnp.concatenate([o2[:, :TQ], o2[:, TQ:]], axis=0).T.astype(BF16)


def _moba(pbt, pb):
    B, _, n_t, _, _ = pbt.shape
    S = pb.shape[1]
    n_pairs = H_MOBA // 2
    return pl.pallas_call(
        functools.partial(_moba_kernel, n_blk=S // MOBA_BLOCK),
        grid=(B, n_t),
        in_specs=[
            pl.BlockSpec((1, n_pairs, 1, LANES, TQ), lambda b, i: (b, PBT_QA // n_pairs, i, 0, 0)),
            pl.BlockSpec((1, S, QKV_A), lambda b, i: (b, 0, PB_KA // n_pairs)),
            pl.BlockSpec((1, n_pairs, n_t, LANES, TQ), lambda b, i: (b, PBT_VA // n_pairs, 0, 0, 0)),
        ],
        out_specs=pl.BlockSpec((1, TQ, QKV_A), lambda b, i: (b, i, 0)),
        out_shape=jax.ShapeDtypeStruct((B, S, QKV_A), BF16),
        scratch_shapes=[pltpu.VMEM((n_pairs, LANES, 2 * TQ), BF16),
                        pltpu.VMEM((n_pairs, S // MOBA_BLOCK, 1, 2 * TQ), F32),
                        pltpu.VMEM((n_pairs, 1, 2 * TQ), F32), pltpu.VMEM((n_pairs, 1, 2 * TQ), F32),
                        pltpu.VMEM((n_pairs, HEAD_DIM, 2 * TQ), F32),
                        pltpu.VMEM((n_pairs, 16, LANES), F32)],
        compiler_params=_cparams(("arbitrary", "arbitrary")),
        name="moba",
    )(pbt, pb, pbt)


def _compress_kernel(kc_ref, vc_ref, pek_ref, pev_ref, w1k_ref, w2k_ref, w1v_ref, w2v_ref, gain_ref,
                     cos_ref, sp_ref, sm_ref, ko_ref, vo_ref, *, n_cmp):
    half_len = NSA_CMP_LEN // 2

    def mlp(t_ref, pe_ref, w1_ref, w2_ref):
        first = jnp.zeros((n_cmp, 2 * NSA_CMP_HIDDEN), F32)
        second = jnp.zeros((n_cmp, 2 * NSA_CMP_HIDDEN), F32)
        for l in range(half_len):
            a = t_ref[0, pl.ds(l, n_cmp, stride=NSA_CMP_STRIDE), :]
            first = first + jnp.dot((a + pe_ref[l:l + 1, :]).astype(BF16), w1_ref[l],
                                    preferred_element_type=F32)
            second = second + jnp.dot((a + pe_ref[half_len + l:half_len + l + 1, :]).astype(BF16),
                                      w1_ref[half_len + l], preferred_element_type=F32)
        hid = jax.nn.gelu(first + pltpu.roll(second, n_cmp - 1, 0))
        return jnp.dot(hid.astype(BF16), w2_ref[...], preferred_element_type=F32)

    valid = lax.broadcasted_iota(jnp.int32, (n_cmp, LANES), 0) < n_cmp - 1
    lo = lax.broadcasted_iota(jnp.int32, (n_cmp, LANES), 1) < HEAD_DIM
    kc = mlp(kc_ref, pek_ref, w1k_ref, w2k_ref)
    kc = _norm_rope(kc, gain_ref[...], cos_ref[...], sp_ref[...], sm_ref[...], lo)
    ko_ref[0] = jnp.where(valid, kc, 0.0).astype(BF16)
    vc = mlp(vc_ref, pev_ref, w1v_ref, w2v_ref)
    vo_ref[0] = jnp.where(valid, vc, 0.0).T.astype(BF16)


def _compress(pf, pe_k, pe_v, w1k, w2k, w1v, w2v, gain, cos_c, sp_c, sm_c):
    B, S, _ = pf.shape
    n_cmp = S // NSA_CMP_STRIDE
    cs = _const_spec
    return pl.pallas_call(
        functools.partial(_compress_kernel, n_cmp=n_cmp),
        grid=(B,),
        in_specs=[
            pl.BlockSpec((1, S, LANES), lambda b: (b, 0, 0)),
            pl.BlockSpec((1, S, LANES), lambda b: (b, 0, 1)),
            cs(pe_k.shape), cs(pe_v.shape), cs(w1k.shape), cs(w2k.shape), cs(w1v.shape), cs(w2v.shape),
            cs(gain.shape), cs(cos_c.shape), cs(sp_c.shape), cs(sm_c.shape),
        ],
        out_specs=[pl.BlockSpec((1, n_cmp, LANES), lambda b: (b, 0, 0)),
                   pl.BlockSpec((1, LANES, n_cmp), lambda b: (b, 0, 0))],
        out_shape=[jax.ShapeDtypeStruct((B, n_cmp, LANES), BF16),
                   jax.ShapeDtypeStruct((B, LANES, n_cmp), BF16)],
        compiler_params=_cparams(("parallel",)),
        name="nsa_compress",
    )(pf, pf, pe_k, pe_v, w1k, w2k, w1v, w2v, gain, cos_c, sp_c, sm_c)


def _nsa_kernel(qt_ref, kc_ref, vct_ref, ks_ref, vst_ref, kw_ref, vwt_ref, g_ref, ovl_ref, o_ref,
                q6_ref, impf_ref, improw_ref, m_ref, l_ref, acc_ref, ot_ref, *, n_cmp, n_slc):
    qi = pl.program_id(1)
    blk_per_tile = TQ // NSA_SEL_BLOCK
    tpos = qi * TQ + lax.broadcasted_iota(jnp.int32, (1, TQ), 1)
    sig = jax.nn.sigmoid(g_ref[0].T)
    kr = lax.broadcasted_iota(jnp.int32, (TQ, TQ), 0)
    ql = lax.broadcasted_iota(jnp.int32, (TQ, TQ), 1)
    causal_bias = jnp.where(kr <= ql, 0.0, NEG)
    nidx = lax.broadcasted_iota(jnp.int32, (n_cmp, TQ), 0)
    cmask = (nidx * NSA_CMP_STRIDE + (NSA_CMP_LEN - 1) <= tpos) & (nidx < n_cmp - 1)
    cmp_bias = jnp.where(cmask, 0.0, NEG)
    jj = lax.broadcasted_iota(jnp.int32, (n_slc, TQ), 0)
    cur = lax.shift_right_arithmetic(tpos, NSA_SEL_BLOCK.bit_length() - 1)
    forced = (jj == 0) | (jj == cur) | (jj == cur - 1)
    gw = NSA_GROUP * TQ

    def half(kk):
        return slice(kk * HEAD_DIM, (kk + 1) * HEAD_DIM)

    def cols(kk):
        return slice(kk * gw, (kk + 1) * gw)

    def tile3(a):
        return jnp.concatenate([a] * NSA_GROUP, axis=1)

    def gate3(kk, branch):
        return jnp.concatenate([sig[3 * (NSA_GROUP * kk + j) + branch:3 * (NSA_GROUP * kk + j) + branch + 1, :]
                                for j in range(NSA_GROUP)], axis=1)

    q6 = jnp.concatenate([_half_rows(qt_ref[0, j, 0], kk) for kk in range(H_NSA_KV) for j in range(NSA_GROUP)],
                         axis=1)
    q6_ref[...] = q6

    s_cmp = _dot(kc_ref[0], q6)
    cmp_bias3 = tile3(cmp_bias)
    for kk in range(H_NSA_KV):
        s = s_cmp[:, cols(kk)] + cmp_bias3
        m = jnp.maximum(jnp.max(s, axis=0, keepdims=True), M_FLOOR)
        p = jnp.exp2(s - m)
        p = p / jnp.maximum(jnp.sum(p, axis=0, keepdims=True), TINY)
        ot_ref[kk] = gate3(kk, 0) * _dot(vct_ref[0, half(kk), :], p.astype(BF16))
        p_sum = (p[:, :TQ] + p[:, TQ:2 * TQ]) + p[:, 2 * TQ:]
        _online_init(m_ref, l_ref, acc_ref, kk)
        p_hi, p_lo = _split_bf16(p_sum)
        imp = _dot(ovl_ref[...], p_hi) + _dot(ovl_ref[...], p_lo)
        imp = jnp.where(jj > cur, NEG, jnp.where(forced, FORCE, imp))
        impf_ref[kk] = imp
        for j in range(n_slc):
            improw_ref[kk, j] = imp[j:j + 1, :]

    def sel_bias(kk, kt):
        imp = impf_ref[kk]
        parts = []
        for c in range(blk_per_tile):
            j = kt * blk_per_tile + c
            row = improw_ref[kk, j]
            ge = jnp.where(imp >= row, 1.0, 0.0)
            gt = jnp.where(imp > row, 1.0, 0.0)
            cnt = jnp.sum(jnp.where(jj < j, ge, gt), axis=0, keepdims=True)
            parts.append(jnp.broadcast_to(jnp.where(cnt < NSA_N_SEL, 0.0, NEG), (NSA_SEL_BLOCK, TQ)))
        return jnp.concatenate(parts, axis=0)

    def sel_pv(kk, kt):
        return lambda pr: _dot(vst_ref[0, 0, kt, half(kk), :], pr)

    def sel_scores(kt):
        off = pl.multiple_of(kt * TQ, TQ)
        return _dot(ks_ref[0, pl.ds(off, TQ), :], q6_ref[...])

    def sel_tile(kt, s6):
        for kk in range(H_NSA_KV):
            s = s6[:, cols(kk)] + tile3(sel_bias(kk, kt))
            _online_update(m_ref, l_ref, acc_ref, kk, s, sel_pv(kk, kt))

    def sel_pair(i, _):
        s_a, s_b = sel_scores(2 * i), sel_scores(2 * i + 1)
        sel_tile(2 * i, s_a)
        sel_tile(2 * i + 1, s_b)
        return 0

    lax.fori_loop(0, qi // 2, sel_pair, 0)

    @pl.when(qi % 2 == 1)
    def _():
        sel_tile(qi - 1, sel_scores(qi - 1))

    off = pl.multiple_of(qi * TQ, TQ)
    kt1 = jnp.maximum(qi - 1, 0)
    kt2 = jnp.maximum(qi - 2, 0)
    ks_d = ks_ref[0, pl.ds(off, TQ), :]
    kw_0 = kw_ref[0, pl.ds(off, TQ), :]
    kw_1 = kw_ref[0, pl.ds(pl.multiple_of(kt1 * TQ, TQ), TQ), :]
    kw_2 = kw_ref[0, pl.ds(pl.multiple_of(kt2 * TQ, TQ), TQ), :]
    win_bias2 = jnp.where(kr > ql + jnp.where(qi >= 2, 0, TQ), 0.0, NEG)
    win_bias1 = jnp.where(qi >= 1, 0.0, NEG)
    q6 = q6_ref[...]
    sd6, w2, w1, w0 = _dot(ks_d, q6), _dot(kw_2, q6), _dot(kw_1, q6), _dot(kw_0, q6)
    causal3 = tile3(causal_bias)
    win_bias2 = tile3(win_bias2)
    for kk in range(H_NSA_KV):
        s = sd6[:, cols(kk)] + tile3(sel_bias(kk, qi) + causal_bias)
        _online_update(m_ref, l_ref, acc_ref, kk, s, sel_pv(kk, qi))
        o_sel = acc_ref[kk] / jnp.maximum(l_ref[kk], TINY)
        s2 = w2[:, cols(kk)] + win_bias2
        s1 = w1[:, cols(kk)] + win_bias1
        s0 = w0[:, cols(kk)] + causal3
        m = jnp.maximum(jnp.maximum(jnp.max(s2, axis=0, keepdims=True), jnp.max(s1, axis=0, keepdims=True)),
                        jnp.max(s0, axis=0, keepdims=True))
        p2, p1, p0 = jnp.exp2(s2 - m), jnp.exp2(s1 - m), jnp.exp2(s0 - m)
        den = (jnp.sum(p2, axis=0, keepdims=True) + jnp.sum(p1, axis=0, keepdims=True)
               + jnp.sum(p0, axis=0, keepdims=True))
        acc = (_dot(vwt_ref[0, 0, kt2, half(kk), :], p2.astype(BF16))
               + _dot(vwt_ref[0, 0, kt1, half(kk), :], p1.astype(BF16))
               + _dot(vwt_ref[0, 0, qi, half(kk), :], p0.astype(BF16)))
        o_win = acc / jnp.maximum(den, TINY)
        ot_ref[kk] = (ot_ref[kk] + gate3(kk, 1) * o_sel) + gate3(kk, 2) * o_win
    for j in range(NSA_GROUP):
        js = slice(j * TQ, (j + 1) * TQ)
        o_ref[0, :, j * LANES:(j + 1) * LANES] = jnp.concatenate(
            [ot_ref[0, :, js], ot_ref[1, :, js]], axis=0).T.astype(BF16)


def _nsa(pbt, pb, pf, kcmp, vcmp_t, ovl_t):
    B, _, n_t, _, _ = pbt.shape
    S = pb.shape[1]
    n_cmp = kcmp.shape[1]
    n_slc = S // NSA_SEL_BLOCK
    gw = NSA_GROUP * TQ
    return pl.pallas_call(
        functools.partial(_nsa_kernel, n_cmp=n_cmp, n_slc=n_slc),
        grid=(B, n_t),
        in_specs=[
            pl.BlockSpec((1, NSA_GROUP, 1, LANES, TQ), lambda b, i: (b, PBT_QB // NSA_GROUP, i, 0, 0)),
            pl.BlockSpec((1, n_cmp, LANES), lambda b, i: (b, 0, 0)),
            pl.BlockSpec((1, LANES, n_cmp), lambda b, i: (b, 0, 0)),
            pl.BlockSpec((1, S, LANES), lambda b, i: (b, 0, PB_KS)),
            pl.BlockSpec((1, 1, n_t, LANES, TQ), lambda b, i: (b, PBT_VS, 0, 0, 0)),
            pl.BlockSpec((1, S, LANES), lambda b, i: (b, 0, PB_KW)),
            pl.BlockSpec((1, 1, n_t, LANES, TQ), lambda b, i: (b, PBT_VW, 0, 0, 0)),
            pl.BlockSpec((1, TQ, LANES), lambda b, i: (b, i, 2)),
            _const_spec(ovl_t.shape),
        ],
        out_specs=pl.BlockSpec((1, TQ, Q_B), lambda b, i: (b, i, 0)),
        out_shape=jax.ShapeDtypeStruct((B, S, Q_B), BF16),
        scratch_shapes=[pltpu.VMEM((LANES, H_NSA * TQ), BF16),
                        pltpu.VMEM((H_NSA_KV, n_slc, TQ), F32), pltpu.VMEM((H_NSA_KV, n_slc, 1, TQ), F32),
                        pltpu.VMEM((H_NSA_KV, 1, gw), F32), pltpu.VMEM((H_NSA_KV, 1, gw), F32),
                        pltpu.VMEM((H_NSA_KV, HEAD_DIM, gw), F32),
                        pltpu.VMEM((H_NSA_KV, HEAD_DIM, gw), F32)],
        compiler_params=_cparams(("parallel", "arbitrary")),
        name="nsa_attn",
    )(pbt, kcmp, vcmp_t, pb, pbt, pb, pbt, pf, ovl_t)


DIL_TILES = 8

def _dilated_kernel(q_ref, k_ref, v_ref, o_ref, qf_ref, kf_ref, vf_ref, os_ref, ls_ref, *, seq):
    blk = BAND_BLOCK
    n_tiles = seq // blk
    lo = lax.broadcasted_iota(jnp.int32, (blk, LANES), 1) < HEAD_DIM
    for g, (window, dil) in enumerate(DIL_CFG):
        max_dist = window // dil
        tiles_per_class = n_tiles // dil
        use_prev = tiles_per_class > 1
        n_keys = 2 * blk if use_prev else blk
        ri = lax.broadcasted_iota(jnp.int32, (2 * blk, n_keys), 0) & (blk - 1)
        ci = lax.broadcasted_iota(jnp.int32, (2 * blk, n_keys), 1)
        dist = ri + (n_keys - blk) - ci
        band = (dist >= 0) & (dist <= max_dist)
        band_bias = jnp.where(band, 0.0, NEG)
        qf_ref[g] = q_ref[0, :, g * LANES:(g + 1) * LANES].astype(F32)
        kf_ref[g] = k_ref[0, :, g * LANES:(g + 1) * LANES].astype(F32)
        vf_ref[g] = v_ref[0, :, g * LANES:(g + 1) * LANES].astype(F32)

        def index(start, dil=dil):
            if dil == 1:
                return pl.ds(pl.multiple_of(start, blk), blk)
            return pl.ds(start, blk, stride=dil)

        def tiles(i, _, g=g, dil=dil, tiles_per_class=tiles_per_class, use_prev=use_prev, band=band,
                  band_bias=band_bias, ci=ci, index=index):
            scored = []
            for u in range(DIL_TILES):
                t = i * DIL_TILES + u
                cls = t // tiles_per_class
                mt = t % tiles_per_class
                start = cls + dil * blk * mt
                q = qf_ref[g, index(start), :]
                q2 = jnp.concatenate([jnp.where(lo, q, 0.0), jnp.where(lo, 0.0, q)], axis=0).astype(BF16)
                if use_prev:
                    prev = jnp.maximum(start - dil * blk, cls)
                    k2 = jnp.concatenate([kf_ref[g, index(prev), :], kf_ref[g, index(start), :]], axis=0)
                    v2 = jnp.concatenate([vf_ref[g, index(prev), :], vf_ref[g, index(start), :]], axis=0)
                    bias = jnp.where(band & (ci >= jnp.where(mt > 0, 0, blk)), 0.0, NEG)
                else:
                    k2, v2, bias = kf_ref[g, index(start), :], vf_ref[g, index(start), :], band_bias
                s = lax.dot_general(q2, k2.astype(BF16), (((1,), (1,)), ((), ())), preferred_element_type=F32)
                scored.append((start, s + bias, v2.astype(BF16)))
            probs = []
            for start, s, v2 in scored:
                m = jnp.max(s, axis=-1, keepdims=True)
                p = jnp.exp2(s - m)
                den = jnp.sum(p, axis=-1, keepdims=True)
                lse = m + jnp.log2(jnp.maximum(den, TINY))
                probs.append((start, lse, (p / jnp.maximum(den, TINY)).astype(BF16), v2))
            for start, lse, p, v2 in probs:
                o = _dot(p, v2)
                os_ref[g, index(start), :] = jnp.where(lo, o[:blk], o[blk:])
                ls_ref[g, index(start), :] = jnp.where(lo, lse[:blk], lse[blk:])
            return 0

        lax.fori_loop(0, n_tiles // DIL_TILES, tiles, 0)

    rows_per_trip = DIL_TILES * blk

    def combine(t, _):
        idx = pl.ds(pl.multiple_of(t * rows_per_trip, rows_per_trip), rows_per_trip)
        lses = [ls_ref[g, idx, :] for g in range(len(DIL_CFG))]
        mx = jnp.maximum(jnp.maximum(lses[0], lses[1]), lses[2])
        es = [jnp.exp2(l - mx) for l in lses]
        tot = es[0] + es[1] + es[2]
        for g in range(len(DIL_CFG)):
            o_ref[0, idx, g * LANES:(g + 1) * LANES] = (os_ref[g, idx, :] * (es[g] / tot)).astype(BF16)
        return 0

    lax.fori_loop(0, seq // rows_per_trip, combine, 0)


def _dilated(pb):
    B, S, _ = pb.shape
    n_g = len(DIL_CFG)
    return pl.pallas_call(
        functools.partial(_dilated_kernel, seq=S),
        grid=(B,),
        in_specs=[pl.BlockSpec((1, S, QKV_C), lambda b: (b, 0, PB_QC // n_g)),
                  pl.BlockSpec((1, S, QKV_C), lambda b: (b, 0, PB_KC // n_g)),
                  pl.BlockSpec((1, S, QKV_C), lambda b: (b, 0, PB_VC // n_g))],
        out_specs=pl.BlockSpec((1, S, QKV_C), lambda b: (b, 0, 0)),
        out_shape=jax.ShapeDtypeStruct((B, S, QKV_C), BF16),
        scratch_shapes=[pltpu.VMEM((n_g, S, LANES), F32)] * 5,
        compiler_params=_cparams(("parallel",)),
        name="dilated_attn",
    )(pb, pb, pb)


HALO = 16
FF_CHUNK = 256


def _proj_ffn_kernel(x_ref, xh_ref, oa_ref, oah_ref, ob_ref, obh_ref, oc_ref, och_ref, wo_ref, ln_ref,
                     wg_ref, wu_ref, cw_ref, cb_ref, wd_ref, y_ref, o_ref, h_ref, x1_ref, a_ref,
                     *, tiles_per_seq):
    i = pl.program_id(0)

    def norm(v):
        return (v * lax.rsqrt(jnp.mean(v * v, axis=-1, keepdims=True) + EPS)) * ln_ref[...]

    col = 0
    for main, halo in ((oa_ref, oah_ref), (ob_ref, obh_ref), (oc_ref, och_ref)):
        width = main.shape[1]
        o_ref[:HALO, col:col + width] = halo[...]
        o_ref[HALO:, col:col + width] = main[...]
        col += width
    attn = _dot(o_ref[...], wo_ref[...])
    x1 = x_ref[...] + attn[HALO:]
    x1_ref[...] = x1
    first = (i % tiles_per_seq) == 0
    h_ref[:HALO, :] = jnp.where(first, 0.0, norm(xh_ref[...] + attn[:HALO])).astype(BF16)
    h_ref[HALO:, :] = norm(x1).astype(BF16)

    for f0 in range(0, D_FF, FF_CHUNK):
        fs = slice(f0, min(f0 + FF_CHUNK, D_FF))
        g = _dot(h_ref[...], wg_ref[:, fs])
        u = _dot(h_ref[HALO:, :], wu_ref[:, fs])
        gc = (cw_ref[0:1, fs] * pltpu.roll(g, 2, 0)[HALO:] + cw_ref[1:2, fs] * pltpu.roll(g, 1, 0)[HALO:]
              + cw_ref[2:3, fs] * g[HALO:] + cb_ref[:, fs])
        a_ref[:, fs] = (jax.nn.silu(gc) * u).astype(BF16)
    y_ref[...] = x1_ref[...] + _dot(a_ref[...], wd_ref[...])


def _proj_ffn(x2, o_a, o_b, o_c, wo, ln, wg, wu, cw, cb, wd, seq, tm=512):
    M, D = x2.shape
    halo_blocks = tm // HALO
    main = lambda w: pl.BlockSpec((tm, w), lambda i: (i, 0))
    halo = lambda w: pl.BlockSpec((HALO, w), lambda i: (jnp.maximum(i * halo_blocks - 1, 0), 0))
    return pl.pallas_call(
        functools.partial(_proj_ffn_kernel, tiles_per_seq=seq // tm),
        grid=(M // tm,),
        in_specs=[
            main(D), halo(D), main(QKV_A), halo(QKV_A), main(Q_B), halo(Q_B), main(QKV_C), halo(QKV_C),
            _const_spec(wo.shape), _const_spec((1, D)),
            _const_spec(wg.shape), _const_spec(wu.shape), _const_spec(cw.shape), _const_spec(cb.shape),
            _const_spec(wd.shape),
        ],
        out_specs=main(D),
        out_shape=jax.ShapeDtypeStruct((M, D), F32),
        scratch_shapes=[pltpu.VMEM((HALO + tm, D), BF16), pltpu.VMEM((HALO + tm, D), BF16),
                        pltpu.VMEM((tm, D), F32), pltpu.VMEM((tm, D_FF), BF16)],
        compiler_params=_cparams(("parallel",)),
        name="proj_ffn",
    )(x2, x2, o_a, o_a, o_b, o_b, o_c, o_c, wo, ln, wg, wu, cw, cb, wd)


def _rope_tables(pos):
    half = ROPE_DIM // 2
    inv_freq = ROPE_THETA ** (-jnp.arange(half, dtype=F32) * 2.0 / ROPE_DIM)
    ang = pos.astype(F32)[:, None] * inv_freq
    c, s = jnp.cos(ang), jnp.sin(ang)
    n = pos.shape[0]
    rest = HEAD_DIM - ROPE_DIM
    cos_h = jnp.concatenate([c, c, jnp.ones((n, rest), F32)], axis=-1)
    sp_h = jnp.concatenate([jnp.zeros((n, half), F32), s, jnp.zeros((n, rest), F32)], axis=-1)
    sm_h = jnp.concatenate([-s, jnp.zeros((n, half + rest), F32)], axis=-1)
    two = lambda t: jnp.concatenate([t, t], axis=-1)
    return two(cos_h), two(sp_h), two(sm_h), c.T, s.T


def _pair_diag(w):
    z = jnp.zeros_like(w)
    return jnp.concatenate([jnp.concatenate([w, z], axis=-1), jnp.concatenate([z, w], axis=-1)], axis=-2)


def _overlap_t(seq):
    n_cmp = seq // NSA_CMP_STRIDE
    n_slc = seq // NSA_SEL_BLOCK
    starts = np.arange(n_cmp) * NSA_CMP_STRIDE
    j = np.arange(n_slc)
    ovl = (starts[None, :] < (j[:, None] + 1) * NSA_SEL_BLOCK) & (starts[None, :] + NSA_CMP_LEN > j[:, None] * NSA_SEL_BLOCK)
    ovl[:, n_cmp - 1] = False
    return jnp.asarray(ovl, BF16)


def _mixers(x, seq_tabs, cmp_tabs, ovl_t, ln1, w, qn_a, kn_a, qn_b, kn_b, pe_k, pe_v, wk1, wk2, wv1, wv2,
            qn_c, kn_c):
    rep = lambda g, n: jnp.tile(g, n)
    w_t, w_r = w
    gain_t = jnp.concatenate([rep(qn_a * Q_SCALE, H_MOBA), rep(qn_b * Q_SCALE, H_NSA)])
    gain_t = jnp.broadcast_to(gain_t[:, None], (gain_t.shape[0], LANES))
    gain_r = jnp.concatenate([rep(kn_a, H_MOBA), rep(kn_b[1], H_NSA_KV), rep(kn_b[2], H_NSA_KV),
                              rep(qn_c * Q_SCALE, H_DIL), rep(kn_c, H_DIL)])[None, :]
    pbt, pb, pf = _in_proj(x, ln1[None, :], w_t, w_r, gain_t, gain_r, *seq_tabs)

    o_a = _moba(pbt, pb)
    two = lambda t: jnp.concatenate([t, t], axis=-1)
    w1 = lambda t: _pair_diag(t.reshape(NSA_CMP_LEN, HEAD_DIM, NSA_CMP_HIDDEN)).astype(BF16)
    kcmp, vcmp_t = _compress(pf, two(pe_k), two(pe_v), w1(wk1), _pair_diag(wk2).astype(BF16),
                             w1(wv1), _pair_diag(wv2).astype(BF16), two(kn_b[0])[None, :], *cmp_tabs[:3])
    o_b = _nsa(pbt, pb, pf, kcmp, vcmp_t, ovl_t)
    o_c = _dilated(pb)
    return o_a, o_b, o_c


def _layer(x, seq_tabs, cmp_tabs, ovl_t, ln1, w_in, qn_a, kn_a, qn_b, kn_b, pe_k, pe_v, wk1, wk2, wv1, wv2,
           qn_c, kn_c, w_out, ln2, w_gate, w_up, conv_w, conv_b, w_down):
    B, S, D = x.shape
    o_a, o_b, o_c = _mixers(x, seq_tabs, cmp_tabs, ovl_t, ln1, w_in, qn_a, kn_a, qn_b, kn_b, pe_k, pe_v,
                            wk1, wk2, wv1, wv2, qn_c, kn_c)
    nsa_rows = [w_out[QKV_A + h * HEAD_DIM:QKV_A + (h + 1) * HEAD_DIM] for h in NSA_ORDER]
    w_o = jnp.concatenate([w_out[:QKV_A]] + nsa_rows + [w_out[QKV_A + Q_B:]], axis=0).astype(BF16)
    M = B * S
    x2 = _proj_ffn(x.reshape(M, D), o_a.reshape(M, QKV_A), o_b.reshape(M, Q_B), o_c.reshape(M, QKV_C), w_o,
                   ln2[None, :], w_gate.astype(BF16), w_up.astype(BF16), conv_w, conv_b[None, :],
                   w_down.astype(BF16), S)
    return x2.reshape(B, S, D)


def kernel(x, ln1, w_in, qn_a, kn_a, qn_b, kn_b, cmp_pe_k, cmp_pe_v, cmp_k_w1, cmp_k_w2, cmp_v_w1, cmp_v_w2,
           qn_c, kn_c, w_out, ln2, w_gate, w_up, conv_w, conv_b, w_down):
    S = x.shape[1]
    seq_tabs = _rope_tables(jnp.arange(S))
    n_cmp = S // NSA_CMP_STRIDE
    cmp_tabs = _rope_tables(jnp.arange(n_cmp) * NSA_CMP_STRIDE + (NSA_CMP_LEN - 1))
    ovl_t = _overlap_t(S)
    for l in range(ln1.shape[0]):
        x = _layer(x, seq_tabs, cmp_tabs, ovl_t, ln1[l], _prep_w_in(w_in, l), qn_a[l], kn_a[l], qn_b[l], kn_b[l],
                   cmp_pe_k[l], cmp_pe_v[l], cmp_k_w1[l], cmp_k_w2[l], cmp_v_w1[l], cmp_v_w2[l], qn_c[l], kn_c[l],
                   w_out[l], ln2[l], w_gate[l], w_up[l], conv_w[l], conv_b[l], w_down[l])
    return x
```

```python
import functools

import numpy as np
import jax
import jax.numpy as jnp
from jax import lax
from jax.experimental import pallas as pl
from jax.experimental.pallas import tpu as pltpu

F32 = jnp.float32
BF16 = jnp.bfloat16

D_MODEL = 1024
HEAD_DIM = 64
N_HEADS = D_MODEL // HEAD_DIM
H_MOBA = N_HEADS // 4
H_NSA = (N_HEADS - H_MOBA) // 2
H_DIL = N_HEADS - H_MOBA - H_NSA
H_NSA_KV = 2
NSA_GROUP = H_NSA // H_NSA_KV
ROPE_DIM = HEAD_DIM // 4
ROPE_THETA = 500000.0
MOBA_BLOCK = 256
MOBA_TOPK = 3
NSA_CMP_LEN = 32
NSA_CMP_STRIDE = 16
NSA_CMP_HIDDEN = 128
NSA_SEL_BLOCK = 64
NSA_N_SEL = 6
NSA_WINDOW = 512
DIL_CFG = ((128, 1), (512, 4), (2048, 16))
BAND_BLOCK = 128
D_FF = 2816
EPS = 1e-6
NEG = -1e30
TINY = 1e-30
FORCE = 1e9
SCALE = HEAD_DIM ** -0.5
Q_SCALE = SCALE * float(np.log2(np.e))
QKV_A = H_MOBA * HEAD_DIM
Q_B = H_NSA * HEAD_DIM
KV_B = H_NSA_KV * HEAD_DIM
G_B = H_NSA * 3
QKV_C = H_DIL * HEAD_DIM
IN_SPLITS = (QKV_A, QKV_A, QKV_A, Q_B, KV_B, KV_B, KV_B, KV_B, KV_B, KV_B, G_B, QKV_C, QKV_C, QKV_C)

LANES = 128
TQ = 256
NSA_ORDER = (0, 3, 1, 4, 2, 5)
VMEM_LIMIT = 56 * 1024 * 1024

M_FLOOR = -1e29

T_GROUP = ("qa", "qb", "va", "vsb", "vwb")
N_T_NORM, N_T = 5, 9
R_GROUP = ("ka", "ksb", "kwb", "qc", "kc", "vc", "kcb", "vcb", "gb")
N_R_NORM, N_R_BF16, N_R = 10, 13, 16
N_PBT, N_PB, N_F32 = N_T, N_R_BF16, N_R - N_R_BF16
PBT_QA, PBT_VA, PBT_VS, PBT_VW, PBT_QB = 0, 2, 4, 5, 6
PB_QC, PB_KC, PB_KA, PB_KS, PB_VC, PB_KW = 0, 3, 6, 8, 9, 12
_PBT_DST = (0, 1, 6, 7, 8, 2, 3, 4, 5)
_PB_DST = (6, 7, 8, 12, 0, 1, 2, 3, 4, 5, 9, 10, 11)
CHUNK = 4


def _cparams(sem):
    return pltpu.CompilerParams(dimension_semantics=sem, vmem_limit_bytes=VMEM_LIMIT)


def _const_spec(shape):
    return pl.BlockSpec(shape, lambda *_: (0,) * len(shape), pipeline_mode=pl.Buffered(1))


def _norm_rope(y, gain, cos_t, sin_p, sin_m, lo):
    ss = y * y
    s_lo = jnp.sum(jnp.where(lo, ss, 0.0), axis=-1, keepdims=True)
    s_hi = jnp.sum(jnp.where(lo, 0.0, ss), axis=-1, keepdims=True)
    ms = jnp.where(lo, s_lo, s_hi) * (1.0 / HEAD_DIM)
    z = (y * lax.rsqrt(ms + EPS)) * gain
    half = ROPE_DIM // 2
    return z * cos_t + pltpu.roll(z, half, 1) * sin_p + pltpu.roll(z, LANES - half, 1) * sin_m


def _online_update(m_ref, l_ref, acc_ref, g, s, pv):
    m_old = m_ref[g]
    m_new = jnp.maximum(m_old, jnp.max(s, axis=0, keepdims=True))
    p = jnp.exp2(s - m_new)
    alpha = jnp.exp2(m_old - m_new)
    l_ref[g] = alpha * l_ref[g] + jnp.sum(p, axis=0, keepdims=True)
    acc_ref[g] = alpha * acc_ref[g] + pv(p.astype(BF16))
    m_ref[g] = m_new


def _online_init(m_ref, l_ref, acc_ref, g):
    m_ref[g] = jnp.full(m_ref.shape[1:], M_FLOOR, F32)
    l_ref[g] = jnp.zeros(l_ref.shape[1:], F32)
    acc_ref[g] = jnp.zeros(acc_ref.shape[1:], F32)


def _dot(a, b):
    return jnp.dot(a, b, preferred_element_type=F32)


def _half_rows(q_t, hh):
    z = jnp.zeros((HEAD_DIM, q_t.shape[1]), q_t.dtype)
    if hh == 0:
        return jnp.concatenate([q_t[:HEAD_DIM], z], axis=0)
    return jnp.concatenate([z, q_t[HEAD_DIM:]], axis=0)


def _split_bf16(a):
    hi = a.astype(BF16)
    return hi, (a - hi.astype(F32)).astype(BF16)


def _in_proj_plan(group):
    names = ("qa", "ka", "va", "qb", "kcb", "vcb", "ksb", "vsb", "kwb", "vwb", "gb", "qc", "kc", "vc")
    start = dict(zip(names, np.cumsum((0,) + IN_SPLITS[:-1]).tolist()))
    width = dict(zip(names, IN_SPLITS))
    plan = []
    for name in group:
        if name == "qb":
            heads = [start[name] + h * HEAD_DIM for h in NSA_ORDER]
            plan += [[(heads[2 * j], HEAD_DIM), (heads[2 * j + 1], HEAD_DIM)] for j in range(H_NSA // 2)]
        else:
            plan += [[(start[name] + c, min(LANES, width[name] - c))] for c in range(0, width[name], LANES)]
    return plan


def _wprep_kernel(w_ref, wt_ref, wr_ref, *, plan_t, plan_r, n_in):
    def take(a, width):
        b0 = a // LANES * LANES
        v0 = w_ref[0, :, b0:min(b0 + LANES, n_in)]
        if a + width <= b0 + LANES:
            return v0[:, a - b0:a - b0 + width]
        v1 = w_ref[0, :, b0 + LANES:min(b0 + 2 * LANES, n_in)]
        return jnp.concatenate([v0[:, a - b0:], v1[:, :a + width - b0 - LANES]], axis=1)

    def block(pieces):
        cols = [take(a, width) for a, width in pieces]
        filled = sum(width for _, width in pieces)
        if filled < LANES:
            cols.append(jnp.zeros((w_ref.shape[1], LANES - filled), F32))
        return cols[0] if len(cols) == 1 else jnp.concatenate(cols, axis=1)

    for d, pieces in enumerate(plan_t):
        wt_ref[d * LANES:(d + 1) * LANES, :] = block(pieces).T.astype(BF16)
    for d, pieces in enumerate(plan_r):
        wr_ref[:, d * LANES:(d + 1) * LANES] = block(pieces).astype(BF16)


def _prep_w_in(w_in, layer, tr=256):
    _, D, n_in = w_in.shape
    plan_t, plan_r = _in_proj_plan(T_GROUP), _in_proj_plan(R_GROUP)
    return pl.pallas_call(
        functools.partial(_wprep_kernel, plan_t=plan_t, plan_r=plan_r, n_in=n_in),
        grid=(D // tr,),
        in_specs=[pl.BlockSpec((1, tr, n_in), lambda i: (layer, i, 0))],
        out_specs=[pl.BlockSpec((N_T * LANES, tr), lambda i: (0, i)),
                   pl.BlockSpec((tr, N_R * LANES), lambda i: (i, 0))],
        out_shape=[jax.ShapeDtypeStruct((N_T * LANES, D), BF16), jax.ShapeDtypeStruct((D, N_R * LANES), BF16)],
        compiler_params=_cparams(("parallel",)),
        name="w_in_layout",
    )(w_in)


def _norm_rope_t(y, gain, cos8, sin8):
    half = ROPE_DIM // 2
    parts = []
    for hh in range(2):
        yh = y[hh * HEAD_DIM:(hh + 1) * HEAD_DIM]
        ms = jnp.sum(yh * yh, axis=0, keepdims=True) * (1.0 / HEAD_DIM)
        z = (yh * lax.rsqrt(ms + EPS)) * gain[hh * HEAD_DIM:(hh + 1) * HEAD_DIM]
        x1, x2 = z[:half], z[half:ROPE_DIM]
        parts += [x1 * cos8 - x2 * sin8, x2 * cos8 + x1 * sin8, z[ROPE_DIM:]]
    return jnp.concatenate(parts, axis=0)


def _inproj_kernel(x_ref, ln_ref, wt_ref, wr_ref, gt_ref, gr_ref, cos_ref, sp_ref, sm_ref, ct_ref, st_ref,
                   pbt_ref, pb_ref, pf_ref, *, tm):
    x = x_ref[0]
    h = x * lax.rsqrt(jnp.mean(x * x, axis=-1, keepdims=True) + EPS)
    h = (h * ln_ref[...]).astype(BF16)
    lo = lax.broadcasted_iota(jnp.int32, (tm, LANES), 1) < HEAD_DIM
    cos_t, sin_p, sin_m = cos_ref[...], sp_ref[...], sm_ref[...]
    for c0 in range(0, N_R, CHUNK):
        y4 = _dot(h, wr_ref[:, c0 * LANES:(c0 + CHUNK) * LANES])
        for c in range(c0, c0 + CHUNK):
            y = y4[:, (c - c0) * LANES:(c - c0 + 1) * LANES]
            if c < N_R_NORM:
                y = _norm_rope(y, gr_ref[:, c * LANES:(c + 1) * LANES], cos_t, sin_p, sin_m, lo)
            if c < N_R_BF16:
                d = _PB_DST[c]
                pb_ref[0, :, d * LANES:(d + 1) * LANES] = y.astype(BF16)
            else:
                d = c - N_R_BF16
                pf_ref[0, :, d * LANES:(d + 1) * LANES] = y
    cos8, sin8 = ct_ref[...], st_ref[...]
    for t0, t1 in ((0, N_T_NORM), (N_T_NORM, N_T)):
        y_t = lax.dot_general(wt_ref[t0 * LANES:t1 * LANES, :], h, (((1,), (1,)), ((), ())),
                              preferred_element_type=F32)
        for t in range(t0, t1):
            y = y_t[(t - t0) * LANES:(t - t0 + 1) * LANES]
            if t < N_T_NORM:
                gain = jnp.concatenate([gt_ref[t * LANES:(t + 1) * LANES, :]] * (tm // LANES), axis=1)
                y = _norm_rope_t(y, gain, cos8, sin8)
            y = y.astype(BF16)
            for u in range(tm // TQ):
                pbt_ref[0, _PBT_DST[t], u] = y[:, u * TQ:(u + 1) * TQ]


def _in_proj(x, ln, w_t, w_r, gain_t, gain_r, cos_t, sin_p, sin_m, cos8, sin8, tm=1024):
    B, S, D = x.shape
    nt = S // tm
    half = ROPE_DIM // 2
    return pl.pallas_call(
        functools.partial(_inproj_kernel, tm=tm),
        grid=(B * nt,),
        in_specs=[
            pl.BlockSpec((1, tm, D), lambda i: (i // nt, i % nt, 0)),
            _const_spec((1, D)),
            _const_spec(w_t.shape), _const_spec(w_r.shape), _const_spec(gain_t.shape), _const_spec(gain_r.shape),
            pl.BlockSpec((tm, LANES), lambda i: (i % nt, 0)),
            pl.BlockSpec((tm, LANES), lambda i: (i % nt, 0)),
            pl.BlockSpec((tm, LANES), lambda i: (i % nt, 0)),
            pl.BlockSpec((half, tm), lambda i: (0, i % nt)),
            pl.BlockSpec((half, tm), lambda i: (0, i % nt)),
        ],
        out_specs=[
            pl.BlockSpec((1, N_PBT, tm // TQ, LANES, TQ), lambda i: (i // nt, 0, i % nt, 0, 0)),
            pl.BlockSpec((1, tm, N_PB * LANES), lambda i: (i // nt, i % nt, 0)),
            pl.BlockSpec((1, tm, N_F32 * LANES), lambda i: (i // nt, i % nt, 0)),
        ],
        out_shape=[
            jax.ShapeDtypeStruct((B, N_PBT, S // TQ, LANES, TQ), BF16),
            jax.ShapeDtypeStruct((B, S, N_PB * LANES), BF16),
            jax.ShapeDtypeStruct((B, S, N_F32 * LANES), F32),
        ],
        compiler_params=_cparams(("parallel",)),
        name="in_proj",
    )(x, ln, w_t, w_r, gain_t, gain_r, cos_t, sin_p, sin_m, cos8, sin8)


def _moba_kernel(qt_ref, k_ref, vt_ref, o_ref, q2_ref, bias_ref, m_ref, l_ref, acc_ref, kmean_ref, *, n_blk):
    n_pairs = H_MOBA // 2
    nidx = lax.broadcasted_iota(jnp.int32, (16, 2 * TQ), 0)
    kr = lax.broadcasted_iota(jnp.int32, (TQ, TQ), 0)
    ql = lax.broadcasted_iota(jnp.int32, (TQ, TQ), 1)
    causal_bias = jnp.where(kr <= ql, 0.0, NEG)
    for p in range(n_pairs):
        ps = slice(p * LANES, (p + 1) * LANES)
        rows = [jnp.sum(k_ref[0, n * TQ:(n + 1) * TQ, ps].astype(F32), axis=0, keepdims=True)
                for n in range(n_blk)]
        kmean_ref[p] = (jnp.concatenate(rows + [jnp.zeros((16 - n_blk, LANES), F32)], axis=0)
                        * (1.0 / MOBA_BLOCK))
    lax.fori_loop(0, n_blk, functools.partial(_moba_tile, qt_ref, k_ref, vt_ref, o_ref, q2_ref, bias_ref, m_ref,
                                              l_ref, acc_ref, kmean_ref, nidx, causal_bias, n_blk), 0)


def _moba_tile(qt_ref, k_ref, vt_ref, o_ref, q2_ref, bias_ref, m_ref, l_ref, acc_ref, kmean_ref, nidx,
               causal_bias, n_blk, qi, carry):
    n_pairs = H_MOBA // 2
    past = nidx < qi
    for p in range(n_pairs):
        km_hi, km_lo = _split_bf16(kmean_ref[p])
        q2 = jnp.concatenate([_half_rows(qt_ref[0, p, qi], hh) for hh in range(2)], axis=1)
        q2_ref[p] = q2
        gate = _dot(km_hi, q2) + _dot(km_lo, q2)
        for n in range(n_blk):
            row = gate[n:n + 1, :]
            ge = jnp.where(gate >= row, 1.0, 0.0)
            gt = jnp.where(gate > row, 1.0, 0.0)
            beats = jnp.where(past, jnp.where(nidx < n, ge, gt), 0.0)
            cnt = jnp.sum(beats, axis=0, keepdims=True)
            bias_ref[p, n] = jnp.where(cnt < MOBA_TOPK, 0.0, NEG)
        _online_init(m_ref, l_ref, acc_ref, p)

    def scores(off, p):
        return _dot(k_ref[0, pl.ds(off, TQ), p * LANES:(p + 1) * LANES], q2_ref[p])

    def pv(p, kt):
        def fn(pr):
            return jnp.concatenate([_dot(vt_ref[0, p, kt, :HEAD_DIM, :], pr[:, :TQ]),
                                    _dot(vt_ref[0, p, kt, HEAD_DIM:, :], pr[:, TQ:])], axis=1)
        return fn

    def past_scores(kt):
        off = pl.multiple_of(kt * TQ, TQ)
        return [scores(off, p) + bias_ref[p, kt] for p in range(n_pairs)]

    def past_tile(kt, s):
        for p in range(n_pairs):
            _online_update(m_ref, l_ref, acc_ref, p, s[p], pv(p, kt))

    def past_pair(i, _):
        s_a, s_b = past_scores(2 * i), past_scores(2 * i + 1)
        past_tile(2 * i, s_a)
        past_tile(2 * i + 1, s_b)
        return 0

    lax.fori_loop(0, qi // 2, past_pair, 0)

    @pl.when(qi % 2 == 1)
    def _():
        past_tile(qi - 1, past_scores(qi - 1))
    off = pl.multiple_of(qi * TQ, TQ)
    causal2 = jnp.concatenate([causal_bias, causal_bias], axis=1)
    s = [scores(off, p) + causal2 for p in range(n_pairs)]
    for p in range(n_pairs):
        _online_update(m_ref, l_ref, acc_ref, p, s[p], pv(p, qi))
        o2 = acc_ref[p] / jnp.maximum(l_ref[p], TINY)
        o_ref[0, pl.ds(off, TQ), p * LANES:(p + 1) * LANES] = jnp.concatenate(
            [o2[:, :TQ], o2[:, TQ:]], axis=0).T.astype(BF16)
    return carry


def _moba(pbt, pb):
    B, _, n_t, _, _ = pbt.shape
    S = pb.shape[1]
    n_pairs = H_MOBA // 2
    assert TQ == MOBA_BLOCK and n_t == S // MOBA_BLOCK
    return pl.pallas_call(
        functools.partial(_moba_kernel, n_blk=S // MOBA_BLOCK),
        grid=(B,),
        in_specs=[
            pl.BlockSpec((1, n_pairs, n_t, LANES, TQ), lambda b: (b, PBT_QA // n_pairs, 0, 0, 0)),
            pl.BlockSpec((1, S, QKV_A), lambda b: (b, 0, PB_KA // n_pairs)),
            pl.BlockSpec((1, n_pairs, n_t, LANES, TQ), lambda b: (b, PBT_VA // n_pairs, 0, 0, 0)),
        ],
        out_specs=pl.BlockSpec((1, S, QKV_A), lambda b: (b, 0, 0)),
        out_shape=jax.ShapeDtypeStruct((B, S, QKV_A), BF16),
        scratch_shapes=[pltpu.VMEM((n_pairs, LANES, 2 * TQ), BF16),
                        pltpu.VMEM((n_pairs, S // MOBA_BLOCK, 1, 2 * TQ), F32),
                        pltpu.VMEM((n_pairs, 1, 2 * TQ), F32), pltpu.VMEM((n_pairs, 1, 2 * TQ), F32),
                        pltpu.VMEM((n_pairs, HEAD_DIM, 2 * TQ), F32),
                        pltpu.VMEM((n_pairs, 16, LANES), F32)],
        compiler_params=_cparams(("parallel",)),
        name="moba",
    )(pbt, pb, pbt)


def _compress_kernel(kc_ref, vc_ref, pek_ref, pev_ref, w1k_ref, w2k_ref, w1v_ref, w2v_ref, gain_ref,
                     cos_ref, sp_ref, sm_ref, ko_ref, vo_ref, *, n_cmp):
    half_len = NSA_CMP_LEN // 2

    def mlp(t_ref, pe_ref, w1_ref, w2_ref):
        first = jnp.zeros((n_cmp, 2 * NSA_CMP_HIDDEN), F32)
        second = jnp.zeros((n_cmp, 2 * NSA_CMP_HIDDEN), F32)
        for l in range(half_len):
            a = t_ref[0, pl.ds(l, n_cmp, stride=NSA_CMP_STRIDE), :]
            first = first + jnp.dot((a + pe_ref[l:l + 1, :]).astype(BF16), w1_ref[l],
                                    preferred_element_type=F32)
            second = second + jnp.dot((a + pe_ref[half_len + l:half_len + l + 1, :]).astype(BF16),
                                      w1_ref[half_len + l], preferred_element_type=F32)
        hid = jax.nn.gelu(first + pltpu.roll(second, n_cmp - 1, 0))
        return jnp.dot(hid.astype(BF16), w2_ref[...], preferred_element_type=F32)

    valid = lax.broadcasted_iota(jnp.int32, (n_cmp, LANES), 0) < n_cmp - 1
    lo = lax.broadcasted_iota(jnp.int32, (n_cmp, LANES), 1) < HEAD_DIM
    kc = mlp(kc_ref, pek_ref, w1k_ref, w2k_ref)
    kc = _norm_rope(kc, gain_ref[...], cos_ref[...], sp_ref[...], sm_ref[...], lo)
    ko_ref[0] = jnp.where(valid, kc, 0.0).astype(BF16)
    vc = mlp(vc_ref, pev_ref, w1v_ref, w2v_ref)
    vo_ref[0] = jnp.where(valid, vc, 0.0).T.astype(BF16)


def _compress(pf, pe_k, pe_v, w1k, w2k, w1v, w2v, gain, cos_c, sp_c, sm_c):
    B, S, _ = pf.shape
    n_cmp = S // NSA_CMP_STRIDE
    cs = _const_spec
    return pl.pallas_call(
        functools.partial(_compress_kernel, n_cmp=n_cmp),
        grid=(B,),
        in_specs=[
            pl.BlockSpec((1, S, LANES), lambda b: (b, 0, 0)),
            pl.BlockSpec((1, S, LANES), lambda b: (b, 0, 1)),
            cs(pe_k.shape), cs(pe_v.shape), cs(w1k.shape), cs(w2k.shape), cs(w1v.shape), cs(w2v.shape),
            cs(gain.shape), cs(cos_c.shape), cs(sp_c.shape), cs(sm_c.shape),
        ],
        out_specs=[pl.BlockSpec((1, n_cmp, LANES), lambda b: (b, 0, 0)),
                   pl.BlockSpec((1, LANES, n_cmp), lambda b: (b, 0, 0))],
        out_shape=[jax.ShapeDtypeStruct((B, n_cmp, LANES), BF16),
                   jax.ShapeDtypeStruct((B, LANES, n_cmp), BF16)],
        compiler_params=_cparams(("parallel",)),
        name="nsa_compress",
    )(pf, pf, pe_k, pe_v, w1k, w2k, w1v, w2v, gain, cos_c, sp_c, sm_c)


def _nsa_kernel(*refs, n_cmp, n_slc, n_t):
    lax.fori_loop(0, n_t, functools.partial(_nsa_tile, refs, n_cmp, n_slc), 0)


def _nsa_tile(refs, n_cmp, n_slc, qi, carry):
    (qt_ref, kc_ref, vct_ref, ks_ref, vst_ref, kw_ref, vwt_ref, g_ref, ovl_ref, o_ref,
     q6_ref, impf_ref, improw_ref, m_ref, l_ref, acc_ref, ot_ref) = refs
    blk_per_tile = TQ // NSA_SEL_BLOCK
    off = pl.multiple_of(qi * TQ, TQ)
    tpos = qi * TQ + lax.broadcasted_iota(jnp.int32, (1, TQ), 1)
    sig = jax.nn.sigmoid(g_ref[0, pl.ds(off, TQ), :].T)
    kr = lax.broadcasted_iota(jnp.int32, (TQ, TQ), 0)
    ql = lax.broadcasted_iota(jnp.int32, (TQ, TQ), 1)
    causal_bias = jnp.where(kr <= ql, 0.0, NEG)
    nidx = lax.broadcasted_iota(jnp.int32, (n_cmp, TQ), 0)
    cmask = (nidx * NSA_CMP_STRIDE + (NSA_CMP_LEN - 1) <= tpos) & (nidx < n_cmp - 1)
    cmp_bias = jnp.where(cmask, 0.0, NEG)
    jj = lax.broadcasted_iota(jnp.int32, (n_slc, TQ), 0)
    cur = lax.shift_right_arithmetic(tpos, NSA_SEL_BLOCK.bit_length() - 1)
    forced = (jj == 0) | (jj == cur) | (jj == cur - 1)
    gw = NSA_GROUP * TQ

    def half(kk):
        return slice(kk * HEAD_DIM, (kk + 1) * HEAD_DIM)

    def cols(kk):
        return slice(kk * gw, (kk + 1) * gw)

    def tile3(a):
        return jnp.concatenate([a] * NSA_GROUP, axis=1)

    def gate3(kk, branch):
        return jnp.concatenate([sig[3 * (NSA_GROUP * kk + j) + branch:3 * (NSA_GROUP * kk + j) + branch + 1, :]
                                for j in range(NSA_GROUP)], axis=1)

    q6 = jnp.concatenate([_half_rows(qt_ref[0, j, qi], kk) for kk in range(H_NSA_KV) for j in range(NSA_GROUP)],
                         axis=1)
    q6_ref[...] = q6

    s_cmp = _dot(kc_ref[0], q6)
    cmp_bias3 = tile3(cmp_bias)
    for kk in range(H_NSA_KV):
        s = s_cmp[:, cols(kk)] + cmp_bias3
        m = jnp.maximum(jnp.max(s, axis=0, keepdims=True), M_FLOOR)
        p = jnp.exp2(s - m)
        p = p / jnp.maximum(jnp.sum(p, axis=0, keepdims=True), TINY)
        ot_ref[kk] = gate3(kk, 0) * _dot(vct_ref[0, half(kk), :], p.astype(BF16))
        p_sum = (p[:, :TQ] + p[:, TQ:2 * TQ]) + p[:, 2 * TQ:]
        _online_init(m_ref, l_ref, acc_ref, kk)
        p_hi, p_lo = _split_bf16(p_sum)
        imp = _dot(ovl_ref[...], p_hi) + _dot(ovl_ref[...], p_lo)
        imp = jnp.where(jj > cur, NEG, jnp.where(forced, FORCE, imp))
        impf_ref[kk] = imp
        for j in range(n_slc):
            improw_ref[kk, j] = imp[j:j + 1, :]

    def sel_bias(kk, kt):
        imp = impf_ref[kk]
        parts = []
        for c in range(blk_per_tile):
            j = kt * blk_per_tile + c
            row = improw_ref[kk, j]
            ge = jnp.where(imp >= row, 1.0, 0.0)
            gt = jnp.where(imp > row, 1.0, 0.0)
            cnt = jnp.sum(jnp.where(jj < j, ge, gt), axis=0, keepdims=True)
            parts.append(jnp.broadcast_to(jnp.where(cnt < NSA_N_SEL, 0.0, NEG), (NSA_SEL_BLOCK, TQ)))
        return jnp.concatenate(parts, axis=0)

    def sel_pv(kk, kt):
        return lambda pr: _dot(vst_ref[0, 0, kt, half(kk), :], pr)

    def sel_scores(kt):
        off = pl.multiple_of(kt * TQ, TQ)
        return _dot(ks_ref[0, pl.ds(off, TQ), :], q6_ref[...])

    def sel_tile(kt, s6):
        for kk in range(H_NSA_KV):
            s = s6[:, cols(kk)] + tile3(sel_bias(kk, kt))
            _online_update(m_ref, l_ref, acc_ref, kk, s, sel_pv(kk, kt))

    def sel_pair(i, _):
        s_a, s_b = sel_scores(2 * i), sel_scores(2 * i + 1)
        sel_tile(2 * i, s_a)
        sel_tile(2 * i + 1, s_b)
        return 0

    lax.fori_loop(0, qi // 2, sel_pair, 0)

    @pl.when(qi % 2 == 1)
    def _():
        sel_tile(qi - 1, sel_scores(qi - 1))

    kt1 = jnp.maximum(qi - 1, 0)
    kt2 = jnp.maximum(qi - 2, 0)
    ks_d = ks_ref[0, pl.ds(off, TQ), :]
    kw_0 = kw_ref[0, pl.ds(off, TQ), :]
    kw_1 = kw_ref[0, pl.ds(pl.multiple_of(kt1 * TQ, TQ), TQ), :]
    kw_2 = kw_ref[0, pl.ds(pl.multiple_of(kt2 * TQ, TQ), TQ), :]
    win_bias2 = jnp.where(kr > ql + jnp.where(qi >= 2, 0, TQ), 0.0, NEG)
    win_bias1 = jnp.where(qi >= 1, 0.0, NEG)
    q6 = q6_ref[...]
    sd6, w2, w1, w0 = _dot(ks_d, q6), _dot(kw_2, q6), _dot(kw_1, q6), _dot(kw_0, q6)
    causal3 = tile3(causal_bias)
    win_bias2 = tile3(win_bias2)
    for kk in range(H_NSA_KV):
        s = sd6[:, cols(kk)] + tile3(sel_bias(kk, qi) + causal_bias)
        _online_update(m_ref, l_ref, acc_ref, kk, s, sel_pv(kk, qi))
        o_sel = acc_ref[kk] / jnp.maximum(l_ref[kk], TINY)
        s2 = w2[:, cols(kk)] + win_bias2
        s1 = w1[:, cols(kk)] + win_bias1
        s0 = w0[:, cols(kk)] + causal3
        m = jnp.maximum(jnp.maximum(jnp.max(s2, axis=0, keepdims=True), jnp.max(s1, axis=0, keepdims=True)),
                        jnp.max(s0, axis=0, keepdims=True))
        p2, p1, p0 = jnp.exp2(s2 - m), jnp.exp2(s1 - m), jnp.exp2(s0 - m)
        den = (jnp.sum(p2, axis=0, keepdims=True) + jnp.sum(p1, axis=0, keepdims=True)
               + jnp.sum(p0, axis=0, keepdims=True))
        acc = (_dot(vwt_ref[0, 0, kt2, half(kk), :], p2.astype(BF16))
               + _dot(vwt_ref[0, 0, kt1, half(kk), :], p1.astype(BF16))
               + _dot(vwt_ref[0, 0, qi, half(kk), :], p0.astype(BF16)))
        o_win = acc / jnp.maximum(den, TINY)
        ot_ref[kk] = (ot_ref[kk] + gate3(kk, 1) * o_sel) + gate3(kk, 2) * o_win
    for j in range(NSA_GROUP):
        js = slice(j * TQ, (j + 1) * TQ)
        o_ref[0, pl.ds(off, TQ), j * LANES:(j + 1) * LANES] = jnp.concatenate(
            [ot_ref[0, :, js], ot_ref[1, :, js]], axis=0).T.astype(BF16)
    return carry


def _nsa(pbt, pb, pf, kcmp, vcmp_t, ovl_t):
    B, _, n_t, _, _ = pbt.shape
    S = pb.shape[1]
    n_cmp = kcmp.shape[1]
    n_slc = S // NSA_SEL_BLOCK
    gw = NSA_GROUP * TQ
    return pl.pallas_call(
        functools.partial(_nsa_kernel, n_cmp=n_cmp, n_slc=n_slc, n_t=n_t),
        grid=(B,),
        in_specs=[
            pl.BlockSpec((1, NSA_GROUP, n_t, LANES, TQ), lambda b: (b, PBT_QB // NSA_GROUP, 0, 0, 0)),
            pl.BlockSpec((1, n_cmp, LANES), lambda b: (b, 0, 0)),
            pl.BlockSpec((1, LANES, n_cmp), lambda b: (b, 0, 0)),
            pl.BlockSpec((1, S, LANES), lambda b: (b, 0, PB_KS)),
            pl.BlockSpec((1, 1, n_t, LANES, TQ), lambda b: (b, PBT_VS, 0, 0, 0)),
            pl.BlockSpec((1, S, LANES), lambda b: (b, 0, PB_KW)),
            pl.BlockSpec((1, 1, n_t, LANES, TQ), lambda b: (b, PBT_VW, 0, 0, 0)),
            pl.BlockSpec((1, S, LANES), lambda b: (b, 0, 2)),
            _const_spec(ovl_t.shape),
        ],
        out_specs=pl.BlockSpec((1, S, Q_B), lambda b: (b, 0, 0)),
        out_shape=jax.ShapeDtypeStruct((B, S, Q_B), BF16),
        scratch_shapes=[pltpu.VMEM((LANES, H_NSA * TQ), BF16),
                        pltpu.VMEM((H_NSA_KV, n_slc, TQ), F32), pltpu.VMEM((H_NSA_KV, n_slc, 1, TQ), F32),
                        pltpu.VMEM((H_NSA_KV, 1, gw), F32), pltpu.VMEM((H_NSA_KV, 1, gw), F32),
                        pltpu.VMEM((H_NSA_KV, HEAD_DIM, gw), F32),
                        pltpu.VMEM((H_NSA_KV, HEAD_DIM, gw), F32)],
        compiler_params=_cparams(("parallel",)),
        name="nsa_attn",
    )(pbt, kcmp, vcmp_t, pb, pbt, pb, pbt, pf, ovl_t)


DIL_TILES = 8

def _dilated_kernel(q_ref, k_ref, v_ref, o_ref, qf_ref, kf_ref, vf_ref, os_ref, ls_ref, *, seq):
    blk = BAND_BLOCK
    n_tiles = seq // blk
    lo = lax.broadcasted_iota(jnp.int32, (blk, LANES), 1) < HEAD_DIM
    for g, (window, dil) in enumerate(DIL_CFG):
        max_dist = window // dil
        tiles_per_class = n_tiles // dil
        use_prev = tiles_per_class > 1
        n_keys = 2 * blk if use_prev else blk
        ri = lax.broadcasted_iota(jnp.int32, (2 * blk, n_keys), 0) & (blk - 1)
        ci = lax.broadcasted_iota(jnp.int32, (2 * blk, n_keys), 1)
        dist = ri + (n_keys - blk) - ci
        band = (dist >= 0) & (dist <= max_dist)
        band_bias = jnp.where(band, 0.0, NEG)
        qf_ref[g] = q_ref[0, :, g * LANES:(g + 1) * LANES].astype(F32)
        kf_ref[g] = k_ref[0, :, g * LANES:(g + 1) * LANES].astype(F32)
        vf_ref[g] = v_ref[0, :, g * LANES:(g + 1) * LANES].astype(F32)

        def index(start, dil=dil):
            if dil == 1:
                return pl.ds(pl.multiple_of(start, blk), blk)
            return pl.ds(start, blk, stride=dil)

        def tiles(i, _, g=g, dil=dil, tiles_per_class=tiles_per_class, use_prev=use_prev, band=band,
                  band_bias=band_bias, ci=ci, index=index):
            scored = []
            for u in range(DIL_TILES):
                t = i * DIL_TILES + u
                cls = t // tiles_per_class
                mt = t % tiles_per_class
                start = cls + dil * blk * mt
                q = qf_ref[g, index(start), :]
                q2 = jnp.concatenate([jnp.where(lo, q, 0.0), jnp.where(lo, 0.0, q)], axis=0).astype(BF16)
                if use_prev:
                    prev = jnp.maximum(start - dil * blk, cls)
                    k2 = jnp.concatenate([kf_ref[g, index(prev), :], kf_ref[g, index(start), :]], axis=0)
                    v2 = jnp.concatenate([vf_ref[g, index(prev), :], vf_ref[g, index(start), :]], axis=0)
                    bias = jnp.where(band & (ci >= jnp.where(mt > 0, 0, blk)), 0.0, NEG)
                else:
                    k2, v2, bias = kf_ref[g, index(start), :], vf_ref[g, index(start), :], band_bias
                s = lax.dot_general(q2, k2.astype(BF16), (((1,), (1,)), ((), ())), preferred_element_type=F32)
                scored.append((start, s + bias, v2.astype(BF16)))
            probs = []
            for start, s, v2 in scored:
                m = jnp.max(s, axis=-1, keepdims=True)
                p = jnp.exp2(s - m)
                den = jnp.sum(p, axis=-1, keepdims=True)
                lse = m + jnp.log2(jnp.maximum(den, TINY))
                probs.append((start, lse, (p / jnp.maximum(den, TINY)).astype(BF16), v2))
            for start, lse, p, v2 in probs:
                o = _dot(p, v2)
                os_ref[g, index(start), :] = jnp.where(lo, o[:blk], o[blk:])
                ls_ref[g, index(start), :] = jnp.where(lo, lse[:blk], lse[blk:])
            return 0

        lax.fori_loop(0, n_tiles // DIL_TILES, tiles, 0)

    rows_per_trip = DIL_TILES * blk

    def combine(t, _):
        idx = pl.ds(pl.multiple_of(t * rows_per_trip, rows_per_trip), rows_per_trip)
        lses = [ls_ref[g, idx, :] for g in range(len(DIL_CFG))]
        mx = jnp.maximum(jnp.maximum(lses[0], lses[1]), lses[2])
        es = [jnp.exp2(l - mx) for l in lses]
        tot = es[0] + es[1] + es[2]
        for g in range(len(DIL_CFG)):
            o_ref[0, idx, g * LANES:(g + 1) * LANES] = (os_ref[g, idx, :] * (es[g] / tot)).astype(BF16)
        return 0

    lax.fori_loop(0, seq // rows_per_trip, combine, 0)


def _dilated(pb):
    B, S, _ = pb.shape
    n_g = len(DIL_CFG)
    return pl.pallas_call(
        functools.partial(_dilated_kernel, seq=S),
        grid=(B,),
        in_specs=[pl.BlockSpec((1, S, QKV_C), lambda b: (b, 0, PB_QC // n_g)),
                  pl.BlockSpec((1, S, QKV_C), lambda b: (b, 0, PB_KC // n_g)),
                  pl.BlockSpec((1, S, QKV_C), lambda b: (b, 0, PB_VC // n_g))],
        out_specs=pl.BlockSpec((1, S, QKV_C), lambda b: (b, 0, 0)),
        out_shape=jax.ShapeDtypeStruct((B, S, QKV_C), BF16),
        scratch_shapes=[pltpu.VMEM((n_g, S, LANES), F32)] * 5,
        compiler_params=_cparams(("parallel",)),
        name="dilated_attn",
    )(pb, pb, pb)


HALO = 16
FF_CHUNK = 256


def _proj_ffn_kernel(x_ref, xh_ref, oa_ref, oah_ref, ob_ref, obh_ref, oc_ref, och_ref, wo_ref, ln_ref,
                     wg_ref, wu_ref, cw_ref, cb_ref, wd_ref, y_ref, o_ref, h_ref, x1_ref, a_ref,
                     *, tiles_per_seq):
    i = pl.program_id(0)

    def norm(v):
        return (v * lax.rsqrt(jnp.mean(v * v, axis=-1, keepdims=True) + EPS)) * ln_ref[...]

    col = 0
    for main, halo in ((oa_ref, oah_ref), (ob_ref, obh_ref), (oc_ref, och_ref)):
        width = main.shape[1]
        o_ref[:HALO, col:col + width] = halo[...]
        o_ref[HALO:, col:col + width] = main[...]
        col += width
    attn = _dot(o_ref[...], wo_ref[...])
    x1 = x_ref[...] + attn[HALO:]
    x1_ref[...] = x1
    first = (i % tiles_per_seq) == 0
    h_ref[:HALO, :] = jnp.where(first, 0.0, norm(xh_ref[...] + attn[:HALO])).astype(BF16)
    h_ref[HALO:, :] = norm(x1).astype(BF16)

    for f0 in range(0, D_FF, FF_CHUNK):
        fs = slice(f0, min(f0 + FF_CHUNK, D_FF))
        g = _dot(h_ref[...], wg_ref[:, fs])
        u = _dot(h_ref[HALO:, :], wu_ref[:, fs])
        gc = (cw_ref[0:1, fs] * pltpu.roll(g, 2, 0)[HALO:] + cw_ref[1:2, fs] * pltpu.roll(g, 1, 0)[HALO:]
              + cw_ref[2:3, fs] * g[HALO:] + cb_ref[:, fs])
        a_ref[:, fs] = (jax.nn.silu(gc) * u).astype(BF16)
    y_ref[...] = x1_ref[...] + _dot(a_ref[...], wd_ref[...])


def _proj_ffn(x2, o_a, o_b, o_c, wo, ln, wg, wu, cw, cb, wd, seq, tm=512):
    M, D = x2.shape
    halo_blocks = tm // HALO
    main = lambda w: pl.BlockSpec((tm, w), lambda i: (i, 0))
    halo = lambda w: pl.BlockSpec((HALO, w), lambda i: (jnp.maximum(i * halo_blocks - 1, 0), 0))
    return pl.pallas_call(
        functools.partial(_proj_ffn_kernel, tiles_per_seq=seq // tm),
        grid=(M // tm,),
        in_specs=[
            main(D), halo(D), main(QKV_A), halo(QKV_A), main(Q_B), halo(Q_B), main(QKV_C), halo(QKV_C),
            _const_spec(wo.shape), _const_spec((1, D)),
            _const_spec(wg.shape), _const_spec(wu.shape), _const_spec(cw.shape), _const_spec(cb.shape),
            _const_spec(wd.shape),
        ],
        out_specs=main(D),
        out_shape=jax.ShapeDtypeStruct((M, D), F32),
        scratch_shapes=[pltpu.VMEM((HALO + tm, D), BF16), pltpu.VMEM((HALO + tm, D), BF16),
                        pltpu.VMEM((tm, D), F32), pltpu.VMEM((tm, D_FF), BF16)],
        compiler_params=_cparams(("parallel",)),
        name="proj_ffn",
    )(x2, x2, o_a, o_a, o_b, o_b, o_c, o_c, wo, ln, wg, wu, cw, cb, wd)


def _rope_tables(pos):
    half = ROPE_DIM // 2
    inv_freq = ROPE_THETA ** (-jnp.arange(half, dtype=F32) * 2.0 / ROPE_DIM)
    ang = pos.astype(F32)[:, None] * inv_freq
    c, s = jnp.cos(ang), jnp.sin(ang)
    n = pos.shape[0]
    rest = HEAD_DIM - ROPE_DIM
    cos_h = jnp.concatenate([c, c, jnp.ones((n, rest), F32)], axis=-1)
    sp_h = jnp.concatenate([jnp.zeros((n, half), F32), s, jnp.zeros((n, rest), F32)], axis=-1)
    sm_h = jnp.concatenate([-s, jnp.zeros((n, half + rest), F32)], axis=-1)
    two = lambda t: jnp.concatenate([t, t], axis=-1)
    return two(cos_h), two(sp_h), two(sm_h), c.T, s.T


def _pair_diag(w):
    z = jnp.zeros_like(w)
    return jnp.concatenate([jnp.concatenate([w, z], axis=-1), jnp.concatenate([z, w], axis=-1)], axis=-2)


def _overlap_t(seq):
    n_cmp = seq // NSA_CMP_STRIDE
    n_slc = seq // NSA_SEL_BLOCK
    starts = np.arange(n_cmp) * NSA_CMP_STRIDE
    j = np.arange(n_slc)
    ovl = (starts[None, :] < (j[:, None] + 1) * NSA_SEL_BLOCK) & (starts[None, :] + NSA_CMP_LEN > j[:, None] * NSA_SEL_BLOCK)
    ovl[:, n_cmp - 1] = False
    return jnp.asarray(ovl, BF16)


def _mixers(x, seq_tabs, cmp_tabs, ovl_t, ln1, w, qn_a, kn_a, qn_b, kn_b, pe_k, pe_v, wk1, wk2, wv1, wv2,
            qn_c, kn_c):
    rep = lambda g, n: jnp.tile(g, n)
    w_t, w_r = w
    gain_t = jnp.concatenate([rep(qn_a * Q_SCALE, H_MOBA), rep(qn_b * Q_SCALE, H_NSA)])
    gain_t = jnp.broadcast_to(gain_t[:, None], (gain_t.shape[0], LANES))
    gain_r = jnp.concatenate([rep(kn_a, H_MOBA), rep(kn_b[1], H_NSA_KV), rep(kn_b[2], H_NSA_KV),
                              rep(qn_c * Q_SCALE, H_DIL), rep(kn_c, H_DIL)])[None, :]
    pbt, pb, pf = _in_proj(x, ln1[None, :], w_t, w_r, gain_t, gain_r, *seq_tabs)

    o_a = _moba(pbt, pb)
    two = lambda t: jnp.concatenate([t, t], axis=-1)
    w1 = lambda t: _pair_diag(t.reshape(NSA_CMP_LEN, HEAD_DIM, NSA_CMP_HIDDEN)).astype(BF16)
    kcmp, vcmp_t = _compress(pf, two(pe_k), two(pe_v), w1(wk1), _pair_diag(wk2).astype(BF16),
                             w1(wv1), _pair_diag(wv2).astype(BF16), two(kn_b[0])[None, :], *cmp_tabs[:3])
    o_b = _nsa(pbt, pb, pf, kcmp, vcmp_t, ovl_t)
    o_c = _dilated(pb)
    return o_a, o_b, o_c


def _layer(x, seq_tabs, cmp_tabs, ovl_t, ln1, w_in, qn_a, kn_a, qn_b, kn_b, pe_k, pe_v, wk1, wk2, wv1, wv2,
           qn_c, kn_c, w_out, ln2, w_gate, w_up, conv_w, conv_b, w_down):
    B, S, D = x.shape
    o_a, o_b, o_c = _mixers(x, seq_tabs, cmp_tabs, ovl_t, ln1, w_in, qn_a, kn_a, qn_b, kn_b, pe_k, pe_v,
                            wk1, wk2, wv1, wv2, qn_c, kn_c)
    nsa_rows = [w_out[QKV_A + h * HEAD_DIM:QKV_A + (h + 1) * HEAD_DIM] for h in NSA_ORDER]
    w_o = jnp.concatenate([w_out[:QKV_A]] + nsa_rows + [w_out[QKV_A + Q_B:]], axis=0).astype(BF16)
    M = B * S
    x2 = _proj_ffn(x.reshape(M, D), o_a.reshape(M, QKV_A), o_b.reshape(M, Q_B), o_c.reshape(M, QKV_C), w_o,
                   ln2[None, :], w_gate.astype(BF16), w_up.astype(BF16), conv_w, conv_b[None, :],
                   w_down.astype(BF16), S)
    return x2.reshape(B, S, D)


def kernel(x, ln1, w_in, qn_a, kn_a, qn_b, kn_b, cmp_pe_k, cmp_pe_v, cmp_k_w1, cmp_k_w2, cmp_v_w1, cmp_v_w2,
           qn_c, kn_c, w_out, ln2, w_gate, w_up, conv_w, conv_b, w_down):
    S = x.shape[1]
    seq_tabs = _rope_tables(jnp.arange(S))
    n_cmp = S // NSA_CMP_STRIDE
    cmp_tabs = _rope_tables(jnp.arange(n_cmp) * NSA_CMP_STRIDE + (NSA_CMP_LEN - 1))
    ovl_t = _overlap_t(S)
    for l in range(ln1.shape[0]):
        x = _layer(x, seq_tabs, cmp_tabs, ovl_t, ln1[l], _prep_w_in(w_in, l), qn_a[l], kn_a[l], qn_b[l], kn_b[l],
                   cmp_pe_k[l], cmp_pe_v[l], cmp_k_w1[l], cmp_k_w2[l], cmp_v_w1[l], cmp_v_w2[l], qn_c[l], kn_c[l],
                   w_out[l], ln2[l], w_gate[l], w_up[l], conv_w[l], conv_b[l], w_down[l])
    return x
```

```python
import functools

import numpy as np
import jax
import jax.numpy as jnp
from jax import lax
from jax.experimental import pallas as pl
from jax.experimental.pallas import tpu as pltpu

F32 = jnp.float32
BF16 = jnp.bfloat16

D_MODEL = 1024
HEAD_DIM = 64
N_HEADS = D_MODEL // HEAD_DIM
H_MOBA = N_HEADS // 4
H_NSA = (N_HEADS - H_MOBA) // 2
H_DIL = N_HEADS - H_MOBA - H_NSA
H_NSA_KV = 2
NSA_GROUP = H_NSA // H_NSA_KV
ROPE_DIM = HEAD_DIM // 4
ROPE_THETA = 500000.0
MOBA_BLOCK = 256
MOBA_TOPK = 3
NSA_CMP_LEN = 32
NSA_CMP_STRIDE = 16
NSA_CMP_HIDDEN = 128
NSA_SEL_BLOCK = 64
NSA_N_SEL = 6
NSA_WINDOW = 512
DIL_CFG = ((128, 1), (512, 4), (2048, 16))
BAND_BLOCK = 128
D_FF = 2816
EPS = 1e-6
NEG = -1e30
TINY = 1e-30
FORCE = 1e9
SCALE = HEAD_DIM ** -0.5
Q_SCALE = SCALE * float(np.log2(np.e))
QKV_A = H_MOBA * HEAD_DIM
Q_B = H_NSA * HEAD_DIM
KV_B = H_NSA_KV * HEAD_DIM
G_B = H_NSA * 3
QKV_C = H_DIL * HEAD_DIM
IN_SPLITS = (QKV_A, QKV_A, QKV_A, Q_B, KV_B, KV_B, KV_B, KV_B, KV_B, KV_B, G_B, QKV_C, QKV_C, QKV_C)

LANES = 128
TQ = 256
NSA_ORDER = (0, 3, 1, 4, 2, 5)
VMEM_LIMIT = 56 * 1024 * 1024

M_FLOOR = -1e29

T_GROUP = ("qa", "qb", "va", "vsb", "vwb")
N_T_NORM, N_T = 5, 9
R_GROUP = ("ka", "ksb", "kwb", "qc", "kc", "vc", "kcb", "vcb", "gb")
N_R_NORM, N_R_BF16, N_R = 10, 13, 16
N_PBT, N_PB, N_F32 = N_T, N_R_BF16, N_R - N_R_BF16
PBT_QA, PBT_VA, PBT_VS, PBT_VW, PBT_QB = 0, 2, 4, 5, 6
PB_QC, PB_KC, PB_KA, PB_KS, PB_VC, PB_KW = 0, 3, 6, 8, 9, 12
_PBT_DST = (0, 1, 6, 7, 8, 2, 3, 4, 5)
_PB_DST = (6, 7, 8, 12, 0, 1, 2, 3, 4, 5, 9, 10, 11)
CHUNK = 4


def _cparams(sem):
    return pltpu.CompilerParams(dimension_semantics=sem, vmem_limit_bytes=VMEM_LIMIT)


def _const_spec(shape):
    return pl.BlockSpec(shape, lambda *_: (0,) * len(shape), pipeline_mode=pl.Buffered(1))


def _norm_rope(y, gain, cos_t, sin_p, sin_m, lo):
    ss = y * y
    s_lo = jnp.sum(jnp.where(lo, ss, 0.0), axis=-1, keepdims=True)
    s_hi = jnp.sum(jnp.where(lo, 0.0, ss), axis=-1, keepdims=True)
    ms = jnp.where(lo, s_lo, s_hi) * (1.0 / HEAD_DIM)
    z = (y * lax.rsqrt(ms + EPS)) * gain
    half = ROPE_DIM // 2
    return z * cos_t + pltpu.roll(z, half, 1) * sin_p + pltpu.roll(z, LANES - half, 1) * sin_m


def _online_update(m_ref, l_ref, acc_ref, g, s, pv):
    m_old = m_ref[g]
    m_new = jnp.maximum(m_old, jnp.max(s, axis=0, keepdims=True))
    p = jnp.exp2(s - m_new)
    alpha = jnp.exp2(m_old - m_new)
    l_ref[g] = alpha * l_ref[g] + jnp.sum(p, axis=0, keepdims=True)
    acc_ref[g] = alpha * acc_ref[g] + pv(p.astype(BF16))
    m_ref[g] = m_new


def _online_init(m_ref, l_ref, acc_ref, g):
    m_ref[g] = jnp.full(m_ref.shape[1:], M_FLOOR, F32)
    l_ref[g] = jnp.zeros(l_ref.shape[1:], F32)
    acc_ref[g] = jnp.zeros(acc_ref.shape[1:], F32)


def _dot(a, b):
    return jnp.dot(a, b, preferred_element_type=F32)


def _half_rows(q_t, hh):
    z = jnp.zeros((HEAD_DIM, q_t.shape[1]), q_t.dtype)
    if hh == 0:
        return jnp.concatenate([q_t[:HEAD_DIM], z], axis=0)
    return jnp.concatenate([z, q_t[HEAD_DIM:]], axis=0)


def _split_bf16(a):
    hi = a.astype(BF16)
    return hi, (a - hi.astype(F32)).astype(BF16)


def _in_proj_plan(group):
    names = ("qa", "ka", "va", "qb", "kcb", "vcb", "ksb", "vsb", "kwb", "vwb", "gb", "qc", "kc", "vc")
    start = dict(zip(names, np.cumsum((0,) + IN_SPLITS[:-1]).tolist()))
    width = dict(zip(names, IN_SPLITS))
    plan = []
    for name in group:
        if name == "qb":
            heads = [start[name] + h * HEAD_DIM for h in NSA_ORDER]
            plan += [[(heads[2 * j], HEAD_DIM), (heads[2 * j + 1], HEAD_DIM)] for j in range(H_NSA // 2)]
        else:
            plan += [[(start[name] + c, min(LANES, width[name] - c))] for c in range(0, width[name], LANES)]
    return plan


def _wprep_kernel(w_ref, wt_ref, wr_ref, *, plan_t, plan_r, n_in):
    def take(a, width):
        b0 = a // LANES * LANES
        v0 = w_ref[0, :, b0:min(b0 + LANES, n_in)]
        if a + width <= b0 + LANES:
            return v0[:, a - b0:a - b0 + width]
        v1 = w_ref[0, :, b0 + LANES:min(b0 + 2 * LANES, n_in)]
        return jnp.concatenate([v0[:, a - b0:], v1[:, :a + width - b0 - LANES]], axis=1)

    def block(pieces):
        cols = [take(a, width) for a, width in pieces]
        filled = sum(width for _, width in pieces)
        if filled < LANES:
            cols.append(jnp.zeros((w_ref.shape[1], LANES - filled), F32))
        return cols[0] if len(cols) == 1 else jnp.concatenate(cols, axis=1)

    for d, pieces in enumerate(plan_t):
        wt_ref[d * LANES:(d + 1) * LANES, :] = block(pieces).T.astype(BF16)
    for d, pieces in enumerate(plan_r):
        wr_ref[:, d * LANES:(d + 1) * LANES] = block(pieces).astype(BF16)


def _prep_w_in(w_in, layer, tr=256):
    _, D, n_in = w_in.shape
    plan_t, plan_r = _in_proj_plan(T_GROUP), _in_proj_plan(R_GROUP)
    return pl.pallas_call(
        functools.partial(_wprep_kernel, plan_t=plan_t, plan_r=plan_r, n_in=n_in),
        grid=(D // tr,),
        in_specs=[pl.BlockSpec((1, tr, n_in), lambda i: (layer, i, 0))],
        out_specs=[pl.BlockSpec((N_T * LANES, tr), lambda i: (0, i)),
                   pl.BlockSpec((tr, N_R * LANES), lambda i: (i, 0))],
        out_shape=[jax.ShapeDtypeStruct((N_T * LANES, D), BF16), jax.ShapeDtypeStruct((D, N_R * LANES), BF16)],
        compiler_params=_cparams(("parallel",)),
        name="w_in_layout",
    )(w_in)


def _norm_rope_t(y, gain, cos8, sin8):
    half = ROPE_DIM // 2
    parts = []
    for hh in range(2):
        yh = y[hh * HEAD_DIM:(hh + 1) * HEAD_DIM]
        ms = jnp.sum(yh * yh, axis=0, keepdims=True) * (1.0 / HEAD_DIM)
        z = (yh * lax.rsqrt(ms + EPS)) * gain[hh * HEAD_DIM:(hh + 1) * HEAD_DIM]
        x1, x2 = z[:half], z[half:ROPE_DIM]
        parts += [x1 * cos8 - x2 * sin8, x2 * cos8 + x1 * sin8, z[ROPE_DIM:]]
    return jnp.concatenate(parts, axis=0)


def _inproj_kernel(x_ref, ln_ref, wt_ref, wr_ref, gt_ref, gr_ref, cos_ref, sp_ref, sm_ref, ct_ref, st_ref,
                   pbt_ref, pb_ref, pf_ref, *, tm):
    x = x_ref[0]
    h = x * lax.rsqrt(jnp.mean(x * x, axis=-1, keepdims=True) + EPS)
    h = (h * ln_ref[...]).astype(BF16)
    lo = lax.broadcasted_iota(jnp.int32, (tm, LANES), 1) < HEAD_DIM
    cos_t, sin_p, sin_m = cos_ref[...], sp_ref[...], sm_ref[...]
    for c0 in range(0, N_R, CHUNK):
        y4 = _dot(h, wr_ref[:, c0 * LANES:(c0 + CHUNK) * LANES])
        for c in range(c0, c0 + CHUNK):
            y = y4[:, (c - c0) * LANES:(c - c0 + 1) * LANES]
            if c < N_R_NORM:
                y = _norm_rope(y, gr_ref[:, c * LANES:(c + 1) * LANES], cos_t, sin_p, sin_m, lo)
            if c < N_R_BF16:
                d = _PB_DST[c]
                pb_ref[0, :, d * LANES:(d + 1) * LANES] = y.astype(BF16)
            else:
                d = c - N_R_BF16
                pf_ref[0, :, d * LANES:(d + 1) * LANES] = y
    cos8, sin8 = ct_ref[...], st_ref[...]
    for t0, t1 in ((0, N_T_NORM), (N_T_NORM, N_T)):
        y_t = lax.dot_general(wt_ref[t0 * LANES:t1 * LANES, :], h, (((1,), (1,)), ((), ())),
                              preferred_element_type=F32)
        for t in range(t0, t1):
            y = y_t[(t - t0) * LANES:(t - t0 + 1) * LANES]
            if t < N_T_NORM:
                gain = jnp.concatenate([gt_ref[t * LANES:(t + 1) * LANES, :]] * (tm // LANES), axis=1)
                y = _norm_rope_t(y, gain, cos8, sin8)
            y = y.astype(BF16)
            for u in range(tm // TQ):
                pbt_ref[0, _PBT_DST[t], u] = y[:, u * TQ:(u + 1) * TQ]


def _in_proj(x, ln, w_t, w_r, gain_t, gain_r, cos_t, sin_p, sin_m, cos8, sin8, tm=1024):
    B, S, D = x.shape
    nt = S // tm
    half = ROPE_DIM // 2
    return pl.pallas_call(
        functools.partial(_inproj_kernel, tm=tm),
        grid=(B * nt,),
        in_specs=[
            pl.BlockSpec((1, tm, D), lambda i: (i // nt, i % nt, 0)),
            _const_spec((1, D)),
            _const_spec(w_t.shape), _const_spec(w_r.shape), _const_spec(gain_t.shape), _const_spec(gain_r.shape),
            pl.BlockSpec((tm, LANES), lambda i: (i % nt, 0)),
            pl.BlockSpec((tm, LANES), lambda i: (i % nt, 0)),
            pl.BlockSpec((tm, LANES), lambda i: (i % nt, 0)),
            pl.BlockSpec((half, tm), lambda i: (0, i % nt)),
            pl.BlockSpec((half, tm), lambda i: (0, i % nt)),
        ],
        out_specs=[
            pl.BlockSpec((1, N_PBT, tm // TQ, LANES, TQ), lambda i: (i // nt, 0, i % nt, 0, 0)),
            pl.BlockSpec((1, tm, N_PB * LANES), lambda i: (i // nt, i % nt, 0)),
            pl.BlockSpec((1, tm, N_F32 * LANES), lambda i: (i // nt, i % nt, 0)),
        ],
        out_shape=[
            jax.ShapeDtypeStruct((B, N_PBT, S // TQ, LANES, TQ), BF16),
            jax.ShapeDtypeStruct((B, S, N_PB * LANES), BF16),
            jax.ShapeDtypeStruct((B, S, N_F32 * LANES), F32),
        ],
        compiler_params=_cparams(("parallel",)),
        name="in_proj",
    )(x, ln, w_t, w_r, gain_t, gain_r, cos_t, sin_p, sin_m, cos8, sin8)


def _moba_kernel(qt_ref, k_ref, vt_ref, o_ref, q2_ref, bias_ref, m_ref, l_ref, acc_ref, kmean_ref, *, n_blk):
    n_pairs = H_MOBA // 2
    nidx = lax.broadcasted_iota(jnp.int32, (16, 2 * TQ), 0)
    kr = lax.broadcasted_iota(jnp.int32, (TQ, TQ), 0)
    ql = lax.broadcasted_iota(jnp.int32, (TQ, TQ), 1)
    causal_bias = jnp.where(kr <= ql, 0.0, NEG)
    for p in range(n_pairs):
        ps = slice(p * LANES, (p + 1) * LANES)
        rows = [jnp.sum(k_ref[0, n * TQ:(n + 1) * TQ, ps].astype(F32), axis=0, keepdims=True)
                for n in range(n_blk)]
        kmean_ref[p] = (jnp.concatenate(rows + [jnp.zeros((16 - n_blk, LANES), F32)], axis=0)
                        * (1.0 / MOBA_BLOCK))
    lax.fori_loop(0, n_blk, functools.partial(_moba_tile, qt_ref, k_ref, vt_ref, o_ref, q2_ref, bias_ref, m_ref,
                                              l_ref, acc_ref, kmean_ref, nidx, causal_bias, n_blk), 0)


def _moba_tile(qt_ref, k_ref, vt_ref, o_ref, q2_ref, bias_ref, m_ref, l_ref, acc_ref, kmean_ref, nidx,
               causal_bias, n_blk, qi, carry):
    n_pairs = H_MOBA // 2
    past = nidx < qi
    for p in range(n_pairs):
        km_hi, km_lo = _split_bf16(kmean_ref[p])
        q2 = jnp.concatenate([_half_rows(qt_ref[0, p, qi], hh) for hh in range(2)], axis=1)
        q2_ref[p] = q2
        gate = _dot(km_hi, q2) + _dot(km_lo, q2)
        for n in range(n_blk):
            row = gate[n:n + 1, :]
            ge = jnp.where(gate >= row, 1.0, 0.0)
            gt = jnp.where(gate > row, 1.0, 0.0)
            beats = jnp.where(past, jnp.where(nidx < n, ge, gt), 0.0)
            cnt = jnp.sum(beats, axis=0, keepdims=True)
            bias_ref[p, n] = jnp.where(cnt < MOBA_TOPK, 0.0, NEG)
        _online_init(m_ref, l_ref, acc_ref, p)

    def scores(off, p):
        return _dot(k_ref[0, pl.ds(off, TQ), p * LANES:(p + 1) * LANES], q2_ref[p])

    def pv(p, kt):
        def fn(pr):
            return jnp.concatenate([_dot(vt_ref[0, p, kt, :HEAD_DIM, :], pr[:, :TQ]),
                                    _dot(vt_ref[0, p, kt, HEAD_DIM:, :], pr[:, TQ:])], axis=1)
        return fn

    def past_scores(kt):
        off = pl.multiple_of(kt * TQ, TQ)
        return [scores(off, p) + bias_ref[p, kt] for p in range(n_pairs)]

    def past_tile(kt, s):
        for p in range(n_pairs):
            _online_update(m_ref, l_ref, acc_ref, p, s[p], pv(p, kt))

    def past_pair(i, _):
        s_a, s_b = past_scores(2 * i), past_scores(2 * i + 1)
        past_tile(2 * i, s_a)
        past_tile(2 * i + 1, s_b)
        return 0

    lax.fori_loop(0, qi // 2, past_pair, 0)

    @pl.when(qi % 2 == 1)
    def _():
        past_tile(qi - 1, past_scores(qi - 1))
    off = pl.multiple_of(qi * TQ, TQ)
    causal2 = jnp.concatenate([causal_bias, causal_bias], axis=1)
    s = [scores(off, p) + causal2 for p in range(n_pairs)]
    for p in range(n_pairs):
        _online_update(m_ref, l_ref, acc_ref, p, s[p], pv(p, qi))
        o2 = acc_ref[p] / jnp.maximum(l_ref[p], TINY)
        o_ref[0, pl.ds(off, TQ), p * LANES:(p + 1) * LANES] = jnp.concatenate(
            [o2[:, :TQ], o2[:, TQ:]], axis=0).T.astype(BF16)
    return carry


def _moba(pbt, pb):
    B, _, n_t, _, _ = pbt.shape
    S = pb.shape[1]
    n_pairs = H_MOBA // 2
    assert TQ == MOBA_BLOCK and n_t == S // MOBA_BLOCK
    return pl.pallas_call(
        functools.partial(_moba_kernel, n_blk=S // MOBA_BLOCK),
        grid=(B,),
        in_specs=[
            pl.BlockSpec((1, n_pairs, n_t, LANES, TQ), lambda b: (b, PBT_QA // n_pairs, 0, 0, 0)),
            pl.BlockSpec((1, S, QKV_A), lambda b: (b, 0, PB_KA // n_pairs)),
            pl.BlockSpec((1, n_pairs, n_t, LANES, TQ), lambda b: (b, PBT_VA // n_pairs, 0, 0, 0)),
        ],
        out_specs=pl.BlockSpec((1, S, QKV_A), lambda b: (b, 0, 0)),
        out_shape=jax.ShapeDtypeStruct((B, S, QKV_A), BF16),
        scratch_shapes=[pltpu.VMEM((n_pairs, LANES, 2 * TQ), BF16),
                        pltpu.VMEM((n_pairs, S // MOBA_BLOCK, 1, 2 * TQ), F32),
                        pltpu.VMEM((n_pairs, 1, 2 * TQ), F32), pltpu.VMEM((n_pairs, 1, 2 * TQ), F32),
                        pltpu.VMEM((n_pairs, HEAD_DIM, 2 * TQ), F32),
                        pltpu.VMEM((n_pairs, 16, LANES), F32)],
        compiler_params=_cparams(("parallel",)),
        name="moba",
    )(pbt, pb, pbt)


def _compress_kernel(kc_ref, vc_ref, pek_ref, pev_ref, w1k_ref, w2k_ref, w1v_ref, w2v_ref, gain_ref,
                     cos_ref, sp_ref, sm_ref, ko_ref, vo_ref, *, n_cmp):
    half_len = NSA_CMP_LEN // 2

    def mlp(t_ref, pe_ref, w1_ref, w2_ref):
        first = jnp.zeros((n_cmp, 2 * NSA_CMP_HIDDEN), F32)
        second = jnp.zeros((n_cmp, 2 * NSA_CMP_HIDDEN), F32)
        for l in range(half_len):
            a = t_ref[0, pl.ds(l, n_cmp, stride=NSA_CMP_STRIDE), :]
            first = first + jnp.dot((a + pe_ref[l:l + 1, :]).astype(BF16), w1_ref[l],
                                    preferred_element_type=F32)
            second = second + jnp.dot((a + pe_ref[half_len + l:half_len + l + 1, :]).astype(BF16),
                                      w1_ref[half_len + l], preferred_element_type=F32)
        hid = jax.nn.gelu(first + pltpu.roll(second, n_cmp - 1, 0))
        return jnp.dot(hid.astype(BF16), w2_ref[...], preferred_element_type=F32)

    valid = lax.broadcasted_iota(jnp.int32, (n_cmp, LANES), 0) < n_cmp - 1
    lo = lax.broadcasted_iota(jnp.int32, (n_cmp, LANES), 1) < HEAD_DIM
    kc = mlp(kc_ref, pek_ref, w1k_ref, w2k_ref)
    kc = _norm_rope(kc, gain_ref[...], cos_ref[...], sp_ref[...], sm_ref[...], lo)
    ko_ref[0] = jnp.where(valid, kc, 0.0).astype(BF16)
    vc = mlp(vc_ref, pev_ref, w1v_ref, w2v_ref)
    vo_ref[0] = jnp.where(valid, vc, 0.0).T.astype(BF16)


def _compress(pf, pe_k, pe_v, w1k, w2k, w1v, w2v, gain, cos_c, sp_c, sm_c):
    B, S, _ = pf.shape
    n_cmp = S // NSA_CMP_STRIDE
    cs = _const_spec
    return pl.pallas_call(
        functools.partial(_compress_kernel, n_cmp=n_cmp),
        grid=(B,),
        in_specs=[
            pl.BlockSpec((1, S, LANES), lambda b: (b, 0, 0)),
            pl.BlockSpec((1, S, LANES), lambda b: (b, 0, 1)),
            cs(pe_k.shape), cs(pe_v.shape), cs(w1k.shape), cs(w2k.shape), cs(w1v.shape), cs(w2v.shape),
            cs(gain.shape), cs(cos_c.shape), cs(sp_c.shape), cs(sm_c.shape),
        ],
        out_specs=[pl.BlockSpec((1, n_cmp, LANES), lambda b: (b, 0, 0)),
                   pl.BlockSpec((1, LANES, n_cmp), lambda b: (b, 0, 0))],
        out_shape=[jax.ShapeDtypeStruct((B, n_cmp, LANES), BF16),
                   jax.ShapeDtypeStruct((B, LANES, n_cmp), BF16)],
        compiler_params=_cparams(("parallel",)),
        name="nsa_compress",
    )(pf, pf, pe_k, pe_v, w1k, w2k, w1v, w2v, gain, cos_c, sp_c, sm_c)


def _nsa_kernel(*refs, n_cmp, n_slc, n_t):
    lax.fori_loop(0, n_t, functools.partial(_nsa_tile, refs, n_cmp, n_slc), 0)


def _nsa_tile(refs, n_cmp, n_slc, qi, carry):
    (qt_ref, kc_ref, vct_ref, ks_ref, vst_ref, kw_ref, vwt_ref, g_ref, ovl_ref, o_ref,
     q6_ref, impf_ref, improw_ref, m_ref, l_ref, acc_ref, ot_ref) = refs
    blk_per_tile = TQ // NSA_SEL_BLOCK
    off = pl.multiple_of(qi * TQ, TQ)
    tpos = qi * TQ + lax.broadcasted_iota(jnp.int32, (1, TQ), 1)
    sig = jax.nn.sigmoid(g_ref[0, pl.ds(off, TQ), :].T)
    kr = lax.broadcasted_iota(jnp.int32, (TQ, TQ), 0)
    ql = lax.broadcasted_iota(jnp.int32, (TQ, TQ), 1)
    causal_bias = jnp.where(kr <= ql, 0.0, NEG)
    nidx = lax.broadcasted_iota(jnp.int32, (n_cmp, TQ), 0)
    cmask = (nidx * NSA_CMP_STRIDE + (NSA_CMP_LEN - 1) <= tpos) & (nidx < n_cmp - 1)
    cmp_bias = jnp.where(cmask, 0.0, NEG)
    jj = lax.broadcasted_iota(jnp.int32, (n_slc, TQ), 0)
    cur = lax.shift_right_arithmetic(tpos, NSA_SEL_BLOCK.bit_length() - 1)
    forced = (jj == 0) | (jj == cur) | (jj == cur - 1)
    gw = NSA_GROUP * TQ

    def half(kk):
        return slice(kk * HEAD_DIM, (kk + 1) * HEAD_DIM)

    def cols(kk):
        return slice(kk * gw, (kk + 1) * gw)

    def tile3(a):
        return jnp.concatenate([a] * NSA_GROUP, axis=1)

    def gate3(kk, branch):
        return jnp.concatenate([sig[3 * (NSA_GROUP * kk + j) + branch:3 * (NSA_GROUP * kk + j) + branch + 1, :]
                                for j in range(NSA_GROUP)], axis=1)

    q6 = jnp.concatenate([_half_rows(qt_ref[0, j, qi], kk) for kk in range(H_NSA_KV) for j in range(NSA_GROUP)],
                         axis=1)
    q6_ref[...] = q6

    s_cmp = _dot(kc_ref[0], q6)
    cmp_bias3 = tile3(cmp_bias)
    for kk in range(H_NSA_KV):
        s = s_cmp[:, cols(kk)] + cmp_bias3
        m = jnp.maximum(jnp.max(s, axis=0, keepdims=True), M_FLOOR)
        p = jnp.exp2(s - m)
        p = p / jnp.maximum(jnp.sum(p, axis=0, keepdims=True), TINY)
        ot_ref[kk] = gate3(kk, 0) * _dot(vct_ref[0, half(kk), :], p.astype(BF16))
        p_sum = (p[:, :TQ] + p[:, TQ:2 * TQ]) + p[:, 2 * TQ:]
        _online_init(m_ref, l_ref, acc_ref, kk)
        p_hi, p_lo = _split_bf16(p_sum)
        imp = _dot(ovl_ref[...], p_hi) + _dot(ovl_ref[...], p_lo)
        imp = jnp.where(jj > cur, NEG, jnp.where(forced, FORCE, imp))
        impf_ref[kk] = imp
        for j in range(n_slc):
            improw_ref[kk, j] = imp[j:j + 1, :]

    def sel_bias(kk, kt):
        imp = impf_ref[kk]
        parts = []
        for c in range(blk_per_tile):
            j = kt * blk_per_tile + c
            row = improw_ref[kk, j]
            ge = jnp.where(imp >= row, 1.0, 0.0)
            gt = jnp.where(imp > row, 1.0, 0.0)
            cnt = jnp.sum(jnp.where(jj < j, ge, gt), axis=0, keepdims=True)
            parts.append(jnp.broadcast_to(jnp.where(cnt < NSA_N_SEL, 0.0, NEG), (NSA_SEL_BLOCK, TQ)))
        return jnp.concatenate(parts, axis=0)

    def sel_pv(kk, kt):
        return lambda pr: _dot(vst_ref[0, 0, kt, half(kk), :], pr)

    def sel_scores(kt):
        off = pl.multiple_of(kt * TQ, TQ)
        return _dot(ks_ref[0, pl.ds(off, TQ), :], q6_ref[...])

    def sel_tile(kt, s6):
        for kk in range(H_NSA_KV):
            s = s6[:, cols(kk)] + tile3(sel_bias(kk, kt))
            _online_update(m_ref, l_ref, acc_ref, kk, s, sel_pv(kk, kt))

    def sel_pair(i, _):
        s_a, s_b = sel_scores(2 * i), sel_scores(2 * i + 1)
        sel_tile(2 * i, s_a)
        sel_tile(2 * i + 1, s_b)
        return 0

    lax.fori_loop(0, qi // 2, sel_pair, 0)

    @pl.when(qi % 2 == 1)
    def _():
        sel_tile(qi - 1, sel_scores(qi - 1))

    kt1 = jnp.maximum(qi - 1, 0)
    kt2 = jnp.maximum(qi - 2, 0)
    ks_d = ks_ref[0, pl.ds(off, TQ), :]
    kw_0 = kw_ref[0, pl.ds(off, TQ), :]
    kw_1 = kw_ref[0, pl.ds(pl.multiple_of(kt1 * TQ, TQ), TQ), :]
    kw_2 = kw_ref[0, pl.ds(pl.multiple_of(kt2 * TQ, TQ), TQ), :]
    win_bias2 = jnp.where(kr > ql + jnp.where(qi >= 2, 0, TQ), 0.0, NEG)
    win_bias1 = jnp.where(qi >= 1, 0.0, NEG)
    q6 = q6_ref[...]
    sd6, w2, w1, w0 = _dot(ks_d, q6), _dot(kw_2, q6), _dot(kw_1, q6), _dot(kw_0, q6)
    causal3 = tile3(causal_bias)
    win_bias2 = tile3(win_bias2)
    for kk in range(H_NSA_KV):
        s = sd6[:, cols(kk)] + tile3(sel_bias(kk, qi) + causal_bias)
        _online_update(m_ref, l_ref, acc_ref, kk, s, sel_pv(kk, qi))
        o_sel = acc_ref[kk] / jnp.maximum(l_ref[kk], TINY)
        s2 = w2[:, cols(kk)] + win_bias2
        s1 = w1[:, cols(kk)] + win_bias1
        s0 = w0[:, cols(kk)] + causal3
        m = jnp.maximum(jnp.maximum(jnp.max(s2, axis=0, keepdims=True), jnp.max(s1, axis=0, keepdims=True)),
                        jnp.max(s0, axis=0, keepdims=True))
        p2, p1, p0 = jnp.exp2(s2 - m), jnp.exp2(s1 - m), jnp.exp2(s0 - m)
        den = (jnp.sum(p2, axis=0, keepdims=True) + jnp.sum(p1, axis=0, keepdims=True)
               + jnp.sum(p0, axis=0, keepdims=True))
        acc = (_dot(vwt_ref[0, 0, kt2, half(kk), :], p2.astype(BF16))
               + _dot(vwt_ref[0, 0, kt1, half(kk), :], p1.astype(BF16))
               + _dot(vwt_ref[0, 0, qi, half(kk), :], p0.astype(BF16)))
        o_win = acc / jnp.maximum(den, TINY)
        ot_ref[kk] = (ot_ref[kk] + gate3(kk, 1) * o_sel) + gate3(kk, 2) * o_win
    for j in range(NSA_GROUP):
        js = slice(j * TQ, (j + 1) * TQ)
        o_ref[0, pl.ds(off, TQ), j * LANES:(j + 1) * LANES] = jnp.concatenate(
            [ot_ref[0, :, js], ot_ref[1, :, js]], axis=0).T.astype(BF16)
    return carry


def _nsa(pbt, pb, pf, kcmp, vcmp_t, ovl_t):
    B, _, n_t, _, _ = pbt.shape
    S = pb.shape[1]
    n_cmp = kcmp.shape[1]
    n_slc = S // NSA_SEL_BLOCK
    gw = NSA_GROUP * TQ
    assert NSA_WINDOW == 2 * TQ and TQ % NSA_SEL_BLOCK == 0 and S % TQ == 0
    return pl.pallas_call(
        functools.partial(_nsa_kernel, n_cmp=n_cmp, n_slc=n_slc, n_t=n_t),
        grid=(B,),
        in_specs=[
            pl.BlockSpec((1, NSA_GROUP, n_t, LANES, TQ), lambda b: (b, PBT_QB // NSA_GROUP, 0, 0, 0)),
            pl.BlockSpec((1, n_cmp, LANES), lambda b: (b, 0, 0)),
            pl.BlockSpec((1, LANES, n_cmp), lambda b: (b, 0, 0)),
            pl.BlockSpec((1, S, LANES), lambda b: (b, 0, PB_KS)),
            pl.BlockSpec((1, 1, n_t, LANES, TQ), lambda b: (b, PBT_VS, 0, 0, 0)),
            pl.BlockSpec((1, S, LANES), lambda b: (b, 0, PB_KW)),
            pl.BlockSpec((1, 1, n_t, LANES, TQ), lambda b: (b, PBT_VW, 0, 0, 0)),
            pl.BlockSpec((1, S, LANES), lambda b: (b, 0, 2)),
            _const_spec(ovl_t.shape),
        ],
        out_specs=pl.BlockSpec((1, S, Q_B), lambda b: (b, 0, 0)),
        out_shape=jax.ShapeDtypeStruct((B, S, Q_B), BF16),
        scratch_shapes=[pltpu.VMEM((LANES, H_NSA * TQ), BF16),
                        pltpu.VMEM((H_NSA_KV, n_slc, TQ), F32), pltpu.VMEM((H_NSA_KV, n_slc, 1, TQ), F32),
                        pltpu.VMEM((H_NSA_KV, 1, gw), F32), pltpu.VMEM((H_NSA_KV, 1, gw), F32),
                        pltpu.VMEM((H_NSA_KV, HEAD_DIM, gw), F32),
                        pltpu.VMEM((H_NSA_KV, HEAD_DIM, gw), F32)],
        compiler_params=_cparams(("parallel",)),
        name="nsa_attn",
    )(pbt, kcmp, vcmp_t, pb, pbt, pb, pbt, pf, ovl_t)


DIL_TILES = 16

def _dilated_kernel(q_ref, k_ref, v_ref, o_ref, qf_ref, kf_ref, vf_ref, os_ref, ls_ref, *, seq):
    blk = BAND_BLOCK
    n_tiles = seq // blk
    lo = lax.broadcasted_iota(jnp.int32, (blk, LANES), 1) < HEAD_DIM
    for g, (window, dil) in enumerate(DIL_CFG):
        max_dist = window // dil
        tiles_per_class = n_tiles // dil
        use_prev = tiles_per_class > 1
        n_keys = 2 * blk if use_prev else blk
        ri = lax.broadcasted_iota(jnp.int32, (2 * blk, n_keys), 0) & (blk - 1)
        ci = lax.broadcasted_iota(jnp.int32, (2 * blk, n_keys), 1)
        dist = ri + (n_keys - blk) - ci
        band = (dist >= 0) & (dist <= max_dist)
        band_bias = jnp.where(band, 0.0, NEG)
        qf_ref[g] = q_ref[0, :, g * LANES:(g + 1) * LANES].astype(F32)
        kf_ref[g] = k_ref[0, :, g * LANES:(g + 1) * LANES].astype(F32)
        vf_ref[g] = v_ref[0, :, g * LANES:(g + 1) * LANES].astype(F32)

        def index(start, dil=dil):
            if dil == 1:
                return pl.ds(pl.multiple_of(start, blk), blk)
            return pl.ds(start, blk, stride=dil)

        def tiles(i, _, g=g, dil=dil, tiles_per_class=tiles_per_class, use_prev=use_prev, band=band,
                  band_bias=band_bias, ci=ci, index=index):
            scored = []
            for u in range(DIL_TILES):
                t = i * DIL_TILES + u
                cls = t // tiles_per_class
                mt = t % tiles_per_class
                start = cls + dil * blk * mt
                q = qf_ref[g, index(start), :]
                q2 = jnp.concatenate([jnp.where(lo, q, 0.0), jnp.where(lo, 0.0, q)], axis=0).astype(BF16)
                if use_prev:
                    prev = jnp.maximum(start - dil * blk, cls)
                    k2 = jnp.concatenate([kf_ref[g, index(prev), :], kf_ref[g, index(start), :]], axis=0)
                    v2 = jnp.concatenate([vf_ref[g, index(prev), :], vf_ref[g, index(start), :]], axis=0)
                    bias = jnp.where(band & (ci >= jnp.where(mt > 0, 0, blk)), 0.0, NEG)
                else:
                    k2, v2, bias = kf_ref[g, index(start), :], vf_ref[g, index(start), :], band_bias
                s = lax.dot_general(q2, k2.astype(BF16), (((1,), (1,)), ((), ())), preferred_element_type=F32)
                scored.append((start, s + bias, v2.astype(BF16)))
            probs = []
            for start, s, v2 in scored:
                m = jnp.max(s, axis=-1, keepdims=True)
                p = jnp.exp2(s - m)
                den = jnp.sum(p, axis=-1, keepdims=True)
                lse = m + jnp.log2(jnp.maximum(den, TINY))
                probs.append((start, lse, (p / jnp.maximum(den, TINY)).astype(BF16), v2))
            for start, lse, p, v2 in probs:
                o = _dot(p, v2)
                os_ref[g, index(start), :] = jnp.where(lo, o[:blk], o[blk:])
                ls_ref[g, index(start), :] = jnp.where(lo, lse[:blk], lse[blk:])
            return 0

        lax.fori_loop(0, n_tiles // DIL_TILES, tiles, 0)

    rows_per_trip = DIL_TILES * blk

    def combine(t, _):
        idx = pl.ds(pl.multiple_of(t * rows_per_trip, rows_per_trip), rows_per_trip)
        lses = [ls_ref[g, idx, :] for g in range(len(DIL_CFG))]
        mx = jnp.maximum(jnp.maximum(lses[0], lses[1]), lses[2])
        es = [jnp.exp2(l - mx) for l in lses]
        tot = es[0] + es[1] + es[2]
        for g in range(len(DIL_CFG)):
            o_ref[0, idx, g * LANES:(g + 1) * LANES] = (os_ref[g, idx, :] * (es[g] / tot)).astype(BF16)
        return 0

    lax.fori_loop(0, seq // rows_per_trip, combine, 0)


def _dilated(pb):
    B, S, _ = pb.shape
    n_g = len(DIL_CFG)
    return pl.pallas_call(
        functools.partial(_dilated_kernel, seq=S),
        grid=(B,),
        in_specs=[pl.BlockSpec((1, S, QKV_C), lambda b: (b, 0, PB_QC // n_g)),
                  pl.BlockSpec((1, S, QKV_C), lambda b: (b, 0, PB_KC // n_g)),
                  pl.BlockSpec((1, S, QKV_C), lambda b: (b, 0, PB_VC // n_g))],
        out_specs=pl.BlockSpec((1, S, QKV_C), lambda b: (b, 0, 0)),
        out_shape=jax.ShapeDtypeStruct((B, S, QKV_C), BF16),
        scratch_shapes=[pltpu.VMEM((n_g, S, LANES), F32)] * 5,
        compiler_params=_cparams(("parallel",)),
        name="dilated_attn",
    )(pb, pb, pb)


HALO = 16
FF_CHUNK = 256


def _proj_ffn_kernel(x_ref, xh_ref, oa_ref, oah_ref, ob_ref, obh_ref, oc_ref, och_ref, wo_ref, ln_ref,
                     wg_ref, wu_ref, cw_ref, cb_ref, wd_ref, y_ref, o_ref, h_ref, x1_ref, a_ref,
                     *, tiles_per_seq):
    i = pl.program_id(0)

    def norm(v):
        return (v * lax.rsqrt(jnp.mean(v * v, axis=-1, keepdims=True) + EPS)) * ln_ref[...]

    col = 0
    for main, halo in ((oa_ref, oah_ref), (ob_ref, obh_ref), (oc_ref, och_ref)):
        width = main.shape[1]
        o_ref[:HALO, col:col + width] = halo[...]
        o_ref[HALO:, col:col + width] = main[...]
        col += width
    attn = _dot(o_ref[...], wo_ref[...])
    x1 = x_ref[...] + attn[HALO:]
    x1_ref[...] = x1
    first = (i % tiles_per_seq) == 0
    h_ref[:HALO, :] = jnp.where(first, 0.0, norm(xh_ref[...] + attn[:HALO])).astype(BF16)
    h_ref[HALO:, :] = norm(x1).astype(BF16)

    for f0 in range(0, D_FF, FF_CHUNK):
        fs = slice(f0, min(f0 + FF_CHUNK, D_FF))
        g = _dot(h_ref[...], wg_ref[:, fs])
        u = _dot(h_ref[HALO:, :], wu_ref[:, fs])
        gc = (cw_ref[0:1, fs] * pltpu.roll(g, 2, 0)[HALO:] + cw_ref[1:2, fs] * pltpu.roll(g, 1, 0)[HALO:]
              + cw_ref[2:3, fs] * g[HALO:] + cb_ref[:, fs])
        a_ref[:, fs] = (jax.nn.silu(gc) * u).astype(BF16)
    y_ref[...] = x1_ref[...] + _dot(a_ref[...], wd_ref[...])


def _proj_ffn(x2, o_a, o_b, o_c, wo, ln, wg, wu, cw, cb, wd, seq, tm=512):
    M, D = x2.shape
    halo_blocks = tm // HALO
    main = lambda w: pl.BlockSpec((tm, w), lambda i: (i, 0))
    halo = lambda w: pl.BlockSpec((HALO, w), lambda i: (jnp.maximum(i * halo_blocks - 1, 0), 0))
    return pl.pallas_call(
        functools.partial(_proj_ffn_kernel, tiles_per_seq=seq // tm),
        grid=(M // tm,),
        in_specs=[
            main(D), halo(D), main(QKV_A), halo(QKV_A), main(Q_B), halo(Q_B), main(QKV_C), halo(QKV_C),
            _const_spec(wo.shape), _const_spec((1, D)),
            _const_spec(wg.shape), _const_spec(wu.shape), _const_spec(cw.shape), _const_spec(cb.shape),
            _const_spec(wd.shape),
        ],
        out_specs=main(D),
        out_shape=jax.ShapeDtypeStruct((M, D), F32),
        scratch_shapes=[pltpu.VMEM((HALO + tm, D), BF16), pltpu.VMEM((HALO + tm, D), BF16),
                        pltpu.VMEM((tm, D), F32), pltpu.VMEM((tm, D_FF), BF16)],
        compiler_params=_cparams(("parallel",)),
        name="proj_ffn",
    )(x2, x2, o_a, o_a, o_b, o_b, o_c, o_c, wo, ln, wg, wu, cw, cb, wd)


def _rope_tables(pos):
    half = ROPE_DIM // 2
    inv_freq = ROPE_THETA ** (-jnp.arange(half, dtype=F32) * 2.0 / ROPE_DIM)
    ang = pos.astype(F32)[:, None] * inv_freq
    c, s = jnp.cos(ang), jnp.sin(ang)
    n = pos.shape[0]
    rest = HEAD_DIM - ROPE_DIM
    cos_h = jnp.concatenate([c, c, jnp.ones((n, rest), F32)], axis=-1)
    sp_h = jnp.concatenate([jnp.zeros((n, half), F32), s, jnp.zeros((n, rest), F32)], axis=-1)
    sm_h = jnp.concatenate([-s, jnp.zeros((n, half + rest), F32)], axis=-1)
    two = lambda t: jnp.concatenate([t, t], axis=-1)
    return two(cos_h), two(sp_h), two(sm_h), c.T, s.T


def _pair_diag(w):
    z = jnp.zeros_like(w)
    return jnp.concatenate([jnp.concatenate([w, z], axis=-1), jnp.concatenate([z, w], axis=-1)], axis=-2)


def _overlap_t(seq):
    n_cmp = seq // NSA_CMP_STRIDE
    n_slc = seq // NSA_SEL_BLOCK
    starts = np.arange(n_cmp) * NSA_CMP_STRIDE
    j = np.arange(n_slc)
    ovl = (starts[None, :] < (j[:, None] + 1) * NSA_SEL_BLOCK) & (starts[None, :] + NSA_CMP_LEN > j[:, None] * NSA_SEL_BLOCK)
    ovl[:, n_cmp - 1] = False
    return jnp.asarray(ovl, BF16)


def _mixers(x, seq_tabs, cmp_tabs, ovl_t, ln1, w, qn_a, kn_a, qn_b, kn_b, pe_k, pe_v, wk1, wk2, wv1, wv2,
            qn_c, kn_c):
    rep = lambda g, n: jnp.tile(g, n)
    w_t, w_r = w
    gain_t = jnp.concatenate([rep(qn_a * Q_SCALE, H_MOBA), rep(qn_b * Q_SCALE, H_NSA)])
    gain_t = jnp.broadcast_to(gain_t[:, None], (gain_t.shape[0], LANES))
    gain_r = jnp.concatenate([rep(kn_a, H_MOBA), rep(kn_b[1], H_NSA_KV), rep(kn_b[2], H_NSA_KV),
                              rep(qn_c * Q_SCALE, H_DIL), rep(kn_c, H_DIL)])[None, :]
    pbt, pb, pf = _in_proj(x, ln1[None, :], w_t, w_r, gain_t, gain_r, *seq_tabs)

    o_a = _moba(pbt, pb)
    two = lambda t: jnp.concatenate([t, t], axis=-1)
    w1 = lambda t: _pair_diag(t.reshape(NSA_CMP_LEN, HEAD_DIM, NSA_CMP_HIDDEN)).astype(BF16)
    kcmp, vcmp_t = _compress(pf, two(pe_k), two(pe_v), w1(wk1), _pair_diag(wk2).astype(BF16),
                             w1(wv1), _pair_diag(wv2).astype(BF16), two(kn_b[0])[None, :], *cmp_tabs[:3])
    o_b = _nsa(pbt, pb, pf, kcmp, vcmp_t, ovl_t)
    o_c = _dilated(pb)
    return o_a, o_b, o_c


def _layer(x, seq_tabs, cmp_tabs, ovl_t, ln1, w_in, qn_a, kn_a, qn_b, kn_b, pe_k, pe_v, wk1, wk2, wv1, wv2,
           qn_c, kn_c, w_out, ln2, w_gate, w_up, conv_w, conv_b, w_down):
    B, S, D = x.shape
    o_a, o_b, o_c = _mixers(x, seq_tabs, cmp_tabs, ovl_t, ln1, w_in, qn_a, kn_a, qn_b, kn_b, pe_k, pe_v,
                            wk1, wk2, wv1, wv2, qn_c, kn_c)
    nsa_rows = [w_out[QKV_A + h * HEAD_DIM:QKV_A + (h + 1) * HEAD_DIM] for h in NSA_ORDER]
    w_o = jnp.concatenate([w_out[:QKV_A]] + nsa_rows + [w_out[QKV_A + Q_B:]], axis=0).astype(BF16)
    M = B * S
    x2 = _proj_ffn(x.reshape(M, D), o_a.reshape(M, QKV_A), o_b.reshape(M, Q_B), o_c.reshape(M, QKV_C), w_o,
                   ln2[None, :], w_gate.astype(BF16), w_up.astype(BF16), conv_w, conv_b[None, :],
                   w_down.astype(BF16), S)
    return x2.reshape(B, S, D)


def kernel(x, ln1, w_in, qn_a, kn_a, qn_b, kn_b, cmp_pe_k, cmp_pe_v, cmp_k_w1, cmp_k_w2, cmp_v_w1, cmp_v_w2,
           qn_c, kn_c, w_out, ln2, w_gate, w_up, conv_w, conv_b, w_down):
    S = x.shape[1]
    seq_tabs = _rope_tables(jnp.arange(S))
    n_cmp = S // NSA_CMP_STRIDE
    cmp_tabs = _rope_tables(jnp.arange(n_cmp) * NSA_CMP_STRIDE + (NSA_CMP_LEN - 1))
    ovl_t = _overlap_t(S)
    for l in range(ln1.shape[0]):
        x = _layer(x, seq_tabs, cmp_tabs, ovl_t, ln1[l], _prep_w_in(w_in, l), qn_a[l], kn_a[l], qn_b[l], kn_b[l],
                   cmp_pe_k[l], cmp_pe_v[l], cmp_k_w1[l], cmp_k_w2[l], cmp_v_w1[l], cmp_v_w2[l], qn_c[l], kn_c[l],
                   w_out[l], ln2[l], w_gate[l], w_up[l], conv_w[l], conv_b[l], w_down[l])
    return x
```

```python
import functools

import numpy as np
import jax
import jax.numpy as jnp
from jax import lax
from jax.experimental import pallas as pl
from jax.experimental.pallas import tpu as pltpu

F32 = jnp.float32
BF16 = jnp.bfloat16

D_MODEL = 1024
HEAD_DIM = 64
N_HEADS = D_MODEL // HEAD_DIM
H_MOBA = N_HEADS // 4
H_NSA = (N_HEADS - H_MOBA) // 2
H_DIL = N_HEADS - H_MOBA - H_NSA
H_NSA_KV = 2
NSA_GROUP = H_NSA // H_NSA_KV
ROPE_DIM = HEAD_DIM // 4
ROPE_THETA = 500000.0
MOBA_BLOCK = 256
MOBA_TOPK = 3
NSA_CMP_LEN = 32
NSA_CMP_STRIDE = 16
NSA_CMP_HIDDEN = 128
NSA_SEL_BLOCK = 64
NSA_N_SEL = 6
NSA_WINDOW = 512
DIL_CFG = ((128, 1), (512, 4), (2048, 16))
BAND_BLOCK = 128
D_FF = 2816
EPS = 1e-6
NEG = -1e30
TINY = 1e-30
FORCE = 1e9
SCALE = HEAD_DIM ** -0.5
Q_SCALE = SCALE * float(np.log2(np.e))
QKV_A = H_MOBA * HEAD_DIM
Q_B = H_NSA * HEAD_DIM
KV_B = H_NSA_KV * HEAD_DIM
G_B = H_NSA * 3
QKV_C = H_DIL * HEAD_DIM
IN_SPLITS = (QKV_A, QKV_A, QKV_A, Q_B, KV_B, KV_B, KV_B, KV_B, KV_B, KV_B, G_B, QKV_C, QKV_C, QKV_C)

LANES = 128
TQ = 256
NSA_ORDER = (0, 3, 1, 4, 2, 5)
VMEM_LIMIT = 56 * 1024 * 1024

M_FLOOR = -1e29
V_ROWS = HEAD_DIM + 16

T_GROUP = ("qa", "qb", "va", "vsb", "vwb")
N_T_NORM, N_T = 5, 9
R_GROUP = ("ka", "ksb", "kwb", "qc", "kc", "vc", "kcb", "vcb", "gb")
N_R_NORM, N_R_BF16, N_R = 10, 13, 16
N_PBT, N_PB, N_F32 = N_T, N_R_BF16, N_R - N_R_BF16
PBT_QA, PBT_VA, PBT_VS, PBT_VW, PBT_QB = 0, 2, 4, 5, 6
PB_QC, PB_KC, PB_KA, PB_KS, PB_VC, PB_KW = 0, 3, 6, 8, 9, 12
_PBT_DST = (0, 1, 6, 7, 8, 2, 3, 4, 5)
_PB_DST = (6, 7, 8, 12, 0, 1, 2, 3, 4, 5, 9, 10, 11)
CHUNK = 4


def _cparams(sem):
    return pltpu.CompilerParams(dimension_semantics=sem, vmem_limit_bytes=VMEM_LIMIT)


def _const_spec(shape):
    return pl.BlockSpec(shape, lambda *_: (0,) * len(shape), pipeline_mode=pl.Buffered(1))


def _norm_rope(y, gain, cos_t, sin_p, sin_m, lo):
    ss = y * y
    s_lo = jnp.sum(jnp.where(lo, ss, 0.0), axis=-1, keepdims=True)
    s_hi = jnp.sum(jnp.where(lo, 0.0, ss), axis=-1, keepdims=True)
    ms = jnp.where(lo, s_lo, s_hi) * (1.0 / HEAD_DIM)
    z = (y * lax.rsqrt(ms + EPS)) * gain
    half = ROPE_DIM // 2
    return z * cos_t + pltpu.roll(z, half, 1) * sin_p + pltpu.roll(z, LANES - half, 1) * sin_m


def _online_update(m_ref, acc_ref, g, s, pv):
    m_old = m_ref[g]
    m_new = jnp.maximum(m_old, jnp.max(s, axis=0, keepdims=True))
    p = jnp.exp2(s - m_new)
    alpha = jnp.exp2(m_old - m_new)
    acc_ref[g] = alpha * acc_ref[g] + pv(p.astype(BF16))
    m_ref[g] = m_new


def _online_init(m_ref, acc_ref, g):
    m_ref[g] = jnp.full(m_ref.shape[1:], M_FLOOR, F32)
    acc_ref[g] = jnp.zeros(acc_ref.shape[1:], F32)


def _with_ones(v_t):
    row = lax.broadcasted_iota(jnp.int32, (V_ROWS - HEAD_DIM, v_t.shape[1]), 0)
    return jnp.concatenate([v_t, jnp.where(row == 0, 1.0, 0.0).astype(v_t.dtype)], axis=0)


def _normalise(acc):
    return acc[:HEAD_DIM] / jnp.maximum(acc[HEAD_DIM:HEAD_DIM + 1], TINY)


def _dot(a, b):
    return jnp.dot(a, b, preferred_element_type=F32)


def _half_rows(q_t, hh):
    z = jnp.zeros((HEAD_DIM, q_t.shape[1]), q_t.dtype)
    if hh == 0:
        return jnp.concatenate([q_t[:HEAD_DIM], z], axis=0)
    return jnp.concatenate([z, q_t[HEAD_DIM:]], axis=0)


def _split_bf16(a):
    hi = a.astype(BF16)
    return hi, (a - hi.astype(F32)).astype(BF16)


def _in_proj_plan(group):
    names = ("qa", "ka", "va", "qb", "kcb", "vcb", "ksb", "vsb", "kwb", "vwb", "gb", "qc", "kc", "vc")
    start = dict(zip(names, np.cumsum((0,) + IN_SPLITS[:-1]).tolist()))
    width = dict(zip(names, IN_SPLITS))
    plan = []
    for name in group:
        if name == "qb":
            heads = [start[name] + h * HEAD_DIM for h in NSA_ORDER]
            plan += [[(heads[2 * j], HEAD_DIM), (heads[2 * j + 1], HEAD_DIM)] for j in range(H_NSA // 2)]
        else:
            plan += [[(start[name] + c, min(LANES, width[name] - c))] for c in range(0, width[name], LANES)]
    return plan


def _wprep_kernel(w_ref, wt_ref, wr_ref, *, plan_t, plan_r, n_in):
    def take(a, width):
        b0 = a // LANES * LANES
        v0 = w_ref[0, :, b0:min(b0 + LANES, n_in)]
        if a + width <= b0 + LANES:
            return v0[:, a - b0:a - b0 + width]
        v1 = w_ref[0, :, b0 + LANES:min(b0 + 2 * LANES, n_in)]
        return jnp.concatenate([v0[:, a - b0:], v1[:, :a + width - b0 - LANES]], axis=1)

    def block(pieces):
        cols = [take(a, width) for a, width in pieces]
        filled = sum(width for _, width in pieces)
        if filled < LANES:
            cols.append(jnp.zeros((w_ref.shape[1], LANES - filled), F32))
        return cols[0] if len(cols) == 1 else jnp.concatenate(cols, axis=1)

    for d, pieces in enumerate(plan_t):
        wt_ref[d * LANES:(d + 1) * LANES, :] = block(pieces).T.astype(BF16)
    for d, pieces in enumerate(plan_r):
        wr_ref[:, d * LANES:(d + 1) * LANES] = block(pieces).astype(BF16)


def _prep_w_in(w_in, layer, tr=256):
    _, D, n_in = w_in.shape
    plan_t, plan_r = _in_proj_plan(T_GROUP), _in_proj_plan(R_GROUP)
    return pl.pallas_call(
        functools.partial(_wprep_kernel, plan_t=plan_t, plan_r=plan_r, n_in=n_in),
        grid=(D // tr,),
        in_specs=[pl.BlockSpec((1, tr, n_in), lambda i: (layer, i, 0))],
        out_specs=[pl.BlockSpec((N_T * LANES, tr), lambda i: (0, i)),
                   pl.BlockSpec((tr, N_R * LANES), lambda i: (i, 0))],
        out_shape=[jax.ShapeDtypeStruct((N_T * LANES, D), BF16), jax.ShapeDtypeStruct((D, N_R * LANES), BF16)],
        compiler_params=_cparams(("parallel",)),
        name="w_in_layout",
    )(w_in)


def _norm_rope_t(y, gain, cos8, sin8):
    half = ROPE_DIM // 2
    parts = []
    for hh in range(2):
        yh = y[hh * HEAD_DIM:(hh + 1) * HEAD_DIM]
        ms = jnp.sum(yh * yh, axis=0, keepdims=True) * (1.0 / HEAD_DIM)
        z = (yh * lax.rsqrt(ms + EPS)) * gain[hh * HEAD_DIM:(hh + 1) * HEAD_DIM]
        x1, x2 = z[:half], z[half:ROPE_DIM]
        parts += [x1 * cos8 - x2 * sin8, x2 * cos8 + x1 * sin8, z[ROPE_DIM:]]
    return jnp.concatenate(parts, axis=0)


def _inproj_kernel(x_ref, ln_ref, wt_ref, wr_ref, gt_ref, gr_ref, cos_ref, sp_ref, sm_ref, ct_ref, st_ref,
                   pbt_ref, pb_ref, pf_ref, *, tm):
    x = x_ref[0]
    h = x * lax.rsqrt(jnp.mean(x * x, axis=-1, keepdims=True) + EPS)
    h = (h * ln_ref[...]).astype(BF16)
    lo = lax.broadcasted_iota(jnp.int32, (tm, LANES), 1) < HEAD_DIM
    cos_t, sin_p, sin_m = cos_ref[...], sp_ref[...], sm_ref[...]
    for c0 in range(0, N_R, CHUNK):
        y4 = _dot(h, wr_ref[:, c0 * LANES:(c0 + CHUNK) * LANES])
        for c in range(c0, c0 + CHUNK):
            y = y4[:, (c - c0) * LANES:(c - c0 + 1) * LANES]
            if c < N_R_NORM:
                y = _norm_rope(y, gr_ref[:, c * LANES:(c + 1) * LANES], cos_t, sin_p, sin_m, lo)
            if c < N_R_BF16:
                d = _PB_DST[c]
                pb_ref[0, :, d * LANES:(d + 1) * LANES] = y.astype(BF16)
            else:
                d = c - N_R_BF16
                pf_ref[0, :, d * LANES:(d + 1) * LANES] = y
    cos8, sin8 = ct_ref[...], st_ref[...]
    for t0, t1 in ((0, N_T_NORM), (N_T_NORM, N_T)):
        y_t = lax.dot_general(wt_ref[t0 * LANES:t1 * LANES, :], h, (((1,), (1,)), ((), ())),
                              preferred_element_type=F32)
        for t in range(t0, t1):
            y = y_t[(t - t0) * LANES:(t - t0 + 1) * LANES]
            if t < N_T_NORM:
                gain = jnp.concatenate([gt_ref[t * LANES:(t + 1) * LANES, :]] * (tm // LANES), axis=1)
                y = _norm_rope_t(y, gain, cos8, sin8)
            y = y.astype(BF16)
            for u in range(tm // TQ):
                pbt_ref[0, _PBT_DST[t], u] = y[:, u * TQ:(u + 1) * TQ]


def _in_proj(x, ln, w_t, w_r, gain_t, gain_r, cos_t, sin_p, sin_m, cos8, sin8, tm=1024):
    B, S, D = x.shape
    nt = S // tm
    half = ROPE_DIM // 2
    return pl.pallas_call(
        functools.partial(_inproj_kernel, tm=tm),
        grid=(B * nt,),
        in_specs=[
            pl.BlockSpec((1, tm, D), lambda i: (i // nt, i % nt, 0)),
            _const_spec((1, D)),
            _const_spec(w_t.shape), _const_spec(w_r.shape), _const_spec(gain_t.shape), _const_spec(gain_r.shape),
            pl.BlockSpec((tm, LANES), lambda i: (i % nt, 0)),
            pl.BlockSpec((tm, LANES), lambda i: (i % nt, 0)),
            pl.BlockSpec((tm, LANES), lambda i: (i % nt, 0)),
            pl.BlockSpec((half, tm), lambda i: (0, i % nt)),
            pl.BlockSpec((half, tm), lambda i: (0, i % nt)),
        ],
        out_specs=[
            pl.BlockSpec((1, N_PBT, tm // TQ, LANES, TQ), lambda i: (i // nt, 0, i % nt, 0, 0)),
            pl.BlockSpec((1, tm, N_PB * LANES), lambda i: (i // nt, i % nt, 0)),
            pl.BlockSpec((1, tm, N_F32 * LANES), lambda i: (i // nt, i % nt, 0)),
        ],
        out_shape=[
            jax.ShapeDtypeStruct((B, N_PBT, S // TQ, LANES, TQ), BF16),
            jax.ShapeDtypeStruct((B, S, N_PB * LANES), BF16),
            jax.ShapeDtypeStruct((B, S, N_F32 * LANES), F32),
        ],
        compiler_params=_cparams(("parallel",)),
        name="in_proj",
    )(x, ln, w_t, w_r, gain_t, gain_r, cos_t, sin_p, sin_m, cos8, sin8)


def _moba_kernel(qt_ref, k_ref, vt_ref, o_ref, q2_ref, bias_ref, m_ref, acc_ref, kmean_ref, *, n_blk):
    n_pairs = H_MOBA // 2
    nidx = lax.broadcasted_iota(jnp.int32, (16, 2 * TQ), 0)
    kr = lax.broadcasted_iota(jnp.int32, (TQ, TQ), 0)
    ql = lax.broadcasted_iota(jnp.int32, (TQ, TQ), 1)
    causal_bias = jnp.where(kr <= ql, 0.0, NEG)
    for p in range(n_pairs):
        ps = slice(p * LANES, (p + 1) * LANES)
        rows = [jnp.sum(k_ref[0, n * TQ:(n + 1) * TQ, ps].astype(F32), axis=0, keepdims=True)
                for n in range(n_blk)]
        kmean_ref[p] = (jnp.concatenate(rows + [jnp.zeros((16 - n_blk, LANES), F32)], axis=0)
                        * (1.0 / MOBA_BLOCK))
    lax.fori_loop(0, n_blk, functools.partial(_moba_tile, qt_ref, k_ref, vt_ref, o_ref, q2_ref, bias_ref, m_ref,
                                              acc_ref, kmean_ref, nidx, causal_bias, n_blk), 0)


def _moba_tile(qt_ref, k_ref, vt_ref, o_ref, q2_ref, bias_ref, m_ref, acc_ref, kmean_ref, nidx,
               causal_bias, n_blk, qi, carry):
    n_pairs = H_MOBA // 2
    past = nidx < qi
    for p in range(n_pairs):
        km_hi, km_lo = _split_bf16(kmean_ref[p])
        q2 = jnp.concatenate([_half_rows(qt_ref[0, p, qi], hh) for hh in range(2)], axis=1)
        q2_ref[p] = q2
        gate = _dot(km_hi, q2) + _dot(km_lo, q2)
        for n in range(n_blk):
            row = gate[n:n + 1, :]
            ge = jnp.where(gate >= row, 1.0, 0.0)
            gt = jnp.where(gate > row, 1.0, 0.0)
            beats = jnp.where(past, jnp.where(nidx < n, ge, gt), 0.0)
            cnt = jnp.sum(beats, axis=0, keepdims=True)
            bias_ref[p, n] = jnp.where(cnt < MOBA_TOPK, 0.0, NEG)
        _online_init(m_ref, acc_ref, p)

    def scores(off, p):
        return _dot(k_ref[0, pl.ds(off, TQ), p * LANES:(p + 1) * LANES], q2_ref[p])

    def pv(p, kt):
        def fn(pr):
            return jnp.concatenate([_dot(_with_ones(vt_ref[0, p, kt, :HEAD_DIM, :]), pr[:, :TQ]),
                                    _dot(_with_ones(vt_ref[0, p, kt, HEAD_DIM:, :]), pr[:, TQ:])], axis=1)
        return fn

    def past_scores(kt):
        off = pl.multiple_of(kt * TQ, TQ)
        return [scores(off, p) + bias_ref[p, kt] for p in range(n_pairs)]

    def past_tile(kt, s):
        for p in range(n_pairs):
            _online_update(m_ref, acc_ref, p, s[p], pv(p, kt))

    def past_pair(i, _):
        s_a, s_b = past_scores(2 * i), past_scores(2 * i + 1)
        past_tile(2 * i, s_a)
        past_tile(2 * i + 1, s_b)
        return 0

    lax.fori_loop(0, qi // 2, past_pair, 0)

    @pl.when(qi % 2 == 1)
    def _():
        past_tile(qi - 1, past_scores(qi - 1))
    off = pl.multiple_of(qi * TQ, TQ)
    causal2 = jnp.concatenate([causal_bias, causal_bias], axis=1)
    s = [scores(off, p) + causal2 for p in range(n_pairs)]
    for p in range(n_pairs):
        _online_update(m_ref, acc_ref, p, s[p], pv(p, qi))
        o2 = _normalise(acc_ref[p])
        o_ref[0, pl.ds(off, TQ), p * LANES:(p + 1) * LANES] = jnp.concatenate(
            [o2[:, :TQ], o2[:, TQ:]], axis=0).T.astype(BF16)
    return carry


def _moba(pbt, pb):
    B, _, n_t, _, _ = pbt.shape
    S = pb.shape[1]
    n_pairs = H_MOBA // 2
    assert TQ == MOBA_BLOCK and n_t == S // MOBA_BLOCK
    return pl.pallas_call(
        functools.partial(_moba_kernel, n_blk=S // MOBA_BLOCK),
        grid=(B,),
        in_specs=[
            pl.BlockSpec((1, n_pairs, n_t, LANES, TQ), lambda b: (b, PBT_QA // n_pairs, 0, 0, 0)),
            pl.BlockSpec((1, S, QKV_A), lambda b: (b, 0, PB_KA // n_pairs)),
            pl.BlockSpec((1, n_pairs, n_t, LANES, TQ), lambda b: (b, PBT_VA // n_pairs, 0, 0, 0)),
        ],
        out_specs=pl.BlockSpec((1, S, QKV_A), lambda b: (b, 0, 0)),
        out_shape=jax.ShapeDtypeStruct((B, S, QKV_A), BF16),
        scratch_shapes=[pltpu.VMEM((n_pairs, LANES, 2 * TQ), BF16),
                        pltpu.VMEM((n_pairs, S // MOBA_BLOCK, 1, 2 * TQ), F32),
                        pltpu.VMEM((n_pairs, 1, 2 * TQ), F32),
                        pltpu.VMEM((n_pairs, V_ROWS, 2 * TQ), F32),
                        pltpu.VMEM((n_pairs, 16, LANES), F32)],
        compiler_params=_cparams(("parallel",)),
        name="moba",
    )(pbt, pb, pbt)


def _compress_kernel(kc_ref, vc_ref, pek_ref, pev_ref, w1k_ref, w2k_ref, w1v_ref, w2v_ref, gain_ref,
                     cos_ref, sp_ref, sm_ref, ko_ref, vo_ref, *, n_cmp):
    half_len = NSA_CMP_LEN // 2

    def mlp(t_ref, pe_ref, w1_ref, w2_ref):
        first = jnp.zeros((n_cmp, 2 * NSA_CMP_HIDDEN), F32)
        second = jnp.zeros((n_cmp, 2 * NSA_CMP_HIDDEN), F32)
        for l in range(half_len):
            a = t_ref[0, pl.ds(l, n_cmp, stride=NSA_CMP_STRIDE), :]
            first = first + jnp.dot((a + pe_ref[l:l + 1, :]).astype(BF16), w1_ref[l],
                                    preferred_element_type=F32)
            second = second + jnp.dot((a + pe_ref[half_len + l:half_len + l + 1, :]).astype(BF16),
                                      w1_ref[half_len + l], preferred_element_type=F32)
        hid = jax.nn.gelu(first + pltpu.roll(second, n_cmp - 1, 0))
        return jnp.dot(hid.astype(BF16), w2_ref[...], preferred_element_type=F32)

    valid = lax.broadcasted_iota(jnp.int32, (n_cmp, LANES), 0) < n_cmp - 1
    lo = lax.broadcasted_iota(jnp.int32, (n_cmp, LANES), 1) < HEAD_DIM
    kc = mlp(kc_ref, pek_ref, w1k_ref, w2k_ref)
    kc = _norm_rope(kc, gain_ref[...], cos_ref[...], sp_ref[...], sm_ref[...], lo)
    ko_ref[0] = jnp.where(valid, kc, 0.0).astype(BF16)
    vc = mlp(vc_ref, pev_ref, w1v_ref, w2v_ref)
    vo_ref[0] = jnp.where(valid, vc, 0.0).T.astype(BF16)


def _compress(pf, pe_k, pe_v, w1k, w2k, w1v, w2v, gain, cos_c, sp_c, sm_c):
    B, S, _ = pf.shape
    n_cmp = S // NSA_CMP_STRIDE
    cs = _const_spec
    return pl.pallas_call(
        functools.partial(_compress_kernel, n_cmp=n_cmp),
        grid=(B,),
        in_specs=[
            pl.BlockSpec((1, S, LANES), lambda b: (b, 0, 0)),
            pl.BlockSpec((1, S, LANES), lambda b: (b, 0, 1)),
            cs(pe_k.shape), cs(pe_v.shape), cs(w1k.shape), cs(w2k.shape), cs(w1v.shape), cs(w2v.shape),
            cs(gain.shape), cs(cos_c.shape), cs(sp_c.shape), cs(sm_c.shape),
        ],
        out_specs=[pl.BlockSpec((1, n_cmp, LANES), lambda b: (b, 0, 0)),
                   pl.BlockSpec((1, LANES, n_cmp), lambda b: (b, 0, 0))],
        out_shape=[jax.ShapeDtypeStruct((B, n_cmp, LANES), BF16),
                   jax.ShapeDtypeStruct((B, LANES, n_cmp), BF16)],
        compiler_params=_cparams(("parallel",)),
        name="nsa_compress",
    )(pf, pf, pe_k, pe_v, w1k, w2k, w1v, w2v, gain, cos_c, sp_c, sm_c)


def _nsa_kernel(*refs, n_cmp, n_slc, n_t):
    lax.fori_loop(0, n_t, functools.partial(_nsa_tile, refs, n_cmp, n_slc), 0)


def _nsa_tile(refs, n_cmp, n_slc, qi, carry):
    (qt_ref, kc_ref, vct_ref, ks_ref, vst_ref, kw_ref, vwt_ref, g_ref, ovl_ref, o_ref,
     q6_ref, impf_ref, improw_ref, m_ref, acc_ref, ot_ref) = refs
    blk_per_tile = TQ // NSA_SEL_BLOCK
    off = pl.multiple_of(qi * TQ, TQ)
    tpos = qi * TQ + lax.broadcasted_iota(jnp.int32, (1, TQ), 1)
    sig = jax.nn.sigmoid(g_ref[0, pl.ds(off, TQ), :].T)
    kr = lax.broadcasted_iota(jnp.int32, (TQ, TQ), 0)
    ql = lax.broadcasted_iota(jnp.int32, (TQ, TQ), 1)
    causal_bias = jnp.where(kr <= ql, 0.0, NEG)
    nidx = lax.broadcasted_iota(jnp.int32, (n_cmp, TQ), 0)
    cmask = (nidx * NSA_CMP_STRIDE + (NSA_CMP_LEN - 1) <= tpos) & (nidx < n_cmp - 1)
    cmp_bias = jnp.where(cmask, 0.0, NEG)
    jj = lax.broadcasted_iota(jnp.int32, (n_slc, TQ), 0)
    cur = lax.shift_right_arithmetic(tpos, NSA_SEL_BLOCK.bit_length() - 1)
    forced = (jj == 0) | (jj == cur) | (jj == cur - 1)
    gw = NSA_GROUP * TQ

    def half(kk):
        return slice(kk * HEAD_DIM, (kk + 1) * HEAD_DIM)

    def cols(kk):
        return slice(kk * gw, (kk + 1) * gw)

    def tile3(a):
        return jnp.concatenate([a] * NSA_GROUP, axis=1)

    def gate3(kk, branch):
        return jnp.concatenate([sig[3 * (NSA_GROUP * kk + j) + branch:3 * (NSA_GROUP * kk + j) + branch + 1, :]
                                for j in range(NSA_GROUP)], axis=1)

    q6 = jnp.concatenate([_half_rows(qt_ref[0, j, qi], kk) for kk in range(H_NSA_KV) for j in range(NSA_GROUP)],
                         axis=1)
    q6_ref[...] = q6

    s_cmp = _dot(kc_ref[0], q6)
    cmp_bias3 = tile3(cmp_bias)
    for kk in range(H_NSA_KV):
        s = s_cmp[:, cols(kk)] + cmp_bias3
        m = jnp.maximum(jnp.max(s, axis=0, keepdims=True), M_FLOOR)
        p = jnp.exp2(s - m)
        p = p / jnp.maximum(jnp.sum(p, axis=0, keepdims=True), TINY)
        ot_ref[kk] = gate3(kk, 0) * _dot(vct_ref[0, half(kk), :], p.astype(BF16))
        p_sum = (p[:, :TQ] + p[:, TQ:2 * TQ]) + p[:, 2 * TQ:]
        _online_init(m_ref, acc_ref, kk)
        p_hi, p_lo = _split_bf16(p_sum)
        imp = _dot(ovl_ref[...], p_hi) + _dot(ovl_ref[...], p_lo)
        imp = jnp.where(jj > cur, NEG, jnp.where(forced, FORCE, imp))
        impf_ref[kk] = imp
        for j in range(n_slc):
            improw_ref[kk, j] = imp[j:j + 1, :]

    def sel_bias(kk, kt):
        imp = impf_ref[kk]
        parts = []
        for c in range(blk_per_tile):
            j = kt * blk_per_tile + c
            row = improw_ref[kk, j]
            ge = jnp.where(imp >= row, 1.0, 0.0)
            gt = jnp.where(imp > row, 1.0, 0.0)
            cnt = jnp.sum(jnp.where(jj < j, ge, gt), axis=0, keepdims=True)
            parts.append(jnp.broadcast_to(jnp.where(cnt < NSA_N_SEL, 0.0, NEG), (NSA_SEL_BLOCK, TQ)))
        return jnp.concatenate(parts, axis=0)

    def sel_pv(kk, kt):
        return lambda pr: _dot(_with_ones(vst_ref[0, 0, kt, half(kk), :]), pr)

    def sel_scores(kt):
        off = pl.multiple_of(kt * TQ, TQ)
        return _dot(ks_ref[0, pl.ds(off, TQ), :], q6_ref[...])

    def sel_tile(kt, s6):
        for kk in range(H_NSA_KV):
            s = s6[:, cols(kk)] + tile3(sel_bias(kk, kt))
            _online_update(m_ref, acc_ref, kk, s, sel_pv(kk, kt))

    def sel_pair(i, _):
        s_a, s_b = sel_scores(2 * i), sel_scores(2 * i + 1)
        sel_tile(2 * i, s_a)
        sel_tile(2 * i + 1, s_b)
        return 0

    lax.fori_loop(0, qi // 2, sel_pair, 0)

    @pl.when(qi % 2 == 1)
    def _():
        sel_tile(qi - 1, sel_scores(qi - 1))

    kt1 = jnp.maximum(qi - 1, 0)
    kt2 = jnp.maximum(qi - 2, 0)
    ks_d = ks_ref[0, pl.ds(off, TQ), :]
    kw_0 = kw_ref[0, pl.ds(off, TQ), :]
    kw_1 = kw_ref[0, pl.ds(pl.multiple_of(kt1 * TQ, TQ), TQ), :]
    kw_2 = kw_ref[0, pl.ds(pl.multiple_of(kt2 * TQ, TQ), TQ), :]
    win_bias2 = jnp.where(kr > ql + jnp.where(qi >= 2, 0, TQ), 0.0, NEG)
    win_bias1 = jnp.where(qi >= 1, 0.0, NEG)
    q6 = q6_ref[...]
    sd6, w2, w1, w0 = _dot(ks_d, q6), _dot(kw_2, q6), _dot(kw_1, q6), _dot(kw_0, q6)
    causal3 = tile3(causal_bias)
    win_bias2 = tile3(win_bias2)
    for kk in range(H_NSA_KV):
        s = sd6[:, cols(kk)] + tile3(sel_bias(kk, qi) + causal_bias)
        _online_update(m_ref, acc_ref, kk, s, sel_pv(kk, qi))
        o_sel = _normalise(acc_ref[kk])
        s2 = w2[:, cols(kk)] + win_bias2
        s1 = w1[:, cols(kk)] + win_bias1
        s0 = w0[:, cols(kk)] + causal3
        m = jnp.maximum(jnp.maximum(jnp.max(s2, axis=0, keepdims=True), jnp.max(s1, axis=0, keepdims=True)),
                        jnp.max(s0, axis=0, keepdims=True))
        p2, p1, p0 = jnp.exp2(s2 - m), jnp.exp2(s1 - m), jnp.exp2(s0 - m)
        acc = (_dot(_with_ones(vwt_ref[0, 0, kt2, half(kk), :]), p2.astype(BF16))
               + _dot(_with_ones(vwt_ref[0, 0, kt1, half(kk), :]), p1.astype(BF16))
               + _dot(_with_ones(vwt_ref[0, 0, qi, half(kk), :]), p0.astype(BF16)))
        o_win = _normalise(acc)
        ot_ref[kk] = (ot_ref[kk] + gate3(kk, 1) * o_sel) + gate3(kk, 2) * o_win
    for j in range(NSA_GROUP):
        js = slice(j * TQ, (j + 1) * TQ)
        o_ref[0, pl.ds(off, TQ), j * LANES:(j + 1) * LANES] = jnp.concatenate(
            [ot_ref[0, :, js], ot_ref[1, :, js]], axis=0).T.astype(BF16)
    return carry


def _nsa(pbt, pb, pf, kcmp, vcmp_t, ovl_t):
    B, _, n_t, _, _ = pbt.shape
    S = pb.shape[1]
    n_cmp = kcmp.shape[1]
    n_slc = S // NSA_SEL_BLOCK
    gw = NSA_GROUP * TQ
    assert NSA_WINDOW == 2 * TQ and TQ % NSA_SEL_BLOCK == 0 and S % TQ == 0
    return pl.pallas_call(
        functools.partial(_nsa_kernel, n_cmp=n_cmp, n_slc=n_slc, n_t=n_t),
        grid=(B,),
        in_specs=[
            pl.BlockSpec((1, NSA_GROUP, n_t, LANES, TQ), lambda b: (b, PBT_QB // NSA_GROUP, 0, 0, 0)),
            pl.BlockSpec((1, n_cmp, LANES), lambda b: (b, 0, 0)),
            pl.BlockSpec((1, LANES, n_cmp), lambda b: (b, 0, 0)),
            pl.BlockSpec((1, S, LANES), lambda b: (b, 0, PB_KS)),
            pl.BlockSpec((1, 1, n_t, LANES, TQ), lambda b: (b, PBT_VS, 0, 0, 0)),
            pl.BlockSpec((1, S, LANES), lambda b: (b, 0, PB_KW)),
            pl.BlockSpec((1, 1, n_t, LANES, TQ), lambda b: (b, PBT_VW, 0, 0, 0)),
            pl.BlockSpec((1, S, LANES), lambda b: (b, 0, 2)),
            _const_spec(ovl_t.shape),
        ],
        out_specs=pl.BlockSpec((1, S, Q_B), lambda b: (b, 0, 0)),
        out_shape=jax.ShapeDtypeStruct((B, S, Q_B), BF16),
        scratch_shapes=[pltpu.VMEM((LANES, H_NSA * TQ), BF16),
                        pltpu.VMEM((H_NSA_KV, n_slc, TQ), F32), pltpu.VMEM((H_NSA_KV, n_slc, 1, TQ), F32),
                        pltpu.VMEM((H_NSA_KV, 1, gw), F32),
                        pltpu.VMEM((H_NSA_KV, V_ROWS, gw), F32),
                        pltpu.VMEM((H_NSA_KV, HEAD_DIM, gw), F32)],
        compiler_params=_cparams(("parallel",)),
        name="nsa_attn",
    )(pbt, kcmp, vcmp_t, pb, pbt, pb, pbt, pf, ovl_t)


DIL_TILES = 16

def _dilated_kernel(q_ref, k_ref, v_ref, o_ref, qf_ref, kf_ref, vf_ref, os_ref, ls_ref, *, seq):
    blk = BAND_BLOCK
    n_tiles = seq // blk
    lo = lax.broadcasted_iota(jnp.int32, (blk, LANES), 1) < HEAD_DIM
    for g, (window, dil) in enumerate(DIL_CFG):
        max_dist = window // dil
        tiles_per_class = n_tiles // dil
        use_prev = tiles_per_class > 1
        n_keys = 2 * blk if use_prev else blk
        ri = lax.broadcasted_iota(jnp.int32, (2 * blk, n_keys), 0) & (blk - 1)
        ci = lax.broadcasted_iota(jnp.int32, (2 * blk, n_keys), 1)
        dist = ri + (n_keys - blk) - ci
        band = (dist >= 0) & (dist <= max_dist)
        band_bias = jnp.where(band, 0.0, NEG)
        qf_ref[g] = q_ref[0, :, g * LANES:(g + 1) * LANES].astype(F32)
        kf_ref[g] = k_ref[0, :, g * LANES:(g + 1) * LANES].astype(F32)
        vf_ref[g] = v_ref[0, :, g * LANES:(g + 1) * LANES].astype(F32)

        def index(start, dil=dil):
            if dil == 1:
                return pl.ds(pl.multiple_of(start, blk), blk)
            return pl.ds(start, blk, stride=dil)

        def tiles(i, _, g=g, dil=dil, tiles_per_class=tiles_per_class, use_prev=use_prev, band=band,
                  band_bias=band_bias, ci=ci, index=index):
            scored = []
            for u in range(DIL_TILES):
                t = i * DIL_TILES + u
                cls = t // tiles_per_class
                mt = t % tiles_per_class
                start = cls + dil * blk * mt
                q = qf_ref[g, index(start), :]
                q2 = jnp.concatenate([jnp.where(lo, q, 0.0), jnp.where(lo, 0.0, q)], axis=0).astype(BF16)
                if use_prev:
                    prev = jnp.maximum(start - dil * blk, cls)
                    k2 = jnp.concatenate([kf_ref[g, index(prev), :], kf_ref[g, index(start), :]], axis=0)
                    v2 = jnp.concatenate([vf_ref[g, index(prev), :], vf_ref[g, index(start), :]], axis=0)
                    bias = jnp.where(band & (ci >= jnp.where(mt > 0, 0, blk)), 0.0, NEG)
                else:
                    k2, v2, bias = kf_ref[g, index(start), :], vf_ref[g, index(start), :], band_bias
                s = lax.dot_general(q2, k2.astype(BF16), (((1,), (1,)), ((), ())), preferred_element_type=F32)
                scored.append((start, s + bias, v2.astype(BF16)))
            probs = []
            for start, s, v2 in scored:
                m = jnp.max(s, axis=-1, keepdims=True)
                p = jnp.exp2(s - m)
                den = jnp.sum(p, axis=-1, keepdims=True)
                lse = m + jnp.log2(jnp.maximum(den, TINY))
                probs.append((start, lse, (p / jnp.maximum(den, TINY)).astype(BF16), v2))
            for start, lse, p, v2 in probs:
                o = _dot(p, v2)
                os_ref[g, index(start), :] = jnp.where(lo, o[:blk], o[blk:])
                ls_ref[g, index(start), :] = jnp.where(lo, lse[:blk], lse[blk:])
            return 0

        lax.fori_loop(0, n_tiles // DIL_TILES, tiles, 0)

    rows_per_trip = DIL_TILES * blk

    def combine(t, _):
        idx = pl.ds(pl.multiple_of(t * rows_per_trip, rows_per_trip), rows_per_trip)
        lses = [ls_ref[g, idx, :] for g in range(len(DIL_CFG))]
        mx = jnp.maximum(jnp.maximum(lses[0], lses[1]), lses[2])
        es = [jnp.exp2(l - mx) for l in lses]
        tot = es[0] + es[1] + es[2]
        for g in range(len(DIL_CFG)):
            o_ref[0, idx, g * LANES:(g + 1) * LANES] = (os_ref[g, idx, :] * (es[g] / tot)).astype(BF16)
        return 0

    lax.fori_loop(0, seq // rows_per_trip, combine, 0)


def _dilated(pb):
    B, S, _ = pb.shape
    n_g = len(DIL_CFG)
    return pl.pallas_call(
        functools.partial(_dilated_kernel, seq=S),
        grid=(B,),
        in_specs=[pl.BlockSpec((1, S, QKV_C), lambda b: (b, 0, PB_QC // n_g)),
                  pl.BlockSpec((1, S, QKV_C), lambda b: (b, 0, PB_KC // n_g)),
                  pl.BlockSpec((1, S, QKV_C), lambda b: (b, 0, PB_VC // n_g))],
        out_specs=pl.BlockSpec((1, S, QKV_C), lambda b: (b, 0, 0)),
        out_shape=jax.ShapeDtypeStruct((B, S, QKV_C), BF16),
        scratch_shapes=[pltpu.VMEM((n_g, S, LANES), F32)] * 5,
        compiler_params=_cparams(("parallel",)),
        name="dilated_attn",
    )(pb, pb, pb)


HALO = 16
FF_CHUNK = 256


def _proj_ffn_kernel(x_ref, xh_ref, oa_ref, oah_ref, ob_ref, obh_ref, oc_ref, och_ref, wo_ref, ln_ref,
                     wg_ref, wu_ref, cw_ref, cb_ref, wd_ref, y_ref, o_ref, h_ref, x1_ref, a_ref,
                     *, tiles_per_seq):
    i = pl.program_id(0)

    def norm(v):
        return (v * lax.rsqrt(jnp.mean(v * v, axis=-1, keepdims=True) + EPS)) * ln_ref[...]

    col = 0
    for main, halo in ((oa_ref, oah_ref), (ob_ref, obh_ref), (oc_ref, och_ref)):
        width = main.shape[1]
        o_ref[:HALO, col:col + width] = halo[...]
        o_ref[HALO:, col:col + width] = main[...]
        col += width
    attn = _dot(o_ref[...], wo_ref[...])
    x1 = x_ref[...] + attn[HALO:]
    x1_ref[...] = x1
    first = (i % tiles_per_seq) == 0
    h_ref[:HALO, :] = jnp.where(first, 0.0, norm(xh_ref[...] + attn[:HALO])).astype(BF16)
    h_ref[HALO:, :] = norm(x1).astype(BF16)

    for f0 in range(0, D_FF, FF_CHUNK):
        fs = slice(f0, min(f0 + FF_CHUNK, D_FF))
        g = _dot(h_ref[...], wg_ref[:, fs])
        u = _dot(h_ref[HALO:, :], wu_ref[:, fs])
        gc = (cw_ref[0:1, fs] * pltpu.roll(g, 2, 0)[HALO:] + cw_ref[1:2, fs] * pltpu.roll(g, 1, 0)[HALO:]
              + cw_ref[2:3, fs] * g[HALO:] + cb_ref[:, fs])
        a_ref[:, fs] = (jax.nn.silu(gc) * u).astype(BF16)
    y_ref[...] = x1_ref[...] + _dot(a_ref[...], wd_ref[...])


def _proj_ffn(x2, o_a, o_b, o_c, wo, ln, wg, wu, cw, cb, wd, seq, tm=512):
    M, D = x2.shape
    halo_blocks = tm // HALO
    main = lambda w: pl.BlockSpec((tm, w), lambda i: (i, 0))
    halo = lambda w: pl.BlockSpec((HALO, w), lambda i: (jnp.maximum(i * halo_blocks - 1, 0), 0))
    return pl.pallas_call(
        functools.partial(_proj_ffn_kernel, tiles_per_seq=seq // tm),
        grid=(M // tm,),
        in_specs=[
            main(D), halo(D), main(QKV_A), halo(QKV_A), main(Q_B), halo(Q_B), main(QKV_C), halo(QKV_C),
            _const_spec(wo.shape), _const_spec((1, D)),
            _const_spec(wg.shape), _const_spec(wu.shape), _const_spec(cw.shape), _const_spec(cb.shape),
            _const_spec(wd.shape),
        ],
        out_specs=main(D),
        out_shape=jax.ShapeDtypeStruct((M, D), F32),
        scratch_shapes=[pltpu.VMEM((HALO + tm, D), BF16), pltpu.VMEM((HALO + tm, D), BF16),
                        pltpu.VMEM((tm, D), F32), pltpu.VMEM((tm, D_FF), BF16)],
        compiler_params=_cparams(("parallel",)),
        name="proj_ffn",
    )(x2, x2, o_a, o_a, o_b, o_b, o_c, o_c, wo, ln, wg, wu, cw, cb, wd)


def _rope_tables(pos):
    half = ROPE_DIM // 2
    inv_freq = ROPE_THETA ** (-jnp.arange(half, dtype=F32) * 2.0 / ROPE_DIM)
    ang = pos.astype(F32)[:, None] * inv_freq
    c, s = jnp.cos(ang), jnp.sin(ang)
    n = pos.shape[0]
    rest = HEAD_DIM - ROPE_DIM
    cos_h = jnp.concatenate([c, c, jnp.ones((n, rest), F32)], axis=-1)
    sp_h = jnp.concatenate([jnp.zeros((n, half), F32), s, jnp.zeros((n, rest), F32)], axis=-1)
    sm_h = jnp.concatenate([-s, jnp.zeros((n, half + rest), F32)], axis=-1)
    two = lambda t: jnp.concatenate([t, t], axis=-1)
    return two(cos_h), two(sp_h), two(sm_h), c.T, s.T


def _pair_diag(w):
    z = jnp.zeros_like(w)
    return jnp.concatenate([jnp.concatenate([w, z], axis=-1), jnp.concatenate([z, w], axis=-1)], axis=-2)


def _overlap_t(seq):
    n_cmp = seq // NSA_CMP_STRIDE
    n_slc = seq // NSA_SEL_BLOCK
    starts = np.arange(n_cmp) * NSA_CMP_STRIDE
    j = np.arange(n_slc)
    ovl = (starts[None, :] < (j[:, None] + 1) * NSA_SEL_BLOCK) & (starts[None, :] + NSA_CMP_LEN > j[:, None] * NSA_SEL_BLOCK)
    ovl[:, n_cmp - 1] = False
    return jnp.asarray(ovl, BF16)


def _mixers(x, seq_tabs, cmp_tabs, ovl_t, ln1, w, qn_a, kn_a, qn_b, kn_b, pe_k, pe_v, wk1, wk2, wv1, wv2,
            qn_c, kn_c):
    rep = lambda g, n: jnp.tile(g, n)
    w_t, w_r = w
    gain_t = jnp.concatenate([rep(qn_a * Q_SCALE, H_MOBA), rep(qn_b * Q_SCALE, H_NSA)])
    gain_t = jnp.broadcast_to(gain_t[:, None], (gain_t.shape[0], LANES))
    gain_r = jnp.concatenate([rep(kn_a, H_MOBA), rep(kn_b[1], H_NSA_KV), rep(kn_b[2], H_NSA_KV),
                              rep(qn_c * Q_SCALE, H_DIL), rep(kn_c, H_DIL)])[None, :]
    pbt, pb, pf = _in_proj(x, ln1[None, :], w_t, w_r, gain_t, gain_r, *seq_tabs)

    o_a = _moba(pbt, pb)
    two = lambda t: jnp.concatenate([t, t], axis=-1)
    w1 = lambda t: _pair_diag(t.reshape(NSA_CMP_LEN, HEAD_DIM, NSA_CMP_HIDDEN)).astype(BF16)
    kcmp, vcmp_t = _compress(pf, two(pe_k), two(pe_v), w1(wk1), _pair_diag(wk2).astype(BF16),
                             w1(wv1), _pair_diag(wv2).astype(BF16), two(kn_b[0])[None, :], *cmp_tabs[:3])
    o_b = _nsa(pbt, pb, pf, kcmp, vcmp_t, ovl_t)
    o_c = _dilated(pb)
    return o_a, o_b, o_c


def _layer(x, seq_tabs, cmp_tabs, ovl_t, ln1, w_in, qn_a, kn_a, qn_b, kn_b, pe_k, pe_v, wk1, wk2, wv1, wv2,
           qn_c, kn_c, w_out, ln2, w_gate, w_up, conv_w, conv_b, w_down):
    B, S, D = x.shape
    o_a, o_b, o_c = _mixers(x, seq_tabs, cmp_tabs, ovl_t, ln1, w_in, qn_a, kn_a, qn_b, kn_b, pe_k, pe_v,
                            wk1, wk2, wv1, wv2, qn_c, kn_c)
    nsa_rows = [w_out[QKV_A + h * HEAD_DIM:QKV_A + (h + 1) * HEAD_DIM] for h in NSA_ORDER]
    w_o = jnp.concatenate([w_out[:QKV_A]] + nsa_rows + [w_out[QKV_A + Q_B:]], axis=0).astype(BF16)
    M = B * S
    x2 = _proj_ffn(x.reshape(M, D), o_a.reshape(M, QKV_A), o_b.reshape(M, Q_B), o_c.reshape(M, QKV_C), w_o,
                   ln2[None, :], w_gate.astype(BF16), w_up.astype(BF16), conv_w, conv_b[None, :],
                   w_down.astype(BF16), S)
    return x2.reshape(B, S, D)


def kernel(x, ln1, w_in, qn_a, kn_a, qn_b, kn_b, cmp_pe_k, cmp_pe_v, cmp_k_w1, cmp_k_w2, cmp_v_w1, cmp_v_w2,
           qn_c, kn_c, w_out, ln2, w_gate, w_up, conv_w, conv_b, w_down):
    S = x.shape[1]
    seq_tabs = _rope_tables(jnp.arange(S))
    n_cmp = S // NSA_CMP_STRIDE
    cmp_tabs = _rope_tables(jnp.arange(n_cmp) * NSA_CMP_STRIDE + (NSA_CMP_LEN - 1))
    ovl_t = _overlap_t(S)
    for l in range(ln1.shape[0]):
        x = _layer(x, seq_tabs, cmp_tabs, ovl_t, ln1[l], _prep_w_in(w_in, l), qn_a[l], kn_a[l], qn_b[l], kn_b[l],
                   cmp_pe_k[l], cmp_pe_v[l], cmp_k_w1[l], cmp_k_w2[l], cmp_v_w1[l], cmp_v_w2[l], qn_c[l], kn_c[l],
                   w_out[l], ln2[l], w_gate[l], w_up[l], conv_w[l], conv_b[l], w_down[l])
    return x
```

```python
import functools

import numpy as np
import jax
import jax.numpy as jnp
from jax import lax
from jax.experimental import pallas as pl
from jax.experimental.pallas import tpu as pltpu

F32 = jnp.float32
BF16 = jnp.bfloat16

D_MODEL = 1024
HEAD_DIM = 64
N_HEADS = D_MODEL // HEAD_DIM
H_MOBA = N_HEADS // 4
H_NSA = (N_HEADS - H_MOBA) // 2
H_DIL = N_HEADS - H_MOBA - H_NSA
H_NSA_KV = 2
NSA_GROUP = H_NSA // H_NSA_KV
ROPE_DIM = HEAD_DIM // 4
ROPE_THETA = 500000.0
MOBA_BLOCK = 256
MOBA_TOPK = 3
NSA_CMP_LEN = 32
NSA_CMP_STRIDE = 16
NSA_CMP_HIDDEN = 128
NSA_SEL_BLOCK = 64
NSA_N_SEL = 6
NSA_WINDOW = 512
DIL_CFG = ((128, 1), (512, 4), (2048, 16))
BAND_BLOCK = 128
D_FF = 2816
EPS = 1e-6
NEG = -1e30
TINY = 1e-30
FORCE = 1e9
SCALE = HEAD_DIM ** -0.5
Q_SCALE = SCALE * float(np.log2(np.e))
QKV_A = H_MOBA * HEAD_DIM
Q_B = H_NSA * HEAD_DIM
KV_B = H_NSA_KV * HEAD_DIM
G_B = H_NSA * 3
QKV_C = H_DIL * HEAD_DIM
IN_SPLITS = (QKV_A, QKV_A, QKV_A, Q_B, KV_B, KV_B, KV_B, KV_B, KV_B, KV_B, G_B, QKV_C, QKV_C, QKV_C)

LANES = 128
TQ = 256
NSA_ORDER = (0, 3, 1, 4, 2, 5)
VMEM_LIMIT = 56 * 1024 * 1024

M_FLOOR = -1e29
V_ROWS = HEAD_DIM + 16

T_GROUP = ("qa", "qb", "va", "vsb", "vwb")
N_T_NORM, N_T = 5, 9
R_GROUP = ("ka", "ksb", "kwb", "qc", "kc", "vc", "kcb", "vcb", "gb")
N_R_NORM, N_R_BF16, N_R = 10, 13, 16
N_PBT, N_PB, N_F32 = N_T, N_R_BF16, N_R - N_R_BF16
PBT_QA, PBT_VA, PBT_VS, PBT_VW, PBT_QB = 0, 2, 4, 5, 6
PB_QC, PB_KC, PB_KA, PB_KS, PB_VC, PB_KW = 0, 3, 6, 8, 9, 12
_PBT_DST = (0, 1, 6, 7, 8, 2, 3, 4, 5)
_PB_DST = (6, 7, 8, 12, 0, 1, 2, 3, 4, 5, 9, 10, 11)
CHUNK = 4


def _cparams(sem):
    return pltpu.CompilerParams(dimension_semantics=sem, vmem_limit_bytes=VMEM_LIMIT)


def _const_spec(shape):
    return pl.BlockSpec(shape, lambda *_: (0,) * len(shape), pipeline_mode=pl.Buffered(1))


def _norm_rope(y, gain, cos_t, sin_p, sin_m, lo):
    ss = y * y
    s_lo = jnp.sum(jnp.where(lo, ss, 0.0), axis=-1, keepdims=True)
    s_hi = jnp.sum(jnp.where(lo, 0.0, ss), axis=-1, keepdims=True)
    ms = jnp.where(lo, s_lo, s_hi) * (1.0 / HEAD_DIM)
    z = (y * lax.rsqrt(ms + EPS)) * gain
    half = ROPE_DIM // 2
    return z * cos_t + pltpu.roll(z, half, 1) * sin_p + pltpu.roll(z, LANES - half, 1) * sin_m


def _online_update(m_ref, acc_ref, g, s, pv):
    m_old = m_ref[g]
    m_new = jnp.maximum(m_old, jnp.max(s, axis=0, keepdims=True))
    p = jnp.exp2(s - m_new)
    alpha = jnp.exp2(m_old - m_new)
    acc_ref[g] = alpha * acc_ref[g] + pv(p.astype(BF16))
    m_ref[g] = m_new


def _online_init(m_ref, acc_ref, g):
    m_ref[g] = jnp.full(m_ref.shape[1:], M_FLOOR, F32)
    acc_ref[g] = jnp.zeros(acc_ref.shape[1:], F32)


def _with_ones(v_t):
    row = lax.broadcasted_iota(jnp.int32, (V_ROWS - HEAD_DIM, v_t.shape[1]), 0)
    return jnp.concatenate([v_t, jnp.where(row == 0, 1.0, 0.0).astype(v_t.dtype)], axis=0)


def _normalise(acc):
    return acc[:HEAD_DIM] / jnp.maximum(acc[HEAD_DIM:HEAD_DIM + 1], TINY)


def _dot(a, b):
    return jnp.dot(a, b, preferred_element_type=F32)


def _half_rows(q_t, hh):
    z = jnp.zeros((HEAD_DIM, q_t.shape[1]), q_t.dtype)
    if hh == 0:
        return jnp.concatenate([q_t[:HEAD_DIM], z], axis=0)
    return jnp.concatenate([z, q_t[HEAD_DIM:]], axis=0)


def _split_bf16(a):
    hi = a.astype(BF16)
    return hi, (a - hi.astype(F32)).astype(BF16)


def _in_proj_plan(group):
    names = ("qa", "ka", "va", "qb", "kcb", "vcb", "ksb", "vsb", "kwb", "vwb", "gb", "qc", "kc", "vc")
    start = dict(zip(names, np.cumsum((0,) + IN_SPLITS[:-1]).tolist()))
    width = dict(zip(names, IN_SPLITS))
    plan = []
    for name in group:
        if name == "qb":
            heads = [start[name] + h * HEAD_DIM for h in NSA_ORDER]
            plan += [[(heads[2 * j], HEAD_DIM), (heads[2 * j + 1], HEAD_DIM)] for j in range(H_NSA // 2)]
        else:
            plan += [[(start[name] + c, min(LANES, width[name] - c))] for c in range(0, width[name], LANES)]
    return plan


def _wprep_kernel(w_ref, wt_ref, wr_ref, *, plan_t, plan_r, n_in):
    def take(a, width):
        b0 = a // LANES * LANES
        v0 = w_ref[0, :, b0:min(b0 + LANES, n_in)]
        if a + width <= b0 + LANES:
            return v0[:, a - b0:a - b0 + width]
        v1 = w_ref[0, :, b0 + LANES:min(b0 + 2 * LANES, n_in)]
        return jnp.concatenate([v0[:, a - b0:], v1[:, :a + width - b0 - LANES]], axis=1)

    def block(pieces):
        cols = [take(a, width) for a, width in pieces]
        filled = sum(width for _, width in pieces)
        if filled < LANES:
            cols.append(jnp.zeros((w_ref.shape[1], LANES - filled), F32))
        return cols[0] if len(cols) == 1 else jnp.concatenate(cols, axis=1)

    for d, pieces in enumerate(plan_t):
        wt_ref[d * LANES:(d + 1) * LANES, :] = block(pieces).T.astype(BF16)
    for d, pieces in enumerate(plan_r):
        wr_ref[:, d * LANES:(d + 1) * LANES] = block(pieces).astype(BF16)


def _prep_w_in(w_in, layer, tr=256):
    _, D, n_in = w_in.shape
    plan_t, plan_r = _in_proj_plan(T_GROUP), _in_proj_plan(R_GROUP)
    return pl.pallas_call(
        functools.partial(_wprep_kernel, plan_t=plan_t, plan_r=plan_r, n_in=n_in),
        grid=(D // tr,),
        in_specs=[pl.BlockSpec((1, tr, n_in), lambda i: (layer, i, 0))],
        out_specs=[pl.BlockSpec((N_T * LANES, tr), lambda i: (0, i)),
                   pl.BlockSpec((tr, N_R * LANES), lambda i: (i, 0))],
        out_shape=[jax.ShapeDtypeStruct((N_T * LANES, D), BF16), jax.ShapeDtypeStruct((D, N_R * LANES), BF16)],
        compiler_params=_cparams(("parallel",)),
        name="w_in_layout",
    )(w_in)


def _norm_rope_t(y, gain, cos8, sin8):
    half = ROPE_DIM // 2
    parts = []
    for hh in range(2):
        yh = y[hh * HEAD_DIM:(hh + 1) * HEAD_DIM]
        ms = jnp.sum(yh * yh, axis=0, keepdims=True) * (1.0 / HEAD_DIM)
        z = (yh * lax.rsqrt(ms + EPS)) * gain[hh * HEAD_DIM:(hh + 1) * HEAD_DIM]
        x1, x2 = z[:half], z[half:ROPE_DIM]
        parts += [x1 * cos8 - x2 * sin8, x2 * cos8 + x1 * sin8, z[ROPE_DIM:]]
    return jnp.concatenate(parts, axis=0)


def _inproj_kernel(x_ref, ln_ref, wt_ref, wr_ref, gt_ref, gr_ref, cos_ref, sp_ref, sm_ref, ct_ref, st_ref,
                   pbt_ref, pb_ref, pf_ref, *, tm):
    x = x_ref[0]
    h = x * lax.rsqrt(jnp.mean(x * x, axis=-1, keepdims=True) + EPS)
    h = (h * ln_ref[...]).astype(BF16)
    lo = lax.broadcasted_iota(jnp.int32, (tm, LANES), 1) < HEAD_DIM
    cos_t, sin_p, sin_m = cos_ref[...], sp_ref[...], sm_ref[...]
    for c0 in range(0, N_R, CHUNK):
        y4 = _dot(h, wr_ref[:, c0 * LANES:(c0 + CHUNK) * LANES])
        for c in range(c0, c0 + CHUNK):
            y = y4[:, (c - c0) * LANES:(c - c0 + 1) * LANES]
            if c < N_R_NORM:
                y = _norm_rope(y, gr_ref[:, c * LANES:(c + 1) * LANES], cos_t, sin_p, sin_m, lo)
            if c < N_R_BF16:
                d = _PB_DST[c]
                pb_ref[0, :, d * LANES:(d + 1) * LANES] = y.astype(BF16)
            else:
                d = c - N_R_BF16
                pf_ref[0, :, d * LANES:(d + 1) * LANES] = y
    cos8, sin8 = ct_ref[...], st_ref[...]
    for t0, t1 in ((0, N_T_NORM), (N_T_NORM, N_T)):
        y_t = lax.dot_general(wt_ref[t0 * LANES:t1 * LANES, :], h, (((1,), (1,)), ((), ())),
                              preferred_element_type=F32)
        for t in range(t0, t1):
            y = y_t[(t - t0) * LANES:(t - t0 + 1) * LANES]
            if t < N_T_NORM:
                gain = jnp.concatenate([gt_ref[t * LANES:(t + 1) * LANES, :]] * (tm // LANES), axis=1)
                y = _norm_rope_t(y, gain, cos8, sin8)
            y = y.astype(BF16)
            for u in range(tm // TQ):
                pbt_ref[0, _PBT_DST[t], u] = y[:, u * TQ:(u + 1) * TQ]


def _in_proj(x, ln, w_t, w_r, gain_t, gain_r, cos_t, sin_p, sin_m, cos8, sin8, tm=1024):
    B, S, D = x.shape
    nt = S // tm
    half = ROPE_DIM // 2
    return pl.pallas_call(
        functools.partial(_inproj_kernel, tm=tm),
        grid=(B * nt,),
        in_specs=[
            pl.BlockSpec((1, tm, D), lambda i: (i // nt, i % nt, 0)),
            _const_spec((1, D)),
            _const_spec(w_t.shape), _const_spec(w_r.shape), _const_spec(gain_t.shape), _const_spec(gain_r.shape),
            pl.BlockSpec((tm, LANES), lambda i: (i % nt, 0)),
            pl.BlockSpec((tm, LANES), lambda i: (i % nt, 0)),
            pl.BlockSpec((tm, LANES), lambda i: (i % nt, 0)),
            pl.BlockSpec((half, tm), lambda i: (0, i % nt)),
            pl.BlockSpec((half, tm), lambda i: (0, i % nt)),
        ],
        out_specs=[
            pl.BlockSpec((1, N_PBT, tm // TQ, LANES, TQ), lambda i: (i // nt, 0, i % nt, 0, 0)),
            pl.BlockSpec((1, tm, N_PB * LANES), lambda i: (i // nt, i % nt, 0)),
            pl.BlockSpec((1, tm, N_F32 * LANES), lambda i: (i // nt, i % nt, 0)),
        ],
        out_shape=[
            jax.ShapeDtypeStruct((B, N_PBT, S // TQ, LANES, TQ), BF16),
            jax.ShapeDtypeStruct((B, S, N_PB * LANES), BF16),
            jax.ShapeDtypeStruct((B, S, N_F32 * LANES), F32),
        ],
        compiler_params=_cparams(("parallel",)),
        name="in_proj",
    )(x, ln, w_t, w_r, gain_t, gain_r, cos_t, sin_p, sin_m, cos8, sin8)


def _moba_kernel(qt_ref, k_ref, vt_ref, o_ref, q2_ref, bias_ref, m_ref, acc_ref, kmean_ref, *, n_blk):
    n_pairs = H_MOBA // 2
    nidx = lax.broadcasted_iota(jnp.int32, (16, 2 * TQ), 0)
    kr = lax.broadcasted_iota(jnp.int32, (TQ, TQ), 0)
    ql = lax.broadcasted_iota(jnp.int32, (TQ, TQ), 1)
    causal_bias = jnp.where(kr <= ql, 0.0, NEG)
    for p in range(n_pairs):
        ps = slice(p * LANES, (p + 1) * LANES)
        rows = [jnp.sum(k_ref[0, n * TQ:(n + 1) * TQ, ps].astype(F32), axis=0, keepdims=True)
                for n in range(n_blk)]
        kmean_ref[p] = (jnp.concatenate(rows + [jnp.zeros((16 - n_blk, LANES), F32)], axis=0)
                        * (1.0 / MOBA_BLOCK))
    lax.fori_loop(0, n_blk, functools.partial(_moba_tile, qt_ref, k_ref, vt_ref, o_ref, q2_ref, bias_ref, m_ref,
                                              acc_ref, kmean_ref, nidx, causal_bias, n_blk), 0)


def _moba_tile(qt_ref, k_ref, vt_ref, o_ref, q2_ref, bias_ref, m_ref, acc_ref, kmean_ref, nidx,
               causal_bias, n_blk, qi, carry):
    n_pairs = H_MOBA // 2
    past = nidx < qi
    for p in range(n_pairs):
        km_hi, km_lo = _split_bf16(kmean_ref[p])
        q2 = jnp.concatenate([_half_rows(qt_ref[0, p, qi], hh) for hh in range(2)], axis=1)
        q2_ref[p] = q2
        gate = _dot(km_hi, q2) + _dot(km_lo, q2)
        for n in range(n_blk):
            row = gate[n:n + 1, :]
            ge = jnp.where(gate >= row, 1.0, 0.0)
            gt = jnp.where(gate > row, 1.0, 0.0)
            beats = jnp.where(past, jnp.where(nidx < n, ge, gt), 0.0)
            cnt = jnp.sum(beats, axis=0, keepdims=True)
            bias_ref[p, n] = jnp.where(cnt < MOBA_TOPK, 0.0, NEG)
        _online_init(m_ref, acc_ref, p)

    def scores(off, p):
        return _dot(k_ref[0, pl.ds(off, TQ), p * LANES:(p + 1) * LANES], q2_ref[p])

    def pv(p, kt):
        def fn(pr):
            return jnp.concatenate([_dot(_with_ones(vt_ref[0, p, kt, :HEAD_DIM, :]), pr[:, :TQ]),
                                    _dot(_with_ones(vt_ref[0, p, kt, HEAD_DIM:, :]), pr[:, TQ:])], axis=1)
        return fn

    def past_scores(kt):
        off = pl.multiple_of(kt * TQ, TQ)
        return [scores(off, p) + bias_ref[p, kt] for p in range(n_pairs)]

    def past_tile(kt, s):
        for p in range(n_pairs):
            _online_update(m_ref, acc_ref, p, s[p], pv(p, kt))

    def past_pair(i, _):
        s_a, s_b = past_scores(2 * i), past_scores(2 * i + 1)
        past_tile(2 * i, s_a)
        past_tile(2 * i + 1, s_b)
        return 0

    lax.fori_loop(0, qi // 2, past_pair, 0)

    @pl.when(qi % 2 == 1)
    def _():
        past_tile(qi - 1, past_scores(qi - 1))
    off = pl.multiple_of(qi * TQ, TQ)
    causal2 = jnp.concatenate([causal_bias, causal_bias], axis=1)
    s = [scores(off, p) + causal2 for p in range(n_pairs)]
    for p in range(n_pairs):
        _online_update(m_ref, acc_ref, p, s[p], pv(p, qi))
        o2 = _normalise(acc_ref[p])
        o_ref[0, pl.ds(off, TQ), p * LANES:(p + 1) * LANES] = jnp.concatenate(
            [o2[:, :TQ], o2[:, TQ:]], axis=0).T.astype(BF16)
    return carry


def _moba(pbt, pb):
    B, _, n_t, _, _ = pbt.shape
    S = pb.shape[1]
    n_pairs = H_MOBA // 2
    assert TQ == MOBA_BLOCK and n_t == S // MOBA_BLOCK
    return pl.pallas_call(
        functools.partial(_moba_kernel, n_blk=S // MOBA_BLOCK),
        grid=(B,),
        in_specs=[
            pl.BlockSpec((1, n_pairs, n_t, LANES, TQ), lambda b: (b, PBT_QA // n_pairs, 0, 0, 0)),
            pl.BlockSpec((1, S, QKV_A), lambda b: (b, 0, PB_KA // n_pairs)),
            pl.BlockSpec((1, n_pairs, n_t, LANES, TQ), lambda b: (b, PBT_VA // n_pairs, 0, 0, 0)),
        ],
        out_specs=pl.BlockSpec((1, S, QKV_A), lambda b: (b, 0, 0)),
        out_shape=jax.ShapeDtypeStruct((B, S, QKV_A), BF16),
        scratch_shapes=[pltpu.VMEM((n_pairs, LANES, 2 * TQ), BF16),
                        pltpu.VMEM((n_pairs, S // MOBA_BLOCK, 1, 2 * TQ), F32),
                        pltpu.VMEM((n_pairs, 1, 2 * TQ), F32),
                        pltpu.VMEM((n_pairs, V_ROWS, 2 * TQ), F32),
                        pltpu.VMEM((n_pairs, 16, LANES), F32)],
        compiler_params=_cparams(("parallel",)),
        name="moba",
    )(pbt, pb, pbt)


def _compress_kernel(kc_ref, vc_ref, pek_ref, pev_ref, w1k_ref, w2k_ref, w1v_ref, w2v_ref, gain_ref,
                     cos_ref, sp_ref, sm_ref, ko_ref, vo_ref, *, n_cmp):
    half_len = NSA_CMP_LEN // 2

    def mlp(t_ref, pe_ref, w1_ref, w2_ref):
        first = jnp.zeros((n_cmp, 2 * NSA_CMP_HIDDEN), F32)
        second = jnp.zeros((n_cmp, 2 * NSA_CMP_HIDDEN), F32)
        for l in range(half_len):
            a = t_ref[0, pl.ds(l, n_cmp, stride=NSA_CMP_STRIDE), :]
            first = first + jnp.dot((a + pe_ref[l:l + 1, :]).astype(BF16), w1_ref[l],
                                    preferred_element_type=F32)
            second = second + jnp.dot((a + pe_ref[half_len + l:half_len + l + 1, :]).astype(BF16),
                                      w1_ref[half_len + l], preferred_element_type=F32)
        hid = jax.nn.gelu(first + pltpu.roll(second, n_cmp - 1, 0))
        return jnp.dot(hid.astype(BF16), w2_ref[...], preferred_element_type=F32)

    valid = lax.broadcasted_iota(jnp.int32, (n_cmp, LANES), 0) < n_cmp - 1
    lo = lax.broadcasted_iota(jnp.int32, (n_cmp, LANES), 1) < HEAD_DIM
    kc = mlp(kc_ref, pek_ref, w1k_ref, w2k_ref)
    kc = _norm_rope(kc, gain_ref[...], cos_ref[...], sp_ref[...], sm_ref[...], lo)
    ko_ref[0] = jnp.where(valid, kc, 0.0).astype(BF16)
    vc = mlp(vc_ref, pev_ref, w1v_ref, w2v_ref)
    vo_ref[0] = jnp.where(valid, vc, 0.0).T.astype(BF16)


def _compress(pf, pe_k, pe_v, w1k, w2k, w1v, w2v, gain, cos_c, sp_c, sm_c):
    B, S, _ = pf.shape
    n_cmp = S // NSA_CMP_STRIDE
    cs = _const_spec
    return pl.pallas_call(
        functools.partial(_compress_kernel, n_cmp=n_cmp),
        grid=(B,),
        in_specs=[
            pl.BlockSpec((1, S, LANES), lambda b: (b, 0, 0)),
            pl.BlockSpec((1, S, LANES), lambda b: (b, 0, 1)),
            cs(pe_k.shape), cs(pe_v.shape), cs(w1k.shape), cs(w2k.shape), cs(w1v.shape), cs(w2v.shape),
            cs(gain.shape), cs(cos_c.shape), cs(sp_c.shape), cs(sm_c.shape),
        ],
        out_specs=[pl.BlockSpec((1, n_cmp, LANES), lambda b: (b, 0, 0)),
                   pl.BlockSpec((1, LANES, n_cmp), lambda b: (b, 0, 0))],
        out_shape=[jax.ShapeDtypeStruct((B, n_cmp, LANES), BF16),
                   jax.ShapeDtypeStruct((B, LANES, n_cmp), BF16)],
        compiler_params=_cparams(("parallel",)),
        name="nsa_compress",
    )(pf, pf, pe_k, pe_v, w1k, w2k, w1v, w2v, gain, cos_c, sp_c, sm_c)


def _nsa_kernel(*refs, n_cmp, n_slc, n_t):
    lax.fori_loop(0, n_t, functools.partial(_nsa_tile, refs, n_cmp, n_slc), 0)


def _nsa_tile(refs, n_cmp, n_slc, qi, carry):
    (qt_ref, kc_ref, vct_ref, ks_ref, vst_ref, kw_ref, vwt_ref, g_ref, ovl_ref, o_ref,
     q6_ref, impf_ref, improw_ref, m_ref, acc_ref, ot_ref) = refs
    blk_per_tile = TQ // NSA_SEL_BLOCK
    off = pl.multiple_of(qi * TQ, TQ)
    tpos = qi * TQ + lax.broadcasted_iota(jnp.int32, (1, TQ), 1)
    sig = jax.nn.sigmoid(g_ref[0, pl.ds(off, TQ), :].T)
    kr = lax.broadcasted_iota(jnp.int32, (TQ, TQ), 0)
    ql = lax.broadcasted_iota(jnp.int32, (TQ, TQ), 1)
    causal_bias = jnp.where(kr <= ql, 0.0, NEG)
    nidx = lax.broadcasted_iota(jnp.int32, (n_cmp, TQ), 0)
    cmask = (nidx * NSA_CMP_STRIDE + (NSA_CMP_LEN - 1) <= tpos) & (nidx < n_cmp - 1)
    cmp_bias = jnp.where(cmask, 0.0, NEG)
    jj = lax.broadcasted_iota(jnp.int32, (n_slc, TQ), 0)
    cur = lax.shift_right_arithmetic(tpos, NSA_SEL_BLOCK.bit_length() - 1)
    forced = (jj == 0) | (jj == cur) | (jj == cur - 1)
    gw = NSA_GROUP * TQ

    def half(kk):
        return slice(kk * HEAD_DIM, (kk + 1) * HEAD_DIM)

    def cols(kk):
        return slice(kk * gw, (kk + 1) * gw)

    def tile3(a):
        return jnp.concatenate([a] * NSA_GROUP, axis=1)

    def gate3(kk, branch):
        return jnp.concatenate([sig[3 * (NSA_GROUP * kk + j) + branch:3 * (NSA_GROUP * kk + j) + branch + 1, :]
                                for j in range(NSA_GROUP)], axis=1)

    q6 = jnp.concatenate([_half_rows(qt_ref[0, j, qi], kk) for kk in range(H_NSA_KV) for j in range(NSA_GROUP)],
                         axis=1)
    q6_ref[...] = q6

    s_cmp = _dot(kc_ref[0], q6)
    cmp_bias3 = tile3(cmp_bias)
    for kk in range(H_NSA_KV):
        s = s_cmp[:, cols(kk)] + cmp_bias3
        m = jnp.maximum(jnp.max(s, axis=0, keepdims=True), M_FLOOR)
        p = jnp.exp2(s - m)
        p = p / jnp.maximum(jnp.sum(p, axis=0, keepdims=True), TINY)
        ot_ref[kk] = gate3(kk, 0) * _dot(vct_ref[0, half(kk), :], p.astype(BF16))
        p_sum = (p[:, :TQ] + p[:, TQ:2 * TQ]) + p[:, 2 * TQ:]
        _online_init(m_ref, acc_ref, kk)
        p_hi, p_lo = _split_bf16(p_sum)
        imp = _dot(ovl_ref[...], p_hi) + _dot(ovl_ref[...], p_lo)
        imp = jnp.where(jj > cur, NEG, jnp.where(forced, FORCE, imp))
        impf_ref[kk] = imp
        for j in range(n_slc):
            improw_ref[kk, j] = imp[j:j + 1, :]

    def sel_bias(kk, kt):
        imp = impf_ref[kk]
        parts = []
        for c in range(blk_per_tile):
            j = kt * blk_per_tile + c
            row = improw_ref[kk, j]
            ge = jnp.where(imp >= row, 1.0, 0.0)
            gt = jnp.where(imp > row, 1.0, 0.0)
            cnt = jnp.sum(jnp.where(jj < j, ge, gt), axis=0, keepdims=True)
            parts.append(jnp.broadcast_to(jnp.where(cnt < NSA_N_SEL, 0.0, NEG), (NSA_SEL_BLOCK, TQ)))
        return jnp.concatenate(parts, axis=0)

    def sel_pv(kk, kt):
        return lambda pr: _dot(_with_ones(vst_ref[0, 0, kt, half(kk), :]), pr)

    def sel_scores(kt):
        off = pl.multiple_of(kt * TQ, TQ)
        return _dot(ks_ref[0, pl.ds(off, TQ), :], q6_ref[...])

    def sel_tile(kt, s6):
        for kk in range(H_NSA_KV):
            s = s6[:, cols(kk)] + tile3(sel_bias(kk, kt))
            _online_update(m_ref, acc_ref, kk, s, sel_pv(kk, kt))

    def sel_pair(i, _):
        s_a, s_b = sel_scores(2 * i), sel_scores(2 * i + 1)
        sel_tile(2 * i, s_a)
        sel_tile(2 * i + 1, s_b)
        return 0

    lax.fori_loop(0, qi // 2, sel_pair, 0)

    @pl.when(qi % 2 == 1)
    def _():
        sel_tile(qi - 1, sel_scores(qi - 1))

    kt1 = jnp.maximum(qi - 1, 0)
    kt2 = jnp.maximum(qi - 2, 0)
    ks_d = ks_ref[0, pl.ds(off, TQ), :]
    kw_0 = kw_ref[0, pl.ds(off, TQ), :]
    kw_1 = kw_ref[0, pl.ds(pl.multiple_of(kt1 * TQ, TQ), TQ), :]
    kw_2 = kw_ref[0, pl.ds(pl.multiple_of(kt2 * TQ, TQ), TQ), :]
    win_bias2 = jnp.where(kr > ql + jnp.where(qi >= 2, 0, TQ), 0.0, NEG)
    win_bias1 = jnp.where(qi >= 1, 0.0, NEG)
    q6 = q6_ref[...]
    sd6, w2, w1, w0 = _dot(ks_d, q6), _dot(kw_2, q6), _dot(kw_1, q6), _dot(kw_0, q6)
    causal3 = tile3(causal_bias)
    win_bias2 = tile3(win_bias2)
    for kk in range(H_NSA_KV):
        s = sd6[:, cols(kk)] + tile3(sel_bias(kk, qi) + causal_bias)
        _online_update(m_ref, acc_ref, kk, s, sel_pv(kk, qi))
        o_sel = _normalise(acc_ref[kk])
        s2 = w2[:, cols(kk)] + win_bias2
        s1 = w1[:, cols(kk)] + win_bias1
        s0 = w0[:, cols(kk)] + causal3
        m = jnp.maximum(jnp.maximum(jnp.max(s2, axis=0, keepdims=True), jnp.max(s1, axis=0, keepdims=True)),
                        jnp.max(s0, axis=0, keepdims=True))
        p2, p1, p0 = jnp.exp2(s2 - m), jnp.exp2(s1 - m), jnp.exp2(s0 - m)
        acc = (_dot(_with_ones(vwt_ref[0, 0, kt2, half(kk), :]), p2.astype(BF16))
               + _dot(_with_ones(vwt_ref[0, 0, kt1, half(kk), :]), p1.astype(BF16))
               + _dot(_with_ones(vwt_ref[0, 0, qi, half(kk), :]), p0.astype(BF16)))
        o_win = _normalise(acc)
        ot_ref[kk] = (ot_ref[kk] + gate3(kk, 1) * o_sel) + gate3(kk, 2) * o_win
    for j in range(NSA_GROUP):
        js = slice(j * TQ, (j + 1) * TQ)
        o_ref[0, pl.ds(off, TQ), j * LANES:(j + 1) * LANES] = jnp.concatenate(
            [ot_ref[0, :, js], ot_ref[1, :, js]], axis=0).T.astype(BF16)
    return carry


def _nsa(pbt, pb, pf, kcmp, vcmp_t, ovl_t):
    B, _, n_t, _, _ = pbt.shape
    S = pb.shape[1]
    n_cmp = kcmp.shape[1]
    n_slc = S // NSA_SEL_BLOCK
    gw = NSA_GROUP * TQ
    assert NSA_WINDOW == 2 * TQ and TQ % NSA_SEL_BLOCK == 0 and S % TQ == 0
    return pl.pallas_call(
        functools.partial(_nsa_kernel, n_cmp=n_cmp, n_slc=n_slc, n_t=n_t),
        grid=(B,),
        in_specs=[
            pl.BlockSpec((1, NSA_GROUP, n_t, LANES, TQ), lambda b: (b, PBT_QB // NSA_GROUP, 0, 0, 0)),
            pl.BlockSpec((1, n_cmp, LANES), lambda b: (b, 0, 0)),
            pl.BlockSpec((1, LANES, n_cmp), lambda b: (b, 0, 0)),
            pl.BlockSpec((1, S, LANES), lambda b: (b, 0, PB_KS)),
            pl.BlockSpec((1, 1, n_t, LANES, TQ), lambda b: (b, PBT_VS, 0, 0, 0)),
            pl.BlockSpec((1, S, LANES), lambda b: (b, 0, PB_KW)),
            pl.BlockSpec((1, 1, n_t, LANES, TQ), lambda b: (b, PBT_VW, 0, 0, 0)),
            pl.BlockSpec((1, S, LANES), lambda b: (b, 0, 2)),
            _const_spec(ovl_t.shape),
        ],
        out_specs=pl.BlockSpec((1, S, Q_B), lambda b: (b, 0, 0)),
        out_shape=jax.ShapeDtypeStruct((B, S, Q_B), BF16),
        scratch_shapes=[pltpu.VMEM((LANES, H_NSA * TQ), BF16),
                        pltpu.VMEM((H_NSA_KV, n_slc, TQ), F32), pltpu.VMEM((H_NSA_KV, n_slc, 1, TQ), F32),
                        pltpu.VMEM((H_NSA_KV, 1, gw), F32),
                        pltpu.VMEM((H_NSA_KV, V_ROWS, gw), F32),
                        pltpu.VMEM((H_NSA_KV, HEAD_DIM, gw), F32)],
        compiler_params=_cparams(("parallel",)),
        name="nsa_attn",
    )(pbt, kcmp, vcmp_t, pb, pbt, pb, pbt, pf, ovl_t)


DIL_TILES = 16

def _dilated_kernel(q_ref, k_ref, v_ref, o_ref, qf_ref, kf_ref, vf_ref, os_ref, ls_ref, *, seq):
    blk = BAND_BLOCK
    n_tiles = seq // blk
    lo = lax.broadcasted_iota(jnp.int32, (blk, LANES), 1) < HEAD_DIM
    for g, (window, dil) in enumerate(DIL_CFG):
        max_dist = window // dil
        tiles_per_class = n_tiles // dil
        use_prev = tiles_per_class > 1
        n_keys = 2 * blk if use_prev else blk
        ri = lax.broadcasted_iota(jnp.int32, (2 * blk, n_keys), 0) & (blk - 1)
        ci = lax.broadcasted_iota(jnp.int32, (2 * blk, n_keys), 1)
        dist = ri + (n_keys - blk) - ci
        band = (dist >= 0) & (dist <= max_dist)
        band_bias = jnp.where(band, 0.0, NEG)
        qf_ref[g] = q_ref[0, :, g * LANES:(g + 1) * LANES].astype(F32)
        kf_ref[g] = k_ref[0, :, g * LANES:(g + 1) * LANES].astype(F32)
        vf_ref[g] = v_ref[0, :, g * LANES:(g + 1) * LANES].astype(F32)

        def index(start, dil=dil):
            if dil == 1:
                return pl.ds(pl.multiple_of(start, blk), blk)
            return pl.ds(start, blk, stride=dil)

        def tiles(i, _, g=g, dil=dil, tiles_per_class=tiles_per_class, use_prev=use_prev, band=band,
                  band_bias=band_bias, ci=ci, index=index):
            scored = []
            for u in range(DIL_TILES):
                t = i * DIL_TILES + u
                cls = t // tiles_per_class
                mt = t % tiles_per_class
                start = cls + dil * blk * mt
                q = qf_ref[g, index(start), :]
                q2 = jnp.concatenate([jnp.where(lo, q, 0.0), jnp.where(lo, 0.0, q)], axis=0).astype(BF16)
                if use_prev:
                    prev = jnp.maximum(start - dil * blk, cls)
                    k2 = jnp.concatenate([kf_ref[g, index(prev), :], kf_ref[g, index(start), :]], axis=0)
                    v2 = jnp.concatenate([vf_ref[g, index(prev), :], vf_ref[g, index(start), :]], axis=0)
                    bias = jnp.where(band & (ci >= jnp.where(mt > 0, 0, blk)), 0.0, NEG)
                else:
                    k2, v2, bias = kf_ref[g, index(start), :], vf_ref[g, index(start), :], band_bias
                s = lax.dot_general(q2, k2.astype(BF16), (((1,), (1,)), ((), ())), preferred_element_type=F32)
                v2 = jnp.concatenate([v2.astype(BF16), jnp.ones((v2.shape[0], LANES), BF16)], axis=1)
                scored.append((start, s + bias, v2))
            probs = []
            for start, s, v2 in scored:
                m = jnp.max(s, axis=-1, keepdims=True)
                probs.append((start, m, jnp.exp2(s - m).astype(BF16), v2))
            for start, m, p, v2 in probs:
                o2 = _dot(p, v2)
                den = jnp.maximum(o2[:, LANES:], TINY)
                o = o2[:, :LANES] / den
                lse = m + jnp.log2(den)
                os_ref[g, index(start), :] = jnp.where(lo, o[:blk], o[blk:])
                ls_ref[g, index(start), :] = jnp.where(lo, lse[:blk], lse[blk:])
            return 0

        lax.fori_loop(0, n_tiles // DIL_TILES, tiles, 0)

    rows_per_trip = DIL_TILES * blk

    def combine(t, _):
        idx = pl.ds(pl.multiple_of(t * rows_per_trip, rows_per_trip), rows_per_trip)
        lses = [ls_ref[g, idx, :] for g in range(len(DIL_CFG))]
        mx = jnp.maximum(jnp.maximum(lses[0], lses[1]), lses[2])
        es = [jnp.exp2(l - mx) for l in lses]
        tot = es[0] + es[1] + es[2]
        for g in range(len(DIL_CFG)):
            o_ref[0, idx, g * LANES:(g + 1) * LANES] = (os_ref[g, idx, :] * (es[g] / tot)).astype(BF16)
        return 0

    lax.fori_loop(0, seq // rows_per_trip, combine, 0)


def _dilated(pb):
    B, S, _ = pb.shape
    n_g = len(DIL_CFG)
    return pl.pallas_call(
        functools.partial(_dilated_kernel, seq=S),
        grid=(B,),
        in_specs=[pl.BlockSpec((1, S, QKV_C), lambda b: (b, 0, PB_QC // n_g)),
                  pl.BlockSpec((1, S, QKV_C), lambda b: (b, 0, PB_KC // n_g)),
                  pl.BlockSpec((1, S, QKV_C), lambda b: (b, 0, PB_VC // n_g))],
        out_specs=pl.BlockSpec((1, S, QKV_C), lambda b: (b, 0, 0)),
        out_shape=jax.ShapeDtypeStruct((B, S, QKV_C), BF16),
        scratch_shapes=[pltpu.VMEM((n_g, S, LANES), F32)] * 5,
        compiler_params=_cparams(("parallel",)),
        name="dilated_attn",
    )(pb, pb, pb)


HALO = 16
FF_CHUNK = 256


def _proj_ffn_kernel(x_ref, xh_ref, oa_ref, oah_ref, ob_ref, obh_ref, oc_ref, och_ref, wo_ref, ln_ref,
                     wg_ref, wu_ref, cw_ref, cb_ref, wd_ref, y_ref, o_ref, h_ref, x1_ref, a_ref,
                     *, tiles_per_seq):
    i = pl.program_id(0)

    def norm(v):
        return (v * lax.rsqrt(jnp.mean(v * v, axis=-1, keepdims=True) + EPS)) * ln_ref[...]

    col = 0
    for main, halo in ((oa_ref, oah_ref), (ob_ref, obh_ref), (oc_ref, och_ref)):
        width = main.shape[1]
        o_ref[:HALO, col:col + width] = halo[...]
        o_ref[HALO:, col:col + width] = main[...]
        col += width
    attn = _dot(o_ref[...], wo_ref[...])
    x1 = x_ref[...] + attn[HALO:]
    x1_ref[...] = x1
    first = (i % tiles_per_seq) == 0
    h_ref[:HALO, :] = jnp.where(first, 0.0, norm(xh_ref[...] + attn[:HALO])).astype(BF16)
    h_ref[HALO:, :] = norm(x1).astype(BF16)

    for f0 in range(0, D_FF, FF_CHUNK):
        fs = slice(f0, min(f0 + FF_CHUNK, D_FF))
        g = _dot(h_ref[...], wg_ref[:, fs])
        u = _dot(h_ref[HALO:, :], wu_ref[:, fs])
        gc = (cw_ref[0:1, fs] * pltpu.roll(g, 2, 0)[HALO:] + cw_ref[1:2, fs] * pltpu.roll(g, 1, 0)[HALO:]
              + cw_ref[2:3, fs] * g[HALO:] + cb_ref[:, fs])
        a_ref[:, fs] = (jax.nn.silu(gc) * u).astype(BF16)
    y_ref[...] = x1_ref[...] + _dot(a_ref[...], wd_ref[...])


def _proj_ffn(x2, o_a, o_b, o_c, wo, ln, wg, wu, cw, cb, wd, seq, tm=512):
    M, D = x2.shape
    halo_blocks = tm // HALO
    main = lambda w: pl.BlockSpec((tm, w), lambda i: (i, 0))
    halo = lambda w: pl.BlockSpec((HALO, w), lambda i: (jnp.maximum(i * halo_blocks - 1, 0), 0))
    return pl.pallas_call(
        functools.partial(_proj_ffn_kernel, tiles_per_seq=seq // tm),
        grid=(M // tm,),
        in_specs=[
            main(D), halo(D), main(QKV_A), halo(QKV_A), main(Q_B), halo(Q_B), main(QKV_C), halo(QKV_C),
            _const_spec(wo.shape), _const_spec((1, D)),
            _const_spec(wg.shape), _const_spec(wu.shape), _const_spec(cw.shape), _const_spec(cb.shape),
            _const_spec(wd.shape),
        ],
        out_specs=main(D),
        out_shape=jax.ShapeDtypeStruct((M, D), F32),
        scratch_shapes=[pltpu.VMEM((HALO + tm, D), BF16), pltpu.VMEM((HALO + tm, D), BF16),
                        pltpu.VMEM((tm, D), F32), pltpu.VMEM((tm, D_FF), BF16)],
        compiler_params=_cparams(("parallel",)),
        name="proj_ffn",
    )(x2, x2, o_a, o_a, o_b, o_b, o_c, o_c, wo, ln, wg, wu, cw, cb, wd)


def _rope_tables(pos):
    half = ROPE_DIM // 2
    inv_freq = ROPE_THETA ** (-jnp.arange(half, dtype=F32) * 2.0 / ROPE_DIM)
    ang = pos.astype(F32)[:, None] * inv_freq
    c, s = jnp.cos(ang), jnp.sin(ang)
    n = pos.shape[0]
    rest = HEAD_DIM - ROPE_DIM
    cos_h = jnp.concatenate([c, c, jnp.ones((n, rest), F32)], axis=-1)
    sp_h = jnp.concatenate([jnp.zeros((n, half), F32), s, jnp.zeros((n, rest), F32)], axis=-1)
    sm_h = jnp.concatenate([-s, jnp.zeros((n, half + rest), F32)], axis=-1)
    two = lambda t: jnp.concatenate([t, t], axis=-1)
    return two(cos_h), two(sp_h), two(sm_h), c.T, s.T


def _pair_diag(w):
    z = jnp.zeros_like(w)
    return jnp.concatenate([jnp.concatenate([w, z], axis=-1), jnp.concatenate([z, w], axis=-1)], axis=-2)


def _overlap_t(seq):
    n_cmp = seq // NSA_CMP_STRIDE
    n_slc = seq // NSA_SEL_BLOCK
    starts = np.arange(n_cmp) * NSA_CMP_STRIDE
    j = np.arange(n_slc)
    ovl = (starts[None, :] < (j[:, None] + 1) * NSA_SEL_BLOCK) & (starts[None, :] + NSA_CMP_LEN > j[:, None] * NSA_SEL_BLOCK)
    ovl[:, n_cmp - 1] = False
    return jnp.asarray(ovl, BF16)


def _mixers(x, seq_tabs, cmp_tabs, ovl_t, ln1, w, qn_a, kn_a, qn_b, kn_b, pe_k, pe_v, wk1, wk2, wv1, wv2,
            qn_c, kn_c):
    rep = lambda g, n: jnp.tile(g, n)
    w_t, w_r = w
    gain_t = jnp.concatenate([rep(qn_a * Q_SCALE, H_MOBA), rep(qn_b * Q_SCALE, H_NSA)])
    gain_t = jnp.broadcast_to(gain_t[:, None], (gain_t.shape[0], LANES))
    gain_r = jnp.concatenate([rep(kn_a, H_MOBA), rep(kn_b[1], H_NSA_KV), rep(kn_b[2], H_NSA_KV),
                              rep(qn_c * Q_SCALE, H_DIL), rep(kn_c, H_DIL)])[None, :]
    pbt, pb, pf = _in_proj(x, ln1[None, :], w_t, w_r, gain_t, gain_r, *seq_tabs)

    o_a = _moba(pbt, pb)
    two = lambda t: jnp.concatenate([t, t], axis=-1)
    w1 = lambda t: _pair_diag(t.reshape(NSA_CMP_LEN, HEAD_DIM, NSA_CMP_HIDDEN)).astype(BF16)
    kcmp, vcmp_t = _compress(pf, two(pe_k), two(pe_v), w1(wk1), _pair_diag(wk2).astype(BF16),
                             w1(wv1), _pair_diag(wv2).astype(BF16), two(kn_b[0])[None, :], *cmp_tabs[:3])
    o_b = _nsa(pbt, pb, pf, kcmp, vcmp_t, ovl_t)
    o_c = _dilated(pb)
    return o_a, o_b, o_c


def _layer(x, seq_tabs, cmp_tabs, ovl_t, ln1, w_in, qn_a, kn_a, qn_b, kn_b, pe_k, pe_v, wk1, wk2, wv1, wv2,
           qn_c, kn_c, w_out, ln2, w_gate, w_up, conv_w, conv_b, w_down):
    B, S, D = x.shape
    o_a, o_b, o_c = _mixers(x, seq_tabs, cmp_tabs, ovl_t, ln1, w_in, qn_a, kn_a, qn_b, kn_b, pe_k, pe_v,
                            wk1, wk2, wv1, wv2, qn_c, kn_c)
    nsa_rows = [w_out[QKV_A + h * HEAD_DIM:QKV_A + (h + 1) * HEAD_DIM] for h in NSA_ORDER]
    w_o = jnp.concatenate([w_out[:QKV_A]] + nsa_rows + [w_out[QKV_A + Q_B:]], axis=0).astype(BF16)
    M = B * S
    x2 = _proj_ffn(x.reshape(M, D), o_a.reshape(M, QKV_A), o_b.reshape(M, Q_B), o_c.reshape(M, QKV_C), w_o,
                   ln2[None, :], w_gate.astype(BF16), w_up.astype(BF16), conv_w, conv_b[None, :],
                   w_down.astype(BF16), S)
    return x2.reshape(B, S, D)


def kernel(x, ln1, w_in, qn_a, kn_a, qn_b, kn_b, cmp_pe_k, cmp_pe_v, cmp_k_w1, cmp_k_w2, cmp_v_w1, cmp_v_w2,
           qn_c, kn_c, w_out, ln2, w_gate, w_up, conv_w, conv_b, w_down):
    S = x.shape[1]
    seq_tabs = _rope_tables(jnp.arange(S))
    n_cmp = S // NSA_CMP_STRIDE
    cmp_tabs = _rope_tables(jnp.arange(n_cmp) * NSA_CMP_STRIDE + (NSA_CMP_LEN - 1))
    ovl_t = _overlap_t(S)
    for l in range(ln1.shape[0]):
        x = _layer(x, seq_tabs, cmp_tabs, ovl_t, ln1[l], _prep_w_in(w_in, l), qn_a[l], kn_a[l], qn_b[l], kn_b[l],
                   cmp_pe_k[l], cmp_pe_v[l], cmp_k_w1[l], cmp_k_w2[l], cmp_v_w1[l], cmp_v_w2[l], qn_c[l], kn_c[l],
                   w_out[l], ln2[l], w_gate[l], w_up[l], conv_w[l], conv_b[l], w_down[l])
    return x
```

```python
import functools

import numpy as np
import jax
import jax.numpy as jnp
from jax import lax
from jax.experimental import pallas as pl
from jax.experimental.pallas import tpu as pltpu

F32 = jnp.float32
BF16 = jnp.bfloat16

D_MODEL = 1024
HEAD_DIM = 64
N_HEADS = D_MODEL // HEAD_DIM
H_MOBA = N_HEADS // 4
H_NSA = (N_HEADS - H_MOBA) // 2
H_DIL = N_HEADS - H_MOBA - H_NSA
H_NSA_KV = 2
NSA_GROUP = H_NSA // H_NSA_KV
ROPE_DIM = HEAD_DIM // 4
ROPE_THETA = 500000.0
MOBA_BLOCK = 256
MOBA_TOPK = 3
NSA_CMP_LEN = 32
NSA_CMP_STRIDE = 16
NSA_CMP_HIDDEN = 128
NSA_SEL_BLOCK = 64
NSA_N_SEL = 6
NSA_WINDOW = 512
DIL_CFG = ((128, 1), (512, 4), (2048, 16))
BAND_BLOCK = 128
D_FF = 2816
EPS = 1e-6
NEG = -1e30
TINY = 1e-30
FORCE = 1e9
SCALE = HEAD_DIM ** -0.5
Q_SCALE = SCALE * float(np.log2(np.e))
QKV_A = H_MOBA * HEAD_DIM
Q_B = H_NSA * HEAD_DIM
KV_B = H_NSA_KV * HEAD_DIM
G_B = H_NSA * 3
QKV_C = H_DIL * HEAD_DIM
IN_SPLITS = (QKV_A, QKV_A, QKV_A, Q_B, KV_B, KV_B, KV_B, KV_B, KV_B, KV_B, G_B, QKV_C, QKV_C, QKV_C)

LANES = 128
TQ = 256
NSA_ORDER = (0, 3, 1, 4, 2, 5)
VMEM_LIMIT = 56 * 1024 * 1024

M_FLOOR = -1e29
V_ROWS = HEAD_DIM + 16

T_GROUP = ("qa", "qb", "va", "vsb", "vwb")
N_T_NORM, N_T = 5, 9
R_GROUP = ("ka", "ksb", "kwb", "qc", "kc", "vc", "kcb", "vcb", "gb")
N_R_NORM, N_R_BF16, N_R = 10, 13, 16
N_PBT, N_PB, N_F32 = N_T, N_R_BF16, N_R - N_R_BF16
PBT_QA, PBT_VA, PBT_VS, PBT_VW, PBT_QB = 0, 2, 4, 5, 6
PB_QC, PB_KC, PB_KA, PB_KS, PB_VC, PB_KW = 0, 3, 6, 8, 9, 12
_PBT_DST = (0, 1, 6, 7, 8, 2, 3, 4, 5)
_PB_DST = (6, 7, 8, 12, 0, 1, 2, 3, 4, 5, 9, 10, 11)
CHUNK = 4


def _cparams(sem):
    return pltpu.CompilerParams(dimension_semantics=sem, vmem_limit_bytes=VMEM_LIMIT)


def _const_spec(shape):
    return pl.BlockSpec(shape, lambda *_: (0,) * len(shape), pipeline_mode=pl.Buffered(1))


def _norm_rope(y, gain, cos_t, sin_p, sin_m, lo):
    ss = y * y
    s_lo = jnp.sum(jnp.where(lo, ss, 0.0), axis=-1, keepdims=True)
    s_hi = jnp.sum(jnp.where(lo, 0.0, ss), axis=-1, keepdims=True)
    ms = jnp.where(lo, s_lo, s_hi) * (1.0 / HEAD_DIM)
    z = (y * lax.rsqrt(ms + EPS)) * gain
    half = ROPE_DIM // 2
    return z * cos_t + pltpu.roll(z, half, 1) * sin_p + pltpu.roll(z, LANES - half, 1) * sin_m


def _online_update(m_ref, acc_ref, g, s, pv):
    m_old = m_ref[g]
    m_new = jnp.maximum(m_old, jnp.max(s, axis=0, keepdims=True))
    p = jnp.exp2(s - m_new)
    alpha = jnp.exp2(m_old - m_new)
    acc_ref[g] = alpha * acc_ref[g] + pv(p.astype(BF16))
    m_ref[g] = m_new


def _online_init(m_ref, acc_ref, g):
    m_ref[g] = jnp.full(m_ref.shape[1:], M_FLOOR, F32)
    acc_ref[g] = jnp.zeros(acc_ref.shape[1:], F32)


def _with_ones(v_t):
    row = lax.broadcasted_iota(jnp.int32, (V_ROWS - HEAD_DIM, v_t.shape[1]), 0)
    return jnp.concatenate([v_t, jnp.where(row == 0, 1.0, 0.0).astype(v_t.dtype)], axis=0)


def _normalise(acc):
    return acc[:HEAD_DIM] / jnp.maximum(acc[HEAD_DIM:HEAD_DIM + 1], TINY)


def _dot(a, b):
    return jnp.dot(a, b, preferred_element_type=F32)


def _half_rows(q_t, hh):
    z = jnp.zeros((HEAD_DIM, q_t.shape[1]), q_t.dtype)
    if hh == 0:
        return jnp.concatenate([q_t[:HEAD_DIM], z], axis=0)
    return jnp.concatenate([z, q_t[HEAD_DIM:]], axis=0)


def _split_bf16(a):
    hi = a.astype(BF16)
    return hi, (a - hi.astype(F32)).astype(BF16)


def _in_proj_plan(group):
    names = ("qa", "ka", "va", "qb", "kcb", "vcb", "ksb", "vsb", "kwb", "vwb", "gb", "qc", "kc", "vc")
    start = dict(zip(names, np.cumsum((0,) + IN_SPLITS[:-1]).tolist()))
    width = dict(zip(names, IN_SPLITS))
    plan = []
    for name in group:
        if name == "qb":
            heads = [start[name] + h * HEAD_DIM for h in NSA_ORDER]
            plan += [[(heads[2 * j], HEAD_DIM), (heads[2 * j + 1], HEAD_DIM)] for j in range(H_NSA // 2)]
        else:
            plan += [[(start[name] + c, min(LANES, width[name] - c))] for c in range(0, width[name], LANES)]
    return plan


def _wprep_kernel(w_ref, wt_ref, wr_ref, *, plan_t, plan_r, n_in):
    def take(a, width):
        b0 = a // LANES * LANES
        v0 = w_ref[0, :, b0:min(b0 + LANES, n_in)]
        if a + width <= b0 + LANES:
            return v0[:, a - b0:a - b0 + width]
        v1 = w_ref[0, :, b0 + LANES:min(b0 + 2 * LANES, n_in)]
        return jnp.concatenate([v0[:, a - b0:], v1[:, :a + width - b0 - LANES]], axis=1)

    def block(pieces):
        cols = [take(a, width) for a, width in pieces]
        filled = sum(width for _, width in pieces)
        if filled < LANES:
            cols.append(jnp.zeros((w_ref.shape[1], LANES - filled), F32))
        return cols[0] if len(cols) == 1 else jnp.concatenate(cols, axis=1)

    for d, pieces in enumerate(plan_t):
        wt_ref[d * LANES:(d + 1) * LANES, :] = block(pieces).T.astype(BF16)
    for d, pieces in enumerate(plan_r):
        wr_ref[:, d * LANES:(d + 1) * LANES] = block(pieces).astype(BF16)


def _prep_w_in(w_in, layer, tr=256):
    _, D, n_in = w_in.shape
    plan_t, plan_r = _in_proj_plan(T_GROUP), _in_proj_plan(R_GROUP)
    return pl.pallas_call(
        functools.partial(_wprep_kernel, plan_t=plan_t, plan_r=plan_r, n_in=n_in),
        grid=(D // tr,),
        in_specs=[pl.BlockSpec((1, tr, n_in), lambda i: (layer, i, 0))],
        out_specs=[pl.BlockSpec((N_T * LANES, tr), lambda i: (0, i)),
                   pl.BlockSpec((tr, N_R * LANES), lambda i: (i, 0))],
        out_shape=[jax.ShapeDtypeStruct((N_T * LANES, D), BF16), jax.ShapeDtypeStruct((D, N_R * LANES), BF16)],
        compiler_params=_cparams(("parallel",)),
        name="w_in_layout",
    )(w_in)


def _norm_rope_t(y, gain, cos8, sin8):
    half = ROPE_DIM // 2
    parts = []
    for hh in range(2):
        yh = y[hh * HEAD_DIM:(hh + 1) * HEAD_DIM]
        ms = jnp.sum(yh * yh, axis=0, keepdims=True) * (1.0 / HEAD_DIM)
        z = (yh * lax.rsqrt(ms + EPS)) * gain[hh * HEAD_DIM:(hh + 1) * HEAD_DIM]
        x1, x2 = z[:half], z[half:ROPE_DIM]
        parts += [x1 * cos8 - x2 * sin8, x2 * cos8 + x1 * sin8, z[ROPE_DIM:]]
    return jnp.concatenate(parts, axis=0)


def _inproj_kernel(x_ref, ln_ref, wt_ref, wr_ref, gt_ref, gr_ref, cos_ref, sp_ref, sm_ref, ct_ref, st_ref,
                   pbt_ref, pb_ref, pf_ref, *, tm):
    x = x_ref[0]
    h = x * lax.rsqrt(jnp.mean(x * x, axis=-1, keepdims=True) + EPS)
    h = (h * ln_ref[...]).astype(BF16)
    lo = lax.broadcasted_iota(jnp.int32, (tm, LANES), 1) < HEAD_DIM
    cos_t, sin_p, sin_m = cos_ref[...], sp_ref[...], sm_ref[...]
    for c0 in range(0, N_R, CHUNK):
        y4 = _dot(h, wr_ref[:, c0 * LANES:(c0 + CHUNK) * LANES])
        for c in range(c0, c0 + CHUNK):
            y = y4[:, (c - c0) * LANES:(c - c0 + 1) * LANES]
            if c < N_R_NORM:
                y = _norm_rope(y, gr_ref[:, c * LANES:(c + 1) * LANES], cos_t, sin_p, sin_m, lo)
            if c < N_R_BF16:
                d = _PB_DST[c]
                pb_ref[0, :, d * LANES:(d + 1) * LANES] = y.astype(BF16)
            else:
                d = c - N_R_BF16
                pf_ref[0, :, d * LANES:(d + 1) * LANES] = y
    cos8, sin8 = ct_ref[...], st_ref[...]
    for t0, t1 in ((0, N_T_NORM), (N_T_NORM, N_T)):
        y_t = lax.dot_general(wt_ref[t0 * LANES:t1 * LANES, :], h, (((1,), (1,)), ((), ())),
                              preferred_element_type=F32)
        for t in range(t0, t1):
            y = y_t[(t - t0) * LANES:(t - t0 + 1) * LANES]
            if t < N_T_NORM:
                gain = jnp.concatenate([gt_ref[t * LANES:(t + 1) * LANES, :]] * (tm // LANES), axis=1)
                y = _norm_rope_t(y, gain, cos8, sin8)
            y = y.astype(BF16)
            for u in range(tm // TQ):
                pbt_ref[0, _PBT_DST[t], u] = y[:, u * TQ:(u + 1) * TQ]


def _in_proj(x, ln, w_t, w_r, gain_t, gain_r, cos_t, sin_p, sin_m, cos8, sin8, tm=1024):
    B, S, D = x.shape
    nt = S // tm
    half = ROPE_DIM // 2
    return pl.pallas_call(
        functools.partial(_inproj_kernel, tm=tm),
        grid=(B * nt,),
        in_specs=[
            pl.BlockSpec((1, tm, D), lambda i: (i // nt, i % nt, 0)),
            _const_spec((1, D)),
            _const_spec(w_t.shape), _const_spec(w_r.shape), _const_spec(gain_t.shape), _const_spec(gain_r.shape),
            pl.BlockSpec((tm, LANES), lambda i: (i % nt, 0)),
            pl.BlockSpec((tm, LANES), lambda i: (i % nt, 0)),
            pl.BlockSpec((tm, LANES), lambda i: (i % nt, 0)),
            pl.BlockSpec((half, tm), lambda i: (0, i % nt)),
            pl.BlockSpec((half, tm), lambda i: (0, i % nt)),
        ],
        out_specs=[
            pl.BlockSpec((1, N_PBT, tm // TQ, LANES, TQ), lambda i: (i // nt, 0, i % nt, 0, 0)),
            pl.BlockSpec((1, tm, N_PB * LANES), lambda i: (i // nt, i % nt, 0)),
            pl.BlockSpec((1, tm, N_F32 * LANES), lambda i: (i // nt, i % nt, 0)),
        ],
        out_shape=[
            jax.ShapeDtypeStruct((B, N_PBT, S // TQ, LANES, TQ), BF16),
            jax.ShapeDtypeStruct((B, S, N_PB * LANES), BF16),
            jax.ShapeDtypeStruct((B, S, N_F32 * LANES), F32),
        ],
        compiler_params=_cparams(("parallel",)),
        name="in_proj",
    )(x, ln, w_t, w_r, gain_t, gain_r, cos_t, sin_p, sin_m, cos8, sin8)


def _moba_kernel(qt_ref, k_ref, vt_ref, o_ref, q2_ref, bias_ref, m_ref, acc_ref, kmean_ref, *, n_blk):
    n_pairs = H_MOBA // 2
    nidx = lax.broadcasted_iota(jnp.int32, (16, 2 * TQ), 0)
    kr = lax.broadcasted_iota(jnp.int32, (TQ, TQ), 0)
    ql = lax.broadcasted_iota(jnp.int32, (TQ, TQ), 1)
    causal_bias = jnp.where(kr <= ql, 0.0, NEG)
    for p in range(n_pairs):
        ps = slice(p * LANES, (p + 1) * LANES)
        rows = [jnp.sum(k_ref[0, n * TQ:(n + 1) * TQ, ps].astype(F32), axis=0, keepdims=True)
                for n in range(n_blk)]
        kmean_ref[p] = (jnp.concatenate(rows + [jnp.zeros((16 - n_blk, LANES), F32)], axis=0)
                        * (1.0 / MOBA_BLOCK))
    lax.fori_loop(0, n_blk, functools.partial(_moba_tile, qt_ref, k_ref, vt_ref, o_ref, q2_ref, bias_ref, m_ref,
                                              acc_ref, kmean_ref, nidx, causal_bias, n_blk), 0)


def _moba_tile(qt_ref, k_ref, vt_ref, o_ref, q2_ref, bias_ref, m_ref, acc_ref, kmean_ref, nidx,
               causal_bias, n_blk, qi, carry):
    n_pairs = H_MOBA // 2
    past = nidx < qi
    for p in range(n_pairs):
        km_hi, km_lo = _split_bf16(kmean_ref[p])
        q2 = jnp.concatenate([_half_rows(qt_ref[0, p, qi], hh) for hh in range(2)], axis=1)
        q2_ref[p] = q2
        gate = _dot(km_hi, q2) + _dot(km_lo, q2)
        for n in range(n_blk):
            row = gate[n:n + 1, :]
            ge = jnp.where(gate >= row, 1.0, 0.0)
            gt = jnp.where(gate > row, 1.0, 0.0)
            beats = jnp.where(past, jnp.where(nidx < n, ge, gt), 0.0)
            cnt = jnp.sum(beats, axis=0, keepdims=True)
            bias_ref[p, n] = jnp.where(cnt < MOBA_TOPK, 0.0, NEG)
        _online_init(m_ref, acc_ref, p)

    def scores(off, p):
        return _dot(k_ref[0, pl.ds(off, TQ), p * LANES:(p + 1) * LANES], q2_ref[p])

    def pv(p, kt):
        def fn(pr):
            return jnp.concatenate([_dot(_with_ones(vt_ref[0, p, kt, :HEAD_DIM, :]), pr[:, :TQ]),
                                    _dot(_with_ones(vt_ref[0, p, kt, HEAD_DIM:, :]), pr[:, TQ:])], axis=1)
        return fn

    def past_scores(kt):
        off = pl.multiple_of(kt * TQ, TQ)
        return [scores(off, p) + bias_ref[p, kt] for p in range(n_pairs)]

    def past_tile(kt, s):
        for p in range(n_pairs):
            _online_update(m_ref, acc_ref, p, s[p], pv(p, kt))

    def past_pair(i, _):
        s_a, s_b = past_scores(2 * i), past_scores(2 * i + 1)
        past_tile(2 * i, s_a)
        past_tile(2 * i + 1, s_b)
        return 0

    lax.fori_loop(0, qi // 2, past_pair, 0)

    @pl.when(qi % 2 == 1)
    def _():
        past_tile(qi - 1, past_scores(qi - 1))
    off = pl.multiple_of(qi * TQ, TQ)
    causal2 = jnp.concatenate([causal_bias, causal_bias], axis=1)
    s = [scores(off, p) + causal2 for p in range(n_pairs)]
    for p in range(n_pairs):
        _online_update(m_ref, acc_ref, p, s[p], pv(p, qi))
        o2 = _normalise(acc_ref[p])
        o_ref[0, pl.ds(off, TQ), p * LANES:(p + 1) * LANES] = jnp.concatenate(
            [o2[:, :TQ], o2[:, TQ:]], axis=0).T.astype(BF16)
    return carry


def _moba(pbt, pb):
    B, _, n_t, _, _ = pbt.shape
    S = pb.shape[1]
    n_pairs = H_MOBA // 2
    assert TQ == MOBA_BLOCK and n_t == S // MOBA_BLOCK
    return pl.pallas_call(
        functools.partial(_moba_kernel, n_blk=S // MOBA_BLOCK),
        grid=(B,),
        in_specs=[
            pl.BlockSpec((1, n_pairs, n_t, LANES, TQ), lambda b: (b, PBT_QA // n_pairs, 0, 0, 0)),
            pl.BlockSpec((1, S, QKV_A), lambda b: (b, 0, PB_KA // n_pairs)),
            pl.BlockSpec((1, n_pairs, n_t, LANES, TQ), lambda b: (b, PBT_VA // n_pairs, 0, 0, 0)),
        ],
        out_specs=pl.BlockSpec((1, S, QKV_A), lambda b: (b, 0, 0)),
        out_shape=jax.ShapeDtypeStruct((B, S, QKV_A), BF16),
        scratch_shapes=[pltpu.VMEM((n_pairs, LANES, 2 * TQ), BF16),
                        pltpu.VMEM((n_pairs, S // MOBA_BLOCK, 1, 2 * TQ), F32),
                        pltpu.VMEM((n_pairs, 1, 2 * TQ), F32),
                        pltpu.VMEM((n_pairs, V_ROWS, 2 * TQ), F32),
                        pltpu.VMEM((n_pairs, 16, LANES), F32)],
        compiler_params=_cparams(("parallel",)),
        name="moba",
    )(pbt, pb, pbt)


def _compress_kernel(kc_ref, vc_ref, pek_ref, pev_ref, w1k_ref, w2k_ref, w1v_ref, w2v_ref, gain_ref,
                     cos_ref, sp_ref, sm_ref, ko_ref, vo_ref, *, n_cmp):
    half_len = NSA_CMP_LEN // 2

    def mlp(t_ref, pe_ref, w1_ref, w2_ref):
        toks = [t_ref[0, pl.ds(l, n_cmp, stride=NSA_CMP_STRIDE), :] for l in range(half_len)]
        first = _dot(jnp.concatenate([(toks[l] + pe_ref[l:l + 1, :]).astype(BF16)
                                      for l in range(half_len)], axis=1), w1_ref[0])
        second = _dot(jnp.concatenate([(toks[l] + pe_ref[half_len + l:half_len + l + 1, :]).astype(BF16)
                                       for l in range(half_len)], axis=1), w1_ref[1])
        hid = jax.nn.gelu(first + pltpu.roll(second, n_cmp - 1, 0))
        return _dot(hid.astype(BF16), w2_ref[...])

    valid = lax.broadcasted_iota(jnp.int32, (n_cmp, LANES), 0) < n_cmp - 1
    lo = lax.broadcasted_iota(jnp.int32, (n_cmp, LANES), 1) < HEAD_DIM
    kc = mlp(kc_ref, pek_ref, w1k_ref, w2k_ref)
    kc = _norm_rope(kc, gain_ref[...], cos_ref[...], sp_ref[...], sm_ref[...], lo)
    ko_ref[0] = jnp.where(valid, kc, 0.0).astype(BF16)
    vc = mlp(vc_ref, pev_ref, w1v_ref, w2v_ref)
    vo_ref[0] = jnp.where(valid, vc, 0.0).T.astype(BF16)


def _compress(pf, pe_k, pe_v, w1k, w2k, w1v, w2v, gain, cos_c, sp_c, sm_c):
    B, S, _ = pf.shape
    n_cmp = S // NSA_CMP_STRIDE
    cs = _const_spec
    return pl.pallas_call(
        functools.partial(_compress_kernel, n_cmp=n_cmp),
        grid=(B,),
        in_specs=[
            pl.BlockSpec((1, S, LANES), lambda b: (b, 0, 0)),
            pl.BlockSpec((1, S, LANES), lambda b: (b, 0, 1)),
            cs(pe_k.shape), cs(pe_v.shape), cs(w1k.shape), cs(w2k.shape), cs(w1v.shape), cs(w2v.shape),
            cs(gain.shape), cs(cos_c.shape), cs(sp_c.shape), cs(sm_c.shape),
        ],
        out_specs=[pl.BlockSpec((1, n_cmp, LANES), lambda b: (b, 0, 0)),
                   pl.BlockSpec((1, LANES, n_cmp), lambda b: (b, 0, 0))],
        out_shape=[jax.ShapeDtypeStruct((B, n_cmp, LANES), BF16),
                   jax.ShapeDtypeStruct((B, LANES, n_cmp), BF16)],
        compiler_params=_cparams(("parallel",)),
        name="nsa_compress",
    )(pf, pf, pe_k, pe_v, w1k, w2k, w1v, w2v, gain, cos_c, sp_c, sm_c)


def _nsa_kernel(*refs, n_cmp, n_slc, n_t):
    lax.fori_loop(0, n_t, functools.partial(_nsa_tile, refs, n_cmp, n_slc), 0)


def _nsa_tile(refs, n_cmp, n_slc, qi, carry):
    (qt_ref, kc_ref, vct_ref, ks_ref, vst_ref, kw_ref, vwt_ref, g_ref, ovl_ref, o_ref,
     q6_ref, impf_ref, improw_ref, m_ref, acc_ref, ot_ref) = refs
    blk_per_tile = TQ // NSA_SEL_BLOCK
    off = pl.multiple_of(qi * TQ, TQ)
    tpos = qi * TQ + lax.broadcasted_iota(jnp.int32, (1, TQ), 1)
    sig = jax.nn.sigmoid(g_ref[0, pl.ds(off, TQ), :].T)
    kr = lax.broadcasted_iota(jnp.int32, (TQ, TQ), 0)
    ql = lax.broadcasted_iota(jnp.int32, (TQ, TQ), 1)
    causal_bias = jnp.where(kr <= ql, 0.0, NEG)
    nidx = lax.broadcasted_iota(jnp.int32, (n_cmp, TQ), 0)
    cmask = (nidx * NSA_CMP_STRIDE + (NSA_CMP_LEN - 1) <= tpos) & (nidx < n_cmp - 1)
    cmp_bias = jnp.where(cmask, 0.0, NEG)
    jj = lax.broadcasted_iota(jnp.int32, (n_slc, TQ), 0)
    cur = lax.shift_right_arithmetic(tpos, NSA_SEL_BLOCK.bit_length() - 1)
    forced = (jj == 0) | (jj == cur) | (jj == cur - 1)
    gw = NSA_GROUP * TQ

    def half(kk):
        return slice(kk * HEAD_DIM, (kk + 1) * HEAD_DIM)

    def cols(kk):
        return slice(kk * gw, (kk + 1) * gw)

    def tile3(a):
        return jnp.concatenate([a] * NSA_GROUP, axis=1)

    def gate3(kk, branch):
        return jnp.concatenate([sig[3 * (NSA_GROUP * kk + j) + branch:3 * (NSA_GROUP * kk + j) + branch + 1, :]
                                for j in range(NSA_GROUP)], axis=1)

    q6 = jnp.concatenate([_half_rows(qt_ref[0, j, qi], kk) for kk in range(H_NSA_KV) for j in range(NSA_GROUP)],
                         axis=1)
    q6_ref[...] = q6

    s_cmp = _dot(kc_ref[0], q6)
    cmp_bias3 = tile3(cmp_bias)
    for kk in range(H_NSA_KV):
        s = s_cmp[:, cols(kk)] + cmp_bias3
        m = jnp.maximum(jnp.max(s, axis=0, keepdims=True), M_FLOOR)
        p = jnp.exp2(s - m)
        p = p / jnp.maximum(jnp.sum(p, axis=0, keepdims=True), TINY)
        ot_ref[kk] = gate3(kk, 0) * _dot(vct_ref[0, half(kk), :], p.astype(BF16))
        p_sum = (p[:, :TQ] + p[:, TQ:2 * TQ]) + p[:, 2 * TQ:]
        _online_init(m_ref, acc_ref, kk)
        p_hi, p_lo = _split_bf16(p_sum)
        imp = _dot(ovl_ref[...], p_hi) + _dot(ovl_ref[...], p_lo)
        imp = jnp.where(jj > cur, NEG, jnp.where(forced, FORCE, imp))
        impf_ref[kk] = imp
        for j in range(n_slc):
            improw_ref[kk, j] = imp[j:j + 1, :]

    def sel_bias(kk, kt):
        imp = impf_ref[kk]
        parts = []
        for c in range(blk_per_tile):
            j = kt * blk_per_tile + c
            row = improw_ref[kk, j]
            ge = jnp.where(imp >= row, 1.0, 0.0)
            gt = jnp.where(imp > row, 1.0, 0.0)
            cnt = jnp.sum(jnp.where(jj < j, ge, gt), axis=0, keepdims=True)
            parts.append(jnp.broadcast_to(jnp.where(cnt < NSA_N_SEL, 0.0, NEG), (NSA_SEL_BLOCK, TQ)))
        return jnp.concatenate(parts, axis=0)

    def sel_pv(kk, kt):
        return lambda pr: _dot(_with_ones(vst_ref[0, 0, kt, half(kk), :]), pr)

    def sel_scores(kt):
        off = pl.multiple_of(kt * TQ, TQ)
        return _dot(ks_ref[0, pl.ds(off, TQ), :], q6_ref[...])

    def sel_tile(kt, s6):
        for kk in range(H_NSA_KV):
            s = s6[:, cols(kk)] + tile3(sel_bias(kk, kt))
            _online_update(m_ref, acc_ref, kk, s, sel_pv(kk, kt))

    def sel_pair(i, _):
        s_a, s_b = sel_scores(2 * i), sel_scores(2 * i + 1)
        sel_tile(2 * i, s_a)
        sel_tile(2 * i + 1, s_b)
        return 0

    lax.fori_loop(0, qi // 2, sel_pair, 0)

    @pl.when(qi % 2 == 1)
    def _():
        sel_tile(qi - 1, sel_scores(qi - 1))

    kt1 = jnp.maximum(qi - 1, 0)
    kt2 = jnp.maximum(qi - 2, 0)
    ks_d = ks_ref[0, pl.ds(off, TQ), :]
    kw_0 = kw_ref[0, pl.ds(off, TQ), :]
    kw_1 = kw_ref[0, pl.ds(pl.multiple_of(kt1 * TQ, TQ), TQ), :]
    kw_2 = kw_ref[0, pl.ds(pl.multiple_of(kt2 * TQ, TQ), TQ), :]
    win_bias2 = jnp.where(kr > ql + jnp.where(qi >= 2, 0, TQ), 0.0, NEG)
    win_bias1 = jnp.where(qi >= 1, 0.0, NEG)
    q6 = q6_ref[...]
    sd6, w2, w1, w0 = _dot(ks_d, q6), _dot(kw_2, q6), _dot(kw_1, q6), _dot(kw_0, q6)
    causal3 = tile3(causal_bias)
    win_bias2 = tile3(win_bias2)
    for kk in range(H_NSA_KV):
        s = sd6[:, cols(kk)] + tile3(sel_bias(kk, qi) + causal_bias)
        _online_update(m_ref, acc_ref, kk, s, sel_pv(kk, qi))
        o_sel = _normalise(acc_ref[kk])
        s2 = w2[:, cols(kk)] + win_bias2
        s1 = w1[:, cols(kk)] + win_bias1
        s0 = w0[:, cols(kk)] + causal3
        m = jnp.maximum(jnp.maximum(jnp.max(s2, axis=0, keepdims=True), jnp.max(s1, axis=0, keepdims=True)),
                        jnp.max(s0, axis=0, keepdims=True))
        p2, p1, p0 = jnp.exp2(s2 - m), jnp.exp2(s1 - m), jnp.exp2(s0 - m)
        acc = (_dot(_with_ones(vwt_ref[0, 0, kt2, half(kk), :]), p2.astype(BF16))
               + _dot(_with_ones(vwt_ref[0, 0, kt1, half(kk), :]), p1.astype(BF16))
               + _dot(_with_ones(vwt_ref[0, 0, qi, half(kk), :]), p0.astype(BF16)))
        o_win = _normalise(acc)
        ot_ref[kk] = (ot_ref[kk] + gate3(kk, 1) * o_sel) + gate3(kk, 2) * o_win
    for j in range(NSA_GROUP):
        js = slice(j * TQ, (j + 1) * TQ)
        o_ref[0, pl.ds(off, TQ), j * LANES:(j + 1) * LANES] = jnp.concatenate(
            [ot_ref[0, :, js], ot_ref[1, :, js]], axis=0).T.astype(BF16)
    return carry


def _nsa(pbt, pb, pf, kcmp, vcmp_t, ovl_t):
    B, _, n_t, _, _ = pbt.shape
    S = pb.shape[1]
    n_cmp = kcmp.shape[1]
    n_slc = S // NSA_SEL_BLOCK
    gw = NSA_GROUP * TQ
    assert NSA_WINDOW == 2 * TQ and TQ % NSA_SEL_BLOCK == 0 and S % TQ == 0
    return pl.pallas_call(
        functools.partial(_nsa_kernel, n_cmp=n_cmp, n_slc=n_slc, n_t=n_t),
        grid=(B,),
        in_specs=[
            pl.BlockSpec((1, NSA_GROUP, n_t, LANES, TQ), lambda b: (b, PBT_QB // NSA_GROUP, 0, 0, 0)),
            pl.BlockSpec((1, n_cmp, LANES), lambda b: (b, 0, 0)),
            pl.BlockSpec((1, LANES, n_cmp), lambda b: (b, 0, 0)),
            pl.BlockSpec((1, S, LANES), lambda b: (b, 0, PB_KS)),
            pl.BlockSpec((1, 1, n_t, LANES, TQ), lambda b: (b, PBT_VS, 0, 0, 0)),
            pl.BlockSpec((1, S, LANES), lambda b: (b, 0, PB_KW)),
            pl.BlockSpec((1, 1, n_t, LANES, TQ), lambda b: (b, PBT_VW, 0, 0, 0)),
            pl.BlockSpec((1, S, LANES), lambda b: (b, 0, 2)),
            _const_spec(ovl_t.shape),
        ],
        out_specs=pl.BlockSpec((1, S, Q_B), lambda b: (b, 0, 0)),
        out_shape=jax.ShapeDtypeStruct((B, S, Q_B), BF16),
        scratch_shapes=[pltpu.VMEM((LANES, H_NSA * TQ), BF16),
                        pltpu.VMEM((H_NSA_KV, n_slc, TQ), F32), pltpu.VMEM((H_NSA_KV, n_slc, 1, TQ), F32),
                        pltpu.VMEM((H_NSA_KV, 1, gw), F32),
                        pltpu.VMEM((H_NSA_KV, V_ROWS, gw), F32),
                        pltpu.VMEM((H_NSA_KV, HEAD_DIM, gw), F32)],
        compiler_params=_cparams(("parallel",)),
        name="nsa_attn",
    )(pbt, kcmp, vcmp_t, pb, pbt, pb, pbt, pf, ovl_t)


DIL_TILES = 16

def _dilated_kernel(q_ref, k_ref, v_ref, o_ref, qf_ref, kf_ref, vf_ref, os_ref, ls_ref, *, seq):
    blk = BAND_BLOCK
    n_tiles = seq // blk
    lo = lax.broadcasted_iota(jnp.int32, (blk, LANES), 1) < HEAD_DIM
    for g, (window, dil) in enumerate(DIL_CFG):
        max_dist = window // dil
        tiles_per_class = n_tiles // dil
        use_prev = tiles_per_class > 1
        n_keys = 2 * blk if use_prev else blk
        ri = lax.broadcasted_iota(jnp.int32, (2 * blk, n_keys), 0) & (blk - 1)
        ci = lax.broadcasted_iota(jnp.int32, (2 * blk, n_keys), 1)
        dist = ri + (n_keys - blk) - ci
        band = (dist >= 0) & (dist <= max_dist)
        band_bias = jnp.where(band, 0.0, NEG)
        qf_ref[g] = q_ref[0, :, g * LANES:(g + 1) * LANES].astype(F32)
        kf_ref[g] = k_ref[0, :, g * LANES:(g + 1) * LANES].astype(F32)
        vf_ref[g] = v_ref[0, :, g * LANES:(g + 1) * LANES].astype(F32)

        def index(start, dil=dil):
            if dil == 1:
                return pl.ds(pl.multiple_of(start, blk), blk)
            return pl.ds(start, blk, stride=dil)

        def tiles(i, _, g=g, dil=dil, tiles_per_class=tiles_per_class, use_prev=use_prev, band=band,
                  band_bias=band_bias, ci=ci, index=index):
            scored = []
            for u in range(DIL_TILES):
                t = i * DIL_TILES + u
                cls = t // tiles_per_class
                mt = t % tiles_per_class
                start = cls + dil * blk * mt
                q = qf_ref[g, index(start), :]
                q2 = jnp.concatenate([jnp.where(lo, q, 0.0), jnp.where(lo, 0.0, q)], axis=0).astype(BF16)
                if use_prev:
                    prev = jnp.maximum(start - dil * blk, cls)
                    k2 = jnp.concatenate([kf_ref[g, index(prev), :], kf_ref[g, index(start), :]], axis=0)
                    v2 = jnp.concatenate([vf_ref[g, index(prev), :], vf_ref[g, index(start), :]], axis=0)
                    bias = jnp.where(band & (ci >= jnp.where(mt > 0, 0, blk)), 0.0, NEG)
                else:
                    k2, v2, bias = kf_ref[g, index(start), :], vf_ref[g, index(start), :], band_bias
                s = lax.dot_general(q2, k2.astype(BF16), (((1,), (1,)), ((), ())), preferred_element_type=F32)
                v2 = jnp.concatenate([v2.astype(BF16), jnp.ones((v2.shape[0], LANES), BF16)], axis=1)
                scored.append((start, s + bias, v2))
            probs = []
            for start, s, v2 in scored:
                m = jnp.max(s, axis=-1, keepdims=True)
                probs.append((start, m, jnp.exp2(s - m).astype(BF16), v2))
            for start, m, p, v2 in probs:
                o2 = _dot(p, v2)
                den = jnp.maximum(o2[:, LANES:], TINY)
                o = o2[:, :LANES] / den
                lse = m + jnp.log2(den)
                os_ref[g, index(start), :] = jnp.where(lo, o[:blk], o[blk:])
                ls_ref[g, index(start), :] = jnp.where(lo, lse[:blk], lse[blk:])
            return 0

        lax.fori_loop(0, n_tiles // DIL_TILES, tiles, 0)

    rows_per_trip = DIL_TILES * blk

    def combine(t, _):
        idx = pl.ds(pl.multiple_of(t * rows_per_trip, rows_per_trip), rows_per_trip)
        lses = [ls_ref[g, idx, :] for g in range(len(DIL_CFG))]
        mx = jnp.maximum(jnp.maximum(lses[0], lses[1]), lses[2])
        es = [jnp.exp2(l - mx) for l in lses]
        tot = es[0] + es[1] + es[2]
        for g in range(len(DIL_CFG)):
            o_ref[0, idx, g * LANES:(g + 1) * LANES] = (os_ref[g, idx, :] * (es[g] / tot)).astype(BF16)
        return 0

    lax.fori_loop(0, seq // rows_per_trip, combine, 0)


def _dilated(pb):
    B, S, _ = pb.shape
    n_g = len(DIL_CFG)
    return pl.pallas_call(
        functools.partial(_dilated_kernel, seq=S),
        grid=(B,),
        in_specs=[pl.BlockSpec((1, S, QKV_C), lambda b: (b, 0, PB_QC // n_g)),
                  pl.BlockSpec((1, S, QKV_C), lambda b: (b, 0, PB_KC // n_g)),
                  pl.BlockSpec((1, S, QKV_C), lambda b: (b, 0, PB_VC // n_g))],
        out_specs=pl.BlockSpec((1, S, QKV_C), lambda b: (b, 0, 0)),
        out_shape=jax.ShapeDtypeStruct((B, S, QKV_C), BF16),
        scratch_shapes=[pltpu.VMEM((n_g, S, LANES), F32)] * 5,
        compiler_params=_cparams(("parallel",)),
        name="dilated_attn",
    )(pb, pb, pb)


HALO = 16
FF_CHUNK = 256


def _proj_ffn_kernel(x_ref, xh_ref, oa_ref, oah_ref, ob_ref, obh_ref, oc_ref, och_ref, wo_ref, ln_ref,
                     wg_ref, wu_ref, cw_ref, cb_ref, wd_ref, y_ref, o_ref, h_ref, x1_ref, a_ref,
                     *, tiles_per_seq):
    i = pl.program_id(0)

    def norm(v):
        return (v * lax.rsqrt(jnp.mean(v * v, axis=-1, keepdims=True) + EPS)) * ln_ref[...]

    col = 0
    for main, halo in ((oa_ref, oah_ref), (ob_ref, obh_ref), (oc_ref, och_ref)):
        width = main.shape[1]
        o_ref[:HALO, col:col + width] = halo[...]
        o_ref[HALO:, col:col + width] = main[...]
        col += width
    attn = _dot(o_ref[...], wo_ref[...])
    x1 = x_ref[...] + attn[HALO:]
    x1_ref[...] = x1
    first = (i % tiles_per_seq) == 0
    h_ref[:HALO, :] = jnp.where(first, 0.0, norm(xh_ref[...] + attn[:HALO])).astype(BF16)
    h_ref[HALO:, :] = norm(x1).astype(BF16)

    for f0 in range(0, D_FF, FF_CHUNK):
        fs = slice(f0, min(f0 + FF_CHUNK, D_FF))
        g = _dot(h_ref[...], wg_ref[:, fs])
        u = _dot(h_ref[HALO:, :], wu_ref[:, fs])
        gc = (cw_ref[0:1, fs] * pltpu.roll(g, 2, 0)[HALO:] + cw_ref[1:2, fs] * pltpu.roll(g, 1, 0)[HALO:]
              + cw_ref[2:3, fs] * g[HALO:] + cb_ref[:, fs])
        a_ref[:, fs] = (jax.nn.silu(gc) * u).astype(BF16)
    y_ref[...] = x1_ref[...] + _dot(a_ref[...], wd_ref[...])


def _proj_ffn(x2, o_a, o_b, o_c, wo, ln, wg, wu, cw, cb, wd, seq, tm=512):
    M, D = x2.shape
    halo_blocks = tm // HALO
    main = lambda w: pl.BlockSpec((tm, w), lambda i: (i, 0))
    halo = lambda w: pl.BlockSpec((HALO, w), lambda i: (jnp.maximum(i * halo_blocks - 1, 0), 0))
    return pl.pallas_call(
        functools.partial(_proj_ffn_kernel, tiles_per_seq=seq // tm),
        grid=(M // tm,),
        in_specs=[
            main(D), halo(D), main(QKV_A), halo(QKV_A), main(Q_B), halo(Q_B), main(QKV_C), halo(QKV_C),
            _const_spec(wo.shape), _const_spec((1, D)),
            _const_spec(wg.shape), _const_spec(wu.shape), _const_spec(cw.shape), _const_spec(cb.shape),
            _const_spec(wd.shape),
        ],
        out_specs=main(D),
        out_shape=jax.ShapeDtypeStruct((M, D), F32),
        scratch_shapes=[pltpu.VMEM((HALO + tm, D), BF16), pltpu.VMEM((HALO + tm, D), BF16),
                        pltpu.VMEM((tm, D), F32), pltpu.VMEM((tm, D_FF), BF16)],
        compiler_params=_cparams(("parallel",)),
        name="proj_ffn",
    )(x2, x2, o_a, o_a, o_b, o_b, o_c, o_c, wo, ln, wg, wu, cw, cb, wd)


def _rope_tables(pos):
    half = ROPE_DIM // 2
    inv_freq = ROPE_THETA ** (-jnp.arange(half, dtype=F32) * 2.0 / ROPE_DIM)
    ang = pos.astype(F32)[:, None] * inv_freq
    c, s = jnp.cos(ang), jnp.sin(ang)
    n = pos.shape[0]
    rest = HEAD_DIM - ROPE_DIM
    cos_h = jnp.concatenate([c, c, jnp.ones((n, rest), F32)], axis=-1)
    sp_h = jnp.concatenate([jnp.zeros((n, half), F32), s, jnp.zeros((n, rest), F32)], axis=-1)
    sm_h = jnp.concatenate([-s, jnp.zeros((n, half + rest), F32)], axis=-1)
    two = lambda t: jnp.concatenate([t, t], axis=-1)
    return two(cos_h), two(sp_h), two(sm_h), c.T, s.T


def _pair_diag(w):
    z = jnp.zeros_like(w)
    return jnp.concatenate([jnp.concatenate([w, z], axis=-1), jnp.concatenate([z, w], axis=-1)], axis=-2)


def _overlap_t(seq):
    n_cmp = seq // NSA_CMP_STRIDE
    n_slc = seq // NSA_SEL_BLOCK
    starts = np.arange(n_cmp) * NSA_CMP_STRIDE
    j = np.arange(n_slc)
    ovl = (starts[None, :] < (j[:, None] + 1) * NSA_SEL_BLOCK) & (starts[None, :] + NSA_CMP_LEN > j[:, None] * NSA_SEL_BLOCK)
    ovl[:, n_cmp - 1] = False
    return jnp.asarray(ovl, BF16)


def _mixers(x, seq_tabs, cmp_tabs, ovl_t, ln1, w, qn_a, kn_a, qn_b, kn_b, pe_k, pe_v, wk1, wk2, wv1, wv2,
            qn_c, kn_c):
    rep = lambda g, n: jnp.tile(g, n)
    w_t, w_r = w
    gain_t = jnp.concatenate([rep(qn_a * Q_SCALE, H_MOBA), rep(qn_b * Q_SCALE, H_NSA)])
    gain_t = jnp.broadcast_to(gain_t[:, None], (gain_t.shape[0], LANES))
    gain_r = jnp.concatenate([rep(kn_a, H_MOBA), rep(kn_b[1], H_NSA_KV), rep(kn_b[2], H_NSA_KV),
                              rep(qn_c * Q_SCALE, H_DIL), rep(kn_c, H_DIL)])[None, :]
    pbt, pb, pf = _in_proj(x, ln1[None, :], w_t, w_r, gain_t, gain_r, *seq_tabs)

    o_a = _moba(pbt, pb)
    two = lambda t: jnp.concatenate([t, t], axis=-1)
    w1 = lambda t: _pair_diag(t.reshape(NSA_CMP_LEN, HEAD_DIM, NSA_CMP_HIDDEN)).astype(BF16).reshape(
        2, NSA_CMP_LEN // 2 * LANES, 2 * NSA_CMP_HIDDEN)
    kcmp, vcmp_t = _compress(pf, two(pe_k), two(pe_v), w1(wk1), _pair_diag(wk2).astype(BF16),
                             w1(wv1), _pair_diag(wv2).astype(BF16), two(kn_b[0])[None, :], *cmp_tabs[:3])
    o_b = _nsa(pbt, pb, pf, kcmp, vcmp_t, ovl_t)
    o_c = _dilated(pb)
    return o_a, o_b, o_c


def _layer(x, seq_tabs, cmp_tabs, ovl_t, ln1, w_in, qn_a, kn_a, qn_b, kn_b, pe_k, pe_v, wk1, wk2, wv1, wv2,
           qn_c, kn_c, w_out, ln2, w_gate, w_up, conv_w, conv_b, w_down):
    B, S, D = x.shape
    o_a, o_b, o_c = _mixers(x, seq_tabs, cmp_tabs, ovl_t, ln1, w_in, qn_a, kn_a, qn_b, kn_b, pe_k, pe_v,
                            wk1, wk2, wv1, wv2, qn_c, kn_c)
    nsa_rows = [w_out[QKV_A + h * HEAD_DIM:QKV_A + (h + 1) * HEAD_DIM] for h in NSA_ORDER]
    w_o = jnp.concatenate([w_out[:QKV_A]] + nsa_rows + [w_out[QKV_A + Q_B:]], axis=0).astype(BF16)
    M = B * S
    x2 = _proj_ffn(x.reshape(M, D), o_a.reshape(M, QKV_A), o_b.reshape(M, Q_B), o_c.reshape(M, QKV_C), w_o,
                   ln2[None, :], w_gate.astype(BF16), w_up.astype(BF16), conv_w, conv_b[None, :],
                   w_down.astype(BF16), S)
    return x2.reshape(B, S, D)


def kernel(x, ln1, w_in, qn_a, kn_a, qn_b, kn_b, cmp_pe_k, cmp_pe_v, cmp_k_w1, cmp_k_w2, cmp_v_w1, cmp_v_w2,
           qn_c, kn_c, w_out, ln2, w_gate, w_up, conv_w, conv_b, w_down):
    S = x.shape[1]
    seq_tabs = _rope_tables(jnp.arange(S))
    n_cmp = S // NSA_CMP_STRIDE
    cmp_tabs = _rope_tables(jnp.arange(n_cmp) * NSA_CMP_STRIDE + (NSA_CMP_LEN - 1))
    ovl_t = _overlap_t(S)
    for l in range(ln1.shape[0]):
        x = _layer(x, seq_tabs, cmp_tabs, ovl_t, ln1[l], _prep_w_in(w_in, l), qn_a[l], kn_a[l], qn_b[l], kn_b[l],
                   cmp_pe_k[l], cmp_pe_v[l], cmp_k_w1[l], cmp_k_w2[l], cmp_v_w1[l], cmp_v_w2[l], qn_c[l], kn_c[l],
                   w_out[l], ln2[l], w_gate[l], w_up[l], conv_w[l], conv_b[l], w_down[l])
    return x
```

```python
import functools

import numpy as np
import jax
import jax.numpy as jnp
from jax import lax
from jax.experimental import pallas as pl
from jax.experimental.pallas import tpu as pltpu

F32 = jnp.float32
BF16 = jnp.bfloat16

D_MODEL = 1024
HEAD_DIM = 64
N_HEADS = D_MODEL // HEAD_DIM
H_MOBA = N_HEADS // 4
H_NSA = (N_HEADS - H_MOBA) // 2
H_DIL = N_HEADS - H_MOBA - H_NSA
H_NSA_KV = 2
NSA_GROUP = H_NSA // H_NSA_KV
ROPE_DIM = HEAD_DIM // 4
ROPE_THETA = 500000.0
MOBA_BLOCK = 256
MOBA_TOPK = 3
NSA_CMP_LEN = 32
NSA_CMP_STRIDE = 16
NSA_CMP_HIDDEN = 128
NSA_SEL_BLOCK = 64
NSA_N_SEL = 6
NSA_WINDOW = 512
DIL_CFG = ((128, 1), (512, 4), (2048, 16))
BAND_BLOCK = 128
D_FF = 2816
EPS = 1e-6
NEG = -1e30
TINY = 1e-30
FORCE = 1e9
SCALE = HEAD_DIM ** -0.5
Q_SCALE = SCALE * float(np.log2(np.e))
QKV_A = H_MOBA * HEAD_DIM
Q_B = H_NSA * HEAD_DIM
KV_B = H_NSA_KV * HEAD_DIM
G_B = H_NSA * 3
QKV_C = H_DIL * HEAD_DIM
IN_SPLITS = (QKV_A, QKV_A, QKV_A, Q_B, KV_B, KV_B, KV_B, KV_B, KV_B, KV_B, G_B, QKV_C, QKV_C, QKV_C)

LANES = 128
TQ = 256
NSA_ORDER = (0, 3, 1, 4, 2, 5)
VMEM_LIMIT = 56 * 1024 * 1024

M_FLOOR = -1e29
V_ROWS = HEAD_DIM + 16

T_GROUP = ("qa", "qb", "va", "vsb", "vwb")
N_T_NORM, N_T = 5, 9
R_GROUP = ("ka", "ksb", "kwb", "qc", "kc", "vc", "kcb", "vcb", "gb")
N_R_NORM, N_R_BF16, N_R = 10, 13, 16
N_PBT, N_PB, N_F32 = N_T, N_R_BF16, N_R - N_R_BF16
PBT_QA, PBT_VA, PBT_VS, PBT_VW, PBT_QB = 0, 2, 4, 5, 6
PB_QC, PB_KC, PB_KA, PB_KS, PB_VC, PB_KW = 0, 3, 6, 8, 9, 12
_PBT_DST = (0, 1, 6, 7, 8, 2, 3, 4, 5)
_PB_DST = (6, 7, 8, 12, 0, 1, 2, 3, 4, 5, 9, 10, 11)
CHUNK = 4


def _cparams(sem):
    return pltpu.CompilerParams(dimension_semantics=sem, vmem_limit_bytes=VMEM_LIMIT)


def _const_spec(shape):
    return pl.BlockSpec(shape, lambda *_: (0,) * len(shape), pipeline_mode=pl.Buffered(1))


def _norm_rope(y, gain, cos_t, sin_p, sin_m, lo):
    ss = y * y
    s_lo = jnp.sum(jnp.where(lo, ss, 0.0), axis=-1, keepdims=True)
    s_hi = jnp.sum(jnp.where(lo, 0.0, ss), axis=-1, keepdims=True)
    ms = jnp.where(lo, s_lo, s_hi) * (1.0 / HEAD_DIM)
    z = (y * lax.rsqrt(ms + EPS)) * gain
    half = ROPE_DIM // 2
    return z * cos_t + pltpu.roll(z, half, 1) * sin_p + pltpu.roll(z, LANES - half, 1) * sin_m


def _online_update(m_ref, acc_ref, g, s, pv):
    m_old = m_ref[g]
    m_new = jnp.maximum(m_old, jnp.max(s, axis=0, keepdims=True))
    p = jnp.exp2(s - m_new)
    alpha = jnp.exp2(m_old - m_new)
    acc_ref[g] = alpha * acc_ref[g] + pv(p.astype(BF16))
    m_ref[g] = m_new


def _online_init(m_ref, acc_ref, g):
    m_ref[g] = jnp.full(m_ref.shape[1:], M_FLOOR, F32)
    acc_ref[g] = jnp.zeros(acc_ref.shape[1:], F32)


def _with_ones(v_t):
    row = lax.broadcasted_iota(jnp.int32, (V_ROWS - HEAD_DIM, v_t.shape[1]), 0)
    return jnp.concatenate([v_t, jnp.where(row == 0, 1.0, 0.0).astype(v_t.dtype)], axis=0)


def _normalise(acc):
    return acc[:HEAD_DIM] / jnp.maximum(acc[HEAD_DIM:HEAD_DIM + 1], TINY)


def _dot(a, b):
    return jnp.dot(a, b, preferred_element_type=F32)


def _half_rows(q_t, hh):
    z = jnp.zeros((HEAD_DIM, q_t.shape[1]), q_t.dtype)
    if hh == 0:
        return jnp.concatenate([q_t[:HEAD_DIM], z], axis=0)
    return jnp.concatenate([z, q_t[HEAD_DIM:]], axis=0)


def _split_bf16(a):
    hi = a.astype(BF16)
    return hi, (a - hi.astype(F32)).astype(BF16)


def _in_proj_plan(group):
    names = ("qa", "ka", "va", "qb", "kcb", "vcb", "ksb", "vsb", "kwb", "vwb", "gb", "qc", "kc", "vc")
    start = dict(zip(names, np.cumsum((0,) + IN_SPLITS[:-1]).tolist()))
    width = dict(zip(names, IN_SPLITS))
    plan = []
    for name in group:
        if name == "qb":
            heads = [start[name] + h * HEAD_DIM for h in NSA_ORDER]
            plan += [[(heads[2 * j], HEAD_DIM), (heads[2 * j + 1], HEAD_DIM)] for j in range(H_NSA // 2)]
        else:
            plan += [[(start[name] + c, min(LANES, width[name] - c))] for c in range(0, width[name], LANES)]
    return plan


def _wprep_kernel(w_ref, wt_ref, wr_ref, *, plan_t, plan_r, n_in):
    def take(a, width):
        b0 = a // LANES * LANES
        v0 = w_ref[0, :, b0:min(b0 + LANES, n_in)]
        if a + width <= b0 + LANES:
            return v0[:, a - b0:a - b0 + width]
        v1 = w_ref[0, :, b0 + LANES:min(b0 + 2 * LANES, n_in)]
        return jnp.concatenate([v0[:, a - b0:], v1[:, :a + width - b0 - LANES]], axis=1)

    def block(pieces):
        cols = [take(a, width) for a, width in pieces]
        filled = sum(width for _, width in pieces)
        if filled < LANES:
            cols.append(jnp.zeros((w_ref.shape[1], LANES - filled), F32))
        return cols[0] if len(cols) == 1 else jnp.concatenate(cols, axis=1)

    for d, pieces in enumerate(plan_t):
        wt_ref[d * LANES:(d + 1) * LANES, :] = block(pieces).T.astype(BF16)
    for d, pieces in enumerate(plan_r):
        wr_ref[:, d * LANES:(d + 1) * LANES] = block(pieces).astype(BF16)


def _prep_w_in(w_in, layer, tr=256):
    _, D, n_in = w_in.shape
    plan_t, plan_r = _in_proj_plan(T_GROUP), _in_proj_plan(R_GROUP)
    return pl.pallas_call(
        functools.partial(_wprep_kernel, plan_t=plan_t, plan_r=plan_r, n_in=n_in),
        grid=(D // tr,),
        in_specs=[pl.BlockSpec((1, tr, n_in), lambda i: (layer, i, 0))],
        out_specs=[pl.BlockSpec((N_T * LANES, tr), lambda i: (0, i)),
                   pl.BlockSpec((tr, N_R * LANES), lambda i: (i, 0))],
        out_shape=[jax.ShapeDtypeStruct((N_T * LANES, D), BF16), jax.ShapeDtypeStruct((D, N_R * LANES), BF16)],
        compiler_params=_cparams(("parallel",)),
        name="w_in_layout",
    )(w_in)


def _norm_rope_t(y, gain, cos8, sin8):
    half = ROPE_DIM // 2
    parts = []
    for hh in range(2):
        yh = y[hh * HEAD_DIM:(hh + 1) * HEAD_DIM]
        ms = jnp.sum(yh * yh, axis=0, keepdims=True) * (1.0 / HEAD_DIM)
        z = (yh * lax.rsqrt(ms + EPS)) * gain[hh * HEAD_DIM:(hh + 1) * HEAD_DIM]
        x1, x2 = z[:half], z[half:ROPE_DIM]
        parts += [x1 * cos8 - x2 * sin8, x2 * cos8 + x1 * sin8, z[ROPE_DIM:]]
    return jnp.concatenate(parts, axis=0)


def _inproj_kernel(x_ref, ln_ref, wt_ref, wr_ref, gt_ref, gr_ref, cos_ref, sp_ref, sm_ref, ct_ref, st_ref,
                   pbt_ref, pb_ref, pf_ref, *, tm):
    x = x_ref[0]
    h = x * lax.rsqrt(jnp.mean(x * x, axis=-1, keepdims=True) + EPS)
    h = (h * ln_ref[...]).astype(BF16)
    lo = lax.broadcasted_iota(jnp.int32, (tm, LANES), 1) < HEAD_DIM
    cos_t, sin_p, sin_m = cos_ref[...], sp_ref[...], sm_ref[...]
    for c0 in range(0, N_R, CHUNK):
        y4 = _dot(h, wr_ref[:, c0 * LANES:(c0 + CHUNK) * LANES])
        for c in range(c0, c0 + CHUNK):
            y = y4[:, (c - c0) * LANES:(c - c0 + 1) * LANES]
            if c < N_R_NORM:
                y = _norm_rope(y, gr_ref[:, c * LANES:(c + 1) * LANES], cos_t, sin_p, sin_m, lo)
            if c < N_R_BF16:
                d = _PB_DST[c]
                pb_ref[0, :, d * LANES:(d + 1) * LANES] = y.astype(BF16)
            else:
                d = c - N_R_BF16
                pf_ref[0, :, d * LANES:(d + 1) * LANES] = y
    cos8, sin8 = ct_ref[...], st_ref[...]
    for t0, t1 in ((0, N_T_NORM), (N_T_NORM, N_T)):
        y_t = lax.dot_general(wt_ref[t0 * LANES:t1 * LANES, :], h, (((1,), (1,)), ((), ())),
                              preferred_element_type=F32)
        for t in range(t0, t1):
            y = y_t[(t - t0) * LANES:(t - t0 + 1) * LANES]
            if t < N_T_NORM:
                gain = jnp.concatenate([gt_ref[t * LANES:(t + 1) * LANES, :]] * (tm // LANES), axis=1)
                y = _norm_rope_t(y, gain, cos8, sin8)
            y = y.astype(BF16)
            for u in range(tm // TQ):
                pbt_ref[0, _PBT_DST[t], u] = y[:, u * TQ:(u + 1) * TQ]


def _in_proj(x, ln, w_t, w_r, gain_t, gain_r, cos_t, sin_p, sin_m, cos8, sin8, tm=1024):
    B, S, D = x.shape
    nt = S // tm
    half = ROPE_DIM // 2
    return pl.pallas_call(
        functools.partial(_inproj_kernel, tm=tm),
        grid=(B * nt,),
        in_specs=[
            pl.BlockSpec((1, tm, D), lambda i: (i // nt, i % nt, 0)),
            _const_spec((1, D)),
            _const_spec(w_t.shape), _const_spec(w_r.shape), _const_spec(gain_t.shape), _const_spec(gain_r.shape),
            pl.BlockSpec((tm, LANES), lambda i: (i % nt, 0)),
            pl.BlockSpec((tm, LANES), lambda i: (i % nt, 0)),
            pl.BlockSpec((tm, LANES), lambda i: (i % nt, 0)),
            pl.BlockSpec((half, tm), lambda i: (0, i % nt)),
            pl.BlockSpec((half, tm), lambda i: (0, i % nt)),
        ],
        out_specs=[
            pl.BlockSpec((1, N_PBT, tm // TQ, LANES, TQ), lambda i: (i // nt, 0, i % nt, 0, 0)),
            pl.BlockSpec((1, tm, N_PB * LANES), lambda i: (i // nt, i % nt, 0)),
            pl.BlockSpec((1, tm, N_F32 * LANES), lambda i: (i // nt, i % nt, 0)),
        ],
        out_shape=[
            jax.ShapeDtypeStruct((B, N_PBT, S // TQ, LANES, TQ), BF16),
            jax.ShapeDtypeStruct((B, S, N_PB * LANES), BF16),
            jax.ShapeDtypeStruct((B, S, N_F32 * LANES), F32),
        ],
        compiler_params=_cparams(("parallel",)),
        name="in_proj",
    )(x, ln, w_t, w_r, gain_t, gain_r, cos_t, sin_p, sin_m, cos8, sin8)


MOBA_BATCH = 2

def _moba_kernel(qt_ref, k_ref, vt_ref, o_ref, q2_ref, bias_ref, m_ref, acc_ref, kmean_ref, *, n_blk):
    n_groups = qt_ref.shape[0] * (H_MOBA // 2)
    nidx = lax.broadcasted_iota(jnp.int32, (16, 2 * TQ), 0)
    kr = lax.broadcasted_iota(jnp.int32, (TQ, TQ), 0)
    ql = lax.broadcasted_iota(jnp.int32, (TQ, TQ), 1)
    causal_bias = jnp.where(kr <= ql, 0.0, NEG)
    for g in range(n_groups):
        bb, p = divmod(g, H_MOBA // 2)
        ps = slice(p * LANES, (p + 1) * LANES)
        rows = [jnp.sum(k_ref[bb, n * TQ:(n + 1) * TQ, ps].astype(F32), axis=0, keepdims=True)
                for n in range(n_blk)]
        kmean_ref[g] = (jnp.concatenate(rows + [jnp.zeros((16 - n_blk, LANES), F32)], axis=0)
                        * (1.0 / MOBA_BLOCK))
    lax.fori_loop(0, n_blk, functools.partial(_moba_tile, qt_ref, k_ref, vt_ref, o_ref, q2_ref, bias_ref, m_ref,
                                              acc_ref, kmean_ref, nidx, causal_bias, n_blk), 0)


def _moba_tile(qt_ref, k_ref, vt_ref, o_ref, q2_ref, bias_ref, m_ref, acc_ref, kmean_ref, nidx,
               causal_bias, n_blk, qi, carry):
    n_pairs = H_MOBA // 2
    groups = [divmod(g, n_pairs) for g in range(qt_ref.shape[0] * n_pairs)]
    past = nidx < qi
    for g, (bb, p) in enumerate(groups):
        km_hi, km_lo = _split_bf16(kmean_ref[g])
        q2 = jnp.concatenate([_half_rows(qt_ref[bb, p, qi], hh) for hh in range(2)], axis=1)
        q2_ref[g] = q2
        gate = _dot(km_hi, q2) + _dot(km_lo, q2)
        for n in range(n_blk):
            row = gate[n:n + 1, :]
            ge = jnp.where(gate >= row, 1.0, 0.0)
            gt = jnp.where(gate > row, 1.0, 0.0)
            beats = jnp.where(past, jnp.where(nidx < n, ge, gt), 0.0)
            cnt = jnp.sum(beats, axis=0, keepdims=True)
            bias_ref[g, n] = jnp.where(cnt < MOBA_TOPK, 0.0, NEG)
        _online_init(m_ref, acc_ref, g)

    def scores(off, g):
        bb, p = groups[g]
        return _dot(k_ref[bb, pl.ds(off, TQ), p * LANES:(p + 1) * LANES], q2_ref[g])

    def pv(g, kt):
        bb, p = groups[g]

        def fn(pr):
            return jnp.concatenate([_dot(_with_ones(vt_ref[bb, p, kt, :HEAD_DIM, :]), pr[:, :TQ]),
                                    _dot(_with_ones(vt_ref[bb, p, kt, HEAD_DIM:, :]), pr[:, TQ:])], axis=1)
        return fn

    def past_scores(kt):
        off = pl.multiple_of(kt * TQ, TQ)
        return [scores(off, g) + bias_ref[g, kt] for g in range(len(groups))]

    def past_tile(kt, s):
        for g in range(len(groups)):
            _online_update(m_ref, acc_ref, g, s[g], pv(g, kt))

    def past_pair(i, _):
        s_a, s_b = past_scores(2 * i), past_scores(2 * i + 1)
        past_tile(2 * i, s_a)
        past_tile(2 * i + 1, s_b)
        return 0

    lax.fori_loop(0, qi // 2, past_pair, 0)

    @pl.when(qi % 2 == 1)
    def _():
        past_tile(qi - 1, past_scores(qi - 1))
    off = pl.multiple_of(qi * TQ, TQ)
    causal2 = jnp.concatenate([causal_bias, causal_bias], axis=1)
    s = [scores(off, g) + causal2 for g in range(len(groups))]
    for g, (bb, p) in enumerate(groups):
        _online_update(m_ref, acc_ref, g, s[g], pv(g, qi))
        o2 = _normalise(acc_ref[g])
        o_ref[bb, pl.ds(off, TQ), p * LANES:(p + 1) * LANES] = jnp.concatenate(
            [o2[:, :TQ], o2[:, TQ:]], axis=0).T.astype(BF16)
    return carry


def _moba(pbt, pb):
    B, _, n_t, _, _ = pbt.shape
    S = pb.shape[1]
    n_pairs = H_MOBA // 2
    n_groups = MOBA_BATCH * n_pairs
    assert TQ == MOBA_BLOCK and n_t == S // MOBA_BLOCK and B % MOBA_BATCH == 0
    return pl.pallas_call(
        functools.partial(_moba_kernel, n_blk=S // MOBA_BLOCK),
        grid=(B // MOBA_BATCH,),
        in_specs=[
            pl.BlockSpec((MOBA_BATCH, n_pairs, n_t, LANES, TQ), lambda b: (b, PBT_QA // n_pairs, 0, 0, 0)),
            pl.BlockSpec((MOBA_BATCH, S, QKV_A), lambda b: (b, 0, PB_KA // n_pairs)),
            pl.BlockSpec((MOBA_BATCH, n_pairs, n_t, LANES, TQ), lambda b: (b, PBT_VA // n_pairs, 0, 0, 0)),
        ],
        out_specs=pl.BlockSpec((MOBA_BATCH, S, QKV_A), lambda b: (b, 0, 0)),
        out_shape=jax.ShapeDtypeStruct((B, S, QKV_A), BF16),
        scratch_shapes=[pltpu.VMEM((n_groups, LANES, 2 * TQ), BF16),
                        pltpu.VMEM((n_groups, S // MOBA_BLOCK, 1, 2 * TQ), F32),
                        pltpu.VMEM((n_groups, 1, 2 * TQ), F32),
                        pltpu.VMEM((n_groups, V_ROWS, 2 * TQ), F32),
                        pltpu.VMEM((n_groups, 16, LANES), F32)],
        compiler_params=_cparams(("parallel",)),
        name="moba",
    )(pbt, pb, pbt)


def _compress_kernel(kc_ref, vc_ref, pek_ref, pev_ref, w1k_ref, w2k_ref, w1v_ref, w2v_ref, gain_ref,
                     cos_ref, sp_ref, sm_ref, ko_ref, vo_ref, *, n_cmp):
    half_len = NSA_CMP_LEN // 2

    def mlp(t_ref, pe_ref, w1_ref, w2_ref):
        toks = [t_ref[0, pl.ds(l, n_cmp, stride=NSA_CMP_STRIDE), :] for l in range(half_len)]
        first = _dot(jnp.concatenate([(toks[l] + pe_ref[l:l + 1, :]).astype(BF16)
                                      for l in range(half_len)], axis=1), w1_ref[0])
        second = _dot(jnp.concatenate([(toks[l] + pe_ref[half_len + l:half_len + l + 1, :]).astype(BF16)
                                       for l in range(half_len)], axis=1), w1_ref[1])
        hid = jax.nn.gelu(first + pltpu.roll(second, n_cmp - 1, 0))
        return _dot(hid.astype(BF16), w2_ref[...])

    valid = lax.broadcasted_iota(jnp.int32, (n_cmp, LANES), 0) < n_cmp - 1
    lo = lax.broadcasted_iota(jnp.int32, (n_cmp, LANES), 1) < HEAD_DIM
    kc = mlp(kc_ref, pek_ref, w1k_ref, w2k_ref)
    kc = _norm_rope(kc, gain_ref[...], cos_ref[...], sp_ref[...], sm_ref[...], lo)
    ko_ref[0] = jnp.where(valid, kc, 0.0).astype(BF16)
    vc = mlp(vc_ref, pev_ref, w1v_ref, w2v_ref)
    vo_ref[0] = jnp.where(valid, vc, 0.0).T.astype(BF16)


def _compress(pf, pe_k, pe_v, w1k, w2k, w1v, w2v, gain, cos_c, sp_c, sm_c):
    B, S, _ = pf.shape
    n_cmp = S // NSA_CMP_STRIDE
    cs = _const_spec
    return pl.pallas_call(
        functools.partial(_compress_kernel, n_cmp=n_cmp),
        grid=(B,),
        in_specs=[
            pl.BlockSpec((1, S, LANES), lambda b: (b, 0, 0)),
            pl.BlockSpec((1, S, LANES), lambda b: (b, 0, 1)),
            cs(pe_k.shape), cs(pe_v.shape), cs(w1k.shape), cs(w2k.shape), cs(w1v.shape), cs(w2v.shape),
            cs(gain.shape), cs(cos_c.shape), cs(sp_c.shape), cs(sm_c.shape),
        ],
        out_specs=[pl.BlockSpec((1, n_cmp, LANES), lambda b: (b, 0, 0)),
                   pl.BlockSpec((1, LANES, n_cmp), lambda b: (b, 0, 0))],
        out_shape=[jax.ShapeDtypeStruct((B, n_cmp, LANES), BF16),
                   jax.ShapeDtypeStruct((B, LANES, n_cmp), BF16)],
        compiler_params=_cparams(("parallel",)),
        name="nsa_compress",
    )(pf, pf, pe_k, pe_v, w1k, w2k, w1v, w2v, gain, cos_c, sp_c, sm_c)


def _nsa_kernel(*refs, n_cmp, n_slc, n_t):
    lax.fori_loop(0, n_t, functools.partial(_nsa_tile, refs, n_cmp, n_slc), 0)


def _nsa_tile(refs, n_cmp, n_slc, qi, carry):
    (qt_ref, kc_ref, vct_ref, ks_ref, vst_ref, kw_ref, vwt_ref, g_ref, ovl_ref, o_ref,
     q6_ref, impf_ref, improw_ref, m_ref, acc_ref, ot_ref) = refs
    blk_per_tile = TQ // NSA_SEL_BLOCK
    off = pl.multiple_of(qi * TQ, TQ)
    tpos = qi * TQ + lax.broadcasted_iota(jnp.int32, (1, TQ), 1)
    sig = jax.nn.sigmoid(g_ref[0, pl.ds(off, TQ), :].T)
    kr = lax.broadcasted_iota(jnp.int32, (TQ, TQ), 0)
    ql = lax.broadcasted_iota(jnp.int32, (TQ, TQ), 1)
    causal_bias = jnp.where(kr <= ql, 0.0, NEG)
    nidx = lax.broadcasted_iota(jnp.int32, (n_cmp, TQ), 0)
    cmask = (nidx * NSA_CMP_STRIDE + (NSA_CMP_LEN - 1) <= tpos) & (nidx < n_cmp - 1)
    cmp_bias = jnp.where(cmask, 0.0, NEG)
    jj = lax.broadcasted_iota(jnp.int32, (n_slc, TQ), 0)
    cur = lax.shift_right_arithmetic(tpos, NSA_SEL_BLOCK.bit_length() - 1)
    forced = (jj == 0) | (jj == cur) | (jj == cur - 1)
    gw = NSA_GROUP * TQ

    def half(kk):
        return slice(kk * HEAD_DIM, (kk + 1) * HEAD_DIM)

    def cols(kk):
        return slice(kk * gw, (kk + 1) * gw)

    def tile3(a):
        return jnp.concatenate([a] * NSA_GROUP, axis=1)

    def gate3(kk, branch):
        return jnp.concatenate([sig[3 * (NSA_GROUP * kk + j) + branch:3 * (NSA_GROUP * kk + j) + branch + 1, :]
                                for j in range(NSA_GROUP)], axis=1)

    q6 = jnp.concatenate([_half_rows(qt_ref[0, j, qi], kk) for kk in range(H_NSA_KV) for j in range(NSA_GROUP)],
                         axis=1)
    q6_ref[...] = q6

    s_cmp = _dot(kc_ref[0], q6)
    cmp_bias3 = tile3(cmp_bias)
    for kk in range(H_NSA_KV):
        s = s_cmp[:, cols(kk)] + cmp_bias3
        m = jnp.maximum(jnp.max(s, axis=0, keepdims=True), M_FLOOR)
        p = jnp.exp2(s - m)
        p = p / jnp.maximum(jnp.sum(p, axis=0, keepdims=True), TINY)
        ot_ref[kk] = gate3(kk, 0) * _dot(vct_ref[0, half(kk), :], p.astype(BF16))
        p_sum = (p[:, :TQ] + p[:, TQ:2 * TQ]) + p[:, 2 * TQ:]
        _online_init(m_ref, acc_ref, kk)
        p_hi, p_lo = _split_bf16(p_sum)
        imp = _dot(ovl_ref[...], p_hi) + _dot(ovl_ref[...], p_lo)
        imp = jnp.where(jj > cur, NEG, jnp.where(forced, FORCE, imp))
        impf_ref[kk] = imp
        for j in range(n_slc):
            improw_ref[kk, j] = imp[j:j + 1, :]

    def sel_bias(kk, kt):
        imp = impf_ref[kk]
        parts = []
        for c in range(blk_per_tile):
            j = kt * blk_per_tile + c
            row = improw_ref[kk, j]
            ge = jnp.where(imp >= row, 1.0, 0.0)
            gt = jnp.where(imp > row, 1.0, 0.0)
            cnt = jnp.sum(jnp.where(jj < j, ge, gt), axis=0, keepdims=True)
            parts.append(jnp.broadcast_to(jnp.where(cnt < NSA_N_SEL, 0.0, NEG), (NSA_SEL_BLOCK, TQ)))
        return jnp.concatenate(parts, axis=0)

    def sel_pv(kk, kt):
        return lambda pr: _dot(_with_ones(vst_ref[0, 0, kt, half(kk), :]), pr)

    def sel_scores(kt):
        off = pl.multiple_of(kt * TQ, TQ)
        return _dot(ks_ref[0, pl.ds(off, TQ), :], q6_ref[...])

    def sel_tile(kt, s6):
        for kk in range(H_NSA_KV):
            s = s6[:, cols(kk)] + tile3(sel_bias(kk, kt))
            _online_update(m_ref, acc_ref, kk, s, sel_pv(kk, kt))

    def sel_pair(i, _):
        s_a, s_b = sel_scores(2 * i), sel_scores(2 * i + 1)
        sel_tile(2 * i, s_a)
        sel_tile(2 * i + 1, s_b)
        return 0

    lax.fori_loop(0, qi // 2, sel_pair, 0)

    @pl.when(qi % 2 == 1)
    def _():
        sel_tile(qi - 1, sel_scores(qi - 1))

    kt1 = jnp.maximum(qi - 1, 0)
    kt2 = jnp.maximum(qi - 2, 0)
    ks_d = ks_ref[0, pl.ds(off, TQ), :]
    kw_0 = kw_ref[0, pl.ds(off, TQ), :]
    kw_1 = kw_ref[0, pl.ds(pl.multiple_of(kt1 * TQ, TQ), TQ), :]
    kw_2 = kw_ref[0, pl.ds(pl.multiple_of(kt2 * TQ, TQ), TQ), :]
    win_bias2 = jnp.where(kr > ql + jnp.where(qi >= 2, 0, TQ), 0.0, NEG)
    win_bias1 = jnp.where(qi >= 1, 0.0, NEG)
    q6 = q6_ref[...]
    sd6, w2, w1, w0 = _dot(ks_d, q6), _dot(kw_2, q6), _dot(kw_1, q6), _dot(kw_0, q6)
    causal3 = tile3(causal_bias)
    win_bias2 = tile3(win_bias2)
    for kk in range(H_NSA_KV):
        s = sd6[:, cols(kk)] + tile3(sel_bias(kk, qi) + causal_bias)
        _online_update(m_ref, acc_ref, kk, s, sel_pv(kk, qi))
        o_sel = _normalise(acc_ref[kk])
        s2 = w2[:, cols(kk)] + win_bias2
        s1 = w1[:, cols(kk)] + win_bias1
        s0 = w0[:, cols(kk)] + causal3
        m = jnp.maximum(jnp.maximum(jnp.max(s2, axis=0, keepdims=True), jnp.max(s1, axis=0, keepdims=True)),
                        jnp.max(s0, axis=0, keepdims=True))
        p2, p1, p0 = jnp.exp2(s2 - m), jnp.exp2(s1 - m), jnp.exp2(s0 - m)
        acc = (_dot(_with_ones(vwt_ref[0, 0, kt2, half(kk), :]), p2.astype(BF16))
               + _dot(_with_ones(vwt_ref[0, 0, kt1, half(kk), :]), p1.astype(BF16))
               + _dot(_with_ones(vwt_ref[0, 0, qi, half(kk), :]), p0.astype(BF16)))
        o_win = _normalise(acc)
        ot_ref[kk] = (ot_ref[kk] + gate3(kk, 1) * o_sel) + gate3(kk, 2) * o_win
    for j in range(NSA_GROUP):
        js = slice(j * TQ, (j + 1) * TQ)
        o_ref[0, pl.ds(off, TQ), j * LANES:(j + 1) * LANES] = jnp.concatenate(
            [ot_ref[0, :, js], ot_ref[1, :, js]], axis=0).T.astype(BF16)
    return carry


def _nsa(pbt, pb, pf, kcmp, vcmp_t, ovl_t):
    B, _, n_t, _, _ = pbt.shape
    S = pb.shape[1]
    n_cmp = kcmp.shape[1]
    n_slc = S // NSA_SEL_BLOCK
    gw = NSA_GROUP * TQ
    assert NSA_WINDOW == 2 * TQ and TQ % NSA_SEL_BLOCK == 0 and S % TQ == 0
    return pl.pallas_call(
        functools.partial(_nsa_kernel, n_cmp=n_cmp, n_slc=n_slc, n_t=n_t),
        grid=(B,),
        in_specs=[
            pl.BlockSpec((1, NSA_GROUP, n_t, LANES, TQ), lambda b: (b, PBT_QB // NSA_GROUP, 0, 0, 0)),
            pl.BlockSpec((1, n_cmp, LANES), lambda b: (b, 0, 0)),
            pl.BlockSpec((1, LANES, n_cmp), lambda b: (b, 0, 0)),
            pl.BlockSpec((1, S, LANES), lambda b: (b, 0, PB_KS)),
            pl.BlockSpec((1, 1, n_t, LANES, TQ), lambda b: (b, PBT_VS, 0, 0, 0)),
            pl.BlockSpec((1, S, LANES), lambda b: (b, 0, PB_KW)),
            pl.BlockSpec((1, 1, n_t, LANES, TQ), lambda b: (b, PBT_VW, 0, 0, 0)),
            pl.BlockSpec((1, S, LANES), lambda b: (b, 0, 2)),
            _const_spec(ovl_t.shape),
        ],
        out_specs=pl.BlockSpec((1, S, Q_B), lambda b: (b, 0, 0)),
        out_shape=jax.ShapeDtypeStruct((B, S, Q_B), BF16),
        scratch_shapes=[pltpu.VMEM((LANES, H_NSA * TQ), BF16),
                        pltpu.VMEM((H_NSA_KV, n_slc, TQ), F32), pltpu.VMEM((H_NSA_KV, n_slc, 1, TQ), F32),
                        pltpu.VMEM((H_NSA_KV, 1, gw), F32),
                        pltpu.VMEM((H_NSA_KV, V_ROWS, gw), F32),
                        pltpu.VMEM((H_NSA_KV, HEAD_DIM, gw), F32)],
        compiler_params=_cparams(("parallel",)),
        name="nsa_attn",
    )(pbt, kcmp, vcmp_t, pb, pbt, pb, pbt, pf, ovl_t)


DIL_TILES = 16

def _dilated_kernel(q_ref, k_ref, v_ref, o_ref, qf_ref, kf_ref, vf_ref, os_ref, ls_ref, *, seq):
    blk = BAND_BLOCK
    n_tiles = seq // blk
    lo = lax.broadcasted_iota(jnp.int32, (blk, LANES), 1) < HEAD_DIM
    for g, (window, dil) in enumerate(DIL_CFG):
        max_dist = window // dil
        tiles_per_class = n_tiles // dil
        use_prev = tiles_per_class > 1
        n_keys = 2 * blk if use_prev else blk
        ri = lax.broadcasted_iota(jnp.int32, (2 * blk, n_keys), 0) & (blk - 1)
        ci = lax.broadcasted_iota(jnp.int32, (2 * blk, n_keys), 1)
        dist = ri + (n_keys - blk) - ci
        band = (dist >= 0) & (dist <= max_dist)
        band_bias = jnp.where(band, 0.0, NEG)
        qf_ref[g] = q_ref[0, :, g * LANES:(g + 1) * LANES].astype(F32)
        kf_ref[g] = k_ref[0, :, g * LANES:(g + 1) * LANES].astype(F32)
        vf_ref[g] = v_ref[0, :, g * LANES:(g + 1) * LANES].astype(F32)

        def index(start, dil=dil):
            if dil == 1:
                return pl.ds(pl.multiple_of(start, blk), blk)
            return pl.ds(start, blk, stride=dil)

        def tiles(i, _, g=g, dil=dil, tiles_per_class=tiles_per_class, use_prev=use_prev, band=band,
                  band_bias=band_bias, ci=ci, index=index):
            scored = []
            for u in range(DIL_TILES):
                t = i * DIL_TILES + u
                cls = t // tiles_per_class
                mt = t % tiles_per_class
                start = cls + dil * blk * mt
                q = qf_ref[g, index(start), :]
                q2 = jnp.concatenate([jnp.where(lo, q, 0.0), jnp.where(lo, 0.0, q)], axis=0).astype(BF16)
                if use_prev:
                    prev = jnp.maximum(start - dil * blk, cls)
                    k2 = jnp.concatenate([kf_ref[g, index(prev), :], kf_ref[g, index(start), :]], axis=0)
                    v2 = jnp.concatenate([vf_ref[g, index(prev), :], vf_ref[g, index(start), :]], axis=0)
                    bias = jnp.where(band & (ci >= jnp.where(mt > 0, 0, blk)), 0.0, NEG)
                else:
                    k2, v2, bias = kf_ref[g, index(start), :], vf_ref[g, index(start), :], band_bias
                s = lax.dot_general(q2, k2.astype(BF16), (((1,), (1,)), ((), ())), preferred_element_type=F32)
                v2 = jnp.concatenate([v2.astype(BF16), jnp.ones((v2.shape[0], LANES), BF16)], axis=1)
                scored.append((start, s + bias, v2))
            probs = []
            for start, s, v2 in scored:
                m = jnp.max(s, axis=-1, keepdims=True)
                probs.append((start, m, jnp.exp2(s - m).astype(BF16), v2))
            for start, m, p, v2 in probs:
                o2 = _dot(p, v2)
                den = jnp.maximum(o2[:, LANES:], TINY)
                o = o2[:, :LANES] / den
                lse = m + jnp.log2(den)
                os_ref[g, index(start), :] = jnp.where(lo, o[:blk], o[blk:])
                ls_ref[g, index(start), :] = jnp.where(lo, lse[:blk], lse[blk:])
            return 0

        lax.fori_loop(0, n_tiles // DIL_TILES, tiles, 0)

    rows_per_trip = DIL_TILES * blk

    def combine(t, _):
        idx = pl.ds(pl.multiple_of(t * rows_per_trip, rows_per_trip), rows_per_trip)
        lses = [ls_ref[g, idx, :] for g in range(len(DIL_CFG))]
        mx = jnp.maximum(jnp.maximum(lses[0], lses[1]), lses[2])
        es = [jnp.exp2(l - mx) for l in lses]
        tot = es[0] + es[1] + es[2]
        for g in range(len(DIL_CFG)):
            o_ref[0, idx, g * LANES:(g + 1) * LANES] = (os_ref[g, idx, :] * (es[g] / tot)).astype(BF16)
        return 0

    lax.fori_loop(0, seq // rows_per_trip, combine, 0)


def _dilated(pb):
    B, S, _ = pb.shape
    n_g = len(DIL_CFG)
    return pl.pallas_call(
        functools.partial(_dilated_kernel, seq=S),
        grid=(B,),
        in_specs=[pl.BlockSpec((1, S, QKV_C), lambda b: (b, 0, PB_QC // n_g)),
                  pl.BlockSpec((1, S, QKV_C), lambda b: (b, 0, PB_KC // n_g)),
                  pl.BlockSpec((1, S, QKV_C), lambda b: (b, 0, PB_VC // n_g))],
        out_specs=pl.BlockSpec((1, S, QKV_C), lambda b: (b, 0, 0)),
        out_shape=jax.ShapeDtypeStruct((B, S, QKV_C), BF16),
        scratch_shapes=[pltpu.VMEM((n_g, S, LANES), F32)] * 5,
        compiler_params=_cparams(("parallel",)),
        name="dilated_attn",
    )(pb, pb, pb)


HALO = 16
FF_CHUNK = 256


def _proj_ffn_kernel(x_ref, xh_ref, oa_ref, oah_ref, ob_ref, obh_ref, oc_ref, och_ref, wo_ref, ln_ref,
                     wg_ref, wu_ref, cw_ref, cb_ref, wd_ref, y_ref, o_ref, h_ref, x1_ref, a_ref,
                     *, tiles_per_seq):
    i = pl.program_id(0)

    def norm(v):
        return (v * lax.rsqrt(jnp.mean(v * v, axis=-1, keepdims=True) + EPS)) * ln_ref[...]

    col = 0
    for main, halo in ((oa_ref, oah_ref), (ob_ref, obh_ref), (oc_ref, och_ref)):
        width = main.shape[1]
        o_ref[:HALO, col:col + width] = halo[...]
        o_ref[HALO:, col:col + width] = main[...]
        col += width
    attn = _dot(o_ref[...], wo_ref[...])
    x1 = x_ref[...] + attn[HALO:]
    x1_ref[...] = x1
    first = (i % tiles_per_seq) == 0
    h_ref[:HALO, :] = jnp.where(first, 0.0, norm(xh_ref[...] + attn[:HALO])).astype(BF16)
    h_ref[HALO:, :] = norm(x1).astype(BF16)

    for f0 in range(0, D_FF, FF_CHUNK):
        fs = slice(f0, min(f0 + FF_CHUNK, D_FF))
        g = _dot(h_ref[...], wg_ref[:, fs])
        u = _dot(h_ref[HALO:, :], wu_ref[:, fs])
        gc = (cw_ref[0:1, fs] * pltpu.roll(g, 2, 0)[HALO:] + cw_ref[1:2, fs] * pltpu.roll(g, 1, 0)[HALO:]
              + cw_ref[2:3, fs] * g[HALO:] + cb_ref[:, fs])
        a_ref[:, fs] = (jax.nn.silu(gc) * u).astype(BF16)
    y_ref[...] = x1_ref[...] + _dot(a_ref[...], wd_ref[...])


def _proj_ffn(x2, o_a, o_b, o_c, wo, ln, wg, wu, cw, cb, wd, seq, tm=512):
    M, D = x2.shape
    halo_blocks = tm // HALO
    main = lambda w: pl.BlockSpec((tm, w), lambda i: (i, 0))
    halo = lambda w: pl.BlockSpec((HALO, w), lambda i: (jnp.maximum(i * halo_blocks - 1, 0), 0))
    return pl.pallas_call(
        functools.partial(_proj_ffn_kernel, tiles_per_seq=seq // tm),
        grid=(M // tm,),
        in_specs=[
            main(D), halo(D), main(QKV_A), halo(QKV_A), main(Q_B), halo(Q_B), main(QKV_C), halo(QKV_C),
            _const_spec(wo.shape), _const_spec((1, D)),
            _const_spec(wg.shape), _const_spec(wu.shape), _const_spec(cw.shape), _const_spec(cb.shape),
            _const_spec(wd.shape),
        ],
        out_specs=main(D),
        out_shape=jax.ShapeDtypeStruct((M, D), F32),
        scratch_shapes=[pltpu.VMEM((HALO + tm, D), BF16), pltpu.VMEM((HALO + tm, D), BF16),
                        pltpu.VMEM((tm, D), F32), pltpu.VMEM((tm, D_FF), BF16)],
        compiler_params=_cparams(("parallel",)),
        name="proj_ffn",
    )(x2, x2, o_a, o_a, o_b, o_b, o_c, o_c, wo, ln, wg, wu, cw, cb, wd)


def _rope_tables(pos):
    half = ROPE_DIM // 2
    inv_freq = ROPE_THETA ** (-jnp.arange(half, dtype=F32) * 2.0 / ROPE_DIM)
    ang = pos.astype(F32)[:, None] * inv_freq
    c, s = jnp.cos(ang), jnp.sin(ang)
    n = pos.shape[0]
    rest = HEAD_DIM - ROPE_DIM
    cos_h = jnp.concatenate([c, c, jnp.ones((n, rest), F32)], axis=-1)
    sp_h = jnp.concatenate([jnp.zeros((n, half), F32), s, jnp.zeros((n, rest), F32)], axis=-1)
    sm_h = jnp.concatenate([-s, jnp.zeros((n, half + rest), F32)], axis=-1)
    two = lambda t: jnp.concatenate([t, t], axis=-1)
    return two(cos_h), two(sp_h), two(sm_h), c.T, s.T


def _pair_diag(w):
    z = jnp.zeros_like(w)
    return jnp.concatenate([jnp.concatenate([w, z], axis=-1), jnp.concatenate([z, w], axis=-1)], axis=-2)


def _overlap_t(seq):
    n_cmp = seq // NSA_CMP_STRIDE
    n_slc = seq // NSA_SEL_BLOCK
    starts = np.arange(n_cmp) * NSA_CMP_STRIDE
    j = np.arange(n_slc)
    ovl = (starts[None, :] < (j[:, None] + 1) * NSA_SEL_BLOCK) & (starts[None, :] + NSA_CMP_LEN > j[:, None] * NSA_SEL_BLOCK)
    ovl[:, n_cmp - 1] = False
    return jnp.asarray(ovl, BF16)


def _mixers(x, seq_tabs, cmp_tabs, ovl_t, ln1, w, qn_a, kn_a, qn_b, kn_b, pe_k, pe_v, wk1, wk2, wv1, wv2,
            qn_c, kn_c):
    rep = lambda g, n: jnp.tile(g, n)
    w_t, w_r = w
    gain_t = jnp.concatenate([rep(qn_a * Q_SCALE, H_MOBA), rep(qn_b * Q_SCALE, H_NSA)])
    gain_t = jnp.broadcast_to(gain_t[:, None], (gain_t.shape[0], LANES))
    gain_r = jnp.concatenate([rep(kn_a, H_MOBA), rep(kn_b[1], H_NSA_KV), rep(kn_b[2], H_NSA_KV),
                              rep(qn_c * Q_SCALE, H_DIL), rep(kn_c, H_DIL)])[None, :]
    pbt, pb, pf = _in_proj(x, ln1[None, :], w_t, w_r, gain_t, gain_r, *seq_tabs)

    o_a = _moba(pbt, pb)
    two = lambda t: jnp.concatenate([t, t], axis=-1)
    w1 = lambda t: _pair_diag(t.reshape(NSA_CMP_LEN, HEAD_DIM, NSA_CMP_HIDDEN)).astype(BF16).reshape(
        2, NSA_CMP_LEN // 2 * LANES, 2 * NSA_CMP_HIDDEN)
    kcmp, vcmp_t = _compress(pf, two(pe_k), two(pe_v), w1(wk1), _pair_diag(wk2).astype(BF16),
                             w1(wv1), _pair_diag(wv2).astype(BF16), two(kn_b[0])[None, :], *cmp_tabs[:3])
    o_b = _nsa(pbt, pb, pf, kcmp, vcmp_t, ovl_t)
    o_c = _dilated(pb)
    return o_a, o_b, o_c


def _layer(x, seq_tabs, cmp_tabs, ovl_t, ln1, w_in, qn_a, kn_a, qn_b, kn_b, pe_k, pe_v, wk1, wk2, wv1, wv2,
           qn_c, kn_c, w_out, ln2, w_gate, w_up, conv_w, conv_b, w_down):
    B, S, D = x.shape
    o_a, o_b, o_c = _mixers(x, seq_tabs, cmp_tabs, ovl_t, ln1, w_in, qn_a, kn_a, qn_b, kn_b, pe_k, pe_v,
                            wk1, wk2, wv1, wv2, qn_c, kn_c)
    nsa_rows = [w_out[QKV_A + h * HEAD_DIM:QKV_A + (h + 1) * HEAD_DIM] for h in NSA_ORDER]
    w_o = jnp.concatenate([w_out[:QKV_A]] + nsa_rows + [w_out[QKV_A + Q_B:]], axis=0).astype(BF16)
    M = B * S
    x2 = _proj_ffn(x.reshape(M, D), o_a.reshape(M, QKV_A), o_b.reshape(M, Q_B), o_c.reshape(M, QKV_C), w_o,
                   ln2[None, :], w_gate.astype(BF16), w_up.astype(BF16), conv_w, conv_b[None, :],
                   w_down.astype(BF16), S)
    return x2.reshape(B, S, D)


def kernel(x, ln1, w_in, qn_a, kn_a, qn_b, kn_b, cmp_pe_k, cmp_pe_v, cmp_k_w1, cmp_k_w2, cmp_v_w1, cmp_v_w2,
           qn_c, kn_c, w_out, ln2, w_gate, w_up, conv_w, conv_b, w_down):
    S = x.shape[1]
    seq_tabs = _rope_tables(jnp.arange(S))
    n_cmp = S // NSA_CMP_STRIDE
    cmp_tabs = _rope_tables(jnp.arange(n_cmp) * NSA_CMP_STRIDE + (NSA_CMP_LEN - 1))
    ovl_t = _overlap_t(S)
    for l in range(ln1.shape[0]):
        x = _layer(x, seq_tabs, cmp_tabs, ovl_t, ln1[l], _prep_w_in(w_in, l), qn_a[l], kn_a[l], qn_b[l], kn_b[l],
                   cmp_pe_k[l], cmp_pe_v[l], cmp_k_w1[l], cmp_k_w2[l], cmp_v_w1[l], cmp_v_w2[l], qn_c[l], kn_c[l],
                   w_out[l], ln2[l], w_gate[l], w_up[l], conv_w[l], conv_b[l], w_down[l])
    return x
```

```python
import functools

import numpy as np
import jax
import jax.numpy as jnp
from jax import lax
from jax.experimental import pallas as pl
from jax.experimental.pallas import tpu as pltpu

F32 = jnp.float32
BF16 = jnp.bfloat16

D_MODEL = 1024
HEAD_DIM = 64
N_HEADS = D_MODEL // HEAD_DIM
H_MOBA = N_HEADS // 4
H_NSA = (N_HEADS - H_MOBA) // 2
H_DIL = N_HEADS - H_MOBA - H_NSA
H_NSA_KV = 2
NSA_GROUP = H_NSA // H_NSA_KV
ROPE_DIM = HEAD_DIM // 4
ROPE_THETA = 500000.0
MOBA_BLOCK = 256
MOBA_TOPK = 3
NSA_CMP_LEN = 32
NSA_CMP_STRIDE = 16
NSA_CMP_HIDDEN = 128
NSA_SEL_BLOCK = 64
NSA_N_SEL = 6
NSA_WINDOW = 512
DIL_CFG = ((128, 1), (512, 4), (2048, 16))
BAND_BLOCK = 128
D_FF = 2816
EPS = 1e-6
NEG = -1e30
TINY = 1e-30
FORCE = 1e9
SCALE = HEAD_DIM ** -0.5
Q_SCALE = SCALE * float(np.log2(np.e))
QKV_A = H_MOBA * HEAD_DIM
Q_B = H_NSA * HEAD_DIM
KV_B = H_NSA_KV * HEAD_DIM
G_B = H_NSA * 3
QKV_C = H_DIL * HEAD_DIM
IN_SPLITS = (QKV_A, QKV_A, QKV_A, Q_B, KV_B, KV_B, KV_B, KV_B, KV_B, KV_B, G_B, QKV_C, QKV_C, QKV_C)

LANES = 128
TQ = 256
NSA_ORDER = (0, 3, 1, 4, 2, 5)
VMEM_LIMIT = 56 * 1024 * 1024

M_FLOOR = -1e29
V_ROWS = HEAD_DIM + 16

T_GROUP = ("qa", "qb", "va", "vsb", "vwb")
N_T_NORM, N_T = 5, 9
R_GROUP = ("ka", "ksb", "kwb", "qc", "kc", "vc", "kcb", "vcb", "gb")
N_R_NORM, N_R_BF16, N_R = 10, 13, 16
N_PBT, N_PB, N_F32 = N_T, N_R_BF16, N_R - N_R_BF16
PBT_QA, PBT_VA, PBT_VS, PBT_VW, PBT_QB = 0, 2, 4, 5, 6
PB_QC, PB_KC, PB_KA, PB_KS, PB_VC, PB_KW = 0, 3, 6, 8, 9, 12
_PBT_DST = (0, 1, 6, 7, 8, 2, 3, 4, 5)
_PB_DST = (6, 7, 8, 12, 0, 1, 2, 3, 4, 5, 9, 10, 11)
CHUNK = 4


def _cparams(sem):
    return pltpu.CompilerParams(dimension_semantics=sem, vmem_limit_bytes=VMEM_LIMIT)


def _const_spec(shape):
    return pl.BlockSpec(shape, lambda *_: (0,) * len(shape), pipeline_mode=pl.Buffered(1))


def _norm_rope(y, gain, cos_t, sin_p, sin_m, lo):
    ss = y * y
    s_lo = jnp.sum(jnp.where(lo, ss, 0.0), axis=-1, keepdims=True)
    s_hi = jnp.sum(jnp.where(lo, 0.0, ss), axis=-1, keepdims=True)
    ms = jnp.where(lo, s_lo, s_hi) * (1.0 / HEAD_DIM)
    z = (y * lax.rsqrt(ms + EPS)) * gain
    half = ROPE_DIM // 2
    return z * cos_t + pltpu.roll(z, half, 1) * sin_p + pltpu.roll(z, LANES - half, 1) * sin_m


def _online_update(m_ref, acc_ref, g, s, pv):
    m_old = m_ref[g]
    m_new = jnp.maximum(m_old, jnp.max(s, axis=0, keepdims=True))
    p = jnp.exp2(s - m_new)
    alpha = jnp.exp2(m_old - m_new)
    acc_ref[g] = alpha * acc_ref[g] + pv(p.astype(BF16))
    m_ref[g] = m_new


def _online_init(m_ref, acc_ref, g):
    m_ref[g] = jnp.full(m_ref.shape[1:], M_FLOOR, F32)
    acc_ref[g] = jnp.zeros(acc_ref.shape[1:], F32)


def _with_ones(v_t):
    row = lax.broadcasted_iota(jnp.int32, (V_ROWS - HEAD_DIM, v_t.shape[1]), 0)
    return jnp.concatenate([v_t, jnp.where(row == 0, 1.0, 0.0).astype(v_t.dtype)], axis=0)


def _normalise(acc):
    return acc[:HEAD_DIM] / jnp.maximum(acc[HEAD_DIM:HEAD_DIM + 1], TINY)


def _dot(a, b):
    return jnp.dot(a, b, preferred_element_type=F32)


def _half_rows(q_t, hh):
    z = jnp.zeros((HEAD_DIM, q_t.shape[1]), q_t.dtype)
    if hh == 0:
        return jnp.concatenate([q_t[:HEAD_DIM], z], axis=0)
    return jnp.concatenate([z, q_t[HEAD_DIM:]], axis=0)


def _split_bf16(a):
    hi = a.astype(BF16)
    return hi, (a - hi.astype(F32)).astype(BF16)


def _in_proj_plan(group):
    names = ("qa", "ka", "va", "qb", "kcb", "vcb", "ksb", "vsb", "kwb", "vwb", "gb", "qc", "kc", "vc")
    start = dict(zip(names, np.cumsum((0,) + IN_SPLITS[:-1]).tolist()))
    width = dict(zip(names, IN_SPLITS))
    plan = []
    for name in group:
        if name == "qb":
            heads = [start[name] + h * HEAD_DIM for h in NSA_ORDER]
            plan += [[(heads[2 * j], HEAD_DIM), (heads[2 * j + 1], HEAD_DIM)] for j in range(H_NSA // 2)]
        else:
            plan += [[(start[name] + c, min(LANES, width[name] - c))] for c in range(0, width[name], LANES)]
    return plan


def _wprep_kernel(w_ref, wt_ref, wr_ref, *, plan_t, plan_r, n_in):
    def take(a, width):
        b0 = a // LANES * LANES
        v0 = w_ref[0, :, b0:min(b0 + LANES, n_in)]
        if a + width <= b0 + LANES:
            return v0[:, a - b0:a - b0 + width]
        v1 = w_ref[0, :, b0 + LANES:min(b0 + 2 * LANES, n_in)]
        return jnp.concatenate([v0[:, a - b0:], v1[:, :a + width - b0 - LANES]], axis=1)

    def block(pieces):
        cols = [take(a, width) for a, width in pieces]
        filled = sum(width for _, width in pieces)
        if filled < LANES:
            cols.append(jnp.zeros((w_ref.shape[1], LANES - filled), F32))
        return cols[0] if len(cols) == 1 else jnp.concatenate(cols, axis=1)

    for d, pieces in enumerate(plan_t):
        wt_ref[d * LANES:(d + 1) * LANES, :] = block(pieces).T.astype(BF16)
    for d, pieces in enumerate(plan_r):
        wr_ref[:, d * LANES:(d + 1) * LANES] = block(pieces).astype(BF16)


def _prep_w_in(w_in, layer, tr=256):
    _, D, n_in = w_in.shape
    plan_t, plan_r = _in_proj_plan(T_GROUP), _in_proj_plan(R_GROUP)
    return pl.pallas_call(
        functools.partial(_wprep_kernel, plan_t=plan_t, plan_r=plan_r, n_in=n_in),
        grid=(D // tr,),
        in_specs=[pl.BlockSpec((1, tr, n_in), lambda i: (layer, i, 0))],
        out_specs=[pl.BlockSpec((N_T * LANES, tr), lambda i: (0, i)),
                   pl.BlockSpec((tr, N_R * LANES), lambda i: (i, 0))],
        out_shape=[jax.ShapeDtypeStruct((N_T * LANES, D), BF16), jax.ShapeDtypeStruct((D, N_R * LANES), BF16)],
        compiler_params=_cparams(("parallel",)),
        name="w_in_layout",
    )(w_in)


def _norm_rope_t(y, gain, cos8, sin8):
    half = ROPE_DIM // 2
    parts = []
    for hh in range(2):
        yh = y[hh * HEAD_DIM:(hh + 1) * HEAD_DIM]
        ms = jnp.sum(yh * yh, axis=0, keepdims=True) * (1.0 / HEAD_DIM)
        z = (yh * lax.rsqrt(ms + EPS)) * gain[hh * HEAD_DIM:(hh + 1) * HEAD_DIM]
        x1, x2 = z[:half], z[half:ROPE_DIM]
        parts += [x1 * cos8 - x2 * sin8, x2 * cos8 + x1 * sin8, z[ROPE_DIM:]]
    return jnp.concatenate(parts, axis=0)


def _inproj_kernel(x_ref, ln_ref, wt_ref, wr_ref, gt_ref, gr_ref, cos_ref, sp_ref, sm_ref, ct_ref, st_ref,
                   pbt_ref, pb_ref, pf_ref, *, tm):
    x = x_ref[0]
    h = x * lax.rsqrt(jnp.mean(x * x, axis=-1, keepdims=True) + EPS)
    h = (h * ln_ref[...]).astype(BF16)
    lo = lax.broadcasted_iota(jnp.int32, (tm, LANES), 1) < HEAD_DIM
    cos_t, sin_p, sin_m = cos_ref[...], sp_ref[...], sm_ref[...]
    for c0 in range(0, N_R, CHUNK):
        y4 = _dot(h, wr_ref[:, c0 * LANES:(c0 + CHUNK) * LANES])
        for c in range(c0, c0 + CHUNK):
            y = y4[:, (c - c0) * LANES:(c - c0 + 1) * LANES]
            if c < N_R_NORM:
                y = _norm_rope(y, gr_ref[:, c * LANES:(c + 1) * LANES], cos_t, sin_p, sin_m, lo)
            if c < N_R_BF16:
                d = _PB_DST[c]
                pb_ref[0, :, d * LANES:(d + 1) * LANES] = y.astype(BF16)
            else:
                d = c - N_R_BF16
                pf_ref[0, :, d * LANES:(d + 1) * LANES] = y
    cos8, sin8 = ct_ref[...], st_ref[...]
    for t0, t1 in ((0, N_T_NORM), (N_T_NORM, N_T)):
        y_t = lax.dot_general(wt_ref[t0 * LANES:t1 * LANES, :], h, (((1,), (1,)), ((), ())),
                              preferred_element_type=F32)
        for t in range(t0, t1):
            y = y_t[(t - t0) * LANES:(t - t0 + 1) * LANES]
            if t < N_T_NORM:
                gain = jnp.concatenate([gt_ref[t * LANES:(t + 1) * LANES, :]] * (tm // LANES), axis=1)
                y = _norm_rope_t(y, gain, cos8, sin8)
            y = y.astype(BF16)
            for u in range(tm // TQ):
                pbt_ref[0, _PBT_DST[t], u] = y[:, u * TQ:(u + 1) * TQ]


def _in_proj(x, ln, w_t, w_r, gain_t, gain_r, cos_t, sin_p, sin_m, cos8, sin8, tm=1024):
    B, S, D = x.shape
    nt = S // tm
    half = ROPE_DIM // 2
    return pl.pallas_call(
        functools.partial(_inproj_kernel, tm=tm),
        grid=(B * nt,),
        in_specs=[
            pl.BlockSpec((1, tm, D), lambda i: (i // nt, i % nt, 0)),
            _const_spec((1, D)),
            _const_spec(w_t.shape), _const_spec(w_r.shape), _const_spec(gain_t.shape), _const_spec(gain_r.shape),
            pl.BlockSpec((tm, LANES), lambda i: (i % nt, 0)),
            pl.BlockSpec((tm, LANES), lambda i: (i % nt, 0)),
            pl.BlockSpec((tm, LANES), lambda i: (i % nt, 0)),
            pl.BlockSpec((half, tm), lambda i: (0, i % nt)),
            pl.BlockSpec((half, tm), lambda i: (0, i % nt)),
        ],
        out_specs=[
            pl.BlockSpec((1, N_PBT, tm // TQ, LANES, TQ), lambda i: (i // nt, 0, i % nt, 0, 0)),
            pl.BlockSpec((1, tm, N_PB * LANES), lambda i: (i // nt, i % nt, 0)),
            pl.BlockSpec((1, tm, N_F32 * LANES), lambda i: (i // nt, i % nt, 0)),
        ],
        out_shape=[
            jax.ShapeDtypeStruct((B, N_PBT, S // TQ, LANES, TQ), BF16),
            jax.ShapeDtypeStruct((B, S, N_PB * LANES), BF16),
            jax.ShapeDtypeStruct((B, S, N_F32 * LANES), F32),
        ],
        compiler_params=_cparams(("parallel",)),
        name="in_proj",
    )(x, ln, w_t, w_r, gain_t, gain_r, cos_t, sin_p, sin_m, cos8, sin8)


MOBA_BATCH = 2

def _moba_kernel(qt_ref, k_ref, vt_ref, o_ref, q2_ref, bias_ref, m_ref, acc_ref, kmean_ref, *, n_blk):
    n_groups = qt_ref.shape[0] * (H_MOBA // 2)
    nidx = lax.broadcasted_iota(jnp.int32, (16, 2 * TQ), 0)
    kr = lax.broadcasted_iota(jnp.int32, (TQ, TQ), 0)
    ql = lax.broadcasted_iota(jnp.int32, (TQ, TQ), 1)
    causal_bias = jnp.where(kr <= ql, 0.0, NEG)
    for g in range(n_groups):
        bb, p = divmod(g, H_MOBA // 2)
        ps = slice(p * LANES, (p + 1) * LANES)
        rows = [jnp.sum(k_ref[bb, n * TQ:(n + 1) * TQ, ps].astype(F32), axis=0, keepdims=True)
                for n in range(n_blk)]
        kmean_ref[g] = (jnp.concatenate(rows + [jnp.zeros((16 - n_blk, LANES), F32)], axis=0)
                        * (1.0 / MOBA_BLOCK))
    lax.fori_loop(0, n_blk, functools.partial(_moba_tile, qt_ref, k_ref, vt_ref, o_ref, q2_ref, bias_ref, m_ref,
                                              acc_ref, kmean_ref, nidx, causal_bias, n_blk), 0)


def _moba_tile(qt_ref, k_ref, vt_ref, o_ref, q2_ref, bias_ref, m_ref, acc_ref, kmean_ref, nidx,
               causal_bias, n_blk, qi, carry):
    n_pairs = H_MOBA // 2
    groups = [divmod(g, n_pairs) for g in range(qt_ref.shape[0] * n_pairs)]
    past = nidx < qi
    for g, (bb, p) in enumerate(groups):
        km_hi, km_lo = _split_bf16(kmean_ref[g])
        q2 = jnp.concatenate([_half_rows(qt_ref[bb, p, qi], hh) for hh in range(2)], axis=1)
        q2_ref[g] = q2
        gate = _dot(km_hi, q2) + _dot(km_lo, q2)
        for n in range(n_blk):
            row = gate[n:n + 1, :]
            ge = jnp.where(gate >= row, 1.0, 0.0)
            gt = jnp.where(gate > row, 1.0, 0.0)
            beats = jnp.where(past, jnp.where(nidx < n, ge, gt), 0.0)
            cnt = jnp.sum(beats, axis=0, keepdims=True)
            bias_ref[g, n] = jnp.where(cnt < MOBA_TOPK, 0.0, NEG)
        _online_init(m_ref, acc_ref, g)

    def scores(off, g):
        bb, p = groups[g]
        return _dot(k_ref[bb, pl.ds(off, TQ), p * LANES:(p + 1) * LANES], q2_ref[g])

    def pv(g, kt):
        bb, p = groups[g]

        def fn(pr):
            return jnp.concatenate([_dot(_with_ones(vt_ref[bb, p, kt, :HEAD_DIM, :]), pr[:, :TQ]),
                                    _dot(_with_ones(vt_ref[bb, p, kt, HEAD_DIM:, :]), pr[:, TQ:])], axis=1)
        return fn

    def past_scores(kt):
        off = pl.multiple_of(kt * TQ, TQ)
        return [scores(off, g) + bias_ref[g, kt] for g in range(len(groups))]

    def past_tile(kt, s):
        for g in range(len(groups)):
            _online_update(m_ref, acc_ref, g, s[g], pv(g, kt))

    def past_pair(i, _):
        s_a, s_b = past_scores(2 * i), past_scores(2 * i + 1)
        past_tile(2 * i, s_a)
        past_tile(2 * i + 1, s_b)
        return 0

    lax.fori_loop(0, qi // 2, past_pair, 0)

    @pl.when(qi % 2 == 1)
    def _():
        past_tile(qi - 1, past_scores(qi - 1))
    off = pl.multiple_of(qi * TQ, TQ)
    causal2 = jnp.concatenate([causal_bias, causal_bias], axis=1)
    s = [scores(off, g) + causal2 for g in range(len(groups))]
    for g, (bb, p) in enumerate(groups):
        _online_update(m_ref, acc_ref, g, s[g], pv(g, qi))
        o2 = _normalise(acc_ref[g])
        o_ref[bb, pl.ds(off, TQ), p * LANES:(p + 1) * LANES] = jnp.concatenate(
            [o2[:, :TQ], o2[:, TQ:]], axis=0).T.astype(BF16)
    return carry


def _moba(pbt, pb):
    B, _, n_t, _, _ = pbt.shape
    S = pb.shape[1]
    n_pairs = H_MOBA // 2
    n_groups = MOBA_BATCH * n_pairs
    assert TQ == MOBA_BLOCK and n_t == S // MOBA_BLOCK and B % MOBA_BATCH == 0
    return pl.pallas_call(
        functools.partial(_moba_kernel, n_blk=S // MOBA_BLOCK),
        grid=(B // MOBA_BATCH,),
        in_specs=[
            pl.BlockSpec((MOBA_BATCH, n_pairs, n_t, LANES, TQ), lambda b: (b, PBT_QA // n_pairs, 0, 0, 0)),
            pl.BlockSpec((MOBA_BATCH, S, QKV_A), lambda b: (b, 0, PB_KA // n_pairs)),
            pl.BlockSpec((MOBA_BATCH, n_pairs, n_t, LANES, TQ), lambda b: (b, PBT_VA // n_pairs, 0, 0, 0)),
        ],
        out_specs=pl.BlockSpec((MOBA_BATCH, S, QKV_A), lambda b: (b, 0, 0)),
        out_shape=jax.ShapeDtypeStruct((B, S, QKV_A), BF16),
        scratch_shapes=[pltpu.VMEM((n_groups, LANES, 2 * TQ), BF16),
                        pltpu.VMEM((n_groups, S // MOBA_BLOCK, 1, 2 * TQ), F32),
                        pltpu.VMEM((n_groups, 1, 2 * TQ), F32),
                        pltpu.VMEM((n_groups, V_ROWS, 2 * TQ), F32),
                        pltpu.VMEM((n_groups, 16, LANES), F32)],
        compiler_params=_cparams(("parallel",)),
        name="moba",
    )(pbt, pb, pbt)


def _compress_kernel(kc_ref, vc_ref, pek_ref, pev_ref, w1k_ref, w2k_ref, w1v_ref, w2v_ref, gain_ref,
                     cos_ref, sp_ref, sm_ref, ko_ref, vo_ref, *, n_cmp):
    half_len = NSA_CMP_LEN // 2

    def mlp(t_ref, pe_ref, w1_ref, w2_ref):
        toks = [t_ref[0, pl.ds(l, n_cmp, stride=NSA_CMP_STRIDE), :] for l in range(half_len)]
        first = _dot(jnp.concatenate([(toks[l] + pe_ref[l:l + 1, :]).astype(BF16)
                                      for l in range(half_len)], axis=1), w1_ref[0])
        second = _dot(jnp.concatenate([(toks[l] + pe_ref[half_len + l:half_len + l + 1, :]).astype(BF16)
                                       for l in range(half_len)], axis=1), w1_ref[1])
        hid = jax.nn.gelu(first + pltpu.roll(second, n_cmp - 1, 0))
        return _dot(hid.astype(BF16), w2_ref[...])

    valid = lax.broadcasted_iota(jnp.int32, (n_cmp, LANES), 0) < n_cmp - 1
    lo = lax.broadcasted_iota(jnp.int32, (n_cmp, LANES), 1) < HEAD_DIM
    kc = mlp(kc_ref, pek_ref, w1k_ref, w2k_ref)
    kc = _norm_rope(kc, gain_ref[...], cos_ref[...], sp_ref[...], sm_ref[...], lo)
    ko_ref[0] = jnp.where(valid, kc, 0.0).astype(BF16)
    vc = mlp(vc_ref, pev_ref, w1v_ref, w2v_ref)
    vo_ref[0] = jnp.where(valid, vc, 0.0).T.astype(BF16)


def _compress(pf, pe_k, pe_v, w1k, w2k, w1v, w2v, gain, cos_c, sp_c, sm_c):
    B, S, _ = pf.shape
    n_cmp = S // NSA_CMP_STRIDE
    cs = _const_spec
    return pl.pallas_call(
        functools.partial(_compress_kernel, n_cmp=n_cmp),
        grid=(B,),
        in_specs=[
            pl.BlockSpec((1, S, LANES), lambda b: (b, 0, 0)),
            pl.BlockSpec((1, S, LANES), lambda b: (b, 0, 1)),
            cs(pe_k.shape), cs(pe_v.shape), cs(w1k.shape), cs(w2k.shape), cs(w1v.shape), cs(w2v.shape),
            cs(gain.shape), cs(cos_c.shape), cs(sp_c.shape), cs(sm_c.shape),
        ],
        out_specs=[pl.BlockSpec((1, n_cmp, LANES), lambda b: (b, 0, 0)),
                   pl.BlockSpec((1, LANES, n_cmp), lambda b: (b, 0, 0))],
        out_shape=[jax.ShapeDtypeStruct((B, n_cmp, LANES), BF16),
                   jax.ShapeDtypeStruct((B, LANES, n_cmp), BF16)],
        compiler_params=_cparams(("parallel",)),
        name="nsa_compress",
    )(pf, pf, pe_k, pe_v, w1k, w2k, w1v, w2v, gain, cos_c, sp_c, sm_c)


NSA_BATCH = 2

def _nsa_kernel(*refs, n_cmp, n_slc, n_t):
    lax.fori_loop(0, n_t, functools.partial(_nsa_tile, refs, n_cmp, n_slc), 0)


def _nsa_tile(refs, n_cmp, n_slc, qi, carry):
    (qt_ref, kc_ref, vct_ref, ks_ref, vst_ref, kw_ref, vwt_ref, g_ref, ovl_ref, o_ref,
     q6_ref, impf_ref, improw_ref, m_ref, acc_ref, ot_ref) = refs
    n_b = qt_ref.shape[0]
    blk_per_tile = TQ // NSA_SEL_BLOCK
    off = pl.multiple_of(qi * TQ, TQ)
    tpos = qi * TQ + lax.broadcasted_iota(jnp.int32, (1, TQ), 1)
    sigs = [jax.nn.sigmoid(g_ref[bb, pl.ds(off, TQ), :].T) for bb in range(n_b)]
    kr = lax.broadcasted_iota(jnp.int32, (TQ, TQ), 0)
    ql = lax.broadcasted_iota(jnp.int32, (TQ, TQ), 1)
    causal_bias = jnp.where(kr <= ql, 0.0, NEG)
    nidx = lax.broadcasted_iota(jnp.int32, (n_cmp, TQ), 0)
    cmask = (nidx * NSA_CMP_STRIDE + (NSA_CMP_LEN - 1) <= tpos) & (nidx < n_cmp - 1)
    cmp_bias = jnp.where(cmask, 0.0, NEG)
    jj = lax.broadcasted_iota(jnp.int32, (n_slc, TQ), 0)
    cur = lax.shift_right_arithmetic(tpos, NSA_SEL_BLOCK.bit_length() - 1)
    forced = (jj == 0) | (jj == cur) | (jj == cur - 1)
    gw = NSA_GROUP * TQ

    def half(kk):
        return slice(kk * HEAD_DIM, (kk + 1) * HEAD_DIM)

    def cols(kk):
        return slice(kk * gw, (kk + 1) * gw)

    def tile3(a):
        return jnp.concatenate([a] * NSA_GROUP, axis=1)

    def gate3(bb, kk, branch):
        sig = sigs[bb]
        return jnp.concatenate([sig[3 * (NSA_GROUP * kk + j) + branch:3 * (NSA_GROUP * kk + j) + branch + 1, :]
                                for j in range(NSA_GROUP)], axis=1)

    slots = [(bb, kk) for bb in range(n_b) for kk in range(H_NSA_KV)]

    for bb in range(n_b):
        q6_ref[bb] = jnp.concatenate([_half_rows(qt_ref[bb, j, qi], kk)
                                      for kk in range(H_NSA_KV) for j in range(NSA_GROUP)], axis=1)

    s_cmp = [_dot(kc_ref[bb], q6_ref[bb]) for bb in range(n_b)]
    cmp_bias3 = tile3(cmp_bias)
    for sl, (bb, kk) in enumerate(slots):
        s = s_cmp[bb][:, cols(kk)] + cmp_bias3
        m = jnp.maximum(jnp.max(s, axis=0, keepdims=True), M_FLOOR)
        p = jnp.exp2(s - m)
        p = p / jnp.maximum(jnp.sum(p, axis=0, keepdims=True), TINY)
        ot_ref[sl] = gate3(bb, kk, 0) * _dot(vct_ref[bb, half(kk), :], p.astype(BF16))
        p_sum = (p[:, :TQ] + p[:, TQ:2 * TQ]) + p[:, 2 * TQ:]
        _online_init(m_ref, acc_ref, sl)
        p_hi, p_lo = _split_bf16(p_sum)
        imp = _dot(ovl_ref[...], p_hi) + _dot(ovl_ref[...], p_lo)
        imp = jnp.where(jj > cur, NEG, jnp.where(forced, FORCE, imp))
        impf_ref[sl] = imp
        for j in range(n_slc):
            improw_ref[sl, j] = imp[j:j + 1, :]

    def sel_bias(sl, kt):
        imp = impf_ref[sl]
        parts = []
        for c in range(blk_per_tile):
            j = kt * blk_per_tile + c
            row = improw_ref[sl, j]
            ge = jnp.where(imp >= row, 1.0, 0.0)
            gt = jnp.where(imp > row, 1.0, 0.0)
            cnt = jnp.sum(jnp.where(jj < j, ge, gt), axis=0, keepdims=True)
            parts.append(jnp.broadcast_to(jnp.where(cnt < NSA_N_SEL, 0.0, NEG), (NSA_SEL_BLOCK, TQ)))
        return jnp.concatenate(parts, axis=0)

    def sel_pv(bb, kk, kt):
        return lambda pr: _dot(_with_ones(vst_ref[bb, 0, kt, half(kk), :]), pr)

    def sel_scores(kt):
        off = pl.multiple_of(kt * TQ, TQ)
        return [_dot(ks_ref[bb, pl.ds(off, TQ), :], q6_ref[bb]) for bb in range(n_b)]

    def sel_tile(kt, s6):
        for sl, (bb, kk) in enumerate(slots):
            s = s6[bb][:, cols(kk)] + tile3(sel_bias(sl, kt))
            _online_update(m_ref, acc_ref, sl, s, sel_pv(bb, kk, kt))

    def sel_pair(i, _):
        s_a, s_b = sel_scores(2 * i), sel_scores(2 * i + 1)
        sel_tile(2 * i, s_a)
        sel_tile(2 * i + 1, s_b)
        return 0

    lax.fori_loop(0, qi // 2, sel_pair, 0)

    @pl.when(qi % 2 == 1)
    def _():
        sel_tile(qi - 1, sel_scores(qi - 1))

    kt1 = jnp.maximum(qi - 1, 0)
    kt2 = jnp.maximum(qi - 2, 0)
    off1 = pl.multiple_of(kt1 * TQ, TQ)
    off2 = pl.multiple_of(kt2 * TQ, TQ)
    win_bias2 = jnp.where(kr > ql + jnp.where(qi >= 2, 0, TQ), 0.0, NEG)
    win_bias1 = jnp.where(qi >= 1, 0.0, NEG)
    sc = [[_dot(k_ref[bb, pl.ds(o, TQ), :], q6_ref[bb])
           for k_ref, o in ((ks_ref, off), (kw_ref, off2), (kw_ref, off1), (kw_ref, off))] for bb in range(n_b)]
    causal3 = tile3(causal_bias)
    win_bias2 = tile3(win_bias2)
    for sl, (bb, kk) in enumerate(slots):
        sd6, w2, w1, w0 = sc[bb]
        s = sd6[:, cols(kk)] + tile3(sel_bias(sl, qi) + causal_bias)
        _online_update(m_ref, acc_ref, sl, s, sel_pv(bb, kk, qi))
        o_sel = _normalise(acc_ref[sl])
        s2 = w2[:, cols(kk)] + win_bias2
        s1 = w1[:, cols(kk)] + win_bias1
        s0 = w0[:, cols(kk)] + causal3
        m = jnp.maximum(jnp.maximum(jnp.max(s2, axis=0, keepdims=True), jnp.max(s1, axis=0, keepdims=True)),
                        jnp.max(s0, axis=0, keepdims=True))
        p2, p1, p0 = jnp.exp2(s2 - m), jnp.exp2(s1 - m), jnp.exp2(s0 - m)
        acc = (_dot(_with_ones(vwt_ref[bb, 0, kt2, half(kk), :]), p2.astype(BF16))
               + _dot(_with_ones(vwt_ref[bb, 0, kt1, half(kk), :]), p1.astype(BF16))
               + _dot(_with_ones(vwt_ref[bb, 0, qi, half(kk), :]), p0.astype(BF16)))
        o_win = _normalise(acc)
        ot_ref[sl] = (ot_ref[sl] + gate3(bb, kk, 1) * o_sel) + gate3(bb, kk, 2) * o_win
    for bb in range(n_b):
        for j in range(NSA_GROUP):
            js = slice(j * TQ, (j + 1) * TQ)
            o_ref[bb, pl.ds(off, TQ), j * LANES:(j + 1) * LANES] = jnp.concatenate(
                [ot_ref[2 * bb, :, js], ot_ref[2 * bb + 1, :, js]], axis=0).T.astype(BF16)
    return carry


def _nsa(pbt, pb, pf, kcmp, vcmp_t, ovl_t):
    B, _, n_t, _, _ = pbt.shape
    S = pb.shape[1]
    n_cmp = kcmp.shape[1]
    n_slc = S // NSA_SEL_BLOCK
    gw = NSA_GROUP * TQ
    assert NSA_WINDOW == 2 * TQ and TQ % NSA_SEL_BLOCK == 0 and S % TQ == 0 and B % NSA_BATCH == 0
    nb = NSA_BATCH
    n_slots = nb * H_NSA_KV
    return pl.pallas_call(
        functools.partial(_nsa_kernel, n_cmp=n_cmp, n_slc=n_slc, n_t=n_t),
        grid=(B // nb,),
        in_specs=[
            pl.BlockSpec((nb, NSA_GROUP, n_t, LANES, TQ), lambda b: (b, PBT_QB // NSA_GROUP, 0, 0, 0)),
            pl.BlockSpec((nb, n_cmp, LANES), lambda b: (b, 0, 0)),
            pl.BlockSpec((nb, LANES, n_cmp), lambda b: (b, 0, 0)),
            pl.BlockSpec((nb, S, LANES), lambda b: (b, 0, PB_KS)),
            pl.BlockSpec((nb, 1, n_t, LANES, TQ), lambda b: (b, PBT_VS, 0, 0, 0)),
            pl.BlockSpec((nb, S, LANES), lambda b: (b, 0, PB_KW)),
            pl.BlockSpec((nb, 1, n_t, LANES, TQ), lambda b: (b, PBT_VW, 0, 0, 0)),
            pl.BlockSpec((nb, S, LANES), lambda b: (b, 0, 2)),
            _const_spec(ovl_t.shape),
        ],
        out_specs=pl.BlockSpec((nb, S, Q_B), lambda b: (b, 0, 0)),
        out_shape=jax.ShapeDtypeStruct((B, S, Q_B), BF16),
        scratch_shapes=[pltpu.VMEM((nb, LANES, H_NSA * TQ), BF16),
                        pltpu.VMEM((n_slots, n_slc, TQ), F32), pltpu.VMEM((n_slots, n_slc, 1, TQ), F32),
                        pltpu.VMEM((n_slots, 1, gw), F32),
                        pltpu.VMEM((n_slots, V_ROWS, gw), F32),
                        pltpu.VMEM((n_slots, HEAD_DIM, gw), F32)],
        compiler_params=_cparams(("parallel",)),
        name="nsa_attn",
    )(pbt, kcmp, vcmp_t, pb, pbt, pb, pbt, pf, ovl_t)


DIL_TILES = 16

def _dilated_kernel(q_ref, k_ref, v_ref, o_ref, qf_ref, kf_ref, vf_ref, os_ref, ls_ref, *, seq):
    blk = BAND_BLOCK
    n_tiles = seq // blk
    lo = lax.broadcasted_iota(jnp.int32, (blk, LANES), 1) < HEAD_DIM
    for g, (window, dil) in enumerate(DIL_CFG):
        max_dist = window // dil
        tiles_per_class = n_tiles // dil
        use_prev = tiles_per_class > 1
        n_keys = 2 * blk if use_prev else blk
        ri = lax.broadcasted_iota(jnp.int32, (2 * blk, n_keys), 0) & (blk - 1)
        ci = lax.broadcasted_iota(jnp.int32, (2 * blk, n_keys), 1)
        dist = ri + (n_keys - blk) - ci
        band = (dist >= 0) & (dist <= max_dist)
        band_bias = jnp.where(band, 0.0, NEG)
        qf_ref[g] = q_ref[0, :, g * LANES:(g + 1) * LANES].astype(F32)
        kf_ref[g] = k_ref[0, :, g * LANES:(g + 1) * LANES].astype(F32)
        vf_ref[g] = v_ref[0, :, g * LANES:(g + 1) * LANES].astype(F32)

        def index(start, dil=dil):
            if dil == 1:
                return pl.ds(pl.multiple_of(start, blk), blk)
            return pl.ds(start, blk, stride=dil)

        def tiles(i, _, g=g, dil=dil, tiles_per_class=tiles_per_class, use_prev=use_prev, band=band,
                  band_bias=band_bias, ci=ci, index=index):
            scored = []
            for u in range(DIL_TILES):
                t = i * DIL_TILES + u
                cls = t // tiles_per_class
                mt = t % tiles_per_class
                start = cls + dil * blk * mt
                q = qf_ref[g, index(start), :]
                q2 = jnp.concatenate([jnp.where(lo, q, 0.0), jnp.where(lo, 0.0, q)], axis=0).astype(BF16)
                if use_prev:
                    prev = jnp.maximum(start - dil * blk, cls)
                    k2 = jnp.concatenate([kf_ref[g, index(prev), :], kf_ref[g, index(start), :]], axis=0)
                    v2 = jnp.concatenate([vf_ref[g, index(prev), :], vf_ref[g, index(start), :]], axis=0)
                    bias = jnp.where(band & (ci >= jnp.where(mt > 0, 0, blk)), 0.0, NEG)
                else:
                    k2, v2, bias = kf_ref[g, index(start), :], vf_ref[g, index(start), :], band_bias
                s = lax.dot_general(q2, k2.astype(BF16), (((1,), (1,)), ((), ())), preferred_element_type=F32)
                v2 = jnp.concatenate([v2.astype(BF16), jnp.ones((v2.shape[0], LANES), BF16)], axis=1)
                scored.append((start, s + bias, v2))
            probs = []
            for start, s, v2 in scored:
                m = jnp.max(s, axis=-1, keepdims=True)
                probs.append((start, m, jnp.exp2(s - m).astype(BF16), v2))
            for start, m, p, v2 in probs:
                o2 = _dot(p, v2)
                den = jnp.maximum(o2[:, LANES:], TINY)
                o = o2[:, :LANES] / den
                lse = m + jnp.log2(den)
                os_ref[g, index(start), :] = jnp.where(lo, o[:blk], o[blk:])
                ls_ref[g, index(start), :] = jnp.where(lo, lse[:blk], lse[blk:])
            return 0

        lax.fori_loop(0, n_tiles // DIL_TILES, tiles, 0)

    rows_per_trip = DIL_TILES * blk

    def combine(t, _):
        idx = pl.ds(pl.multiple_of(t * rows_per_trip, rows_per_trip), rows_per_trip)
        lses = [ls_ref[g, idx, :] for g in range(len(DIL_CFG))]
        mx = jnp.maximum(jnp.maximum(lses[0], lses[1]), lses[2])
        es = [jnp.exp2(l - mx) for l in lses]
        tot = es[0] + es[1] + es[2]
        for g in range(len(DIL_CFG)):
            o_ref[0, idx, g * LANES:(g + 1) * LANES] = (os_ref[g, idx, :] * (es[g] / tot)).astype(BF16)
        return 0

    lax.fori_loop(0, seq // rows_per_trip, combine, 0)


def _dilated(pb):
    B, S, _ = pb.shape
    n_g = len(DIL_CFG)
    return pl.pallas_call(
        functools.partial(_dilated_kernel, seq=S),
        grid=(B,),
        in_specs=[pl.BlockSpec((1, S, QKV_C), lambda b: (b, 0, PB_QC // n_g)),
                  pl.BlockSpec((1, S, QKV_C), lambda b: (b, 0, PB_KC // n_g)),
                  pl.BlockSpec((1, S, QKV_C), lambda b: (b, 0, PB_VC // n_g))],
        out_specs=pl.BlockSpec((1, S, QKV_C), lambda b: (b, 0, 0)),
        out_shape=jax.ShapeDtypeStruct((B, S, QKV_C), BF16),
        scratch_shapes=[pltpu.VMEM((n_g, S, LANES), F32)] * 5,
        compiler_params=_cparams(("parallel",)),
        name="dilated_attn",
    )(pb, pb, pb)


HALO = 16
FF_CHUNK = 256


def _proj_ffn_kernel(x_ref, xh_ref, oa_ref, oah_ref, ob_ref, obh_ref, oc_ref, och_ref, wo_ref, ln_ref,
                     wg_ref, wu_ref, cw_ref, cb_ref, wd_ref, y_ref, o_ref, h_ref, x1_ref, a_ref,
                     *, tiles_per_seq):
    i = pl.program_id(0)

    def norm(v):
        return (v * lax.rsqrt(jnp.mean(v * v, axis=-1, keepdims=True) + EPS)) * ln_ref[...]

    col = 0
    for main, halo in ((oa_ref, oah_ref), (ob_ref, obh_ref), (oc_ref, och_ref)):
        width = main.shape[1]
        o_ref[:HALO, col:col + width] = halo[...]
        o_ref[HALO:, col:col + width] = main[...]
        col += width
    attn = _dot(o_ref[...], wo_ref[...])
    x1 = x_ref[...] + attn[HALO:]
    x1_ref[...] = x1
    first = (i % tiles_per_seq) == 0
    h_ref[:HALO, :] = jnp.where(first, 0.0, norm(xh_ref[...] + attn[:HALO])).astype(BF16)
    h_ref[HALO:, :] = norm(x1).astype(BF16)

    for f0 in range(0, D_FF, FF_CHUNK):
        fs = slice(f0, min(f0 + FF_CHUNK, D_FF))
        g = _dot(h_ref[...], wg_ref[:, fs])
        u = _dot(h_ref[HALO:, :], wu_ref[:, fs])
        gc = (cw_ref[0:1, fs] * pltpu.roll(g, 2, 0)[HALO:] + cw_ref[1:2, fs] * pltpu.roll(g, 1, 0)[HALO:]
              + cw_ref[2:3, fs] * g[HALO:] + cb_ref[:, fs])
        a_ref[:, fs] = (jax.nn.silu(gc) * u).astype(BF16)
    y_ref[...] = x1_ref[...] + _dot(a_ref[...], wd_ref[...])


def _proj_ffn(x2, o_a, o_b, o_c, wo, ln, wg, wu, cw, cb, wd, seq, tm=512):
    M, D = x2.shape
    halo_blocks = tm // HALO
    main = lambda w: pl.BlockSpec((tm, w), lambda i: (i, 0))
    halo = lambda w: pl.BlockSpec((HALO, w), lambda i: (jnp.maximum(i * halo_blocks - 1, 0), 0))
    return pl.pallas_call(
        functools.partial(_proj_ffn_kernel, tiles_per_seq=seq // tm),
        grid=(M // tm,),
        in_specs=[
            main(D), halo(D), main(QKV_A), halo(QKV_A), main(Q_B), halo(Q_B), main(QKV_C), halo(QKV_C),
            _const_spec(wo.shape), _const_spec((1, D)),
            _const_spec(wg.shape), _const_spec(wu.shape), _const_spec(cw.shape), _const_spec(cb.shape),
            _const_spec(wd.shape),
        ],
        out_specs=main(D),
        out_shape=jax.ShapeDtypeStruct((M, D), F32),
        scratch_shapes=[pltpu.VMEM((HALO + tm, D), BF16), pltpu.VMEM((HALO + tm, D), BF16),
                        pltpu.VMEM((tm, D), F32), pltpu.VMEM((tm, D_FF), BF16)],
        compiler_params=_cparams(("parallel",)),
        name="proj_ffn",
    )(x2, x2, o_a, o_a, o_b, o_b, o_c, o_c, wo, ln, wg, wu, cw, cb, wd)


def _rope_tables(pos):
    half = ROPE_DIM // 2
    inv_freq = ROPE_THETA ** (-jnp.arange(half, dtype=F32) * 2.0 / ROPE_DIM)
    ang = pos.astype(F32)[:, None] * inv_freq
    c, s = jnp.cos(ang), jnp.sin(ang)
    n = pos.shape[0]
    rest = HEAD_DIM - ROPE_DIM
    cos_h = jnp.concatenate([c, c, jnp.ones((n, rest), F32)], axis=-1)
    sp_h = jnp.concatenate([jnp.zeros((n, half), F32), s, jnp.zeros((n, rest), F32)], axis=-1)
    sm_h = jnp.concatenate([-s, jnp.zeros((n, half + rest), F32)], axis=-1)
    two = lambda t: jnp.concatenate([t, t], axis=-1)
    return two(cos_h), two(sp_h), two(sm_h), c.T, s.T


def _pair_diag(w):
    z = jnp.zeros_like(w)
    return jnp.concatenate([jnp.concatenate([w, z], axis=-1), jnp.concatenate([z, w], axis=-1)], axis=-2)


def _overlap_t(seq):
    n_cmp = seq // NSA_CMP_STRIDE
    n_slc = seq // NSA_SEL_BLOCK
    starts = np.arange(n_cmp) * NSA_CMP_STRIDE
    j = np.arange(n_slc)
    ovl = (starts[None, :] < (j[:, None] + 1) * NSA_SEL_BLOCK) & (starts[None, :] + NSA_CMP_LEN > j[:, None] * NSA_SEL_BLOCK)
    ovl[:, n_cmp - 1] = False
    return jnp.asarray(ovl, BF16)


def _mixers(x, seq_tabs, cmp_tabs, ovl_t, ln1, w, qn_a, kn_a, qn_b, kn_b, pe_k, pe_v, wk1, wk2, wv1, wv2,
            qn_c, kn_c):
    rep = lambda g, n: jnp.tile(g, n)
    w_t, w_r = w
    gain_t = jnp.concatenate([rep(qn_a * Q_SCALE, H_MOBA), rep(qn_b * Q_SCALE, H_NSA)])
    gain_t = jnp.broadcast_to(gain_t[:, None], (gain_t.shape[0], LANES))
    gain_r = jnp.concatenate([rep(kn_a, H_MOBA), rep(kn_b[1], H_NSA_KV), rep(kn_b[2], H_NSA_KV),
                              rep(qn_c * Q_SCALE, H_DIL), rep(kn_c, H_DIL)])[None, :]
    pbt, pb, pf = _in_proj(x, ln1[None, :], w_t, w_r, gain_t, gain_r, *seq_tabs)

    o_a = _moba(pbt, pb)
    two = lambda t: jnp.concatenate([t, t], axis=-1)
    w1 = lambda t: _pair_diag(t.reshape(NSA_CMP_LEN, HEAD_DIM, NSA_CMP_HIDDEN)).astype(BF16).reshape(
        2, NSA_CMP_LEN // 2 * LANES, 2 * NSA_CMP_HIDDEN)
    kcmp, vcmp_t = _compress(pf, two(pe_k), two(pe_v), w1(wk1), _pair_diag(wk2).astype(BF16),
                             w1(wv1), _pair_diag(wv2).astype(BF16), two(kn_b[0])[None, :], *cmp_tabs[:3])
    o_b = _nsa(pbt, pb, pf, kcmp, vcmp_t, ovl_t)
    o_c = _dilated(pb)
    return o_a, o_b, o_c


def _layer(x, seq_tabs, cmp_tabs, ovl_t, ln1, w_in, qn_a, kn_a, qn_b, kn_b, pe_k, pe_v, wk1, wk2, wv1, wv2,
           qn_c, kn_c, w_out, ln2, w_gate, w_up, conv_w, conv_b, w_down):
    B, S, D = x.shape
    o_a, o_b, o_c = _mixers(x, seq_tabs, cmp_tabs, ovl_t, ln1, w_in, qn_a, kn_a, qn_b, kn_b, pe_k, pe_v,
                            wk1, wk2, wv1, wv2, qn_c, kn_c)
    nsa_rows = [w_out[QKV_A + h * HEAD_DIM:QKV_A + (h + 1) * HEAD_DIM] for h in NSA_ORDER]
    w_o = jnp.concatenate([w_out[:QKV_A]] + nsa_rows + [w_out[QKV_A + Q_B:]], axis=0).astype(BF16)
    M = B * S
    x2 = _proj_ffn(x.reshape(M, D), o_a.reshape(M, QKV_A), o_b.reshape(M, Q_B), o_c.reshape(M, QKV_C), w_o,
                   ln2[None, :], w_gate.astype(BF16), w_up.astype(BF16), conv_w, conv_b[None, :],
                   w_down.astype(BF16), S)
    return x2.reshape(B, S, D)


def kernel(x, ln1, w_in, qn_a, kn_a, qn_b, kn_b, cmp_pe_k, cmp_pe_v, cmp_k_w1, cmp_k_w2, cmp_v_w1, cmp_v_w2,
           qn_c, kn_c, w_out, ln2, w_gate, w_up, conv_w, conv_b, w_down):
    S = x.shape[1]
    seq_tabs = _rope_tables(jnp.arange(S))
    n_cmp = S // NSA_CMP_STRIDE
    cmp_tabs = _rope_tables(jnp.arange(n_cmp) * NSA_CMP_STRIDE + (NSA_CMP_LEN - 1))
    ovl_t = _overlap_t(S)
    for l in range(ln1.shape[0]):
        x = _layer(x, seq_tabs, cmp_tabs, ovl_t, ln1[l], _prep_w_in(w_in, l), qn_a[l], kn_a[l], qn_b[l], kn_b[l],
                   cmp_pe_k[l], cmp_pe_v[l], cmp_k_w1[l], cmp_k_w2[l], cmp_v_w1[l], cmp_v_w2[l], qn_c[l], kn_c[l],
                   w_out[l], ln2[l], w_gate[l], w_up[l], conv_w[l], conv_b[l], w_down[l])
    return x
```

```python
import functools

import numpy as np
import jax
import jax.numpy as jnp
from jax import lax
from jax.experimental import pallas as pl
from jax.experimental.pallas import tpu as pltpu

F32 = jnp.float32
BF16 = jnp.bfloat16

D_MODEL = 1024
HEAD_DIM = 64
N_HEADS = D_MODEL // HEAD_DIM
H_MOBA = N_HEADS // 4
H_NSA = (N_HEADS - H_MOBA) // 2
H_DIL = N_HEADS - H_MOBA - H_NSA
H_NSA_KV = 2
NSA_GROUP = H_NSA // H_NSA_KV
ROPE_DIM = HEAD_DIM // 4
ROPE_THETA = 500000.0
MOBA_BLOCK = 256
MOBA_TOPK = 3
NSA_CMP_LEN = 32
NSA_CMP_STRIDE = 16
NSA_CMP_HIDDEN = 128
NSA_SEL_BLOCK = 64
NSA_N_SEL = 6
NSA_WINDOW = 512
DIL_CFG = ((128, 1), (512, 4), (2048, 16))
BAND_BLOCK = 128
D_FF = 2816
EPS = 1e-6
NEG = -1e30
TINY = 1e-30
FORCE = 1e9
SCALE = HEAD_DIM ** -0.5
Q_SCALE = SCALE * float(np.log2(np.e))
QKV_A = H_MOBA * HEAD_DIM
Q_B = H_NSA * HEAD_DIM
KV_B = H_NSA_KV * HEAD_DIM
G_B = H_NSA * 3
QKV_C = H_DIL * HEAD_DIM
IN_SPLITS = (QKV_A, QKV_A, QKV_A, Q_B, KV_B, KV_B, KV_B, KV_B, KV_B, KV_B, G_B, QKV_C, QKV_C, QKV_C)

LANES = 128
TQ = 256
NSA_ORDER = (0, 3, 1, 4, 2, 5)
VMEM_LIMIT = 56 * 1024 * 1024

M_FLOOR = -1e29
V_ROWS = HEAD_DIM + 16

T_GROUP = ("qa", "qb", "va", "vsb", "vwb")
N_T_NORM, N_T = 5, 9
R_GROUP = ("ka", "ksb", "kwb", "qc", "kc", "vc", "kcb", "vcb", "gb")
N_R_NORM, N_R_BF16, N_R = 10, 13, 16
N_PBT, N_PB, N_F32 = N_T, N_R_BF16, N_R - N_R_BF16
PBT_QA, PBT_VA, PBT_VS, PBT_VW, PBT_QB = 0, 2, 4, 5, 6
PB_QC, PB_KC, PB_KA, PB_KS, PB_VC, PB_KW = 0, 3, 6, 8, 9, 12
_PBT_DST = (0, 1, 6, 7, 8, 2, 3, 4, 5)
_PB_DST = (6, 7, 8, 12, 0, 1, 2, 3, 4, 5, 9, 10, 11)
CHUNK = 4


def _cparams(sem):
    return pltpu.CompilerParams(dimension_semantics=sem, vmem_limit_bytes=VMEM_LIMIT)


def _const_spec(shape):
    return pl.BlockSpec(shape, lambda *_: (0,) * len(shape), pipeline_mode=pl.Buffered(1))


def _norm_rope(y, gain, cos_t, sin_p, sin_m, lo):
    ss = y * y
    s_lo = jnp.sum(jnp.where(lo, ss, 0.0), axis=-1, keepdims=True)
    s_hi = jnp.sum(jnp.where(lo, 0.0, ss), axis=-1, keepdims=True)
    ms = jnp.where(lo, s_lo, s_hi) * (1.0 / HEAD_DIM)
    z = (y * lax.rsqrt(ms + EPS)) * gain
    half = ROPE_DIM // 2
    return z * cos_t + pltpu.roll(z, half, 1) * sin_p + pltpu.roll(z, LANES - half, 1) * sin_m


def _online_update(m_ref, acc_ref, g, s, pv):
    m_old = m_ref[g]
    m_new = jnp.maximum(m_old, jnp.max(s, axis=0, keepdims=True))
    p = jnp.exp2(s - m_new)
    alpha = jnp.exp2(m_old - m_new)
    acc_ref[g] = alpha * acc_ref[g] + pv(p.astype(BF16))
    m_ref[g] = m_new


def _online_init(m_ref, acc_ref, g):
    m_ref[g] = jnp.full(m_ref.shape[1:], M_FLOOR, F32)
    acc_ref[g] = jnp.zeros(acc_ref.shape[1:], F32)


def _with_ones(v_t):
    row = lax.broadcasted_iota(jnp.int32, (V_ROWS - HEAD_DIM, v_t.shape[1]), 0)
    return jnp.concatenate([v_t, jnp.where(row == 0, 1.0, 0.0).astype(v_t.dtype)], axis=0)


def _normalise(acc):
    return acc[:HEAD_DIM] / jnp.maximum(acc[HEAD_DIM:HEAD_DIM + 1], TINY)


def _dot(a, b):
    return jnp.dot(a, b, preferred_element_type=F32)


def _half_rows(q_t, hh):
    z = jnp.zeros((HEAD_DIM, q_t.shape[1]), q_t.dtype)
    if hh == 0:
        return jnp.concatenate([q_t[:HEAD_DIM], z], axis=0)
    return jnp.concatenate([z, q_t[HEAD_DIM:]], axis=0)


def _split_bf16(a):
    hi = a.astype(BF16)
    return hi, (a - hi.astype(F32)).astype(BF16)


def _in_proj_plan(group):
    names = ("qa", "ka", "va", "qb", "kcb", "vcb", "ksb", "vsb", "kwb", "vwb", "gb", "qc", "kc", "vc")
    start = dict(zip(names, np.cumsum((0,) + IN_SPLITS[:-1]).tolist()))
    width = dict(zip(names, IN_SPLITS))
    plan = []
    for name in group:
        if name == "qb":
            heads = [start[name] + h * HEAD_DIM for h in NSA_ORDER]
            plan += [[(heads[2 * j], HEAD_DIM), (heads[2 * j + 1], HEAD_DIM)] for j in range(H_NSA // 2)]
        else:
            plan += [[(start[name] + c, min(LANES, width[name] - c))] for c in range(0, width[name], LANES)]
    return plan


def _wprep_kernel(w_ref, wt_ref, wr_ref, *, plan_t, plan_r, n_in):
    def take(a, width):
        b0 = a // LANES * LANES
        v0 = w_ref[0, :, b0:min(b0 + LANES, n_in)]
        if a + width <= b0 + LANES:
            return v0[:, a - b0:a - b0 + width]
        v1 = w_ref[0, :, b0 + LANES:min(b0 + 2 * LANES, n_in)]
        return jnp.concatenate([v0[:, a - b0:], v1[:, :a + width - b0 - LANES]], axis=1)

    def block(pieces):
        cols = [take(a, width) for a, width in pieces]
        filled = sum(width for _, width in pieces)
        if filled < LANES:
            cols.append(jnp.zeros((w_ref.shape[1], LANES - filled), F32))
        return cols[0] if len(cols) == 1 else jnp.concatenate(cols, axis=1)

    for d, pieces in enumerate(plan_t):
        wt_ref[d * LANES:(d + 1) * LANES, :] = block(pieces).T.astype(BF16)
    for d, pieces in enumerate(plan_r):
        wr_ref[:, d * LANES:(d + 1) * LANES] = block(pieces).astype(BF16)


def _prep_w_in(w_in, layer, tr=256):
    _, D, n_in = w_in.shape
    plan_t, plan_r = _in_proj_plan(T_GROUP), _in_proj_plan(R_GROUP)
    return pl.pallas_call(
        functools.partial(_wprep_kernel, plan_t=plan_t, plan_r=plan_r, n_in=n_in),
        grid=(D // tr,),
        in_specs=[pl.BlockSpec((1, tr, n_in), lambda i: (layer, i, 0))],
        out_specs=[pl.BlockSpec((N_T * LANES, tr), lambda i: (0, i)),
                   pl.BlockSpec((tr, N_R * LANES), lambda i: (i, 0))],
        out_shape=[jax.ShapeDtypeStruct((N_T * LANES, D), BF16), jax.ShapeDtypeStruct((D, N_R * LANES), BF16)],
        compiler_params=_cparams(("parallel",)),
        name="w_in_layout",
    )(w_in)


def _norm_rope_t(y, gain, cos8, sin8):
    half = ROPE_DIM // 2
    parts = []
    for hh in range(2):
        yh = y[hh * HEAD_DIM:(hh + 1) * HEAD_DIM]
        ms = jnp.sum(yh * yh, axis=0, keepdims=True) * (1.0 / HEAD_DIM)
        z = (yh * lax.rsqrt(ms + EPS)) * gain[hh * HEAD_DIM:(hh + 1) * HEAD_DIM]
        x1, x2 = z[:half], z[half:ROPE_DIM]
        parts += [x1 * cos8 - x2 * sin8, x2 * cos8 + x1 * sin8, z[ROPE_DIM:]]
    return jnp.concatenate(parts, axis=0)


def _inproj_kernel(x_ref, ln_ref, wt_ref, wr_ref, gt_ref, gr_ref, cos_ref, sp_ref, sm_ref, ct_ref, st_ref,
                   pbt_ref, pb_ref, pf_ref, *, tm):
    x = x_ref[0]
    h = x * lax.rsqrt(jnp.mean(x * x, axis=-1, keepdims=True) + EPS)
    h = (h * ln_ref[...]).astype(BF16)
    lo = lax.broadcasted_iota(jnp.int32, (tm, LANES), 1) < HEAD_DIM
    cos_t, sin_p, sin_m = cos_ref[...], sp_ref[...], sm_ref[...]
    for c0 in range(0, N_R, CHUNK):
        y4 = _dot(h, wr_ref[:, c0 * LANES:(c0 + CHUNK) * LANES])
        for c in range(c0, c0 + CHUNK):
            y = y4[:, (c - c0) * LANES:(c - c0 + 1) * LANES]
            if c < N_R_NORM:
                y = _norm_rope(y, gr_ref[:, c * LANES:(c + 1) * LANES], cos_t, sin_p, sin_m, lo)
            if c < N_R_BF16:
                d = _PB_DST[c]
                pb_ref[0, :, d * LANES:(d + 1) * LANES] = y.astype(BF16)
            else:
                d = c - N_R_BF16
                pf_ref[0, :, d * LANES:(d + 1) * LANES] = y
    cos8, sin8 = ct_ref[...], st_ref[...]
    for t0, t1 in ((0, N_T_NORM), (N_T_NORM, N_T)):
        y_t = lax.dot_general(wt_ref[t0 * LANES:t1 * LANES, :], h, (((1,), (1,)), ((), ())),
                              preferred_element_type=F32)
        for t in range(t0, t1):
            y = y_t[(t - t0) * LANES:(t - t0 + 1) * LANES]
            if t < N_T_NORM:
                gain = jnp.concatenate([gt_ref[t * LANES:(t + 1) * LANES, :]] * (tm // LANES), axis=1)
                y = _norm_rope_t(y, gain, cos8, sin8)
            y = y.astype(BF16)
            for u in range(tm // TQ):
                pbt_ref[0, _PBT_DST[t], u] = y[:, u * TQ:(u + 1) * TQ]


def _in_proj(x, ln, w_t, w_r, gain_t, gain_r, cos_t, sin_p, sin_m, cos8, sin8, tm=1024):
    B, S, D = x.shape
    nt = S // tm
    half = ROPE_DIM // 2
    return pl.pallas_call(
        functools.partial(_inproj_kernel, tm=tm),
        grid=(B * nt,),
        in_specs=[
            pl.BlockSpec((1, tm, D), lambda i: (i // nt, i % nt, 0)),
            _const_spec((1, D)),
            _const_spec(w_t.shape), _const_spec(w_r.shape), _const_spec(gain_t.shape), _const_spec(gain_r.shape),
            pl.BlockSpec((tm, LANES), lambda i: (i % nt, 0)),
            pl.BlockSpec((tm, LANES), lambda i: (i % nt, 0)),
            pl.BlockSpec((tm, LANES), lambda i: (i % nt, 0)),
            pl.BlockSpec((half, tm), lambda i: (0, i % nt)),
            pl.BlockSpec((half, tm), lambda i: (0, i % nt)),
        ],
        out_specs=[
            pl.BlockSpec((1, N_PBT, tm // TQ, LANES, TQ), lambda i: (i // nt, 0, i % nt, 0, 0)),
            pl.BlockSpec((1, tm, N_PB * LANES), lambda i: (i // nt, i % nt, 0)),
            pl.BlockSpec((1, tm, N_F32 * LANES), lambda i: (i // nt, i % nt, 0)),
        ],
        out_shape=[
            jax.ShapeDtypeStruct((B, N_PBT, S // TQ, LANES, TQ), BF16),
            jax.ShapeDtypeStruct((B, S, N_PB * LANES), BF16),
            jax.ShapeDtypeStruct((B, S, N_F32 * LANES), F32),
        ],
        compiler_params=_cparams(("parallel",)),
        name="in_proj",
    )(x, ln, w_t, w_r, gain_t, gain_r, cos_t, sin_p, sin_m, cos8, sin8)


MOBA_BATCH = 4

def _moba_kernel(qt_ref, k_ref, vt_ref, o_ref, q2_ref, bias_ref, m_ref, acc_ref, kmean_ref, *, n_blk):
    n_groups = qt_ref.shape[0] * (H_MOBA // 2)
    nidx = lax.broadcasted_iota(jnp.int32, (16, 2 * TQ), 0)
    kr = lax.broadcasted_iota(jnp.int32, (TQ, TQ), 0)
    ql = lax.broadcasted_iota(jnp.int32, (TQ, TQ), 1)
    causal_bias = jnp.where(kr <= ql, 0.0, NEG)
    for g in range(n_groups):
        bb, p = divmod(g, H_MOBA // 2)
        ps = slice(p * LANES, (p + 1) * LANES)
        rows = [jnp.sum(k_ref[bb, n * TQ:(n + 1) * TQ, ps].astype(F32), axis=0, keepdims=True)
                for n in range(n_blk)]
        kmean_ref[g] = (jnp.concatenate(rows + [jnp.zeros((16 - n_blk, LANES), F32)], axis=0)
                        * (1.0 / MOBA_BLOCK))
    lax.fori_loop(0, n_blk, functools.partial(_moba_tile, qt_ref, k_ref, vt_ref, o_ref, q2_ref, bias_ref, m_ref,
                                              acc_ref, kmean_ref, nidx, causal_bias, n_blk), 0)


def _moba_tile(qt_ref, k_ref, vt_ref, o_ref, q2_ref, bias_ref, m_ref, acc_ref, kmean_ref, nidx,
               causal_bias, n_blk, qi, carry):
    n_pairs = H_MOBA // 2
    groups = [divmod(g, n_pairs) for g in range(qt_ref.shape[0] * n_pairs)]
    past = nidx < qi
    for g, (bb, p) in enumerate(groups):
        km_hi, km_lo = _split_bf16(kmean_ref[g])
        q2 = jnp.concatenate([_half_rows(qt_ref[bb, p, qi], hh) for hh in range(2)], axis=1)
        q2_ref[g] = q2
        gate = _dot(km_hi, q2) + _dot(km_lo, q2)
        for n in range(n_blk):
            row = gate[n:n + 1, :]
            ge = jnp.where(gate >= row, 1.0, 0.0)
            gt = jnp.where(gate > row, 1.0, 0.0)
            beats = jnp.where(past, jnp.where(nidx < n, ge, gt), 0.0)
            cnt = jnp.sum(beats, axis=0, keepdims=True)
            bias_ref[g, n] = jnp.where(cnt < MOBA_TOPK, 0.0, NEG)
        _online_init(m_ref, acc_ref, g)

    def scores(off, g):
        bb, p = groups[g]
        return _dot(k_ref[bb, pl.ds(off, TQ), p * LANES:(p + 1) * LANES], q2_ref[g])

    def pv(g, kt):
        bb, p = groups[g]

        def fn(pr):
            return jnp.concatenate([_dot(_with_ones(vt_ref[bb, p, kt, :HEAD_DIM, :]), pr[:, :TQ]),
                                    _dot(_with_ones(vt_ref[bb, p, kt, HEAD_DIM:, :]), pr[:, TQ:])], axis=1)
        return fn

    def past_scores(kt):
        off = pl.multiple_of(kt * TQ, TQ)
        return [scores(off, g) + bias_ref[g, kt] for g in range(len(groups))]

    def past_tile(kt, s):
        for g in range(len(groups)):
            _online_update(m_ref, acc_ref, g, s[g], pv(g, kt))

    def past_pair(i, _):
        s_a, s_b = past_scores(2 * i), past_scores(2 * i + 1)
        past_tile(2 * i, s_a)
        past_tile(2 * i + 1, s_b)
        return 0

    lax.fori_loop(0, qi // 2, past_pair, 0)

    @pl.when(qi % 2 == 1)
    def _():
        past_tile(qi - 1, past_scores(qi - 1))
    off = pl.multiple_of(qi * TQ, TQ)
    causal2 = jnp.concatenate([causal_bias, causal_bias], axis=1)
    s = [scores(off, g) + causal2 for g in range(len(groups))]
    for g, (bb, p) in enumerate(groups):
        _online_update(m_ref, acc_ref, g, s[g], pv(g, qi))
        o2 = _normalise(acc_ref[g])
        o_ref[bb, pl.ds(off, TQ), p * LANES:(p + 1) * LANES] = jnp.concatenate(
            [o2[:, :TQ], o2[:, TQ:]], axis=0).T.astype(BF16)
    return carry


def _moba(pbt, pb):
    B, _, n_t, _, _ = pbt.shape
    S = pb.shape[1]
    n_pairs = H_MOBA // 2
    n_groups = MOBA_BATCH * n_pairs
    assert TQ == MOBA_BLOCK and n_t == S // MOBA_BLOCK and B % MOBA_BATCH == 0
    return pl.pallas_call(
        functools.partial(_moba_kernel, n_blk=S // MOBA_BLOCK),
        grid=(B // MOBA_BATCH,),
        in_specs=[
            pl.BlockSpec((MOBA_BATCH, n_pairs, n_t, LANES, TQ), lambda b: (b, PBT_QA // n_pairs, 0, 0, 0)),
            pl.BlockSpec((MOBA_BATCH, S, QKV_A), lambda b: (b, 0, PB_KA // n_pairs)),
            pl.BlockSpec((MOBA_BATCH, n_pairs, n_t, LANES, TQ), lambda b: (b, PBT_VA // n_pairs, 0, 0, 0)),
        ],
        out_specs=pl.BlockSpec((MOBA_BATCH, S, QKV_A), lambda b: (b, 0, 0)),
        out_shape=jax.ShapeDtypeStruct((B, S, QKV_A), BF16),
        scratch_shapes=[pltpu.VMEM((n_groups, LANES, 2 * TQ), BF16),
                        pltpu.VMEM((n_groups, S // MOBA_BLOCK, 1, 2 * TQ), F32),
                        pltpu.VMEM((n_groups, 1, 2 * TQ), F32),
                        pltpu.VMEM((n_groups, V_ROWS, 2 * TQ), F32),
                        pltpu.VMEM((n_groups, 16, LANES), F32)],
        compiler_params=_cparams(("parallel",)),
        name="moba",
    )(pbt, pb, pbt)


def _compress_kernel(kc_ref, vc_ref, pek_ref, pev_ref, w1k_ref, w2k_ref, w1v_ref, w2v_ref, gain_ref,
                     cos_ref, sp_ref, sm_ref, ko_ref, vo_ref, *, n_cmp):
    half_len = NSA_CMP_LEN // 2

    def mlp(t_ref, pe_ref, w1_ref, w2_ref):
        toks = [t_ref[0, pl.ds(l, n_cmp, stride=NSA_CMP_STRIDE), :] for l in range(half_len)]
        first = _dot(jnp.concatenate([(toks[l] + pe_ref[l:l + 1, :]).astype(BF16)
                                      for l in range(half_len)], axis=1), w1_ref[0])
        second = _dot(jnp.concatenate([(toks[l] + pe_ref[half_len + l:half_len + l + 1, :]).astype(BF16)
                                       for l in range(half_len)], axis=1), w1_ref[1])
        hid = jax.nn.gelu(first + pltpu.roll(second, n_cmp - 1, 0))
        return _dot(hid.astype(BF16), w2_ref[...])

    valid = lax.broadcasted_iota(jnp.int32, (n_cmp, LANES), 0) < n_cmp - 1
    lo = lax.broadcasted_iota(jnp.int32, (n_cmp, LANES), 1) < HEAD_DIM
    kc = mlp(kc_ref, pek_ref, w1k_ref, w2k_ref)
    kc = _norm_rope(kc, gain_ref[...], cos_ref[...], sp_ref[...], sm_ref[...], lo)
    ko_ref[0] = jnp.where(valid, kc, 0.0).astype(BF16)
    vc = mlp(vc_ref, pev_ref, w1v_ref, w2v_ref)
    vo_ref[0] = jnp.where(valid, vc, 0.0).T.astype(BF16)


def _compress(pf, pe_k, pe_v, w1k, w2k, w1v, w2v, gain, cos_c, sp_c, sm_c):
    B, S, _ = pf.shape
    n_cmp = S // NSA_CMP_STRIDE
    cs = _const_spec
    return pl.pallas_call(
        functools.partial(_compress_kernel, n_cmp=n_cmp),
        grid=(B,),
        in_specs=[
            pl.BlockSpec((1, S, LANES), lambda b: (b, 0, 0)),
            pl.BlockSpec((1, S, LANES), lambda b: (b, 0, 1)),
            cs(pe_k.shape), cs(pe_v.shape), cs(w1k.shape), cs(w2k.shape), cs(w1v.shape), cs(w2v.shape),
            cs(gain.shape), cs(cos_c.shape), cs(sp_c.shape), cs(sm_c.shape),
        ],
        out_specs=[pl.BlockSpec((1, n_cmp, LANES), lambda b: (b, 0, 0)),
                   pl.BlockSpec((1, LANES, n_cmp), lambda b: (b, 0, 0))],
        out_shape=[jax.ShapeDtypeStruct((B, n_cmp, LANES), BF16),
                   jax.ShapeDtypeStruct((B, LANES, n_cmp), BF16)],
        compiler_params=_cparams(("parallel",)),
        name="nsa_compress",
    )(pf, pf, pe_k, pe_v, w1k, w2k, w1v, w2v, gain, cos_c, sp_c, sm_c)


NSA_BATCH = 2

def _nsa_kernel(*refs, n_cmp, n_slc, n_t):
    lax.fori_loop(0, n_t, functools.partial(_nsa_tile, refs, n_cmp, n_slc), 0)


def _nsa_tile(refs, n_cmp, n_slc, qi, carry):
    (qt_ref, kc_ref, vct_ref, ks_ref, vst_ref, kw_ref, vwt_ref, g_ref, ovl_ref, o_ref,
     q6_ref, impf_ref, improw_ref, m_ref, acc_ref, ot_ref) = refs
    n_b = qt_ref.shape[0]
    blk_per_tile = TQ // NSA_SEL_BLOCK
    off = pl.multiple_of(qi * TQ, TQ)
    tpos = qi * TQ + lax.broadcasted_iota(jnp.int32, (1, TQ), 1)
    sigs = [jax.nn.sigmoid(g_ref[bb, pl.ds(off, TQ), :].T) for bb in range(n_b)]
    kr = lax.broadcasted_iota(jnp.int32, (TQ, TQ), 0)
    ql = lax.broadcasted_iota(jnp.int32, (TQ, TQ), 1)
    causal_bias = jnp.where(kr <= ql, 0.0, NEG)
    nidx = lax.broadcasted_iota(jnp.int32, (n_cmp, TQ), 0)
    cmask = (nidx * NSA_CMP_STRIDE + (NSA_CMP_LEN - 1) <= tpos) & (nidx < n_cmp - 1)
    cmp_bias = jnp.where(cmask, 0.0, NEG)
    jj = lax.broadcasted_iota(jnp.int32, (n_slc, TQ), 0)
    cur = lax.shift_right_arithmetic(tpos, NSA_SEL_BLOCK.bit_length() - 1)
    forced = (jj == 0) | (jj == cur) | (jj == cur - 1)
    gw = NSA_GROUP * TQ

    def half(kk):
        return slice(kk * HEAD_DIM, (kk + 1) * HEAD_DIM)

    def cols(kk):
        return slice(kk * gw, (kk + 1) * gw)

    def tile3(a):
        return jnp.concatenate([a] * NSA_GROUP, axis=1)

    def gate3(bb, kk, branch):
        sig = sigs[bb]
        return jnp.concatenate([sig[3 * (NSA_GROUP * kk + j) + branch:3 * (NSA_GROUP * kk + j) + branch + 1, :]
                                for j in range(NSA_GROUP)], axis=1)

    slots = [(bb, kk) for bb in range(n_b) for kk in range(H_NSA_KV)]

    for bb in range(n_b):
        q6_ref[bb] = jnp.concatenate([_half_rows(qt_ref[bb, j, qi], kk)
                                      for kk in range(H_NSA_KV) for j in range(NSA_GROUP)], axis=1)

    s_cmp = [_dot(kc_ref[bb], q6_ref[bb]) for bb in range(n_b)]
    cmp_bias3 = tile3(cmp_bias)
    for sl, (bb, kk) in enumerate(slots):
        s = s_cmp[bb][:, cols(kk)] + cmp_bias3
        m = jnp.maximum(jnp.max(s, axis=0, keepdims=True), M_FLOOR)
        p = jnp.exp2(s - m)
        p = p / jnp.maximum(jnp.sum(p, axis=0, keepdims=True), TINY)
        ot_ref[sl] = gate3(bb, kk, 0) * _dot(vct_ref[bb, half(kk), :], p.astype(BF16))
        p_sum = (p[:, :TQ] + p[:, TQ:2 * TQ]) + p[:, 2 * TQ:]
        _online_init(m_ref, acc_ref, sl)
        p_hi, p_lo = _split_bf16(p_sum)
        imp = _dot(ovl_ref[...], p_hi) + _dot(ovl_ref[...], p_lo)
        imp = jnp.where(jj > cur, NEG, jnp.where(forced, FORCE, imp))
        impf_ref[sl] = imp
        for j in range(n_slc):
            improw_ref[sl, j] = imp[j:j + 1, :]

    def sel_bias(sl, kt):
        imp = impf_ref[sl]
        parts = []
        for c in range(blk_per_tile):
            j = kt * blk_per_tile + c
            row = improw_ref[sl, j]
            ge = jnp.where(imp >= row, 1.0, 0.0)
            gt = jnp.where(imp > row, 1.0, 0.0)
            cnt = jnp.sum(jnp.where(jj < j, ge, gt), axis=0, keepdims=True)
            parts.append(jnp.broadcast_to(jnp.where(cnt < NSA_N_SEL, 0.0, NEG), (NSA_SEL_BLOCK, TQ)))
        return jnp.concatenate(parts, axis=0)

    def sel_pv(bb, kk, kt):
        return lambda pr: _dot(_with_ones(vst_ref[bb, 0, kt, half(kk), :]), pr)

    def sel_scores(kt):
        off = pl.multiple_of(kt * TQ, TQ)
        return [_dot(ks_ref[bb, pl.ds(off, TQ), :], q6_ref[bb]) for bb in range(n_b)]

    def sel_tile(kt, s6):
        for sl, (bb, kk) in enumerate(slots):
            s = s6[bb][:, cols(kk)] + tile3(sel_bias(sl, kt))
            _online_update(m_ref, acc_ref, sl, s, sel_pv(bb, kk, kt))

    def sel_pair(i, _):
        s_a, s_b = sel_scores(2 * i), sel_scores(2 * i + 1)
        sel_tile(2 * i, s_a)
        sel_tile(2 * i + 1, s_b)
        return 0

    lax.fori_loop(0, qi // 2, sel_pair, 0)

    @pl.when(qi % 2 == 1)
    def _():
        sel_tile(qi - 1, sel_scores(qi - 1))

    kt1 = jnp.maximum(qi - 1, 0)
    kt2 = jnp.maximum(qi - 2, 0)
    off1 = pl.multiple_of(kt1 * TQ, TQ)
    off2 = pl.multiple_of(kt2 * TQ, TQ)
    win_bias2 = jnp.where(kr > ql + jnp.where(qi >= 2, 0, TQ), 0.0, NEG)
    win_bias1 = jnp.where(qi >= 1, 0.0, NEG)
    sc = [[_dot(k_ref[bb, pl.ds(o, TQ), :], q6_ref[bb])
           for k_ref, o in ((ks_ref, off), (kw_ref, off2), (kw_ref, off1), (kw_ref, off))] for bb in range(n_b)]
    causal3 = tile3(causal_bias)
    win_bias2 = tile3(win_bias2)
    for sl, (bb, kk) in enumerate(slots):
        sd6, w2, w1, w0 = sc[bb]
        s = sd6[:, cols(kk)] + tile3(sel_bias(sl, qi) + causal_bias)
        _online_update(m_ref, acc_ref, sl, s, sel_pv(bb, kk, qi))
        o_sel = _normalise(acc_ref[sl])
        s2 = w2[:, cols(kk)] + win_bias2
        s1 = w1[:, cols(kk)] + win_bias1
        s0 = w0[:, cols(kk)] + causal3
        m = jnp.maximum(jnp.maximum(jnp.max(s2, axis=0, keepdims=True), jnp.max(s1, axis=0, keepdims=True)),
                        jnp.max(s0, axis=0, keepdims=True))
        p2, p1, p0 = jnp.exp2(s2 - m), jnp.exp2(s1 - m), jnp.exp2(s0 - m)
        acc = (_dot(_with_ones(vwt_ref[bb, 0, kt2, half(kk), :]), p2.astype(BF16))
               + _dot(_with_ones(vwt_ref[bb, 0, kt1, half(kk), :]), p1.astype(BF16))
               + _dot(_with_ones(vwt_ref[bb, 0, qi, half(kk), :]), p0.astype(BF16)))
        o_win = _normalise(acc)
        ot_ref[sl] = (ot_ref[sl] + gate3(bb, kk, 1) * o_sel) + gate3(bb, kk, 2) * o_win
    for bb in range(n_b):
        for j in range(NSA_GROUP):
            js = slice(j * TQ, (j + 1) * TQ)
            o_ref[bb, pl.ds(off, TQ), j * LANES:(j + 1) * LANES] = jnp.concatenate(
                [ot_ref[2 * bb, :, js], ot_ref[2 * bb + 1, :, js]], axis=0).T.astype(BF16)
    return carry


def _nsa(pbt, pb, pf, kcmp, vcmp_t, ovl_t):
    B, _, n_t, _, _ = pbt.shape
    S = pb.shape[1]
    n_cmp = kcmp.shape[1]
    n_slc = S // NSA_SEL_BLOCK
    gw = NSA_GROUP * TQ
    assert NSA_WINDOW == 2 * TQ and TQ % NSA_SEL_BLOCK == 0 and S % TQ == 0 and B % NSA_BATCH == 0
    nb = NSA_BATCH
    n_slots = nb * H_NSA_KV
    return pl.pallas_call(
        functools.partial(_nsa_kernel, n_cmp=n_cmp, n_slc=n_slc, n_t=n_t),
        grid=(B // nb,),
        in_specs=[
            pl.BlockSpec((nb, NSA_GROUP, n_t, LANES, TQ), lambda b: (b, PBT_QB // NSA_GROUP, 0, 0, 0)),
            pl.BlockSpec((nb, n_cmp, LANES), lambda b: (b, 0, 0)),
            pl.BlockSpec((nb, LANES, n_cmp), lambda b: (b, 0, 0)),
            pl.BlockSpec((nb, S, LANES), lambda b: (b, 0, PB_KS)),
            pl.BlockSpec((nb, 1, n_t, LANES, TQ), lambda b: (b, PBT_VS, 0, 0, 0)),
            pl.BlockSpec((nb, S, LANES), lambda b: (b, 0, PB_KW)),
            pl.BlockSpec((nb, 1, n_t, LANES, TQ), lambda b: (b, PBT_VW, 0, 0, 0)),
            pl.BlockSpec((nb, S, LANES), lambda b: (b, 0, 2)),
            _const_spec(ovl_t.shape),
        ],
        out_specs=pl.BlockSpec((nb, S, Q_B), lambda b: (b, 0, 0)),
        out_shape=jax.ShapeDtypeStruct((B, S, Q_B), BF16),
        scratch_shapes=[pltpu.VMEM((nb, LANES, H_NSA * TQ), BF16),
                        pltpu.VMEM((n_slots, n_slc, TQ), F32), pltpu.VMEM((n_slots, n_slc, 1, TQ), F32),
                        pltpu.VMEM((n_slots, 1, gw), F32),
                        pltpu.VMEM((n_slots, V_ROWS, gw), F32),
                        pltpu.VMEM((n_slots, HEAD_DIM, gw), F32)],
        compiler_params=_cparams(("parallel",)),
        name="nsa_attn",
    )(pbt, kcmp, vcmp_t, pb, pbt, pb, pbt, pf, ovl_t)


DIL_TILES = 16

def _dilated_kernel(q_ref, k_ref, v_ref, o_ref, qf_ref, kf_ref, vf_ref, os_ref, ls_ref, *, seq):
    blk = BAND_BLOCK
    n_tiles = seq // blk
    lo = lax.broadcasted_iota(jnp.int32, (blk, LANES), 1) < HEAD_DIM
    for g, (window, dil) in enumerate(DIL_CFG):
        max_dist = window // dil
        tiles_per_class = n_tiles // dil
        use_prev = tiles_per_class > 1
        n_keys = 2 * blk if use_prev else blk
        ri = lax.broadcasted_iota(jnp.int32, (2 * blk, n_keys), 0) & (blk - 1)
        ci = lax.broadcasted_iota(jnp.int32, (2 * blk, n_keys), 1)
        dist = ri + (n_keys - blk) - ci
        band = (dist >= 0) & (dist <= max_dist)
        band_bias = jnp.where(band, 0.0, NEG)
        qf_ref[g] = q_ref[0, :, g * LANES:(g + 1) * LANES].astype(F32)
        kf_ref[g] = k_ref[0, :, g * LANES:(g + 1) * LANES].astype(F32)
        vf_ref[g] = v_ref[0, :, g * LANES:(g + 1) * LANES].astype(F32)

        def index(start, dil=dil):
            if dil == 1:
                return pl.ds(pl.multiple_of(start, blk), blk)
            return pl.ds(start, blk, stride=dil)

        def tiles(i, _, g=g, dil=dil, tiles_per_class=tiles_per_class, use_prev=use_prev, band=band,
                  band_bias=band_bias, ci=ci, index=index):
            scored = []
            for u in range(DIL_TILES):
                t = i * DIL_TILES + u
                cls = t // tiles_per_class
                mt = t % tiles_per_class
                start = cls + dil * blk * mt
                q = qf_ref[g, index(start), :]
                q2 = jnp.concatenate([jnp.where(lo, q, 0.0), jnp.where(lo, 0.0, q)], axis=0).astype(BF16)
                if use_prev:
                    prev = jnp.maximum(start - dil * blk, cls)
                    k2 = jnp.concatenate([kf_ref[g, index(prev), :], kf_ref[g, index(start), :]], axis=0)
                    v2 = jnp.concatenate([vf_ref[g, index(prev), :], vf_ref[g, index(start), :]], axis=0)
                    bias = jnp.where(band & (ci >= jnp.where(mt > 0, 0, blk)), 0.0, NEG)
                else:
                    k2, v2, bias = kf_ref[g, index(start), :], vf_ref[g, index(start), :], band_bias
                s = lax.dot_general(q2, k2.astype(BF16), (((1,), (1,)), ((), ())), preferred_element_type=F32)
                v2 = jnp.concatenate([v2.astype(BF16), jnp.ones((v2.shape[0], LANES), BF16)], axis=1)
                scored.append((start, s + bias, v2))
            probs = []
            for start, s, v2 in scored:
                m = jnp.max(s, axis=-1, keepdims=True)
                probs.append((start, m, jnp.exp2(s - m).astype(BF16), v2))
            for start, m, p, v2 in probs:
                o2 = _dot(p, v2)
                den = jnp.maximum(o2[:, LANES:], TINY)
                o = o2[:, :LANES] / den
                lse = m + jnp.log2(den)
                os_ref[g, index(start), :] = jnp.where(lo, o[:blk], o[blk:])
                ls_ref[g, index(start), :] = jnp.where(lo, lse[:blk], lse[blk:])
            return 0

        lax.fori_loop(0, n_tiles // DIL_TILES, tiles, 0)

    rows_per_trip = DIL_TILES * blk

    def combine(t, _):
        idx = pl.ds(pl.multiple_of(t * rows_per_trip, rows_per_trip), rows_per_trip)
        lses = [ls_ref[g, idx, :] for g in range(len(DIL_CFG))]
        mx = jnp.maximum(jnp.maximum(lses[0], lses[1]), lses[2])
        es = [jnp.exp2(l - mx) for l in lses]
        tot = es[0] + es[1] + es[2]
        for g in range(len(DIL_CFG)):
            o_ref[0, idx, g * LANES:(g + 1) * LANES] = (os_ref[g, idx, :] * (es[g] / tot)).astype(BF16)
        return 0

    lax.fori_loop(0, seq // rows_per_trip, combine, 0)


def _dilated(pb):
    B, S, _ = pb.shape
    n_g = len(DIL_CFG)
    return pl.pallas_call(
        functools.partial(_dilated_kernel, seq=S),
        grid=(B,),
        in_specs=[pl.BlockSpec((1, S, QKV_C), lambda b: (b, 0, PB_QC // n_g)),
                  pl.BlockSpec((1, S, QKV_C), lambda b: (b, 0, PB_KC // n_g)),
                  pl.BlockSpec((1, S, QKV_C), lambda b: (b, 0, PB_VC // n_g))],
        out_specs=pl.BlockSpec((1, S, QKV_C), lambda b: (b, 0, 0)),
        out_shape=jax.ShapeDtypeStruct((B, S, QKV_C), BF16),
        scratch_shapes=[pltpu.VMEM((n_g, S, LANES), F32)] * 5,
        compiler_params=_cparams(("parallel",)),
        name="dilated_attn",
    )(pb, pb, pb)


HALO = 16
FF_CHUNK = 256


def _proj_ffn_kernel(x_ref, xh_ref, oa_ref, oah_ref, ob_ref, obh_ref, oc_ref, och_ref, wo_ref, ln_ref,
                     wg_ref, wu_ref, cw_ref, cb_ref, wd_ref, y_ref, o_ref, h_ref, x1_ref, a_ref,
                     *, tiles_per_seq):
    i = pl.program_id(0)

    def norm(v):
        return (v * lax.rsqrt(jnp.mean(v * v, axis=-1, keepdims=True) + EPS)) * ln_ref[...]

    col = 0
    for main, halo in ((oa_ref, oah_ref), (ob_ref, obh_ref), (oc_ref, och_ref)):
        width = main.shape[1]
        o_ref[:HALO, col:col + width] = halo[...]
        o_ref[HALO:, col:col + width] = main[...]
        col += width
    attn = _dot(o_ref[...], wo_ref[...])
    x1 = x_ref[...] + attn[HALO:]
    x1_ref[...] = x1
    first = (i % tiles_per_seq) == 0
    h_ref[:HALO, :] = jnp.where(first, 0.0, norm(xh_ref[...] + attn[:HALO])).astype(BF16)
    h_ref[HALO:, :] = norm(x1).astype(BF16)

    for f0 in range(0, D_FF, FF_CHUNK):
        fs = slice(f0, min(f0 + FF_CHUNK, D_FF))
        g = _dot(h_ref[...], wg_ref[:, fs])
        u = _dot(h_ref[HALO:, :], wu_ref[:, fs])
        gc = (cw_ref[0:1, fs] * pltpu.roll(g, 2, 0)[HALO:] + cw_ref[1:2, fs] * pltpu.roll(g, 1, 0)[HALO:]
              + cw_ref[2:3, fs] * g[HALO:] + cb_ref[:, fs])
        a_ref[:, fs] = (jax.nn.silu(gc) * u).astype(BF16)
    y_ref[...] = x1_ref[...] + _dot(a_ref[...], wd_ref[...])


def _proj_ffn(x2, o_a, o_b, o_c, wo, ln, wg, wu, cw, cb, wd, seq, tm=512):
    M, D = x2.shape
    halo_blocks = tm // HALO
    main = lambda w: pl.BlockSpec((tm, w), lambda i: (i, 0))
    halo = lambda w: pl.BlockSpec((HALO, w), lambda i: (jnp.maximum(i * halo_blocks - 1, 0), 0))
    return pl.pallas_call(
        functools.partial(_proj_ffn_kernel, tiles_per_seq=seq // tm),
        grid=(M // tm,),
        in_specs=[
            main(D), halo(D), main(QKV_A), halo(QKV_A), main(Q_B), halo(Q_B), main(QKV_C), halo(QKV_C),
            _const_spec(wo.shape), _const_spec((1, D)),
            _const_spec(wg.shape), _const_spec(wu.shape), _const_spec(cw.shape), _const_spec(cb.shape),
            _const_spec(wd.shape),
        ],
        out_specs=main(D),
        out_shape=jax.ShapeDtypeStruct((M, D), F32),
        scratch_shapes=[pltpu.VMEM((HALO + tm, D), BF16), pltpu.VMEM((HALO + tm, D), BF16),
                        pltpu.VMEM((tm, D), F32), pltpu.VMEM((tm, D_FF), BF16)],
        compiler_params=_cparams(("parallel",)),
        name="proj_ffn",
    )(x2, x2, o_a, o_a, o_b, o_b, o_c, o_c, wo, ln, wg, wu, cw, cb, wd)


def _rope_tables(pos):
    half = ROPE_DIM // 2
    inv_freq = ROPE_THETA ** (-jnp.arange(half, dtype=F32) * 2.0 / ROPE_DIM)
    ang = pos.astype(F32)[:, None] * inv_freq
    c, s = jnp.cos(ang), jnp.sin(ang)
    n = pos.shape[0]
    rest = HEAD_DIM - ROPE_DIM
    cos_h = jnp.concatenate([c, c, jnp.ones((n, rest), F32)], axis=-1)
    sp_h = jnp.concatenate([jnp.zeros((n, half), F32), s, jnp.zeros((n, rest), F32)], axis=-1)
    sm_h = jnp.concatenate([-s, jnp.zeros((n, half + rest), F32)], axis=-1)
    two = lambda t: jnp.concatenate([t, t], axis=-1)
    return two(cos_h), two(sp_h), two(sm_h), c.T, s.T


def _pair_diag(w):
    z = jnp.zeros_like(w)
    return jnp.concatenate([jnp.concatenate([w, z], axis=-1), jnp.concatenate([z, w], axis=-1)], axis=-2)


def _overlap_t(seq):
    n_cmp = seq // NSA_CMP_STRIDE
    n_slc = seq // NSA_SEL_BLOCK
    starts = np.arange(n_cmp) * NSA_CMP_STRIDE
    j = np.arange(n_slc)
    ovl = (starts[None, :] < (j[:, None] + 1) * NSA_SEL_BLOCK) & (starts[None, :] + NSA_CMP_LEN > j[:, None] * NSA_SEL_BLOCK)
    ovl[:, n_cmp - 1] = False
    return jnp.asarray(ovl, BF16)


def _mixers(x, seq_tabs, cmp_tabs, ovl_t, ln1, w, qn_a, kn_a, qn_b, kn_b, pe_k, pe_v, wk1, wk2, wv1, wv2,
            qn_c, kn_c):
    rep = lambda g, n: jnp.tile(g, n)
    w_t, w_r = w
    gain_t = jnp.concatenate([rep(qn_a * Q_SCALE, H_MOBA), rep(qn_b * Q_SCALE, H_NSA)])
    gain_t = jnp.broadcast_to(gain_t[:, None], (gain_t.shape[0], LANES))
    gain_r = jnp.concatenate([rep(kn_a, H_MOBA), rep(kn_b[1], H_NSA_KV), rep(kn_b[2], H_NSA_KV),
                              rep(qn_c * Q_SCALE, H_DIL), rep(kn_c, H_DIL)])[None, :]
    pbt, pb, pf = _in_proj(x, ln1[None, :], w_t, w_r, gain_t, gain_r, *seq_tabs)

    o_a = _moba(pbt, pb)
    two = lambda t: jnp.concatenate([t, t], axis=-1)
    w1 = lambda t: _pair_diag(t.reshape(NSA_CMP_LEN, HEAD_DIM, NSA_CMP_HIDDEN)).astype(BF16).reshape(
        2, NSA_CMP_LEN // 2 * LANES, 2 * NSA_CMP_HIDDEN)
    kcmp, vcmp_t = _compress(pf, two(pe_k), two(pe_v), w1(wk1), _pair_diag(wk2).astype(BF16),
                             w1(wv1), _pair_diag(wv2).astype(BF16), two(kn_b[0])[None, :], *cmp_tabs[:3])
    o_b = _nsa(pbt, pb, pf, kcmp, vcmp_t, ovl_t)
    o_c = _dilated(pb)
    return o_a, o_b, o_c


def _layer(x, seq_tabs, cmp_tabs, ovl_t, ln1, w_in, qn_a, kn_a, qn_b, kn_b, pe_k, pe_v, wk1, wk2, wv1, wv2,
           qn_c, kn_c, w_out, ln2, w_gate, w_up, conv_w, conv_b, w_down):
    B, S, D = x.shape
    o_a, o_b, o_c = _mixers(x, seq_tabs, cmp_tabs, ovl_t, ln1, w_in, qn_a, kn_a, qn_b, kn_b, pe_k, pe_v,
                            wk1, wk2, wv1, wv2, qn_c, kn_c)
    nsa_rows = [w_out[QKV_A + h * HEAD_DIM:QKV_A + (h + 1) * HEAD_DIM] for h in NSA_ORDER]
    w_o = jnp.concatenate([w_out[:QKV_A]] + nsa_rows + [w_out[QKV_A + Q_B:]], axis=0).astype(BF16)
    M = B * S
    x2 = _proj_ffn(x.reshape(M, D), o_a.reshape(M, QKV_A), o_b.reshape(M, Q_B), o_c.reshape(M, QKV_C), w_o,
                   ln2[None, :], w_gate.astype(BF16), w_up.astype(BF16), conv_w, conv_b[None, :],
                   w_down.astype(BF16), S)
    return x2.reshape(B, S, D)


def kernel(x, ln1, w_in, qn_a, kn_a, qn_b, kn_b, cmp_pe_k, cmp_pe_v, cmp_k_w1, cmp_k_w2, cmp_v_w1, cmp_v_w2,
           qn_c, kn_c, w_out, ln2, w_gate, w_up, conv_w, conv_b, w_down):
    S = x.shape[1]
    seq_tabs = _rope_tables(jnp.arange(S))
    n_cmp = S // NSA_CMP_STRIDE
    cmp_tabs = _rope_tables(jnp.arange(n_cmp) * NSA_CMP_STRIDE + (NSA_CMP_LEN - 1))
    ovl_t = _overlap_t(S)
    for l in range(ln1.shape[0]):
        x = _layer(x, seq_tabs, cmp_tabs, ovl_t, ln1[l], _prep_w_in(w_in, l), qn_a[l], kn_a[l], qn_b[l], kn_b[l],
                   cmp_pe_k[l], cmp_pe_v[l], cmp_k_w1[l], cmp_k_w2[l], cmp_v_w1[l], cmp_v_w2[l], qn_c[l], kn_c[l],
                   w_out[l], ln2[l], w_gate[l], w_up[l], conv_w[l], conv_b[l], w_down[l])
    return x
```

```python
import functools

import numpy as np
import jax
import jax.numpy as jnp
from jax import lax
from jax.experimental import pallas as pl
from jax.experimental.pallas import tpu as pltpu

F32 = jnp.float32
BF16 = jnp.bfloat16

D_MODEL = 1024
HEAD_DIM = 64
N_HEADS = D_MODEL // HEAD_DIM
H_MOBA = N_HEADS // 4
H_NSA = (N_HEADS - H_MOBA) // 2
H_DIL = N_HEADS - H_MOBA - H_NSA
H_NSA_KV = 2
NSA_GROUP = H_NSA // H_NSA_KV
ROPE_DIM = HEAD_DIM // 4
ROPE_THETA = 500000.0
MOBA_BLOCK = 256
MOBA_TOPK = 3
NSA_CMP_LEN = 32
NSA_CMP_STRIDE = 16
NSA_CMP_HIDDEN = 128
NSA_SEL_BLOCK = 64
NSA_N_SEL = 6
NSA_WINDOW = 512
DIL_CFG = ((128, 1), (512, 4), (2048, 16))
BAND_BLOCK = 128
D_FF = 2816
EPS = 1e-6
NEG = -1e30
TINY = 1e-30
FORCE = 1e9
SCALE = HEAD_DIM ** -0.5
Q_SCALE = SCALE * float(np.log2(np.e))
QKV_A = H_MOBA * HEAD_DIM
Q_B = H_NSA * HEAD_DIM
KV_B = H_NSA_KV * HEAD_DIM
G_B = H_NSA * 3
QKV_C = H_DIL * HEAD_DIM
IN_SPLITS = (QKV_A, QKV_A, QKV_A, Q_B, KV_B, KV_B, KV_B, KV_B, KV_B, KV_B, G_B, QKV_C, QKV_C, QKV_C)

LANES = 128
TQ = 256
NSA_ORDER = (0, 3, 1, 4, 2, 5)
VMEM_LIMIT = 56 * 1024 * 1024

M_FLOOR = -1e29
V_ROWS = HEAD_DIM + 16

T_GROUP = ("qa", "qb", "va", "vsb", "vwb")
N_T_NORM, N_T = 5, 9
R_GROUP = ("ka", "ksb", "kwb", "qc", "kc", "vc", "kcb", "vcb", "gb")
N_R_NORM, N_R_BF16, N_R = 10, 13, 16
N_PBT, N_PB, N_F32 = N_T, N_R_BF16, N_R - N_R_BF16
PBT_QA, PBT_VA, PBT_VS, PBT_VW, PBT_QB = 0, 2, 4, 5, 6
PB_QC, PB_KC, PB_KA, PB_KS, PB_VC, PB_KW = 0, 3, 6, 8, 9, 12
_PBT_DST = (0, 1, 6, 7, 8, 2, 3, 4, 5)
_PB_DST = (6, 7, 8, 12, 0, 1, 2, 3, 4, 5, 9, 10, 11)
CHUNK = 4


def _cparams(sem):
    return pltpu.CompilerParams(dimension_semantics=sem, vmem_limit_bytes=VMEM_LIMIT)


def _const_spec(shape):
    return pl.BlockSpec(shape, lambda *_: (0,) * len(shape), pipeline_mode=pl.Buffered(1))


def _norm_rope(y, gain, cos_t, sin_p, sin_m, lo):
    ss = y * y
    s_lo = jnp.sum(jnp.where(lo, ss, 0.0), axis=-1, keepdims=True)
    s_hi = jnp.sum(jnp.where(lo, 0.0, ss), axis=-1, keepdims=True)
    ms = jnp.where(lo, s_lo, s_hi) * (1.0 / HEAD_DIM)
    z = (y * lax.rsqrt(ms + EPS)) * gain
    half = ROPE_DIM // 2
    return z * cos_t + pltpu.roll(z, half, 1) * sin_p + pltpu.roll(z, LANES - half, 1) * sin_m


def _online_update(m_ref, acc_ref, g, s, pv):
    m_old = m_ref[g]
    m_new = jnp.maximum(m_old, jnp.max(s, axis=0, keepdims=True))
    p = jnp.exp2(s - m_new)
    alpha = jnp.exp2(m_old - m_new)
    acc_ref[g] = alpha * acc_ref[g] + pv(p.astype(BF16))
    m_ref[g] = m_new


def _online_init(m_ref, acc_ref, g):
    m_ref[g] = jnp.full(m_ref.shape[1:], M_FLOOR, F32)
    acc_ref[g] = jnp.zeros(acc_ref.shape[1:], F32)


def _with_ones(v_t):
    row = lax.broadcasted_iota(jnp.int32, (V_ROWS - HEAD_DIM, v_t.shape[1]), 0)
    return jnp.concatenate([v_t, jnp.where(row == 0, 1.0, 0.0).astype(v_t.dtype)], axis=0)


def _normalise(acc):
    return acc[:HEAD_DIM] / jnp.maximum(acc[HEAD_DIM:HEAD_DIM + 1], TINY)


def _dot(a, b):
    return jnp.dot(a, b, preferred_element_type=F32)


def _half_rows(q_t, hh):
    z = jnp.zeros((HEAD_DIM, q_t.shape[1]), q_t.dtype)
    if hh == 0:
        return jnp.concatenate([q_t[:HEAD_DIM], z], axis=0)
    return jnp.concatenate([z, q_t[HEAD_DIM:]], axis=0)


def _split_bf16(a):
    hi = a.astype(BF16)
    return hi, (a - hi.astype(F32)).astype(BF16)


def _in_proj_plan(group):
    names = ("qa", "ka", "va", "qb", "kcb", "vcb", "ksb", "vsb", "kwb", "vwb", "gb", "qc", "kc", "vc")
    start = dict(zip(names, np.cumsum((0,) + IN_SPLITS[:-1]).tolist()))
    width = dict(zip(names, IN_SPLITS))
    plan = []
    for name in group:
        if name == "qb":
            heads = [start[name] + h * HEAD_DIM for h in NSA_ORDER]
            plan += [[(heads[2 * j], HEAD_DIM), (heads[2 * j + 1], HEAD_DIM)] for j in range(H_NSA // 2)]
        else:
            plan += [[(start[name] + c, min(LANES, width[name] - c))] for c in range(0, width[name], LANES)]
    return plan


def _wprep_kernel(w_ref, wt_ref, wr_ref, *, plan_t, plan_r, n_in):
    def take(a, width):
        b0 = a // LANES * LANES
        v0 = w_ref[0, :, b0:min(b0 + LANES, n_in)]
        if a + width <= b0 + LANES:
            return v0[:, a - b0:a - b0 + width]
        v1 = w_ref[0, :, b0 + LANES:min(b0 + 2 * LANES, n_in)]
        return jnp.concatenate([v0[:, a - b0:], v1[:, :a + width - b0 - LANES]], axis=1)

    def block(pieces):
        cols = [take(a, width) for a, width in pieces]
        filled = sum(width for _, width in pieces)
        if filled < LANES:
            cols.append(jnp.zeros((w_ref.shape[1], LANES - filled), F32))
        return cols[0] if len(cols) == 1 else jnp.concatenate(cols, axis=1)

    for d, pieces in enumerate(plan_t):
        wt_ref[d * LANES:(d + 1) * LANES, :] = block(pieces).T.astype(BF16)
    for d, pieces in enumerate(plan_r):
        wr_ref[:, d * LANES:(d + 1) * LANES] = block(pieces).astype(BF16)


def _prep_w_in(w_in, layer, tr=256):
    _, D, n_in = w_in.shape
    plan_t, plan_r = _in_proj_plan(T_GROUP), _in_proj_plan(R_GROUP)
    return pl.pallas_call(
        functools.partial(_wprep_kernel, plan_t=plan_t, plan_r=plan_r, n_in=n_in),
        grid=(D // tr,),
        in_specs=[pl.BlockSpec((1, tr, n_in), lambda i: (layer, i, 0))],
        out_specs=[pl.BlockSpec((N_T * LANES, tr), lambda i: (0, i)),
                   pl.BlockSpec((tr, N_R * LANES), lambda i: (i, 0))],
        out_shape=[jax.ShapeDtypeStruct((N_T * LANES, D), BF16), jax.ShapeDtypeStruct((D, N_R * LANES), BF16)],
        compiler_params=_cparams(("parallel",)),
        name="w_in_layout",
    )(w_in)


def _norm_rope_t(y, gain, cos8, sin8):
    half = ROPE_DIM // 2
    parts = []
    for hh in range(2):
        yh = y[hh * HEAD_DIM:(hh + 1) * HEAD_DIM]
        ms = jnp.sum(yh * yh, axis=0, keepdims=True) * (1.0 / HEAD_DIM)
        z = (yh * lax.rsqrt(ms + EPS)) * gain[hh * HEAD_DIM:(hh + 1) * HEAD_DIM]
        x1, x2 = z[:half], z[half:ROPE_DIM]
        parts += [x1 * cos8 - x2 * sin8, x2 * cos8 + x1 * sin8, z[ROPE_DIM:]]
    return jnp.concatenate(parts, axis=0)


def _inproj_kernel(x_ref, ln_ref, wt_ref, wr_ref, gt_ref, gr_ref, cos_ref, sp_ref, sm_ref, ct_ref, st_ref,
                   pbt_ref, pb_ref, pf_ref, *, tm):
    x = x_ref[0]
    h = x * lax.rsqrt(jnp.mean(x * x, axis=-1, keepdims=True) + EPS)
    h = (h * ln_ref[...]).astype(BF16)
    lo = lax.broadcasted_iota(jnp.int32, (tm, LANES), 1) < HEAD_DIM
    cos_t, sin_p, sin_m = cos_ref[...], sp_ref[...], sm_ref[...]
    for c0 in range(0, N_R, CHUNK):
        y4 = _dot(h, wr_ref[:, c0 * LANES:(c0 + CHUNK) * LANES])
        for c in range(c0, c0 + CHUNK):
            y = y4[:, (c - c0) * LANES:(c - c0 + 1) * LANES]
            if c < N_R_NORM:
                y = _norm_rope(y, gr_ref[:, c * LANES:(c + 1) * LANES], cos_t, sin_p, sin_m, lo)
            if c < N_R_BF16:
                d = _PB_DST[c]
                pb_ref[0, :, d * LANES:(d + 1) * LANES] = y.astype(BF16)
            else:
                d = c - N_R_BF16
                pf_ref[0, :, d * LANES:(d + 1) * LANES] = y
    cos8, sin8 = ct_ref[...], st_ref[...]
    for t0, t1 in ((0, N_T_NORM), (N_T_NORM, N_T)):
        y_t = lax.dot_general(wt_ref[t0 * LANES:t1 * LANES, :], h, (((1,), (1,)), ((), ())),
                              preferred_element_type=F32)
        for t in range(t0, t1):
            y = y_t[(t - t0) * LANES:(t - t0 + 1) * LANES]
            if t < N_T_NORM:
                gain = jnp.concatenate([gt_ref[t * LANES:(t + 1) * LANES, :]] * (tm // LANES), axis=1)
                y = _norm_rope_t(y, gain, cos8, sin8)
            y = y.astype(BF16)
            for u in range(tm // TQ):
                pbt_ref[0, _PBT_DST[t], u] = y[:, u * TQ:(u + 1) * TQ]


def _in_proj(x, ln, w_t, w_r, gain_t, gain_r, cos_t, sin_p, sin_m, cos8, sin8, tm=1024):
    B, S, D = x.shape
    nt = S // tm
    half = ROPE_DIM // 2
    return pl.pallas_call(
        functools.partial(_inproj_kernel, tm=tm),
        grid=(B * nt,),
        in_specs=[
            pl.BlockSpec((1, tm, D), lambda i: (i // nt, i % nt, 0)),
            _const_spec((1, D)),
            _const_spec(w_t.shape), _const_spec(w_r.shape), _const_spec(gain_t.shape), _const_spec(gain_r.shape),
            pl.BlockSpec((tm, LANES), lambda i: (i % nt, 0)),
            pl.BlockSpec((tm, LANES), lambda i: (i % nt, 0)),
            pl.BlockSpec((tm, LANES), lambda i: (i % nt, 0)),
            pl.BlockSpec((half, tm), lambda i: (0, i % nt)),
            pl.BlockSpec((half, tm), lambda i: (0, i % nt)),
        ],
        out_specs=[
            pl.BlockSpec((1, N_PBT, tm // TQ, LANES, TQ), lambda i: (i // nt, 0, i % nt, 0, 0)),
            pl.BlockSpec((1, tm, N_PB * LANES), lambda i: (i // nt, i % nt, 0)),
            pl.BlockSpec((1, tm, N_F32 * LANES), lambda i: (i // nt, i % nt, 0)),
        ],
        out_shape=[
            jax.ShapeDtypeStruct((B, N_PBT, S // TQ, LANES, TQ), BF16),
            jax.ShapeDtypeStruct((B, S, N_PB * LANES), BF16),
            jax.ShapeDtypeStruct((B, S, N_F32 * LANES), F32),
        ],
        compiler_params=_cparams(("parallel",)),
        name="in_proj",
    )(x, ln, w_t, w_r, gain_t, gain_r, cos_t, sin_p, sin_m, cos8, sin8)


MOBA_BATCH = 4

def _moba_kernel(qt_ref, k_ref, vt_ref, o_ref, q2_ref, bias_ref, m_ref, acc_ref, kmean_ref, *, n_blk):
    n_groups = qt_ref.shape[0] * (H_MOBA // 2)
    nidx = lax.broadcasted_iota(jnp.int32, (16, 2 * TQ), 0)
    kr = lax.broadcasted_iota(jnp.int32, (TQ, TQ), 0)
    ql = lax.broadcasted_iota(jnp.int32, (TQ, TQ), 1)
    causal_bias = jnp.where(kr <= ql, 0.0, NEG)
    for g in range(n_groups):
        bb, p = divmod(g, H_MOBA // 2)
        ps = slice(p * LANES, (p + 1) * LANES)
        rows = [jnp.sum(k_ref[bb, n * TQ:(n + 1) * TQ, ps].astype(F32), axis=0, keepdims=True)
                for n in range(n_blk)]
        kmean_ref[g] = (jnp.concatenate(rows + [jnp.zeros((16 - n_blk, LANES), F32)], axis=0)
                        * (1.0 / MOBA_BLOCK))
    lax.fori_loop(0, n_blk, functools.partial(_moba_tile, qt_ref, k_ref, vt_ref, o_ref, q2_ref, bias_ref, m_ref,
                                              acc_ref, kmean_ref, nidx, causal_bias, n_blk), 0)


def _moba_tile(qt_ref, k_ref, vt_ref, o_ref, q2_ref, bias_ref, m_ref, acc_ref, kmean_ref, nidx,
               causal_bias, n_blk, qi, carry):
    n_pairs = H_MOBA // 2
    groups = [divmod(g, n_pairs) for g in range(qt_ref.shape[0] * n_pairs)]
    past = nidx < qi
    for g, (bb, p) in enumerate(groups):
        km_hi, km_lo = _split_bf16(kmean_ref[g])
        q2 = jnp.concatenate([_half_rows(qt_ref[bb, p, qi], hh) for hh in range(2)], axis=1)
        q2_ref[g] = q2
        gate = _dot(km_hi, q2) + _dot(km_lo, q2)
        for n in range(n_blk):
            row = gate[n:n + 1, :]
            ge = jnp.where(gate >= row, 1.0, 0.0)
            gt = jnp.where(gate > row, 1.0, 0.0)
            beats = jnp.where(past, jnp.where(nidx < n, ge, gt), 0.0)
            cnt = jnp.sum(beats, axis=0, keepdims=True)
            bias_ref[g, n] = jnp.where(cnt < MOBA_TOPK, 0.0, NEG)
        _online_init(m_ref, acc_ref, g)

    def scores(off, g):
        bb, p = groups[g]
        return _dot(k_ref[bb, pl.ds(off, TQ), p * LANES:(p + 1) * LANES], q2_ref[g])

    def pv(g, kt):
        bb, p = groups[g]

        def fn(pr):
            return jnp.concatenate([_dot(_with_ones(vt_ref[bb, p, kt, :HEAD_DIM, :]), pr[:, :TQ]),
                                    _dot(_with_ones(vt_ref[bb, p, kt, HEAD_DIM:, :]), pr[:, TQ:])], axis=1)
        return fn

    def past_scores(kt):
        off = pl.multiple_of(kt * TQ, TQ)
        return [scores(off, g) + bias_ref[g, kt] for g in range(len(groups))]

    def past_tile(kt, s):
        for g in range(len(groups)):
            _online_update(m_ref, acc_ref, g, s[g], pv(g, kt))

    def past_pair(i, _):
        s_a, s_b = past_scores(2 * i), past_scores(2 * i + 1)
        past_tile(2 * i, s_a)
        past_tile(2 * i + 1, s_b)
        return 0

    lax.fori_loop(0, qi // 2, past_pair, 0)

    @pl.when(qi % 2 == 1)
    def _():
        past_tile(qi - 1, past_scores(qi - 1))
    off = pl.multiple_of(qi * TQ, TQ)
    causal2 = jnp.concatenate([causal_bias, causal_bias], axis=1)
    s = [scores(off, g) + causal2 for g in range(len(groups))]
    for g, (bb, p) in enumerate(groups):
        _online_update(m_ref, acc_ref, g, s[g], pv(g, qi))
        o2 = _normalise(acc_ref[g])
        o_ref[bb, pl.ds(off, TQ), p * LANES:(p + 1) * LANES] = jnp.concatenate(
            [o2[:, :TQ], o2[:, TQ:]], axis=0).T.astype(BF16)
    return carry


def _moba(pbt, pb):
    B, _, n_t, _, _ = pbt.shape
    S = pb.shape[1]
    n_pairs = H_MOBA // 2
    n_groups = MOBA_BATCH * n_pairs
    assert TQ == MOBA_BLOCK and n_t == S // MOBA_BLOCK and B % MOBA_BATCH == 0
    return pl.pallas_call(
        functools.partial(_moba_kernel, n_blk=S // MOBA_BLOCK),
        grid=(B // MOBA_BATCH,),
        in_specs=[
            pl.BlockSpec((MOBA_BATCH, n_pairs, n_t, LANES, TQ), lambda b: (b, PBT_QA // n_pairs, 0, 0, 0)),
            pl.BlockSpec((MOBA_BATCH, S, QKV_A), lambda b: (b, 0, PB_KA // n_pairs)),
            pl.BlockSpec((MOBA_BATCH, n_pairs, n_t, LANES, TQ), lambda b: (b, PBT_VA // n_pairs, 0, 0, 0)),
        ],
        out_specs=pl.BlockSpec((MOBA_BATCH, S, QKV_A), lambda b: (b, 0, 0)),
        out_shape=jax.ShapeDtypeStruct((B, S, QKV_A), BF16),
        scratch_shapes=[pltpu.VMEM((n_groups, LANES, 2 * TQ), BF16),
                        pltpu.VMEM((n_groups, S // MOBA_BLOCK, 1, 2 * TQ), F32),
                        pltpu.VMEM((n_groups, 1, 2 * TQ), F32),
                        pltpu.VMEM((n_groups, V_ROWS, 2 * TQ), F32),
                        pltpu.VMEM((n_groups, 16, LANES), F32)],
        compiler_params=_cparams(("parallel",)),
        name="moba",
    )(pbt, pb, pbt)


CMP_BATCH = 4

def _compress_kernel(kc_ref, vc_ref, pek_ref, pev_ref, w1k_ref, w2k_ref, w1v_ref, w2v_ref, gain_ref,
                     cos_ref, sp_ref, sm_ref, ko_ref, vo_ref, *, n_cmp):
    half_len = NSA_CMP_LEN // 2
    n_b = kc_ref.shape[0]
    rows = n_b * n_cmp

    def mlp(t_ref, pe_ref, w1_ref, w2_ref):
        toks = [jnp.concatenate([t_ref[bb, pl.ds(l, n_cmp, stride=NSA_CMP_STRIDE), :] for bb in range(n_b)], axis=0)
                for l in range(half_len)]
        first = _dot(jnp.concatenate([(toks[l] + pe_ref[l:l + 1, :]).astype(BF16)
                                      for l in range(half_len)], axis=1), w1_ref[0])
        second = _dot(jnp.concatenate([(toks[l] + pe_ref[half_len + l:half_len + l + 1, :]).astype(BF16)
                                       for l in range(half_len)], axis=1), w1_ref[1])
        hid = jax.nn.gelu(first + pltpu.roll(second, rows - 1, 0))
        return _dot(hid.astype(BF16), w2_ref[...])

    row = lax.broadcasted_iota(jnp.int32, (rows, LANES), 0)
    valid = (row & (n_cmp - 1)) < n_cmp - 1
    lo = lax.broadcasted_iota(jnp.int32, (rows, LANES), 1) < HEAD_DIM
    tabs = [jnp.concatenate([t[...]] * n_b, axis=0) for t in (cos_ref, sp_ref, sm_ref)]
    kc = mlp(kc_ref, pek_ref, w1k_ref, w2k_ref)
    kc = jnp.where(valid, _norm_rope(kc, gain_ref[...], *tabs, lo), 0.0).astype(BF16)
    vc = jnp.where(valid, mlp(vc_ref, pev_ref, w1v_ref, w2v_ref), 0.0)
    for bb in range(n_b):
        ko_ref[bb] = kc[bb * n_cmp:(bb + 1) * n_cmp]
        vo_ref[bb] = vc[bb * n_cmp:(bb + 1) * n_cmp].T.astype(BF16)


def _compress(pf, pe_k, pe_v, w1k, w2k, w1v, w2v, gain, cos_c, sp_c, sm_c):
    B, S, _ = pf.shape
    n_cmp = S // NSA_CMP_STRIDE
    cs = _const_spec
    nb = CMP_BATCH
    assert B % nb == 0 and n_cmp & (n_cmp - 1) == 0
    return pl.pallas_call(
        functools.partial(_compress_kernel, n_cmp=n_cmp),
        grid=(B // nb,),
        in_specs=[
            pl.BlockSpec((nb, S, LANES), lambda b: (b, 0, 0)),
            pl.BlockSpec((nb, S, LANES), lambda b: (b, 0, 1)),
            cs(pe_k.shape), cs(pe_v.shape), cs(w1k.shape), cs(w2k.shape), cs(w1v.shape), cs(w2v.shape),
            cs(gain.shape), cs(cos_c.shape), cs(sp_c.shape), cs(sm_c.shape),
        ],
        out_specs=[pl.BlockSpec((nb, n_cmp, LANES), lambda b: (b, 0, 0)),
                   pl.BlockSpec((nb, LANES, n_cmp), lambda b: (b, 0, 0))],
        out_shape=[jax.ShapeDtypeStruct((B, n_cmp, LANES), BF16),
                   jax.ShapeDtypeStruct((B, LANES, n_cmp), BF16)],
        compiler_params=_cparams(("parallel",)),
        name="nsa_compress",
    )(pf, pf, pe_k, pe_v, w1k, w2k, w1v, w2v, gain, cos_c, sp_c, sm_c)


NSA_BATCH = 2

def _nsa_kernel(*refs, n_cmp, n_slc, n_t):
    lax.fori_loop(0, n_t, functools.partial(_nsa_tile, refs, n_cmp, n_slc), 0)


def _nsa_tile(refs, n_cmp, n_slc, qi, carry):
    (qt_ref, kc_ref, vct_ref, ks_ref, vst_ref, kw_ref, vwt_ref, g_ref, ovl_ref, o_ref,
     q6_ref, impf_ref, improw_ref, m_ref, acc_ref, ot_ref) = refs
    n_b = qt_ref.shape[0]
    blk_per_tile = TQ // NSA_SEL_BLOCK
    off = pl.multiple_of(qi * TQ, TQ)
    tpos = qi * TQ + lax.broadcasted_iota(jnp.int32, (1, TQ), 1)
    sigs = [jax.nn.sigmoid(g_ref[bb, pl.ds(off, TQ), :].T) for bb in range(n_b)]
    kr = lax.broadcasted_iota(jnp.int32, (TQ, TQ), 0)
    ql = lax.broadcasted_iota(jnp.int32, (TQ, TQ), 1)
    causal_bias = jnp.where(kr <= ql, 0.0, NEG)
    nidx = lax.broadcasted_iota(jnp.int32, (n_cmp, TQ), 0)
    cmask = (nidx * NSA_CMP_STRIDE + (NSA_CMP_LEN - 1) <= tpos) & (nidx < n_cmp - 1)
    cmp_bias = jnp.where(cmask, 0.0, NEG)
    jj = lax.broadcasted_iota(jnp.int32, (n_slc, TQ), 0)
    cur = lax.shift_right_arithmetic(tpos, NSA_SEL_BLOCK.bit_length() - 1)
    forced = (jj == 0) | (jj == cur) | (jj == cur - 1)
    gw = NSA_GROUP * TQ

    def half(kk):
        return slice(kk * HEAD_DIM, (kk + 1) * HEAD_DIM)

    def cols(kk):
        return slice(kk * gw, (kk + 1) * gw)

    def tile3(a):
        return jnp.concatenate([a] * NSA_GROUP, axis=1)

    def gate3(bb, kk, branch):
        sig = sigs[bb]
        return jnp.concatenate([sig[3 * (NSA_GROUP * kk + j) + branch:3 * (NSA_GROUP * kk + j) + branch + 1, :]
                                for j in range(NSA_GROUP)], axis=1)

    slots = [(bb, kk) for bb in range(n_b) for kk in range(H_NSA_KV)]

    for bb in range(n_b):
        q6_ref[bb] = jnp.concatenate([_half_rows(qt_ref[bb, j, qi], kk)
                                      for kk in range(H_NSA_KV) for j in range(NSA_GROUP)], axis=1)

    s_cmp = [_dot(kc_ref[bb], q6_ref[bb]) for bb in range(n_b)]
    cmp_bias3 = tile3(cmp_bias)
    for sl, (bb, kk) in enumerate(slots):
        s = s_cmp[bb][:, cols(kk)] + cmp_bias3
        m = jnp.maximum(jnp.max(s, axis=0, keepdims=True), M_FLOOR)
        p = jnp.exp2(s - m)
        p = p / jnp.maximum(jnp.sum(p, axis=0, keepdims=True), TINY)
        ot_ref[sl] = gate3(bb, kk, 0) * _dot(vct_ref[bb, half(kk), :], p.astype(BF16))
        p_sum = (p[:, :TQ] + p[:, TQ:2 * TQ]) + p[:, 2 * TQ:]
        _online_init(m_ref, acc_ref, sl)
        p_hi, p_lo = _split_bf16(p_sum)
        imp = _dot(ovl_ref[...], p_hi) + _dot(ovl_ref[...], p_lo)
        imp = jnp.where(jj > cur, NEG, jnp.where(forced, FORCE, imp))
        impf_ref[sl] = imp
        for j in range(n_slc):
            improw_ref[sl, j] = imp[j:j + 1, :]

    def sel_bias(sl, kt):
        imp = impf_ref[sl]
        parts = []
        for c in range(blk_per_tile):
            j = kt * blk_per_tile + c
            row = improw_ref[sl, j]
            ge = jnp.where(imp >= row, 1.0, 0.0)
            gt = jnp.where(imp > row, 1.0, 0.0)
            cnt = jnp.sum(jnp.where(jj < j, ge, gt), axis=0, keepdims=True)
            parts.append(jnp.broadcast_to(jnp.where(cnt < NSA_N_SEL, 0.0, NEG), (NSA_SEL_BLOCK, TQ)))
        return jnp.concatenate(parts, axis=0)

    def sel_pv(bb, kk, kt):
        return lambda pr: _dot(_with_ones(vst_ref[bb, 0, kt, half(kk), :]), pr)

    def sel_scores(kt):
        off = pl.multiple_of(kt * TQ, TQ)
        return [_dot(ks_ref[bb, pl.ds(off, TQ), :], q6_ref[bb]) for bb in range(n_b)]

    def sel_tile(kt, s6):
        for sl, (bb, kk) in enumerate(slots):
            s = s6[bb][:, cols(kk)] + tile3(sel_bias(sl, kt))
            _online_update(m_ref, acc_ref, sl, s, sel_pv(bb, kk, kt))

    def sel_pair(i, _):
        s_a, s_b = sel_scores(2 * i), sel_scores(2 * i + 1)
        sel_tile(2 * i, s_a)
        sel_tile(2 * i + 1, s_b)
        return 0

    lax.fori_loop(0, qi // 2, sel_pair, 0)

    @pl.when(qi % 2 == 1)
    def _():
        sel_tile(qi - 1, sel_scores(qi - 1))

    kt1 = jnp.maximum(qi - 1, 0)
    kt2 = jnp.maximum(qi - 2, 0)
    off1 = pl.multiple_of(kt1 * TQ, TQ)
    off2 = pl.multiple_of(kt2 * TQ, TQ)
    win_bias2 = jnp.where(kr > ql + jnp.where(qi >= 2, 0, TQ), 0.0, NEG)
    win_bias1 = jnp.where(qi >= 1, 0.0, NEG)
    sc = [[_dot(k_ref[bb, pl.ds(o, TQ), :], q6_ref[bb])
           for k_ref, o in ((ks_ref, off), (kw_ref, off2), (kw_ref, off1), (kw_ref, off))] for bb in range(n_b)]
    causal3 = tile3(causal_bias)
    win_bias2 = tile3(win_bias2)
    for sl, (bb, kk) in enumerate(slots):
        sd6, w2, w1, w0 = sc[bb]
        s = sd6[:, cols(kk)] + tile3(sel_bias(sl, qi) + causal_bias)
        _online_update(m_ref, acc_ref, sl, s, sel_pv(bb, kk, qi))
        o_sel = _normalise(acc_ref[sl])
        s2 = w2[:, cols(kk)] + win_bias2
        s1 = w1[:, cols(kk)] + win_bias1
        s0 = w0[:, cols(kk)] + causal3
        m = jnp.maximum(jnp.maximum(jnp.max(s2, axis=0, keepdims=True), jnp.max(s1, axis=0, keepdims=True)),
                        jnp.max(s0, axis=0, keepdims=True))
        p2, p1, p0 = jnp.exp2(s2 - m), jnp.exp2(s1 - m), jnp.exp2(s0 - m)
        acc = (_dot(_with_ones(vwt_ref[bb, 0, kt2, half(kk), :]), p2.astype(BF16))
               + _dot(_with_ones(vwt_ref[bb, 0, kt1, half(kk), :]), p1.astype(BF16))
               + _dot(_with_ones(vwt_ref[bb, 0, qi, half(kk), :]), p0.astype(BF16)))
        o_win = _normalise(acc)
        ot_ref[sl] = (ot_ref[sl] + gate3(bb, kk, 1) * o_sel) + gate3(bb, kk, 2) * o_win
    for bb in range(n_b):
        for j in range(NSA_GROUP):
            js = slice(j * TQ, (j + 1) * TQ)
            o_ref[bb, pl.ds(off, TQ), j * LANES:(j + 1) * LANES] = jnp.concatenate(
                [ot_ref[2 * bb, :, js], ot_ref[2 * bb + 1, :, js]], axis=0).T.astype(BF16)
    return carry


def _nsa(pbt, pb, pf, kcmp, vcmp_t, ovl_t):
    B, _, n_t, _, _ = pbt.shape
    S = pb.shape[1]
    n_cmp = kcmp.shape[1]
    n_slc = S // NSA_SEL_BLOCK
    gw = NSA_GROUP * TQ
    assert NSA_WINDOW == 2 * TQ and TQ % NSA_SEL_BLOCK == 0 and S % TQ == 0 and B % NSA_BATCH == 0
    nb = NSA_BATCH
    n_slots = nb * H_NSA_KV
    return pl.pallas_call(
        functools.partial(_nsa_kernel, n_cmp=n_cmp, n_slc=n_slc, n_t=n_t),
        grid=(B // nb,),
        in_specs=[
            pl.BlockSpec((nb, NSA_GROUP, n_t, LANES, TQ), lambda b: (b, PBT_QB // NSA_GROUP, 0, 0, 0)),
            pl.BlockSpec((nb, n_cmp, LANES), lambda b: (b, 0, 0)),
            pl.BlockSpec((nb, LANES, n_cmp), lambda b: (b, 0, 0)),
            pl.BlockSpec((nb, S, LANES), lambda b: (b, 0, PB_KS)),
            pl.BlockSpec((nb, 1, n_t, LANES, TQ), lambda b: (b, PBT_VS, 0, 0, 0)),
            pl.BlockSpec((nb, S, LANES), lambda b: (b, 0, PB_KW)),
            pl.BlockSpec((nb, 1, n_t, LANES, TQ), lambda b: (b, PBT_VW, 0, 0, 0)),
            pl.BlockSpec((nb, S, LANES), lambda b: (b, 0, 2)),
            _const_spec(ovl_t.shape),
        ],
        out_specs=pl.BlockSpec((nb, S, Q_B), lambda b: (b, 0, 0)),
        out_shape=jax.ShapeDtypeStruct((B, S, Q_B), BF16),
        scratch_shapes=[pltpu.VMEM((nb, LANES, H_NSA * TQ), BF16),
                        pltpu.VMEM((n_slots, n_slc, TQ), F32), pltpu.VMEM((n_slots, n_slc, 1, TQ), F32),
                        pltpu.VMEM((n_slots, 1, gw), F32),
                        pltpu.VMEM((n_slots, V_ROWS, gw), F32),
                        pltpu.VMEM((n_slots, HEAD_DIM, gw), F32)],
        compiler_params=_cparams(("parallel",)),
        name="nsa_attn",
    )(pbt, kcmp, vcmp_t, pb, pbt, pb, pbt, pf, ovl_t)


DIL_TILES = 16

def _dilated_kernel(q_ref, k_ref, v_ref, o_ref, qf_ref, kf_ref, vf_ref, os_ref, ls_ref, *, seq):
    blk = BAND_BLOCK
    n_tiles = seq // blk
    lo = lax.broadcasted_iota(jnp.int32, (blk, LANES), 1) < HEAD_DIM
    for g, (window, dil) in enumerate(DIL_CFG):
        max_dist = window // dil
        tiles_per_class = n_tiles // dil
        use_prev = tiles_per_class > 1
        n_keys = 2 * blk if use_prev else blk
        ri = lax.broadcasted_iota(jnp.int32, (2 * blk, n_keys), 0) & (blk - 1)
        ci = lax.broadcasted_iota(jnp.int32, (2 * blk, n_keys), 1)
        dist = ri + (n_keys - blk) - ci
        band = (dist >= 0) & (dist <= max_dist)
        band_bias = jnp.where(band, 0.0, NEG)
        qf_ref[g] = q_ref[0, :, g * LANES:(g + 1) * LANES].astype(F32)
        kf_ref[g] = k_ref[0, :, g * LANES:(g + 1) * LANES].astype(F32)
        vf_ref[g] = v_ref[0, :, g * LANES:(g + 1) * LANES].astype(F32)

        def index(start, dil=dil):
            if dil == 1:
                return pl.ds(pl.multiple_of(start, blk), blk)
            return pl.ds(start, blk, stride=dil)

        def tiles(i, _, g=g, dil=dil, tiles_per_class=tiles_per_class, use_prev=use_prev, band=band,
                  band_bias=band_bias, ci=ci, index=index):
            scored = []
            for u in range(DIL_TILES):
                t = i * DIL_TILES + u
                cls = t // tiles_per_class
                mt = t % tiles_per_class
                start = cls + dil * blk * mt
                q = qf_ref[g, index(start), :]
                q2 = jnp.concatenate([jnp.where(lo, q, 0.0), jnp.where(lo, 0.0, q)], axis=0).astype(BF16)
                if use_prev:
                    prev = jnp.maximum(start - dil * blk, cls)
                    k2 = jnp.concatenate([kf_ref[g, index(prev), :], kf_ref[g, index(start), :]], axis=0)
                    v2 = jnp.concatenate([vf_ref[g, index(prev), :], vf_ref[g, index(start), :]], axis=0)
                    bias = jnp.where(band & (ci >= jnp.where(mt > 0, 0, blk)), 0.0, NEG)
                else:
                    k2, v2, bias = kf_ref[g, index(start), :], vf_ref[g, index(start), :], band_bias
                s = lax.dot_general(q2, k2.astype(BF16), (((1,), (1,)), ((), ())), preferred_element_type=F32)
                v2 = jnp.concatenate([v2.astype(BF16), jnp.ones((v2.shape[0], LANES), BF16)], axis=1)
                scored.append((start, s + bias, v2))
            probs = []
            for start, s, v2 in scored:
                m = jnp.max(s, axis=-1, keepdims=True)
                probs.append((start, m, jnp.exp2(s - m).astype(BF16), v2))
            for start, m, p, v2 in probs:
                o2 = _dot(p, v2)
                den = jnp.maximum(o2[:, LANES:], TINY)
                o = o2[:, :LANES] / den
                lse = m + jnp.log2(den)
                os_ref[g, index(start), :] = jnp.where(lo, o[:blk], o[blk:])
                ls_ref[g, index(start), :] = jnp.where(lo, lse[:blk], lse[blk:])
            return 0

        lax.fori_loop(0, n_tiles // DIL_TILES, tiles, 0)

    rows_per_trip = DIL_TILES * blk

    def combine(t, _):
        idx = pl.ds(pl.multiple_of(t * rows_per_trip, rows_per_trip), rows_per_trip)
        lses = [ls_ref[g, idx, :] for g in range(len(DIL_CFG))]
        mx = jnp.maximum(jnp.maximum(lses[0], lses[1]), lses[2])
        es = [jnp.exp2(l - mx) for l in lses]
        tot = es[0] + es[1] + es[2]
        for g in range(len(DIL_CFG)):
            o_ref[0, idx, g * LANES:(g + 1) * LANES] = (os_ref[g, idx, :] * (es[g] / tot)).astype(BF16)
        return 0

    lax.fori_loop(0, seq // rows_per_trip, combine, 0)


def _dilated(pb):
    B, S, _ = pb.shape
    n_g = len(DIL_CFG)
    return pl.pallas_call(
        functools.partial(_dilated_kernel, seq=S),
        grid=(B,),
        in_specs=[pl.BlockSpec((1, S, QKV_C), lambda b: (b, 0, PB_QC // n_g)),
                  pl.BlockSpec((1, S, QKV_C), lambda b: (b, 0, PB_KC // n_g)),
                  pl.BlockSpec((1, S, QKV_C), lambda b: (b, 0, PB_VC // n_g))],
        out_specs=pl.BlockSpec((1, S, QKV_C), lambda b: (b, 0, 0)),
        out_shape=jax.ShapeDtypeStruct((B, S, QKV_C), BF16),
        scratch_shapes=[pltpu.VMEM((n_g, S, LANES), F32)] * 5,
        compiler_params=_cparams(("parallel",)),
        name="dilated_attn",
    )(pb, pb, pb)


HALO = 16
FF_CHUNK = 256


def _proj_ffn_kernel(x_ref, xh_ref, oa_ref, oah_ref, ob_ref, obh_ref, oc_ref, och_ref, wo_ref, ln_ref,
                     wg_ref, wu_ref, cw_ref, cb_ref, wd_ref, y_ref, o_ref, h_ref, x1_ref, a_ref,
                     *, tiles_per_seq):
    i = pl.program_id(0)

    def norm(v):
        return (v * lax.rsqrt(jnp.mean(v * v, axis=-1, keepdims=True) + EPS)) * ln_ref[...]

    col = 0
    for main, halo in ((oa_ref, oah_ref), (ob_ref, obh_ref), (oc_ref, och_ref)):
        width = main.shape[1]
        o_ref[:HALO, col:col + width] = halo[...]
        o_ref[HALO:, col:col + width] = main[...]
        col += width
    attn = _dot(o_ref[...], wo_ref[...])
    x1 = x_ref[...] + attn[HALO:]
    x1_ref[...] = x1
    first = (i % tiles_per_seq) == 0
    h_ref[:HALO, :] = jnp.where(first, 0.0, norm(xh_ref[...] + attn[:HALO])).astype(BF16)
    h_ref[HALO:, :] = norm(x1).astype(BF16)

    for f0 in range(0, D_FF, FF_CHUNK):
        fs = slice(f0, min(f0 + FF_CHUNK, D_FF))
        g = _dot(h_ref[...], wg_ref[:, fs])
        u = _dot(h_ref[HALO:, :], wu_ref[:, fs])
        gc = (cw_ref[0:1, fs] * pltpu.roll(g, 2, 0)[HALO:] + cw_ref[1:2, fs] * pltpu.roll(g, 1, 0)[HALO:]
              + cw_ref[2:3, fs] * g[HALO:] + cb_ref[:, fs])
        a_ref[:, fs] = (jax.nn.silu(gc) * u).astype(BF16)
    y_ref[...] = x1_ref[...] + _dot(a_ref[...], wd_ref[...])


def _proj_ffn(x2, o_a, o_b, o_c, wo, ln, wg, wu, cw, cb, wd, seq, tm=512):
    M, D = x2.shape
    halo_blocks = tm // HALO
    main = lambda w: pl.BlockSpec((tm, w), lambda i: (i, 0))
    halo = lambda w: pl.BlockSpec((HALO, w), lambda i: (jnp.maximum(i * halo_blocks - 1, 0), 0))
    return pl.pallas_call(
        functools.partial(_proj_ffn_kernel, tiles_per_seq=seq // tm),
        grid=(M // tm,),
        in_specs=[
            main(D), halo(D), main(QKV_A), halo(QKV_A), main(Q_B), halo(Q_B), main(QKV_C), halo(QKV_C),
            _const_spec(wo.shape), _const_spec((1, D)),
            _const_spec(wg.shape), _const_spec(wu.shape), _const_spec(cw.shape), _const_spec(cb.shape),
            _const_spec(wd.shape),
        ],
        out_specs=main(D),
        out_shape=jax.ShapeDtypeStruct((M, D), F32),
        scratch_shapes=[pltpu.VMEM((HALO + tm, D), BF16), pltpu.VMEM((HALO + tm, D), BF16),
                        pltpu.VMEM((tm, D), F32), pltpu.VMEM((tm, D_FF), BF16)],
        compiler_params=_cparams(("parallel",)),
        name="proj_ffn",
    )(x2, x2, o_a, o_a, o_b, o_b, o_c, o_c, wo, ln, wg, wu, cw, cb, wd)


def _rope_tables(pos):
    half = ROPE_DIM // 2
    inv_freq = ROPE_THETA ** (-jnp.arange(half, dtype=F32) * 2.0 / ROPE_DIM)
    ang = pos.astype(F32)[:, None] * inv_freq
    c, s = jnp.cos(ang), jnp.sin(ang)
    n = pos.shape[0]
    rest = HEAD_DIM - ROPE_DIM
    cos_h = jnp.concatenate([c, c, jnp.ones((n, rest), F32)], axis=-1)
    sp_h = jnp.concatenate([jnp.zeros((n, half), F32), s, jnp.zeros((n, rest), F32)], axis=-1)
    sm_h = jnp.concatenate([-s, jnp.zeros((n, half + rest), F32)], axis=-1)
    two = lambda t: jnp.concatenate([t, t], axis=-1)
    return two(cos_h), two(sp_h), two(sm_h), c.T, s.T


def _pair_diag(w):
    z = jnp.zeros_like(w)
    return jnp.concatenate([jnp.concatenate([w, z], axis=-1), jnp.concatenate([z, w], axis=-1)], axis=-2)


def _overlap_t(seq):
    n_cmp = seq // NSA_CMP_STRIDE
    n_slc = seq // NSA_SEL_BLOCK
    starts = np.arange(n_cmp) * NSA_CMP_STRIDE
    j = np.arange(n_slc)
    ovl = (starts[None, :] < (j[:, None] + 1) * NSA_SEL_BLOCK) & (starts[None, :] + NSA_CMP_LEN > j[:, None] * NSA_SEL_BLOCK)
    ovl[:, n_cmp - 1] = False
    return jnp.asarray(ovl, BF16)


def _mixers(x, seq_tabs, cmp_tabs, ovl_t, ln1, w, qn_a, kn_a, qn_b, kn_b, pe_k, pe_v, wk1, wk2, wv1, wv2,
            qn_c, kn_c):
    rep = lambda g, n: jnp.tile(g, n)
    w_t, w_r = w
    gain_t = jnp.concatenate([rep(qn_a * Q_SCALE, H_MOBA), rep(qn_b * Q_SCALE, H_NSA)])
    gain_t = jnp.broadcast_to(gain_t[:, None], (gain_t.shape[0], LANES))
    gain_r = jnp.concatenate([rep(kn_a, H_MOBA), rep(kn_b[1], H_NSA_KV), rep(kn_b[2], H_NSA_KV),
                              rep(qn_c * Q_SCALE, H_DIL), rep(kn_c, H_DIL)])[None, :]
    pbt, pb, pf = _in_proj(x, ln1[None, :], w_t, w_r, gain_t, gain_r, *seq_tabs)

    o_a = _moba(pbt, pb)
    two = lambda t: jnp.concatenate([t, t], axis=-1)
    w1 = lambda t: _pair_diag(t.reshape(NSA_CMP_LEN, HEAD_DIM, NSA_CMP_HIDDEN)).astype(BF16).reshape(
        2, NSA_CMP_LEN // 2 * LANES, 2 * NSA_CMP_HIDDEN)
    kcmp, vcmp_t = _compress(pf, two(pe_k), two(pe_v), w1(wk1), _pair_diag(wk2).astype(BF16),
                             w1(wv1), _pair_diag(wv2).astype(BF16), two(kn_b[0])[None, :], *cmp_tabs[:3])
    o_b = _nsa(pbt, pb, pf, kcmp, vcmp_t, ovl_t)
    o_c = _dilated(pb)
    return o_a, o_b, o_c


def _layer(x, seq_tabs, cmp_tabs, ovl_t, ln1, w_in, qn_a, kn_a, qn_b, kn_b, pe_k, pe_v, wk1, wk2, wv1, wv2,
           qn_c, kn_c, w_out, ln2, w_gate, w_up, conv_w, conv_b, w_down):
    B, S, D = x.shape
    o_a, o_b, o_c = _mixers(x, seq_tabs, cmp_tabs, ovl_t, ln1, w_in, qn_a, kn_a, qn_b, kn_b, pe_k, pe_v,
                            wk1, wk2, wv1, wv2, qn_c, kn_c)
    nsa_rows = [w_out[QKV_A + h * HEAD_DIM:QKV_A + (h + 1) * HEAD_DIM] for h in NSA_ORDER]
    w_o = jnp.concatenate([w_out[:QKV_A]] + nsa_rows + [w_out[QKV_A + Q_B:]], axis=0).astype(BF16)
    M = B * S
    x2 = _proj_ffn(x.reshape(M, D), o_a.reshape(M, QKV_A), o_b.reshape(M, Q_B), o_c.reshape(M, QKV_C), w_o,
                   ln2[None, :], w_gate.astype(BF16), w_up.astype(BF16), conv_w, conv_b[None, :],
                   w_down.astype(BF16), S)
    return x2.reshape(B, S, D)


def kernel(x, ln1, w_in, qn_a, kn_a, qn_b, kn_b, cmp_pe_k, cmp_pe_v, cmp_k_w1, cmp_k_w2, cmp_v_w1, cmp_v_w2,
           qn_c, kn_c, w_out, ln2, w_gate, w_up, conv_w, conv_b, w_down):
    S = x.shape[1]
    seq_tabs = _rope_tables(jnp.arange(S))
    n_cmp = S // NSA_CMP_STRIDE
    cmp_tabs = _rope_tables(jnp.arange(n_cmp) * NSA_CMP_STRIDE + (NSA_CMP_LEN - 1))
    ovl_t = _overlap_t(S)
    for l in range(ln1.shape[0]):
        x = _layer(x, seq_tabs, cmp_tabs, ovl_t, ln1[l], _prep_w_in(w_in, l), qn_a[l], kn_a[l], qn_b[l], kn_b[l],
                   cmp_pe_k[l], cmp_pe_v[l], cmp_k_w1[l], cmp_k_w2[l], cmp_v_w1[l], cmp_v_w2[l], qn_c[l], kn_c[l],
                   w_out[l], ln2[l], w_gate[l], w_up[l], conv_w[l], conv_b[l], w_down[l])
    return x
```
